```python
import math
import numpy as np
import jax, jax.numpy as jnp
from jax import lax

D_MODEL = 1024
BATCH = 2
SEQ = 8192
DEPTH = 1

D_MIX = D_MODEL
RET_HEADS = D_MODEL // 256
RET_QK_DIM = 128
RET_V_DIM = 128
RET_WIDTH = RET_HEADS * RET_V_DIM
RET_CHUNK = 128
MLA_HEADS = D_MODEL // 256
MLA_Q_RANK = 256
MLA_KV_RANK = 256
MLA_NOPE_DIM = 128
MLA_ROPE_DIM = 64
MLA_V_DIM = 128
MLA_WIDTH = MLA_HEADS * MLA_V_DIM
Q_BLOCK = 128
D_FF = 2816
CONV_WIDTH = 3
ROPE_BASE = 10000.0
LN_EPS = 1e-5
RMS_EPS = 1e-6
DN_ALPHA = (2.0 * DEPTH) ** 0.25
DN_BETA = (8.0 * DEPTH) ** -0.25
N_MOD = 6

IN_SIZES = [RET_HEADS * RET_QK_DIM,
            RET_HEADS * RET_QK_DIM,
            RET_HEADS * RET_V_DIM,
            RET_WIDTH,
            MLA_Q_RANK,
            MLA_KV_RANK,
            MLA_ROPE_DIM]
IN_DIM = sum(IN_SIZES)
IN_SPLITS = [int(s) for s in np.cumsum(IN_SIZES)[:-1]]

kernel_name = "hybrid_retention_mla_convffn_deepnorm_adaln"


def layer_norm_plain(x):
    xf = x.astype(jnp.float32)
    mu = jnp.mean(xf, axis=-1, keepdims=True)
    var = jnp.mean(jnp.square(xf - mu), axis=-1, keepdims=True)
    return ((xf - mu) * lax.rsqrt(var + LN_EPS)).astype(x.dtype)


def layer_norm(x, w, b):
    return layer_norm_plain(x) * w + b


def rms_norm(x, w):
    xf = x.astype(jnp.float32)
    y = xf * lax.rsqrt(jnp.mean(jnp.square(xf), axis=-1, keepdims=True) + RMS_EPS)
    return y.astype(x.dtype) * w


def rope(x, pos):
    half = x.shape[-1] // 2
    inv = ROPE_BASE ** (-jnp.arange(half, dtype=jnp.float32) / half)
    ang = pos[:, None] * inv[None, :]
    cos = jnp.cos(ang).astype(x.dtype)
    sin = jnp.sin(ang).astype(x.dtype)
    x1, x2 = x[..., :half], x[..., half:]
    return jnp.concatenate([x1 * cos - x2 * sin, x2 * cos + x1 * sin], axis=-1)


def retention(q, k, v):
    B, H, S, dk = q.shape
    dv = v.shape[-1]
    C = RET_CHUNK
    nC = S // C
    q = q.astype(jnp.float32).reshape(B, H, nC, C, dk)
    k = k.astype(jnp.float32).reshape(B, H, nC, C, dk)
    v = v.astype(jnp.float32).reshape(B, H, nC, C, dv)
    log_g = jnp.log1p(-(2.0 ** (-5.0 - jnp.arange(H, dtype=jnp.float32))))
    n = jnp.arange(C, dtype=jnp.float32)
    rel = n[:, None] - n[None, :]
    decay = jnp.where(rel >= 0, jnp.exp(log_g[:, None, None] * jnp.maximum(rel, 0.0)), 0.0)
    scores = jnp.einsum('bhcnd,bhcmd->bhcnm', q, k) * decay[None, :, None]
    inner = jnp.einsum('bhcnm,bhcme->bhcne', scores, v)
    k_w = jnp.exp(log_g[:, None] * (C - 1.0 - n)[None, :])
    chunk_kv = jnp.einsum('bhcmd,hm,bhcme->bhcde', k, k_w, v)
    chunk_decay = jnp.exp(log_g * C)[None, :, None, None]

    def step(state, kv):
        return state * chunk_decay + kv, state

    init = jnp.zeros((B, H, dk, dv), jnp.float32)
    _, prev = lax.scan(step, init, jnp.moveaxis(chunk_kv, 2, 0))
    prev = jnp.moveaxis(prev, 0, 2)
    q_w = jnp.exp(log_g[:, None] * (n + 1.0)[None, :])
    cross = jnp.einsum('bhcnd,bhcde,hn->bhcne', q, prev, q_w)
    return (inner + cross).reshape(B, H, S, dv)


def causal_block_attention(q, k, v):
    B, H, S, dq = q.shape
    nb = S // Q_BLOCK
    scale = dq ** -0.5
    qb = jnp.moveaxis(q.reshape(B, H, nb, Q_BLOCK, dq), 2, 0)
    key_pos = jnp.arange(S)

    def block(args):
        q_blk, start = args
        s = jnp.einsum('bhqd,bhkd->bhqk', q_blk, k).astype(jnp.float32) * scale
        q_pos = start + jnp.arange(Q_BLOCK)
        s = jnp.where(key_pos[None, :] <= q_pos[:, None], s, -jnp.inf)
        p = jax.nn.softmax(s, axis=-1).astype(v.dtype)
        return jnp.einsum('bhqk,bhkd->bhqd', p, v)

    out = lax.map(block, (qb, jnp.arange(nb) * Q_BLOCK))
    return jnp.moveaxis(out, 0, 2).reshape(B, H, S, v.shape[-1])


def token_mixer(h, w_in, ret_gn_w, mla_q_norm_w, w_uq, mla_kv_norm_w, w_ukv, w_out, pos):
    B, S, _ = h.shape
    proj = h @ w_in
    rq, rk, rv, rg, cq, ckv, kr = jnp.split(proj, IN_SPLITS, axis=-1)

    def heads(t, n_heads):
        return t.reshape(B, S, n_heads, -1).transpose(0, 2, 1, 3)

    rq = rope(heads(rq, RET_HEADS), pos)
    rk = rope(heads(rk, RET_HEADS), pos) * (RET_QK_DIM ** -0.5)
    ro = retention(rq, rk, heads(rv, RET_HEADS))
    mu = jnp.mean(ro, axis=-1, keepdims=True)
    var = jnp.mean(jnp.square(ro - mu), axis=-1, keepdims=True)
    ro = ((ro - mu) * lax.rsqrt(var + LN_EPS)).astype(h.dtype)
    ro = ro * ret_gn_w.reshape(RET_HEADS, 1, RET_V_DIM)
    ret_out = ro.transpose(0, 2, 1, 3).reshape(B, S, RET_WIDTH) * jax.nn.silu(rg)

    q = (rms_norm(cq, mla_q_norm_w) @ w_uq).reshape(B, S, MLA_HEADS, MLA_NOPE_DIM + MLA_ROPE_DIM)
    q = q.transpose(0, 2, 1, 3)
    q = jnp.concatenate([q[..., :MLA_NOPE_DIM], rope(q[..., MLA_NOPE_DIM:], pos)], axis=-1)
    kv = (rms_norm(ckv, mla_kv_norm_w) @ w_ukv).reshape(B, S, MLA_HEADS, MLA_NOPE_DIM + MLA_V_DIM)
    kv = kv.transpose(0, 2, 1, 3)
    k_nope, mv = kv[..., :MLA_NOPE_DIM], kv[..., MLA_NOPE_DIM:]
    k_rope = rope(kr[:, None], pos)
    k = jnp.concatenate([k_nope, jnp.broadcast_to(k_rope, (B, MLA_HEADS, S, MLA_ROPE_DIM))], axis=-1)
    mo = causal_block_attention(q, k, mv)
    mla_out = mo.transpose(0, 2, 1, 3).reshape(B, S, MLA_WIDTH)

    return jnp.concatenate([ret_out, mla_out], axis=-1) @ w_out


def conv_ffn(h, w_up, conv_w, conv_b, w_down):
    S = h.shape[1]
    u = h @ w_up
    up = jnp.pad(u, ((0, 0), (CONV_WIDTH - 1, 0), (0, 0)))
    u = conv_b + sum(up[:, j:j + S] * conv_w[j] for j in range(CONV_WIDTH))
    gate, val = jnp.split(u, 2, axis=-1)
    return (jax.nn.silu(gate) * val) @ w_down


def setup_inputs(seed: int = 0) -> dict:
    key = jax.random.key(seed)
    ks = jax.random.split(key, 20)
    L = DEPTH
    nrm = lambda k, shape, s: jax.random.normal(k, shape, jnp.float32) * s
    gain = lambda k, n: 1.0 + 0.02 * jax.random.normal(k, (L, n), jnp.float32)

    in_scale = jnp.concatenate([jnp.ones((IN_SPLITS[1],), jnp.float32),
                                jnp.full((IN_SIZES[2],), DN_BETA, jnp.float32),
                                jnp.ones((IN_DIM - IN_SPLITS[2],), jnp.float32)])
    ukv_scale = jnp.tile(jnp.concatenate([jnp.ones((MLA_NOPE_DIM,), jnp.float32),
                                          jnp.full((MLA_V_DIM,), DN_BETA, jnp.float32)]), MLA_HEADS)
    return {
        "x": nrm(ks[0], (BATCH, SEQ, D_MODEL), 1.0),
        "c": nrm(ks[1], (BATCH, D_MODEL), 1.0),
        "w_ada": nrm(ks[2], (L, D_MODEL, N_MOD * D_MODEL), 0.2 * D_MODEL ** -0.5),
        "b_ada": nrm(ks[3], (L, N_MOD * D_MODEL), 0.01),
        "w_in": nrm(ks[4], (L, D_MODEL, IN_DIM), D_MODEL ** -0.5) * in_scale,
        "ret_gn_w": gain(ks[5], RET_WIDTH),
        "mla_q_norm_w": gain(ks[6], MLA_Q_RANK),
        "w_uq": nrm(ks[7], (L, MLA_Q_RANK, MLA_HEADS * (MLA_NOPE_DIM + MLA_ROPE_DIM)), MLA_Q_RANK ** -0.5),
        "mla_kv_norm_w": gain(ks[8], MLA_KV_RANK),
        "w_ukv": nrm(ks[9], (L, MLA_KV_RANK, MLA_HEADS * (MLA_NOPE_DIM + MLA_V_DIM)), MLA_KV_RANK ** -0.5) * ukv_scale,
        "w_out": nrm(ks[10], (L, D_MIX, D_MODEL), DN_BETA * D_MIX ** -0.5),
        "ln1_w": gain(ks[11], D_MODEL),
        "ln1_b": nrm(ks[12], (L, D_MODEL), 0.02),
        "w_up": nrm(ks[13], (L, D_MODEL, 2 * D_FF), D_MODEL ** -0.5),
        "conv_w": nrm(ks[14], (L, CONV_WIDTH, 2 * D_FF), CONV_WIDTH ** -0.5),
        "conv_b": nrm(ks[15], (L, 2 * D_FF), 0.02),
        "w_down": nrm(ks[16], (L, D_FF, D_MODEL), DN_BETA * D_FF ** -0.5),
        "ln2_w": gain(ks[17], D_MODEL),
        "ln2_b": nrm(ks[18], (L, D_MODEL), 0.02),
    }


def reference(x, c, w_ada, b_ada, w_in, ret_gn_w, mla_q_norm_w, w_uq, mla_kv_norm_w, w_ukv,
              w_out, ln1_w, ln1_b, w_up, conv_w, conv_b, w_down, ln2_w, ln2_b):
    S = x.shape[1]
    pos = jnp.arange(S, dtype=jnp.float32)
    cond = jax.nn.silu(c)
    for l in range(DEPTH):
        mod = (cond @ w_ada[l] + b_ada[l])[:, None, :]
        sh1, sc1, g1, sh2, sc2, g2 = jnp.split(mod, N_MOD, axis=-1)
        h = layer_norm_plain(x) * (1.0 + sc1) + sh1
        y = token_mixer(h, w_in[l], ret_gn_w[l], mla_q_norm_w[l], w_uq[l], mla_kv_norm_w[l],
                        w_ukv[l], w_out[l], pos)
        x = layer_norm(DN_ALPHA * x + (1.0 + g1) * y, ln1_w[l], ln1_b[l])
        h = layer_norm_plain(x) * (1.0 + sc2) + sh2
        y = conv_ffn(h, w_up[l], conv_w[l], conv_b[l], w_down[l])
        x = layer_norm(DN_ALPHA * x + (1.0 + g2) * y, ln2_w[l], ln2_b[l])
    return x
```

```python
import functools
import math

import numpy as np
import jax
import jax.numpy as jnp
from jax import lax
from jax.experimental import pallas as pl
from jax.experimental.pallas import tpu as pltpu

F32 = jnp.float32
BF16 = jnp.bfloat16

D_MODEL = 1024
DEPTH = 1
N_HEADS = 4
HEAD_DIM = 128
RET_WIDTH = N_HEADS * HEAD_DIM
RET_CHUNK = 128
MLA_RANK = 256
MLA_ROPE_DIM = 64
MLA_QK_DIM = HEAD_DIM + MLA_ROPE_DIM
D_FF = 2816
ROPE_BASE = 10000.0
LN_EPS = 1e-5
RMS_EPS = 1e-6
DN_ALPHA = (2.0 * DEPTH) ** 0.25
N_MOD = 6
LOG_G = [math.log1p(-(2.0 ** (-5.0 - h))) for h in range(N_HEADS)]

LANES = 128
SUBLANES = 8
MXU_DIM = 256
VMEM_LIMIT = 56 * 1024 * 1024

ROW_TILE = 512
ATTN_TQ = 512
ATTN_TK = 512
FF_CHUNK = MXU_DIM
N_FF_CHUNKS = D_FF // FF_CHUNK
QK_PAD = 2 * LANES
NEG_BIG = -1e30


def _ln_plain(x):
    mu = jnp.mean(x, axis=-1, keepdims=True)
    xc = x - mu
    var = jnp.mean(xc * xc, axis=-1, keepdims=True)
    return xc * lax.rsqrt(var + LN_EPS)


def _rms(x):
    return x * lax.rsqrt(jnp.mean(x * x, axis=-1, keepdims=True) + RMS_EPS)


def _silu(x):
    return x * jax.nn.sigmoid(x)


def _dot(a, b):
    return jnp.dot(a, b, preferred_element_type=F32)


def _dot_nt(a, b):
    return lax.dot_general(a, b, (((1,), (1,)), ((), ())), preferred_element_type=F32)


def _dot_tn(a, b):
    return lax.dot_general(a, b, (((0,), (0,)), ((), ())), preferred_element_type=F32)


def _ada_kernel(c_ref, w_ref, b_ref, o_ref):
    cond = _silu(c_ref[...])
    o_ref[...] = _dot(cond.astype(BF16), w_ref[...].astype(BF16)) + b_ref[...]


def _ada_call(c_pad, w_ada, b_ada):
    n_out = w_ada.shape[1]
    return pl.pallas_call(
        _ada_kernel,
        grid=(n_out // D_MODEL,),
        in_specs=[
            pl.BlockSpec((SUBLANES, D_MODEL), lambda j: (0, 0)),
            pl.BlockSpec((D_MODEL, D_MODEL), lambda j: (0, j)),
            pl.BlockSpec((1, D_MODEL), lambda j: (0, j)),
        ],
        out_specs=pl.BlockSpec((SUBLANES, D_MODEL), lambda j: (0, j)),
        out_shape=jax.ShapeDtypeStruct((SUBLANES, n_out), F32),
        compiler_params=pltpu.CompilerParams(dimension_semantics=("arbitrary",)),
        name="ada_mod",
    )(c_pad, w_ada, b_ada)


def _rope128(x, cos, sin_signed):
    return x * cos + pltpu.roll(x, LANES // 2, axis=1) * sin_signed


def _inproj_kernel(x_ref, mod_ref, wr_ref, wl_ref, qnw_ref, kvnw_ref, wuq_ref, wukv_ref,
                   cosr_ref, sinr_ref, cosm_ref, sinm_ref,
                   rq_ref, rk_ref, rv_ref, rg_ref, q_ref, k_ref, v_ref, *, q_scale, rk_scale):
    x = x_ref[0]
    shift = mod_ref[0, 0:1, :]
    scale = mod_ref[0, 1:2, :]
    h = (_ln_plain(x) * (1.0 + scale) + shift).astype(BF16)

    cosr = cosr_ref[...]
    sinr = sinr_ref[...]
    cosm = cosm_ref[...]
    sinm = sinm_ref[...]

    pq = _dot(h, wr_ref[:, 0:RET_WIDTH])
    pk = _dot(h, wr_ref[:, RET_WIDTH:2 * RET_WIDTH])
    for hd in range(N_HEADS):
        sl = slice(hd * HEAD_DIM, (hd + 1) * HEAD_DIM)
        rq_ref[0, :, sl] = _rope128(pq[:, sl], cosr, sinr).astype(BF16)
        rk_ref[0, :, sl] = (_rope128(pk[:, sl], cosr, sinr) * rk_scale).astype(BF16)
    rv_ref[0] = _dot(h, wr_ref[:, 2 * RET_WIDTH:3 * RET_WIDTH]).astype(BF16)
    rg_ref[0] = _silu(_dot(h, wr_ref[:, 3 * RET_WIDTH:4 * RET_WIDTH])).astype(BF16)

    lat = _dot(h, wl_ref[...])
    cq = (_rms(lat[:, 0:MLA_RANK]) * qnw_ref[...]).astype(BF16)
    ckv = (_rms(lat[:, MLA_RANK:2 * MLA_RANK]) * kvnw_ref[...]).astype(BF16)
    kr = _rope128(lat[:, 2 * MLA_RANK:2 * MLA_RANK + LANES], cosm, sinm)
    lane = lax.broadcasted_iota(jnp.int32, kr.shape, 1)
    first_of_pair = (lane % (LANES // 2)) < (LANES // 4)
    kr_even = jnp.where(first_of_pair, kr, 0.0).astype(BF16)
    kr_odd = jnp.where(first_of_pair, 0.0, kr).astype(BF16)

    qf = _dot(cq, wuq_ref[...])
    kvf = _dot(ckv, wukv_ref[...])
    v_ref[0] = kvf[:, RET_WIDTH:2 * RET_WIDTH].astype(BF16)
    for p in range(N_HEADS // 2):
        qr = qf[:, RET_WIDTH + p * LANES:RET_WIDTH + (p + 1) * LANES]
        qr = (_rope128(qr, cosm, sinm) * q_scale).astype(BF16)
        for hd in (2 * p, 2 * p + 1):
            sl = slice(hd * HEAD_DIM, (hd + 1) * HEAD_DIM)
            q_ref[0, :, hd * QK_PAD:hd * QK_PAD + LANES] = (qf[:, sl] * q_scale).astype(BF16)
            q_ref[0, :, hd * QK_PAD + LANES:(hd + 1) * QK_PAD] = qr
            k_ref[0, :, hd * QK_PAD:hd * QK_PAD + LANES] = kvf[:, sl].astype(BF16)
            k_ref[0, :, hd * QK_PAD + LANES:(hd + 1) * QK_PAD] = kr_even if hd % 2 == 0 else kr_odd


def _const_spec(shape):
    nd = len(shape)
    return pl.BlockSpec(shape, lambda *_: (0,) * nd, pipeline_mode=pl.Buffered(1))


def _inproj_call(x, mod, w_r, w_l, qnw, kvnw, w_uq, w_ukv, cosr, sinr, cosm, sinm):
    B, S, D = x.shape
    tm = ROW_TILE
    q_scale = (MLA_QK_DIM ** -0.5) * math.log2(math.e)
    rk_scale = HEAD_DIM ** -0.5
    row = lambda w: pl.BlockSpec((1, tm, w), lambda b, s: (b, s, 0))
    tab = pl.BlockSpec((tm, LANES), lambda b, s: (s, 0))
    out_shapes = (
        [jax.ShapeDtypeStruct((B, S, RET_WIDTH), BF16)] * 4
        + [jax.ShapeDtypeStruct((B, S, N_HEADS * QK_PAD), BF16)] * 2
        + [jax.ShapeDtypeStruct((B, S, RET_WIDTH), BF16)]
    )
    return pl.pallas_call(
        functools.partial(_inproj_kernel, q_scale=q_scale, rk_scale=rk_scale),
        grid=(B, S // tm),
        in_specs=[
            row(D),
            pl.BlockSpec((1, N_MOD, D), lambda b, s: (b, 0, 0)),
            _const_spec(w_r.shape), _const_spec(w_l.shape),
            _const_spec(qnw.shape), _const_spec(kvnw.shape),
            _const_spec(w_uq.shape), _const_spec(w_ukv.shape),
            tab, tab, tab, tab,
        ],
        out_specs=[row(RET_WIDTH)] * 4 + [row(N_HEADS * QK_PAD)] * 2 + [row(RET_WIDTH)],
        out_shape=out_shapes,
        compiler_params=pltpu.CompilerParams(
            dimension_semantics=("arbitrary", "arbitrary"), vmem_limit_bytes=VMEM_LIMIT),
        name="inproj",
    )(x, mod, w_r, w_l, qnw, kvnw, w_uq, w_ukv, cosr, sinr, cosm, sinm)


def _retention_kernel(q_ref, k_ref, v_ref, g_ref, gnw_ref, o_ref, state_ref, *, chunks):
    C = RET_CHUNK

    @pl.when(pl.program_id(1) == 0)
    def _():
        state_ref[...] = jnp.zeros_like(state_ref)

    row = lax.broadcasted_iota(jnp.int32, (C, C), 0)
    col = lax.broadcasted_iota(jnp.int32, (C, C), 1)
    rel = (row - col).astype(F32)
    n = lax.broadcasted_iota(jnp.int32, (C, 1), 0).astype(F32)
    for hd in range(N_HEADS):
        lg = LOG_G[hd]
        decay = jnp.where(rel >= 0.0, jnp.exp(lg * jnp.maximum(rel, 0.0)), 0.0)
        q_w = jnp.exp(lg * (n + 1.0))
        k_w = jnp.exp(lg * (C - 1.0 - n))
        chunk_decay = math.exp(lg * C)
        hs = slice(hd * HEAD_DIM, (hd + 1) * HEAD_DIM)
        gnw = gnw_ref[:, hs]
        for c in range(chunks):
            rs = slice(c * C, (c + 1) * C)
            q = q_ref[0, rs, hs]
            k = k_ref[0, rs, hs]
            v = v_ref[0, rs, hs]
            scores = _dot_nt(q, k) * decay
            inner = _dot(scores.astype(BF16), v)
            state = state_ref[hd]
            cross = _dot(q, state.astype(BF16)) * q_w
            kw = (k.astype(F32) * k_w).astype(BF16)
            state_ref[hd] = state * chunk_decay + _dot_tn(kw, v)
            o = _ln_plain(inner + cross) * gnw
            o_ref[0, rs, hs] = (o * g_ref[0, rs, hs].astype(F32)).astype(BF16)


def _retention_call(rq, rk, rv, rg, gnw):
    B, S, W = rq.shape
    tc = ROW_TILE
    row = pl.BlockSpec((1, tc, W), lambda b, s: (b, s, 0))
    return pl.pallas_call(
        functools.partial(_retention_kernel, chunks=tc // RET_CHUNK),
        grid=(B, S // tc),
        in_specs=[row, row, row, row, pl.BlockSpec((1, W), lambda b, s: (0, 0))],
        out_specs=row,
        out_shape=jax.ShapeDtypeStruct((B, S, W), BF16),
        scratch_shapes=[pltpu.VMEM((N_HEADS, HEAD_DIM, HEAD_DIM), F32)],
        compiler_params=pltpu.CompilerParams(dimension_semantics=("arbitrary", "arbitrary")),
        name="retention",
    )(rq, rk, rv, rg, gnw)


def _attn_kernel(q_ref, k_ref, v_ref, o_ref, *, tq, tk):
    qi = pl.program_id(2)
    q = q_ref[0]

    def step(j, carry, masked):
        m, l, acc = carry
        start = pl.multiple_of(j * tk, tk)
        s = _dot_nt(q, k_ref[0, pl.ds(start, tk), :])
        if masked:
            r = lax.broadcasted_iota(jnp.int32, s.shape, 0)
            c = lax.broadcasted_iota(jnp.int32, s.shape, 1)
            s = jnp.where(c <= r, s, NEG_BIG)
        m_new = jnp.maximum(m, jnp.max(s, axis=-1, keepdims=True))
        alpha = jnp.exp2(m - m_new)
        p = jnp.exp2(s - m_new)
        l = alpha * l + jnp.sum(p, axis=-1, keepdims=True)
        acc = alpha * acc + _dot(p.astype(BF16), v_ref[0, pl.ds(start, tk), :])
        return m_new, l, acc

    init = (jnp.full((tq, 1), NEG_BIG, F32), jnp.zeros((tq, 1), F32), jnp.zeros((tq, HEAD_DIM), F32))
    carry = lax.fori_loop(0, qi, functools.partial(step, masked=False), init)
    _, l, acc = step(qi, carry, masked=True)
    o_ref[0] = (acc / l).astype(BF16)


def _attn_call(q, k, v):
    B, S, _ = q.shape
    tq, tk = ATTN_TQ, ATTN_TK
    assert tq == tk
    return pl.pallas_call(
        functools.partial(_attn_kernel, tq=tq, tk=tk),
        grid=(B, N_HEADS, S // tq),
        in_specs=[
            pl.BlockSpec((1, tq, QK_PAD), lambda b, h, i: (b, i, h)),
            pl.BlockSpec((1, S, QK_PAD), lambda b, h, i: (b, 0, h)),
            pl.BlockSpec((1, S, HEAD_DIM), lambda b, h, i: (b, 0, h)),
        ],
        out_specs=pl.BlockSpec((1, tq, HEAD_DIM), lambda b, h, i: (b, i, h)),
        out_shape=jax.ShapeDtypeStruct((B, S, N_HEADS * HEAD_DIM), BF16),
        compiler_params=pltpu.CompilerParams(
            dimension_semantics=("arbitrary", "arbitrary", "arbitrary"), vmem_limit_bytes=VMEM_LIMIT),
        name="mla_attn",
    )(q, k, v)


def _mlp_kernel(x_ref, ret_ref, mla_ref, mod_ref, wo_ref, ln1w_ref, ln1b_ref, wup_ref, cw_ref, cb_ref,
                wdn_ref, ln2w_ref, ln2b_ref, o_ref, carry_ref, ubuf_ref, *, tm):
    @pl.when(pl.program_id(1) == 0)
    def _():
        carry_ref[...] = jnp.zeros_like(carry_ref)

    gate1 = mod_ref[0, 2:3, :]
    shift2 = mod_ref[0, 3:4, :]
    scale2 = mod_ref[0, 4:5, :]
    gate2 = mod_ref[0, 5:6, :]

    y = _dot(ret_ref[0], wo_ref[0:RET_WIDTH, :]) + _dot(mla_ref[0], wo_ref[RET_WIDTH:2 * RET_WIDTH, :])
    x1 = _ln_plain(DN_ALPHA * x_ref[0] + (1.0 + gate1) * y) * ln1w_ref[...] + ln1b_ref[...]
    h2 = (_ln_plain(x1) * (1.0 + scale2) + shift2).astype(BF16)

    def conv(u, cols):
        ubuf_ref[0:SUBLANES, :] = carry_ref[:, cols]
        ubuf_ref[SUBLANES:SUBLANES + tm, :] = u
        carry_ref[:, cols] = u[tm - SUBLANES:tm, :]
        u1 = ubuf_ref[SUBLANES - 1:SUBLANES - 1 + tm, :]
        u2 = ubuf_ref[SUBLANES - 2:SUBLANES - 2 + tm, :]
        return cb_ref[:, cols] + cw_ref[2:3, cols] * u + cw_ref[1:2, cols] * u1 + cw_ref[0:1, cols] * u2

    acc = None
    for c in range(N_FF_CHUNKS):
        gcols = slice(c * FF_CHUNK, (c + 1) * FF_CHUNK)
        vcols = slice(D_FF + c * FF_CHUNK, D_FF + (c + 1) * FF_CHUNK)
        g = conv(_dot(h2, wup_ref[:, gcols]), gcols)
        val = conv(_dot(h2, wup_ref[:, vcols]), vcols)
        part = _dot((_silu(g) * val).astype(BF16), wdn_ref[gcols, :])
        acc = part if acc is None else acc + part

    o_ref[0] = _ln_plain(DN_ALPHA * x1 + (1.0 + gate2) * acc) * ln2w_ref[...] + ln2b_ref[...]


def _mlp_call(x, ret, mla, mod, w_out, ln1w, ln1b, w_up, conv_w, conv_b, w_down, ln2w, ln2b):
    B, S, D = x.shape
    tm = ROW_TILE
    row = lambda w: pl.BlockSpec((1, tm, w), lambda b, s: (b, s, 0))
    return pl.pallas_call(
        functools.partial(_mlp_kernel, tm=tm),
        grid=(B, S // tm),
        in_specs=[
            row(D), row(RET_WIDTH), row(RET_WIDTH),
            pl.BlockSpec((1, N_MOD, D), lambda b, s: (b, 0, 0)),
            _const_spec(w_out.shape), _const_spec(ln1w.shape), _const_spec(ln1b.shape),
            _const_spec(w_up.shape), _const_spec(conv_w.shape), _const_spec(conv_b.shape),
            _const_spec(w_down.shape), _const_spec(ln2w.shape), _const_spec(ln2b.shape),
        ],
        out_specs=row(D),
        out_shape=jax.ShapeDtypeStruct((B, S, D), F32),
        scratch_shapes=[
            pltpu.VMEM((SUBLANES, 2 * D_FF), F32),
            pltpu.VMEM((SUBLANES + tm, FF_CHUNK), F32),
        ],
        compiler_params=pltpu.CompilerParams(
            dimension_semantics=("arbitrary", "arbitrary"), vmem_limit_bytes=VMEM_LIMIT),
        name="outproj_mlp",
    )(x, ret, mla, mod, w_out, ln1w, ln1b, w_up, conv_w, conv_b, w_down, ln2w, ln2b)


def _rope_tables(seq, half, reps):
    pos = jnp.arange(seq, dtype=F32)
    inv = ROPE_BASE ** (-jnp.arange(half, dtype=F32) / half)
    ang = pos[:, None] * inv[None, :]
    cos = jnp.cos(ang)
    sin = jnp.sin(ang)
    cos_t = jnp.tile(cos, (1, 2 * reps))
    sin_t = jnp.concatenate([jnp.tile(-sin, (1, reps)), jnp.tile(sin, (1, reps))], axis=-1)
    return cos_t, sin_t


def _uq_columns():
    half = MLA_ROPE_DIM // 2
    nope = [h * MLA_QK_DIM + np.arange(HEAD_DIM) for h in range(N_HEADS)]
    rope = []
    for p in range(N_HEADS // 2):
        for part in range(2):
            for h in (2 * p, 2 * p + 1):
                rope.append(h * MLA_QK_DIM + HEAD_DIM + part * half + np.arange(half))
    return np.concatenate(nope + rope)


def _ukv_columns():
    k = [h * 2 * HEAD_DIM + np.arange(HEAD_DIM) for h in range(N_HEADS)]
    v = [h * 2 * HEAD_DIM + HEAD_DIM + np.arange(HEAD_DIM) for h in range(N_HEADS)]
    return np.concatenate(k + v)


def _kr_columns(base):
    half = MLA_ROPE_DIM // 2
    x1 = base + np.arange(half)
    x2 = base + half + np.arange(half)
    return np.concatenate([x1, x1, x2, x2])


def kernel(x, c, w_ada, b_ada, w_in, ret_gn_w, mla_q_norm_w, w_uq, mla_kv_norm_w, w_ukv, w_out,
           ln1_w, ln1_b, w_up, conv_w, conv_b, w_down, ln2_w, ln2_b):
    B, S, D = x.shape
    assert D == D_MODEL and S % ROW_TILE == 0 and S % ATTN_TQ == 0 and w_ada.shape[0] == DEPTH == 1
    l = 0

    c_pad = jnp.pad(c, ((0, SUBLANES - B), (0, 0)))
    mod = _ada_call(c_pad, w_ada[l], b_ada[l][None, :])[:B].reshape(B, N_MOD, D)

    lat0 = 4 * RET_WIDTH
    w_in_b = w_in[l].astype(BF16)
    w_r = w_in_b[:, :lat0]
    w_l = jnp.concatenate(
        [w_in_b[:, lat0:lat0 + 2 * MLA_RANK], w_in_b[:, _kr_columns(lat0 + 2 * MLA_RANK)]], axis=1)
    w_uq_p = w_uq[l].astype(BF16)[:, _uq_columns()]
    w_ukv_p = w_ukv[l].astype(BF16)[:, _ukv_columns()]
    cosr, sinr = _rope_tables(S, HEAD_DIM // 2, 1)
    cosm, sinm = _rope_tables(S, MLA_ROPE_DIM // 2, 2)

    rq, rk, rv, rg, q, k, v = _inproj_call(
        x, mod, w_r, w_l, mla_q_norm_w[l][None, :], mla_kv_norm_w[l][None, :], w_uq_p, w_ukv_p,
        cosr, sinr, cosm, sinm)
    ret = _retention_call(rq, rk, rv, rg, ret_gn_w[l][None, :])
    mla = _attn_call(q, k, v)
    return _mlp_call(
        x, ret, mla, mod, w_out[l].astype(BF16), ln1_w[l][None, :], ln1_b[l][None, :],
        w_up[l].astype(BF16), conv_w[l], conv_b[l][None, :], w_down[l].astype(BF16),
        ln2_w[l][None, :], ln2_b[l][None, :])
```

```python
import functools
import math

import numpy as np
import jax
import jax.numpy as jnp
from jax import lax
from jax.experimental import pallas as pl
from jax.experimental.pallas import tpu as pltpu

F32 = jnp.float32
BF16 = jnp.bfloat16

D_MODEL = 1024
DEPTH = 1
N_HEADS = 4
HEAD_DIM = 128
RET_WIDTH = N_HEADS * HEAD_DIM
RET_CHUNK = 128
MLA_RANK = 256
MLA_ROPE_DIM = 64
MLA_QK_DIM = HEAD_DIM + MLA_ROPE_DIM
D_FF = 2816
ROPE_BASE = 10000.0
LN_EPS = 1e-5
RMS_EPS = 1e-6
DN_ALPHA = (2.0 * DEPTH) ** 0.25
N_MOD = 6
LOG_G = [math.log1p(-(2.0 ** (-5.0 - h))) for h in range(N_HEADS)]

LANES = 128
SUBLANES = 8
MXU_DIM = 256
VMEM_LIMIT = 56 * 1024 * 1024

ROW_TILE = 512
ATTN_TQ = 1024
ATTN_TK = 512
FF_CHUNK = MXU_DIM
N_FF_CHUNKS = D_FF // FF_CHUNK
QK_PAD = 2 * LANES
NEG_BIG = -1e30


def _ln_plain(x):
    mu = jnp.mean(x, axis=-1, keepdims=True)
    xc = x - mu
    var = jnp.mean(xc * xc, axis=-1, keepdims=True)
    return xc * lax.rsqrt(var + LN_EPS)


def _rms(x):
    return x * lax.rsqrt(jnp.mean(x * x, axis=-1, keepdims=True) + RMS_EPS)


def _silu(x):
    return x * jax.nn.sigmoid(x)


def _dot(a, b):
    return jnp.dot(a, b, preferred_element_type=F32)


def _dot_nt(a, b):
    return lax.dot_general(a, b, (((1,), (1,)), ((), ())), preferred_element_type=F32)


def _dot_tn(a, b):
    return lax.dot_general(a, b, (((0,), (0,)), ((), ())), preferred_element_type=F32)


def _ada_kernel(c_ref, w_ref, b_ref, o_ref):
    cond = _silu(c_ref[...])
    o_ref[...] = _dot(cond.astype(BF16), w_ref[...].astype(BF16)) + b_ref[...]


def _ada_call(c_pad, w_ada, b_ada):
    n_out = w_ada.shape[1]
    return pl.pallas_call(
        _ada_kernel,
        grid=(n_out // D_MODEL,),
        in_specs=[
            pl.BlockSpec((SUBLANES, D_MODEL), lambda j: (0, 0)),
            pl.BlockSpec((D_MODEL, D_MODEL), lambda j: (0, j)),
            pl.BlockSpec((1, D_MODEL), lambda j: (0, j)),
        ],
        out_specs=pl.BlockSpec((SUBLANES, D_MODEL), lambda j: (0, j)),
        out_shape=jax.ShapeDtypeStruct((SUBLANES, n_out), F32),
        compiler_params=pltpu.CompilerParams(dimension_semantics=("arbitrary",)),
        name="ada_mod",
    )(c_pad, w_ada, b_ada)


def _rope128(x, cos, sin_signed):
    return x * cos + pltpu.roll(x, LANES // 2, axis=1) * sin_signed


def _inproj_kernel(x_ref, mod_ref, wr_ref, wl_ref, qnw_ref, kvnw_ref, wuq_ref, wukv_ref,
                   cosr_ref, sinr_ref, cosm_ref, sinm_ref,
                   rq_ref, rk_ref, rv_ref, rg_ref, q_ref, k_ref, vt_ref, *, q_scale, rk_scale):
    x = x_ref[0]
    shift = mod_ref[0, 0:1, :]
    scale = mod_ref[0, 1:2, :]
    h = (_ln_plain(x) * (1.0 + scale) + shift).astype(BF16)

    cosr = cosr_ref[...]
    sinr = sinr_ref[...]
    cosm = cosm_ref[...]
    sinm = sinm_ref[...]

    pq = _dot(h, wr_ref[:, 0:RET_WIDTH])
    pk = _dot(h, wr_ref[:, RET_WIDTH:2 * RET_WIDTH])
    for hd in range(N_HEADS):
        sl = slice(hd * HEAD_DIM, (hd + 1) * HEAD_DIM)
        rq_ref[0, :, sl] = _rope128(pq[:, sl], cosr, sinr).astype(BF16)
        rk_ref[0, :, sl] = (_rope128(pk[:, sl], cosr, sinr) * rk_scale).astype(BF16)
    rv_ref[0] = _dot(h, wr_ref[:, 2 * RET_WIDTH:3 * RET_WIDTH]).astype(BF16)
    rg_ref[0] = _silu(_dot(h, wr_ref[:, 3 * RET_WIDTH:4 * RET_WIDTH])).astype(BF16)

    lat = _dot(h, wl_ref[...])
    cq = (_rms(lat[:, 0:MLA_RANK]) * qnw_ref[...]).astype(BF16)
    ckv = (_rms(lat[:, MLA_RANK:2 * MLA_RANK]) * kvnw_ref[...]).astype(BF16)
    kr = _rope128(lat[:, 2 * MLA_RANK:2 * MLA_RANK + LANES], cosm, sinm)
    lane = lax.broadcasted_iota(jnp.int32, kr.shape, 1)
    first_of_pair = (lane % (LANES // 2)) < (LANES // 4)
    kr_even = jnp.where(first_of_pair, kr, 0.0).astype(BF16)
    kr_odd = jnp.where(first_of_pair, 0.0, kr).astype(BF16)

    qf = _dot(cq, wuq_ref[...])
    kvf = _dot(ckv, wukv_ref[...])
    for hd in range(N_HEADS):
        vt_ref[0, hd] = kvf[:, RET_WIDTH + hd * HEAD_DIM:RET_WIDTH + (hd + 1) * HEAD_DIM].T.astype(BF16)
    for p in range(N_HEADS // 2):
        qr = qf[:, RET_WIDTH + p * LANES:RET_WIDTH + (p + 1) * LANES]
        qr = (_rope128(qr, cosm, sinm) * q_scale).astype(BF16)
        for hd in (2 * p, 2 * p + 1):
            sl = slice(hd * HEAD_DIM, (hd + 1) * HEAD_DIM)
            q_ref[0, :, hd * QK_PAD:hd * QK_PAD + LANES] = (qf[:, sl] * q_scale).astype(BF16)
            q_ref[0, :, hd * QK_PAD + LANES:(hd + 1) * QK_PAD] = qr
            k_ref[0, :, hd * QK_PAD:hd * QK_PAD + LANES] = kvf[:, sl].astype(BF16)
            k_ref[0, :, hd * QK_PAD + LANES:(hd + 1) * QK_PAD] = kr_even if hd % 2 == 0 else kr_odd


def _const_spec(shape):
    nd = len(shape)
    return pl.BlockSpec(shape, lambda *_: (0,) * nd, pipeline_mode=pl.Buffered(1))


def _inproj_call(x, mod, w_r, w_l, qnw, kvnw, w_uq, w_ukv, cosr, sinr, cosm, sinm):
    B, S, D = x.shape
    tm = ROW_TILE
    q_scale = (MLA_QK_DIM ** -0.5) * math.log2(math.e)
    rk_scale = HEAD_DIM ** -0.5
    row = lambda w: pl.BlockSpec((1, tm, w), lambda b, s: (b, s, 0))
    tab = pl.BlockSpec((tm, LANES), lambda b, s: (s, 0))
    out_shapes = (
        [jax.ShapeDtypeStruct((B, S, RET_WIDTH), BF16)] * 4
        + [jax.ShapeDtypeStruct((B, S, N_HEADS * QK_PAD), BF16)] * 2
        + [jax.ShapeDtypeStruct((B, N_HEADS, HEAD_DIM, S), BF16)]
    )
    vt_spec = pl.BlockSpec((1, N_HEADS, HEAD_DIM, tm), lambda b, s: (b, 0, 0, s))
    return pl.pallas_call(
        functools.partial(_inproj_kernel, q_scale=q_scale, rk_scale=rk_scale),
        grid=(B, S // tm),
        in_specs=[
            row(D),
            pl.BlockSpec((1, N_MOD, D), lambda b, s: (b, 0, 0)),
            _const_spec(w_r.shape), _const_spec(w_l.shape),
            _const_spec(qnw.shape), _const_spec(kvnw.shape),
            _const_spec(w_uq.shape), _const_spec(w_ukv.shape),
            tab, tab, tab, tab,
        ],
        out_specs=[row(RET_WIDTH)] * 4 + [row(N_HEADS * QK_PAD)] * 2 + [vt_spec],
        out_shape=out_shapes,
        compiler_params=pltpu.CompilerParams(
            dimension_semantics=("arbitrary", "arbitrary"), vmem_limit_bytes=VMEM_LIMIT),
        name="inproj",
    )(x, mod, w_r, w_l, qnw, kvnw, w_uq, w_ukv, cosr, sinr, cosm, sinm)


def _retention_kernel(q_ref, k_ref, v_ref, g_ref, gnw_ref, o_ref, state_ref, *, chunks):
    C = RET_CHUNK

    @pl.when(pl.program_id(1) == 0)
    def _():
        state_ref[...] = jnp.zeros_like(state_ref)

    row = lax.broadcasted_iota(jnp.int32, (C, C), 0)
    col = lax.broadcasted_iota(jnp.int32, (C, C), 1)
    rel = (row - col).astype(F32)
    n = lax.broadcasted_iota(jnp.int32, (C, 1), 0).astype(F32)
    for hd in range(N_HEADS):
        lg = LOG_G[hd]
        decay = jnp.where(rel >= 0.0, jnp.exp(lg * jnp.maximum(rel, 0.0)), 0.0)
        q_w = jnp.exp(lg * (n + 1.0))
        k_w = jnp.exp(lg * (C - 1.0 - n))
        chunk_decay = math.exp(lg * C)
        hs = slice(hd * HEAD_DIM, (hd + 1) * HEAD_DIM)
        gnw = gnw_ref[:, hs]
        for c in range(chunks):
            rs = slice(c * C, (c + 1) * C)
            q = q_ref[0, rs, hs]
            k = k_ref[0, rs, hs]
            v = v_ref[0, rs, hs]
            scores = _dot_nt(q, k) * decay
            inner = _dot(scores.astype(BF16), v)
            state = state_ref[hd]
            cross = _dot(q, state.astype(BF16)) * q_w
            kw = (k.astype(F32) * k_w).astype(BF16)
            state_ref[hd] = state * chunk_decay + _dot_tn(kw, v)
            o = _ln_plain(inner + cross) * gnw
            o_ref[0, rs, hs] = (o * g_ref[0, rs, hs].astype(F32)).astype(BF16)


def _retention_call(rq, rk, rv, rg, gnw):
    B, S, W = rq.shape
    tc = ROW_TILE
    row = pl.BlockSpec((1, tc, W), lambda b, s: (b, s, 0))
    return pl.pallas_call(
        functools.partial(_retention_kernel, chunks=tc // RET_CHUNK),
        grid=(B, S // tc),
        in_specs=[row, row, row, row, pl.BlockSpec((1, W), lambda b, s: (0, 0))],
        out_specs=row,
        out_shape=jax.ShapeDtypeStruct((B, S, W), BF16),
        scratch_shapes=[pltpu.VMEM((N_HEADS, HEAD_DIM, HEAD_DIM), F32)],
        compiler_params=pltpu.CompilerParams(dimension_semantics=("arbitrary", "arbitrary")),
        name="retention",
    )(rq, rk, rv, rg, gnw)


def _attn_kernel(q_ref, k_ref, vt_ref, o_ref, qt_ref, s_buf, p_buf, acc_ref, *, tq, tk):
    qi = pl.program_id(2)
    qt_ref[...] = q_ref[0].T

    def scores(j, slot):
        start = pl.multiple_of(j * tk, tk)
        s_buf[slot] = _dot(k_ref[0, pl.ds(start, tk), :], qt_ref[...])

    def value_update(j, slot, alpha):
        start = pl.multiple_of(j * tk, tk)
        acc_ref[...] = alpha * acc_ref[...] + _dot(vt_ref[0, 0, :, pl.ds(start, tk)], p_buf[slot])

    def softmax(slot, m, l, first_key=None):
        s = s_buf[slot]
        if first_key is not None:
            key = lax.broadcasted_iota(jnp.int32, s.shape, 0) + first_key
            qry = lax.broadcasted_iota(jnp.int32, s.shape, 1)
            s = jnp.where(key <= qry, s, NEG_BIG)
        m_new = jnp.maximum(m, jnp.max(s, axis=0, keepdims=True))
        alpha = jnp.exp2(m - m_new)
        p = jnp.exp2(s - m_new)
        l = alpha * l + jnp.sum(p, axis=0, keepdims=True)
        p_buf[slot] = p.astype(BF16)
        return m_new, l, alpha

    def two_blocks(u, carry):
        alpha_prev, m, l = carry
        t = 2 * u
        scores(t + 1, 1)
        value_update(jnp.maximum(t - 1, 0), 1, alpha_prev)
        m, l, alpha_even = softmax(0, m, l)
        scores(t + 2, 0)
        value_update(t, 0, alpha_even)
        m, l, alpha_odd = softmax(1, m, l)
        return alpha_odd, m, l

    scores(0, 0)
    p_buf[1] = jnp.zeros((tk, tq), BF16)
    acc_ref[...] = jnp.zeros_like(acc_ref)
    init = (jnp.ones((1, tq), F32), jnp.full((1, tq), NEG_BIG, F32), jnp.zeros((1, tq), F32))
    alpha_prev, m, l = lax.fori_loop(0, qi, two_blocks, init)

    t = 2 * qi
    scores(t + 1, 1)
    value_update(jnp.maximum(t - 1, 0), 1, alpha_prev)
    m, l, alpha_even = softmax(0, m, l, first_key=0)
    value_update(t, 0, alpha_even)
    m, l, alpha_odd = softmax(1, m, l, first_key=tk)
    value_update(t + 1, 1, alpha_odd)
    o_ref[0] = (acc_ref[...] / l).T.astype(BF16)


def _attn_call(q, k, vt):
    B, S, _ = q.shape
    tq, tk = ATTN_TQ, ATTN_TK
    assert tq == 2 * tk
    return pl.pallas_call(
        functools.partial(_attn_kernel, tq=tq, tk=tk),
        grid=(B, N_HEADS, S // tq),
        in_specs=[
            pl.BlockSpec((1, tq, QK_PAD), lambda b, h, i: (b, i, h)),
            pl.BlockSpec((1, S, QK_PAD), lambda b, h, i: (b, 0, h)),
            pl.BlockSpec((1, 1, HEAD_DIM, S), lambda b, h, i: (b, h, 0, 0)),
        ],
        out_specs=pl.BlockSpec((1, tq, HEAD_DIM), lambda b, h, i: (b, i, h)),
        out_shape=jax.ShapeDtypeStruct((B, S, N_HEADS * HEAD_DIM), BF16),
        scratch_shapes=[
            pltpu.VMEM((QK_PAD, tq), BF16),
            pltpu.VMEM((2, tk, tq), F32),
            pltpu.VMEM((2, tk, tq), BF16),
            pltpu.VMEM((HEAD_DIM, tq), F32),
        ],
        compiler_params=pltpu.CompilerParams(
            dimension_semantics=("arbitrary", "arbitrary", "arbitrary"), vmem_limit_bytes=VMEM_LIMIT),
        name="mla_attn",
    )(q, k, vt)


def _mlp_kernel(x_ref, ret_ref, mla_ref, mod_ref, wo_ref, ln1w_ref, ln1b_ref, wup_ref, cw_ref, cb_ref,
                wdn_ref, ln2w_ref, ln2b_ref, o_ref, carry_ref, ubuf_ref, act_ref, *, tm):
    @pl.when(pl.program_id(1) == 0)
    def _():
        carry_ref[...] = jnp.zeros_like(carry_ref)

    gate1 = mod_ref[0, 2:3, :]
    shift2 = mod_ref[0, 3:4, :]
    scale2 = mod_ref[0, 4:5, :]
    gate2 = mod_ref[0, 5:6, :]

    y = _dot(ret_ref[0], wo_ref[0:RET_WIDTH, :]) + _dot(mla_ref[0], wo_ref[RET_WIDTH:2 * RET_WIDTH, :])
    x1 = _ln_plain(DN_ALPHA * x_ref[0] + (1.0 + gate1) * y) * ln1w_ref[...] + ln1b_ref[...]
    h2 = (_ln_plain(x1) * (1.0 + scale2) + shift2).astype(BF16)

    def conv(u, col0, kind):
        outs = []
        for g in range(FF_CHUNK // LANES):
            cols = slice(col0 + g * LANES, col0 + (g + 1) * LANES)
            ug = u[:, g * LANES:(g + 1) * LANES]
            buf = ubuf_ref.at[kind, g]
            buf[0:SUBLANES, :] = carry_ref[:, cols]
            buf[SUBLANES:SUBLANES + tm, :] = ug
            carry_ref[:, cols] = ug[tm - SUBLANES:tm, :]
            u1 = buf[SUBLANES - 1:SUBLANES - 1 + tm, :]
            u2 = buf[SUBLANES - 2:SUBLANES - 2 + tm, :]
            outs.append(cb_ref[:, cols] + cw_ref[2:3, cols] * ug + cw_ref[1:2, cols] * u1
                        + cw_ref[0:1, cols] * u2)
        return jnp.concatenate(outs, axis=1)

    for c in range(N_FF_CHUNKS):
        gcol0 = c * FF_CHUNK
        vcol0 = D_FF + c * FF_CHUNK
        g = conv(_dot(h2, wup_ref[:, gcol0:gcol0 + FF_CHUNK]), gcol0, 0)
        val = conv(_dot(h2, wup_ref[:, vcol0:vcol0 + FF_CHUNK]), vcol0, 1)
        act_ref[:, gcol0:gcol0 + FF_CHUNK] = (_silu(g) * val).astype(BF16)

    y2 = _dot(act_ref[...], wdn_ref[...])
    o_ref[0] = _ln_plain(DN_ALPHA * x1 + (1.0 + gate2) * y2) * ln2w_ref[...] + ln2b_ref[...]


def _mlp_call(x, ret, mla, mod, w_out, ln1w, ln1b, w_up, conv_w, conv_b, w_down, ln2w, ln2b):
    B, S, D = x.shape
    tm = ROW_TILE
    row = lambda w: pl.BlockSpec((1, tm, w), lambda b, s: (b, s, 0))
    return pl.pallas_call(
        functools.partial(_mlp_kernel, tm=tm),
        grid=(B, S // tm),
        in_specs=[
            row(D), row(RET_WIDTH), row(RET_WIDTH),
            pl.BlockSpec((1, N_MOD, D), lambda b, s: (b, 0, 0)),
            _const_spec(w_out.shape), _const_spec(ln1w.shape), _const_spec(ln1b.shape),
            _const_spec(w_up.shape), _const_spec(conv_w.shape), _const_spec(conv_b.shape),
            _const_spec(w_down.shape), _const_spec(ln2w.shape), _const_spec(ln2b.shape),
        ],
        out_specs=row(D),
        out_shape=jax.ShapeDtypeStruct((B, S, D), F32),
        scratch_shapes=[
            pltpu.VMEM((SUBLANES, 2 * D_FF), F32),
            pltpu.VMEM((2, FF_CHUNK // LANES, SUBLANES + tm, LANES), F32),
            pltpu.VMEM((tm, D_FF), BF16),
        ],
        compiler_params=pltpu.CompilerParams(
            dimension_semantics=("arbitrary", "arbitrary"), vmem_limit_bytes=VMEM_LIMIT),
        name="outproj_mlp",
    )(x, ret, mla, mod, w_out, ln1w, ln1b, w_up, conv_w, conv_b, w_down, ln2w, ln2b)


def _rope_tables(seq, half, reps):
    pos = np.arange(seq, dtype=np.float32)
    inv = np.float32(ROPE_BASE) ** (-np.arange(half, dtype=np.float32) / np.float32(half))
    ang = (pos[:, None] * inv[None, :]).astype(np.float32)
    cos = np.cos(ang).astype(np.float32)
    sin = np.sin(ang).astype(np.float32)
    cos_t = np.tile(cos, (1, 2 * reps))
    sin_t = np.concatenate([np.tile(-sin, (1, reps)), np.tile(sin, (1, reps))], axis=-1)
    return jnp.asarray(cos_t), jnp.asarray(sin_t)


def _uq_columns():
    half = MLA_ROPE_DIM // 2
    nope = [h * MLA_QK_DIM + np.arange(HEAD_DIM) for h in range(N_HEADS)]
    rope = []
    for p in range(N_HEADS // 2):
        for part in range(2):
            for h in (2 * p, 2 * p + 1):
                rope.append(h * MLA_QK_DIM + HEAD_DIM + part * half + np.arange(half))
    return np.concatenate(nope + rope)


def _ukv_columns():
    k = [h * 2 * HEAD_DIM + np.arange(HEAD_DIM) for h in range(N_HEADS)]
    v = [h * 2 * HEAD_DIM + HEAD_DIM + np.arange(HEAD_DIM) for h in range(N_HEADS)]
    return np.concatenate(k + v)


def _kr_columns(base):
    half = MLA_ROPE_DIM // 2
    x1 = base + np.arange(half)
    x2 = base + half + np.arange(half)
    return np.concatenate([x1, x1, x2, x2])


def kernel(x, c, w_ada, b_ada, w_in, ret_gn_w, mla_q_norm_w, w_uq, mla_kv_norm_w, w_ukv, w_out,
           ln1_w, ln1_b, w_up, conv_w, conv_b, w_down, ln2_w, ln2_b):
    B, S, D = x.shape
    assert D == D_MODEL and S % ROW_TILE == 0 and S % ATTN_TQ == 0 and w_ada.shape[0] == DEPTH == 1
    l = 0

    c_pad = jnp.pad(c, ((0, SUBLANES - B), (0, 0)))
    mod = _ada_call(c_pad, w_ada[l], b_ada[l][None, :])[:B].reshape(B, N_MOD, D)

    lat0 = 4 * RET_WIDTH
    w_in_b = w_in[l].astype(BF16)
    w_r = w_in_b[:, :lat0]
    w_l = jnp.concatenate(
        [w_in_b[:, lat0:lat0 + 2 * MLA_RANK], w_in_b[:, _kr_columns(lat0 + 2 * MLA_RANK)]], axis=1)
    w_uq_p = w_uq[l].astype(BF16)[:, _uq_columns()]
    w_ukv_p = w_ukv[l].astype(BF16)[:, _ukv_columns()]
    cosr, sinr = _rope_tables(S, HEAD_DIM // 2, 1)
    cosm, sinm = _rope_tables(S, MLA_ROPE_DIM // 2, 2)

    rq, rk, rv, rg, q, k, vt = _inproj_call(
        x, mod, w_r, w_l, mla_q_norm_w[l][None, :], mla_kv_norm_w[l][None, :], w_uq_p, w_ukv_p,
        cosr, sinr, cosm, sinm)
    ret = _retention_call(rq, rk, rv, rg, ret_gn_w[l][None, :])
    mla = _attn_call(q, k, vt)
    return _mlp_call(
        x, ret, mla, mod, w_out[l].astype(BF16), ln1_w[l][None, :], ln1_b[l][None, :],
        w_up[l].astype(BF16), conv_w[l], conv_b[l][None, :], w_down[l].astype(BF16),
        ln2_w[l][None, :], ln2_b[l][None, :])
```

```python
import functools
import math

import numpy as np
import jax
import jax.numpy as jnp
from jax import lax
from jax.experimental import pallas as pl
from jax.experimental.pallas import tpu as pltpu

F32 = jnp.float32
BF16 = jnp.bfloat16

D_MODEL = 1024
DEPTH = 1
N_HEADS = 4
HEAD_DIM = 128
RET_WIDTH = N_HEADS * HEAD_DIM
RET_CHUNK = 128
LAT_COL0 = 4 * RET_WIDTH
MLA_RANK = 256
MLA_ROPE_DIM = 64
MLA_QK_DIM = HEAD_DIM + MLA_ROPE_DIM
D_FF = 2816
ROPE_BASE = 10000.0
LN_EPS = 1e-5
RMS_EPS = 1e-6
DN_ALPHA = (2.0 * DEPTH) ** 0.25
N_MOD = 6
LOG_G = [math.log1p(-(2.0 ** (-5.0 - h))) for h in range(N_HEADS)]

LANES = 128
SUBLANES = 8
MXU_DIM = 256
VMEM_LIMIT = 56 * 1024 * 1024

ROW_TILE = 512
ATTN_TQ = 1024
ATTN_TK = 512
FF_CHUNK = MXU_DIM
N_FF_CHUNKS = D_FF // FF_CHUNK
QK_PAD = 2 * LANES
BF16_ROWS = 2 * SUBLANES
ACC_ROWS = HEAD_DIM + BF16_ROWS
NEG_BIG = -1e30


def _ln_plain(x):
    mu = jnp.mean(x, axis=-1, keepdims=True)
    xc = x - mu
    var = jnp.mean(xc * xc, axis=-1, keepdims=True)
    return xc * lax.rsqrt(var + LN_EPS)


def _rms(x):
    return x * lax.rsqrt(jnp.mean(x * x, axis=-1, keepdims=True) + RMS_EPS)


def _silu(x):
    return x * jax.nn.sigmoid(x)


def _dot(a, b):
    return jnp.dot(a, b, preferred_element_type=F32)


def _dot_nt(a, b):
    return lax.dot_general(a, b, (((1,), (1,)), ((), ())), preferred_element_type=F32)


def _dot_tn(a, b):
    return lax.dot_general(a, b, (((0,), (0,)), ((), ())), preferred_element_type=F32)


def _ada_kernel(c_ref, w_ref, b_ref, o_ref):
    cond = _silu(c_ref[...])
    o_ref[...] = _dot(cond.astype(BF16), w_ref[...].astype(BF16)) + b_ref[...]


def _ada_call(c_pad, w_ada, b_ada):
    n_out = w_ada.shape[1]
    return pl.pallas_call(
        _ada_kernel,
        grid=(n_out // D_MODEL,),
        in_specs=[
            pl.BlockSpec((SUBLANES, D_MODEL), lambda j: (0, 0)),
            pl.BlockSpec((D_MODEL, D_MODEL), lambda j: (0, j)),
            pl.BlockSpec((1, D_MODEL), lambda j: (0, j)),
        ],
        out_specs=pl.BlockSpec((SUBLANES, D_MODEL), lambda j: (0, j)),
        out_shape=jax.ShapeDtypeStruct((SUBLANES, n_out), F32),
        compiler_params=pltpu.CompilerParams(dimension_semantics=("arbitrary",)),
        name="ada_mod",
    )(c_pad, w_ada, b_ada)


def _rope128(x, cos, sin_signed):
    return x * cos + pltpu.roll(x, LANES // 2, axis=1) * sin_signed


def _retention_head(hd, q, k, v, gate, gnw, state_ref, ret_ref):
    C = RET_CHUNK
    lg = LOG_G[hd]
    row = lax.broadcasted_iota(jnp.int32, (C, C), 0)
    col = lax.broadcasted_iota(jnp.int32, (C, C), 1)
    rel = (row - col).astype(F32)
    n = lax.broadcasted_iota(jnp.int32, (C, 1), 0).astype(F32)
    decay = jnp.where(rel >= 0.0, jnp.exp(lg * jnp.maximum(rel, 0.0)), 0.0)
    q_w = jnp.exp(lg * (n + 1.0))
    k_w = jnp.exp(lg * (C - 1.0 - n))
    chunk_decay = math.exp(lg * C)
    hs = slice(hd * HEAD_DIM, (hd + 1) * HEAD_DIM)
    for c in range(q.shape[0] // C):
        rs = slice(c * C, (c + 1) * C)
        qc = q[rs].astype(BF16)
        vc = v[rs].astype(BF16)
        scores = _dot_nt(qc, k[rs].astype(BF16)) * decay
        inner = _dot(scores.astype(BF16), vc)
        state = state_ref[hd]
        cross = _dot(qc, state.astype(BF16)) * q_w
        state_ref[hd] = state * chunk_decay + _dot_tn((k[rs] * k_w).astype(BF16), vc)
        o = _ln_plain(inner + cross) * gnw
        ret_ref[0, rs, hs] = (o * gate[rs]).astype(BF16)


def _inproj_kernel(x_ref, mod_ref, win_ref, gnw_ref, qnw_ref, kvnw_ref, wuq_ref, wukv_ref,
                   cosr_ref, sinr_ref, cosm_ref, sinm_ref,
                   ret_ref, q_ref, k_ref, vt_ref, state_ref, *, q_scale, rk_scale):
    @pl.when(pl.program_id(1) == 0)
    def _():
        state_ref[...] = jnp.zeros_like(state_ref)

    x = x_ref[0]
    shift = mod_ref[0, 0:1, :]
    scale = mod_ref[0, 1:2, :]
    h = (_ln_plain(x) * (1.0 + scale) + shift).astype(BF16)

    cosr = cosr_ref[...]
    sinr = sinr_ref[...]
    cosm = cosm_ref[...]
    sinm = sinm_ref[...]

    pq = _dot(h, win_ref[:, 0:RET_WIDTH])
    pk = _dot(h, win_ref[:, RET_WIDTH:2 * RET_WIDTH])
    pv = _dot(h, win_ref[:, 2 * RET_WIDTH:3 * RET_WIDTH])
    pg = _silu(_dot(h, win_ref[:, 3 * RET_WIDTH:LAT_COL0]))
    for hd in range(N_HEADS):
        sl = slice(hd * HEAD_DIM, (hd + 1) * HEAD_DIM)
        _retention_head(hd, _rope128(pq[:, sl], cosr, sinr), _rope128(pk[:, sl], cosr, sinr) * rk_scale,
                        pv[:, sl], pg[:, sl], gnw_ref[:, sl], state_ref, ret_ref)

    lat = _dot(h, win_ref[:, LAT_COL0:])
    cq = (_rms(lat[:, 0:MLA_RANK]) * qnw_ref[...]).astype(BF16)
    ckv = (_rms(lat[:, MLA_RANK:2 * MLA_RANK]) * kvnw_ref[...]).astype(BF16)
    kr = _rope128(lat[:, 2 * MLA_RANK:2 * MLA_RANK + LANES], cosm, sinm)
    lane = lax.broadcasted_iota(jnp.int32, kr.shape, 1)
    first_of_pair = (lane % (LANES // 2)) < (LANES // 4)
    kr_even = jnp.where(first_of_pair, kr, 0.0).astype(BF16)
    kr_odd = jnp.where(first_of_pair, 0.0, kr).astype(BF16)

    qf = _dot(cq, wuq_ref[...])
    kvf = _dot(ckv, wukv_ref[...])
    for hd in range(N_HEADS):
        vt_ref[0, hd] = kvf[:, RET_WIDTH + hd * HEAD_DIM:RET_WIDTH + (hd + 1) * HEAD_DIM].T.astype(BF16)
    for p in range(N_HEADS // 2):
        qr = qf[:, RET_WIDTH + p * LANES:RET_WIDTH + (p + 1) * LANES]
        qr = (_rope128(qr, cosm, sinm) * q_scale).astype(BF16)
        for hd in (2 * p, 2 * p + 1):
            sl = slice(hd * HEAD_DIM, (hd + 1) * HEAD_DIM)
            q_ref[0, :, hd * QK_PAD:hd * QK_PAD + LANES] = (qf[:, sl] * q_scale).astype(BF16)
            q_ref[0, :, hd * QK_PAD + LANES:(hd + 1) * QK_PAD] = qr
            k_ref[0, :, hd * QK_PAD:hd * QK_PAD + LANES] = kvf[:, sl].astype(BF16)
            k_ref[0, :, hd * QK_PAD + LANES:(hd + 1) * QK_PAD] = kr_even if hd % 2 == 0 else kr_odd


def _const_spec(shape):
    nd = len(shape)
    return pl.BlockSpec(shape, lambda *_: (0,) * nd, pipeline_mode=pl.Buffered(1))


def _inproj_call(x, mod, w_in, gnw, qnw, kvnw, w_uq, w_ukv, cosr, sinr, cosm, sinm):
    B, S, D = x.shape
    tm = ROW_TILE
    q_scale = (MLA_QK_DIM ** -0.5) * math.log2(math.e)
    rk_scale = HEAD_DIM ** -0.5
    row = lambda w: pl.BlockSpec((1, tm, w), lambda b, s: (b, s, 0))
    tab = pl.BlockSpec((tm, LANES), lambda b, s: (s, 0))
    out_shapes = (
        [jax.ShapeDtypeStruct((B, S, RET_WIDTH), BF16)]
        + [jax.ShapeDtypeStruct((B, S, N_HEADS * QK_PAD), BF16)] * 2
        + [jax.ShapeDtypeStruct((B, N_HEADS, HEAD_DIM, S), BF16)]
    )
    vt_spec = pl.BlockSpec((1, N_HEADS, HEAD_DIM, tm), lambda b, s: (b, 0, 0, s))
    return pl.pallas_call(
        functools.partial(_inproj_kernel, q_scale=q_scale, rk_scale=rk_scale),
        grid=(B, S // tm),
        in_specs=[
            row(D),
            pl.BlockSpec((1, N_MOD, D), lambda b, s: (b, 0, 0)),
            _const_spec(w_in.shape), _const_spec(gnw.shape),
            _const_spec(qnw.shape), _const_spec(kvnw.shape),
            _const_spec(w_uq.shape), _const_spec(w_ukv.shape),
            tab, tab, tab, tab,
        ],
        out_specs=[row(RET_WIDTH)] + [row(N_HEADS * QK_PAD)] * 2 + [vt_spec],
        out_shape=out_shapes,
        scratch_shapes=[pltpu.VMEM((N_HEADS, HEAD_DIM, HEAD_DIM), F32)],
        compiler_params=pltpu.CompilerParams(
            dimension_semantics=("arbitrary", "arbitrary"), vmem_limit_bytes=VMEM_LIMIT),
        name="inproj_retention",
    )(x, mod, w_in, gnw, qnw, kvnw, w_uq, w_ukv, cosr, sinr, cosm, sinm)


def _attn_kernel(q_ref, k_ref, vt_ref, o_ref, qt_ref, s_buf, p_buf, acc_ref, *, tq, tk):
    qi = pl.program_id(2)
    qt_ref[...] = q_ref[0].T

    ones = jnp.ones((BF16_ROWS, tk), BF16)

    def k_blk(j):
        return k_ref[0, pl.ds(pl.multiple_of(j * tk, tk), tk), :]

    def v_blk(j):
        return jnp.concatenate([vt_ref[0, 0, :, pl.ds(pl.multiple_of(j * tk, tk), tk)], ones], axis=0)

    def scores(j, slot):
        s_buf[slot] = _dot(k_blk(j), qt_ref[...])

    def value_update(j, slot, alpha):
        acc_ref[...] = alpha * acc_ref[...] + _dot(v_blk(j), p_buf[slot])

    def softmax(s, m):
        m_new = jnp.maximum(m, jnp.max(s, axis=0, keepdims=True))
        return m_new, jnp.exp2(m - m_new), jnp.exp2(s - m_new).astype(BF16)

    def two_blocks(u, carry):
        alpha_prev, m = carry
        t = 2 * u
        scores(t + 1, 1)
        value_update(jnp.maximum(t - 1, 0), 1, alpha_prev)
        m, alpha_even, p_buf[0] = softmax(s_buf[0], m)
        scores(t + 2, 0)
        value_update(t, 0, alpha_even)
        m, alpha_odd, p_buf[1] = softmax(s_buf[1], m)
        return alpha_odd, m

    scores(0, 0)
    p_buf[1] = jnp.zeros((tk, tq), BF16)
    acc_ref[...] = jnp.zeros_like(acc_ref)
    init = (jnp.ones((1, tq), F32), jnp.full((1, tq), NEG_BIG, F32))
    alpha_prev, m = lax.fori_loop(0, qi, two_blocks, init)

    t = 2 * qi
    key = lax.broadcasted_iota(jnp.int32, (tk, tk), 0)
    qry = lax.broadcasted_iota(jnp.int32, (tk, tk), 1)
    causal = key <= qry
    s_right = _dot(k_blk(t + 1), qt_ref[:, tk:])
    value_update(jnp.maximum(t - 1, 0), 1, alpha_prev)
    s = s_buf[0]
    s = jnp.concatenate([jnp.where(causal, s[:, :tk], NEG_BIG), s[:, tk:]], axis=1)
    m, alpha, p_buf[0] = softmax(s, m)
    value_update(t, 0, alpha)
    _, alpha_r, p_r = softmax(jnp.where(causal, s_right, NEG_BIG), m[:, tk:])
    acc_ref[:, tk:] = alpha_r * acc_ref[:, tk:] + _dot(v_blk(t + 1), p_r)
    acc = acc_ref[...]
    o_ref[0] = (acc[:HEAD_DIM] / acc[HEAD_DIM:HEAD_DIM + 1]).T.astype(BF16)


def _attn_call(q, k, vt):
    B, S, _ = q.shape
    tq, tk = ATTN_TQ, ATTN_TK
    assert tq == 2 * tk
    return pl.pallas_call(
        functools.partial(_attn_kernel, tq=tq, tk=tk),
        grid=(B, N_HEADS, S // tq),
        in_specs=[
            pl.BlockSpec((1, tq, QK_PAD), lambda b, h, i: (b, i, h)),
            pl.BlockSpec((1, S, QK_PAD), lambda b, h, i: (b, 0, h)),
            pl.BlockSpec((1, 1, HEAD_DIM, S), lambda b, h, i: (b, h, 0, 0)),
        ],
        out_specs=pl.BlockSpec((1, tq, HEAD_DIM), lambda b, h, i: (b, i, h)),
        out_shape=jax.ShapeDtypeStruct((B, S, N_HEADS * HEAD_DIM), BF16),
        scratch_shapes=[
            pltpu.VMEM((QK_PAD, tq), BF16),
            pltpu.VMEM((2, tk, tq), F32),
            pltpu.VMEM((2, tk, tq), BF16),
            pltpu.VMEM((ACC_ROWS, tq), F32),
        ],
        compiler_params=pltpu.CompilerParams(
            dimension_semantics=("arbitrary", "arbitrary", "arbitrary"), vmem_limit_bytes=VMEM_LIMIT),
        name="mla_attn",
    )(q, k, vt)


def _mlp_kernel(x_ref, ret_ref, mla_ref, mod_ref, wo_ref, ln1w_ref, ln1b_ref, wup_ref, cw_ref, cb_ref,
                wdn_ref, ln2w_ref, ln2b_ref, o_ref, carry_ref, ubuf_ref, act_ref, *, tm):
    @pl.when(pl.program_id(1) == 0)
    def _():
        carry_ref[...] = jnp.zeros_like(carry_ref)

    gate1 = mod_ref[0, 2:3, :]
    shift2 = mod_ref[0, 3:4, :]
    scale2 = mod_ref[0, 4:5, :]
    gate2 = mod_ref[0, 5:6, :]

    y = _dot(ret_ref[0], wo_ref[0:RET_WIDTH, :]) + _dot(mla_ref[0], wo_ref[RET_WIDTH:2 * RET_WIDTH, :])
    x1 = _ln_plain(DN_ALPHA * x_ref[0] + (1.0 + gate1) * y) * ln1w_ref[...] + ln1b_ref[...]
    h2 = (_ln_plain(x1) * (1.0 + scale2) + shift2).astype(BF16)

    def conv(u, col0, kind):
        outs = []
        for g in range(FF_CHUNK // LANES):
            cols = slice(col0 + g * LANES, col0 + (g + 1) * LANES)
            ug = u[:, g * LANES:(g + 1) * LANES]
            buf = ubuf_ref.at[kind, g]
            buf[0:SUBLANES, :] = carry_ref[:, cols]
            buf[SUBLANES:SUBLANES + tm, :] = ug
            carry_ref[:, cols] = ug[tm - SUBLANES:tm, :]
            u1 = buf[SUBLANES - 1:SUBLANES - 1 + tm, :]
            u2 = buf[SUBLANES - 2:SUBLANES - 2 + tm, :]
            outs.append(cb_ref[:, cols] + cw_ref[2:3, cols] * ug + cw_ref[1:2, cols] * u1
                        + cw_ref[0:1, cols] * u2)
        return jnp.concatenate(outs, axis=1)

    for c in range(N_FF_CHUNKS):
        gcol0 = c * FF_CHUNK
        vcol0 = D_FF + c * FF_CHUNK
        g = conv(_dot(h2, wup_ref[:, gcol0:gcol0 + FF_CHUNK]), gcol0, 0)
        val = conv(_dot(h2, wup_ref[:, vcol0:vcol0 + FF_CHUNK]), vcol0, 1)
        act_ref[:, gcol0:gcol0 + FF_CHUNK] = (_silu(g) * val).astype(BF16)

    y2 = _dot(act_ref[...], wdn_ref[...])
    o_ref[0] = _ln_plain(DN_ALPHA * x1 + (1.0 + gate2) * y2) * ln2w_ref[...] + ln2b_ref[...]


def _mlp_call(x, ret, mla, mod, w_out, ln1w, ln1b, w_up, conv_w, conv_b, w_down, ln2w, ln2b):
    B, S, D = x.shape
    tm = ROW_TILE
    row = lambda w: pl.BlockSpec((1, tm, w), lambda b, s: (b, s, 0))
    return pl.pallas_call(
        functools.partial(_mlp_kernel, tm=tm),
        grid=(B, S // tm),
        in_specs=[
            row(D), row(RET_WIDTH), row(RET_WIDTH),
            pl.BlockSpec((1, N_MOD, D), lambda b, s: (b, 0, 0)),
            _const_spec(w_out.shape), _const_spec(ln1w.shape), _const_spec(ln1b.shape),
            _const_spec(w_up.shape), _const_spec(conv_w.shape), _const_spec(conv_b.shape),
            _const_spec(w_down.shape), _const_spec(ln2w.shape), _const_spec(ln2b.shape),
        ],
        out_specs=row(D),
        out_shape=jax.ShapeDtypeStruct((B, S, D), F32),
        scratch_shapes=[
            pltpu.VMEM((SUBLANES, 2 * D_FF), F32),
            pltpu.VMEM((2, FF_CHUNK // LANES, SUBLANES + tm, LANES), F32),
            pltpu.VMEM((tm, D_FF), BF16),
        ],
        compiler_params=pltpu.CompilerParams(
            dimension_semantics=("arbitrary", "arbitrary"), vmem_limit_bytes=VMEM_LIMIT),
        name="outproj_mlp",
    )(x, ret, mla, mod, w_out, ln1w, ln1b, w_up, conv_w, conv_b, w_down, ln2w, ln2b)


def _rope_tables(seq, half, reps):
    pos = np.arange(seq, dtype=np.float32)
    inv = np.float32(ROPE_BASE) ** (-np.arange(half, dtype=np.float32) / np.float32(half))
    ang = (pos[:, None] * inv[None, :]).astype(np.float32)
    cos = np.cos(ang).astype(np.float32)
    sin = np.sin(ang).astype(np.float32)
    cos_t = np.tile(cos, (1, 2 * reps))
    sin_t = np.concatenate([np.tile(-sin, (1, reps)), np.tile(sin, (1, reps))], axis=-1)
    return jnp.asarray(cos_t), jnp.asarray(sin_t)


def _uq_columns():
    half = MLA_ROPE_DIM // 2
    nope = [h * MLA_QK_DIM + np.arange(HEAD_DIM) for h in range(N_HEADS)]
    rope = []
    for p in range(N_HEADS // 2):
        for part in range(2):
            for h in (2 * p, 2 * p + 1):
                rope.append(h * MLA_QK_DIM + HEAD_DIM + part * half + np.arange(half))
    return np.concatenate(nope + rope)


def _ukv_columns():
    k = [h * 2 * HEAD_DIM + np.arange(HEAD_DIM) for h in range(N_HEADS)]
    v = [h * 2 * HEAD_DIM + HEAD_DIM + np.arange(HEAD_DIM) for h in range(N_HEADS)]
    return np.concatenate(k + v)


def _in_columns():
    half = MLA_ROPE_DIM // 2
    base = LAT_COL0 + 2 * MLA_RANK
    x1 = base + np.arange(half)
    x2 = base + half + np.arange(half)
    return np.concatenate([np.arange(base), x1, x1, x2, x2])


def kernel(x, c, w_ada, b_ada, w_in, ret_gn_w, mla_q_norm_w, w_uq, mla_kv_norm_w, w_ukv, w_out,
           ln1_w, ln1_b, w_up, conv_w, conv_b, w_down, ln2_w, ln2_b):
    B, S, D = x.shape
    assert D == D_MODEL and S % ROW_TILE == 0 and S % ATTN_TQ == 0 and w_ada.shape[0] == DEPTH == 1
    l = 0

    c_pad = jnp.pad(c, ((0, SUBLANES - B), (0, 0)))
    mod = _ada_call(c_pad, w_ada[l], b_ada[l][None, :])[:B].reshape(B, N_MOD, D)

    w_in_p = w_in[l][:, _in_columns()].astype(BF16)
    w_uq_p = w_uq[l][:, _uq_columns()].astype(BF16)
    w_ukv_p = w_ukv[l][:, _ukv_columns()].astype(BF16)
    cosr, sinr = _rope_tables(S, HEAD_DIM // 2, 1)
    cosm, sinm = _rope_tables(S, MLA_ROPE_DIM // 2, 2)

    ret, q, k, vt = _inproj_call(
        x, mod, w_in_p, ret_gn_w[l][None, :], mla_q_norm_w[l][None, :], mla_kv_norm_w[l][None, :],
        w_uq_p, w_ukv_p, cosr, sinr, cosm, sinm)
    mla = _attn_call(q, k, vt)
    return _mlp_call(
        x, ret, mla, mod, w_out[l].astype(BF16), ln1_w[l][None, :], ln1_b[l][None, :],
        w_up[l].astype(BF16), conv_w[l], conv_b[l][None, :], w_down[l].astype(BF16),
        ln2_w[l][None, :], ln2_b[l][None, :])
```

```python
import functools
import math

import numpy as np
import jax
import jax.numpy as jnp
from jax import lax
from jax.experimental import pallas as pl
from jax.experimental.pallas import tpu as pltpu

F32 = jnp.float32
BF16 = jnp.bfloat16

D_MODEL = 1024
DEPTH = 1
N_HEADS = 4
HEAD_DIM = 128
RET_WIDTH = N_HEADS * HEAD_DIM
LAT_COL0 = 4 * RET_WIDTH
MLA_RANK = 256
MLA_ROPE_DIM = 64
MLA_QK_DIM = HEAD_DIM + MLA_ROPE_DIM
D_FF = 2816
ROPE_BASE = 10000.0
LN_EPS = 1e-5
RMS_EPS = 1e-6
DN_ALPHA = (2.0 * DEPTH) ** 0.25
N_MOD = 6
LOG_G = [math.log1p(-(2.0 ** (-5.0 - h))) for h in range(N_HEADS)]

LANES = 128
SUBLANES = 8
MXU_DIM = 256
VMEM_LIMIT = 56 * 1024 * 1024

ROW_TILE = 512
ATTN_TQ = 1024
ATTN_TK = 512
FF_CHUNK = MXU_DIM
N_FF_CHUNKS = D_FF // FF_CHUNK
QK_PAD = 2 * LANES
BF16_ROWS = 2 * SUBLANES
ACC_ROWS = HEAD_DIM + BF16_ROWS
NEG_BIG = -1e30


def _ln_plain(x):
    mu = jnp.mean(x, axis=-1, keepdims=True)
    xc = x - mu
    var = jnp.mean(xc * xc, axis=-1, keepdims=True)
    return xc * lax.rsqrt(var + LN_EPS)


def _rms(x):
    return x * lax.rsqrt(jnp.mean(x * x, axis=-1, keepdims=True) + RMS_EPS)


def _silu(x):
    return x * jax.nn.sigmoid(x)


def _dot(a, b):
    return jnp.dot(a, b, preferred_element_type=F32)


def _dot_nt(a, b):
    return lax.dot_general(a, b, (((1,), (1,)), ((), ())), preferred_element_type=F32)


def _dot_tn(a, b):
    return lax.dot_general(a, b, (((0,), (0,)), ((), ())), preferred_element_type=F32)


def _ada_kernel(c_ref, w_ref, b_ref, o_ref):
    cond = _silu(c_ref[...])
    o_ref[...] = _dot(cond.astype(BF16), w_ref[...].astype(BF16)) + b_ref[...]


def _ada_call(c_pad, w_ada, b_ada):
    n_out = w_ada.shape[1]
    return pl.pallas_call(
        _ada_kernel,
        grid=(n_out // D_MODEL,),
        in_specs=[
            pl.BlockSpec((SUBLANES, D_MODEL), lambda j: (0, 0)),
            pl.BlockSpec((D_MODEL, D_MODEL), lambda j: (0, j)),
            pl.BlockSpec((1, D_MODEL), lambda j: (0, j)),
        ],
        out_specs=pl.BlockSpec((SUBLANES, D_MODEL), lambda j: (0, j)),
        out_shape=jax.ShapeDtypeStruct((SUBLANES, n_out), F32),
        compiler_params=pltpu.CompilerParams(dimension_semantics=("arbitrary",)),
        name="ada_mod",
    )(c_pad, w_ada, b_ada)


def _rope128(x, cos, sin_signed):
    return x * cos + pltpu.roll(x, LANES // 2, axis=1) * sin_signed


def _decay_matrix(hd, c):
    row = lax.broadcasted_iota(jnp.int32, (c, c), 0)
    col = lax.broadcasted_iota(jnp.int32, (c, c), 1)
    rel = (row - col).astype(F32)
    return jnp.where(rel >= 0.0, jnp.exp(LOG_G[hd] * jnp.maximum(rel, 0.0)), 0.0)


def _retention_head(hd, q, k, v, gate, gnw, decay_ref, state_ref, ret_ref):
    c = q.shape[0]
    lg = LOG_G[hd]
    n = lax.broadcasted_iota(jnp.int32, (c, 1), 0).astype(F32)
    q_w = jnp.exp(lg * (n + 1.0))
    k_w = jnp.exp(lg * (c - 1.0 - n))
    qb = q.astype(BF16)
    vb = v.astype(BF16)
    scores = _dot_nt(qb, k.astype(BF16)) * decay_ref[hd]
    inner = _dot(scores.astype(BF16), vb)
    state = state_ref[hd]
    cross = _dot(qb, state.astype(BF16)) * q_w
    state_ref[hd] = state * math.exp(lg * c) + _dot_tn((k * k_w).astype(BF16), vb)
    o = _ln_plain(inner + cross) * gnw
    ret_ref[0, :, hd * HEAD_DIM:(hd + 1) * HEAD_DIM] = (o * gate).astype(BF16)


def _inproj_kernel(x_ref, mod_ref, win_ref, gnw_ref, qnw_ref, kvnw_ref, wuq_ref, wukv_ref,
                   cosr_ref, sinr_ref, cosm_ref, sinm_ref,
                   ret_ref, q_ref, k_ref, vt_ref, state_ref, decay_ref, *, q_scale, rk_scale):
    @pl.when(jnp.logical_and(pl.program_id(0) == 0, pl.program_id(1) == 0))
    def _():
        for hd in range(N_HEADS):
            decay_ref[hd] = _decay_matrix(hd, decay_ref.shape[1])

    @pl.when(pl.program_id(1) == 0)
    def _():
        state_ref[...] = jnp.zeros_like(state_ref)

    x = x_ref[0]
    shift = mod_ref[0, 0:1, :]
    scale = mod_ref[0, 1:2, :]
    h = (_ln_plain(x) * (1.0 + scale) + shift).astype(BF16)

    cosr = cosr_ref[...]
    sinr = sinr_ref[...]
    cosm = cosm_ref[...]
    sinm = sinm_ref[...]

    pq = _dot(h, win_ref[:, 0:RET_WIDTH])
    pk = _dot(h, win_ref[:, RET_WIDTH:2 * RET_WIDTH])
    pv = _dot(h, win_ref[:, 2 * RET_WIDTH:3 * RET_WIDTH])
    pg = _silu(_dot(h, win_ref[:, 3 * RET_WIDTH:LAT_COL0]))
    for hd in range(N_HEADS):
        sl = slice(hd * HEAD_DIM, (hd + 1) * HEAD_DIM)
        _retention_head(hd, _rope128(pq[:, sl], cosr, sinr), _rope128(pk[:, sl], cosr, sinr) * rk_scale,
                        pv[:, sl], pg[:, sl], gnw_ref[:, sl], decay_ref, state_ref, ret_ref)

    lat = _dot(h, win_ref[:, LAT_COL0:])
    cq = (_rms(lat[:, 0:MLA_RANK]) * qnw_ref[...]).astype(BF16)
    ckv = (_rms(lat[:, MLA_RANK:2 * MLA_RANK]) * kvnw_ref[...]).astype(BF16)
    kr = _rope128(lat[:, 2 * MLA_RANK:2 * MLA_RANK + LANES], cosm, sinm)
    lane = lax.broadcasted_iota(jnp.int32, kr.shape, 1)
    first_of_pair = (lane % (LANES // 2)) < (LANES // 4)
    kr_even = jnp.where(first_of_pair, kr, 0.0).astype(BF16)
    kr_odd = jnp.where(first_of_pair, 0.0, kr).astype(BF16)

    qf = _dot(cq, wuq_ref[...])
    kvf = _dot(ckv, wukv_ref[...])
    for hd in range(N_HEADS):
        vt_ref[0, hd] = kvf[:, RET_WIDTH + hd * HEAD_DIM:RET_WIDTH + (hd + 1) * HEAD_DIM].T.astype(BF16)
    for p in range(N_HEADS // 2):
        qr = qf[:, RET_WIDTH + p * LANES:RET_WIDTH + (p + 1) * LANES]
        qr = (_rope128(qr, cosm, sinm) * q_scale).astype(BF16)
        for hd in (2 * p, 2 * p + 1):
            sl = slice(hd * HEAD_DIM, (hd + 1) * HEAD_DIM)
            q_ref[0, :, hd * QK_PAD:hd * QK_PAD + LANES] = (qf[:, sl] * q_scale).astype(BF16)
            q_ref[0, :, hd * QK_PAD + LANES:(hd + 1) * QK_PAD] = qr
            k_ref[0, :, hd * QK_PAD:hd * QK_PAD + LANES] = kvf[:, sl].astype(BF16)
            k_ref[0, :, hd * QK_PAD + LANES:(hd + 1) * QK_PAD] = kr_even if hd % 2 == 0 else kr_odd


def _const_spec(shape):
    nd = len(shape)
    return pl.BlockSpec(shape, lambda *_: (0,) * nd, pipeline_mode=pl.Buffered(1))


def _inproj_call(x, mod, w_in, gnw, qnw, kvnw, w_uq, w_ukv, cosr, sinr, cosm, sinm):
    B, S, D = x.shape
    tm = ROW_TILE
    q_scale = (MLA_QK_DIM ** -0.5) * math.log2(math.e)
    rk_scale = HEAD_DIM ** -0.5
    row = lambda w: pl.BlockSpec((1, tm, w), lambda b, s: (b, s, 0))
    tab = pl.BlockSpec((tm, LANES), lambda b, s: (s, 0))
    out_shapes = (
        [jax.ShapeDtypeStruct((B, S, RET_WIDTH), BF16)]
        + [jax.ShapeDtypeStruct((B, S, N_HEADS * QK_PAD), BF16)] * 2
        + [jax.ShapeDtypeStruct((B, N_HEADS, HEAD_DIM, S), BF16)]
    )
    vt_spec = pl.BlockSpec((1, N_HEADS, HEAD_DIM, tm), lambda b, s: (b, 0, 0, s))
    return pl.pallas_call(
        functools.partial(_inproj_kernel, q_scale=q_scale, rk_scale=rk_scale),
        grid=(B, S // tm),
        in_specs=[
            row(D),
            pl.BlockSpec((1, N_MOD, D), lambda b, s: (b, 0, 0)),
            _const_spec(w_in.shape), _const_spec(gnw.shape),
            _const_spec(qnw.shape), _const_spec(kvnw.shape),
            _const_spec(w_uq.shape), _const_spec(w_ukv.shape),
            tab, tab, tab, tab,
        ],
        out_specs=[row(RET_WIDTH)] + [row(N_HEADS * QK_PAD)] * 2 + [vt_spec],
        out_shape=out_shapes,
        scratch_shapes=[
            pltpu.VMEM((N_HEADS, HEAD_DIM, HEAD_DIM), F32),
            pltpu.VMEM((N_HEADS, tm, tm), F32),
        ],
        compiler_params=pltpu.CompilerParams(
            dimension_semantics=("arbitrary", "arbitrary"), vmem_limit_bytes=VMEM_LIMIT),
        name="inproj_retention",
    )(x, mod, w_in, gnw, qnw, kvnw, w_uq, w_ukv, cosr, sinr, cosm, sinm)


def _attn_kernel(q_ref, k_ref, vt_ref, o_ref, qt_ref, s_buf, p_buf, acc_ref, *, tq, tk):
    qi = pl.program_id(2)
    qt_ref[...] = q_ref[0].T

    ones = jnp.ones((BF16_ROWS, tk), BF16)

    def k_blk(j):
        return k_ref[0, pl.ds(pl.multiple_of(j * tk, tk), tk), :]

    def v_blk(j):
        return jnp.concatenate([vt_ref[0, 0, :, pl.ds(pl.multiple_of(j * tk, tk), tk)], ones], axis=0)

    def scores(j, slot):
        s_buf[slot] = _dot(k_blk(j), qt_ref[...])

    def value_update(j, slot, alpha):
        acc_ref[...] = alpha * acc_ref[...] + _dot(v_blk(j), p_buf[slot])

    def softmax(s, m):
        m_new = jnp.maximum(m, jnp.max(s, axis=0, keepdims=True))
        return m_new, jnp.exp2(m - m_new), jnp.exp2(s - m_new).astype(BF16)

    def two_blocks(u, carry):
        alpha_prev, m = carry
        t = 2 * u
        scores(t + 1, 1)
        value_update(jnp.maximum(t - 1, 0), 1, alpha_prev)
        m, alpha_even, p_buf[0] = softmax(s_buf[0], m)
        scores(t + 2, 0)
        value_update(t, 0, alpha_even)
        m, alpha_odd, p_buf[1] = softmax(s_buf[1], m)
        return alpha_odd, m

    scores(0, 0)
    p_buf[1] = jnp.zeros((tk, tq), BF16)
    acc_ref[...] = jnp.zeros_like(acc_ref)
    init = (jnp.ones((1, tq), F32), jnp.full((1, tq), NEG_BIG, F32))
    alpha_prev, m = lax.fori_loop(0, qi, two_blocks, init)

    t = 2 * qi
    key = lax.broadcasted_iota(jnp.int32, (tk, tk), 0)
    qry = lax.broadcasted_iota(jnp.int32, (tk, tk), 1)
    causal = key <= qry
    s_right = _dot(k_blk(t + 1), qt_ref[:, tk:])
    value_update(jnp.maximum(t - 1, 0), 1, alpha_prev)
    s = s_buf[0]
    s = jnp.concatenate([jnp.where(causal, s[:, :tk], NEG_BIG), s[:, tk:]], axis=1)
    m, alpha, p_buf[0] = softmax(s, m)
    value_update(t, 0, alpha)
    _, alpha_r, p_r = softmax(jnp.where(causal, s_right, NEG_BIG), m[:, tk:])
    acc_ref[:, tk:] = alpha_r * acc_ref[:, tk:] + _dot(v_blk(t + 1), p_r)
    acc = acc_ref[...]
    o_ref[0] = (acc[:HEAD_DIM] / acc[HEAD_DIM:HEAD_DIM + 1]).T.astype(BF16)


def _attn_call(q, k, vt):
    B, S, _ = q.shape
    tq, tk = ATTN_TQ, ATTN_TK
    assert tq == 2 * tk
    return pl.pallas_call(
        functools.partial(_attn_kernel, tq=tq, tk=tk),
        grid=(B, N_HEADS, S // tq),
        in_specs=[
            pl.BlockSpec((1, tq, QK_PAD), lambda b, h, i: (b, i, h)),
            pl.BlockSpec((1, S, QK_PAD), lambda b, h, i: (b, 0, h)),
            pl.BlockSpec((1, 1, HEAD_DIM, S), lambda b, h, i: (b, h, 0, 0)),
        ],
        out_specs=pl.BlockSpec((1, tq, HEAD_DIM), lambda b, h, i: (b, i, h)),
        out_shape=jax.ShapeDtypeStruct((B, S, N_HEADS * HEAD_DIM), BF16),
        scratch_shapes=[
            pltpu.VMEM((QK_PAD, tq), BF16),
            pltpu.VMEM((2, tk, tq), F32),
            pltpu.VMEM((2, tk, tq), BF16),
            pltpu.VMEM((ACC_ROWS, tq), F32),
        ],
        compiler_params=pltpu.CompilerParams(
            dimension_semantics=("arbitrary", "arbitrary", "arbitrary"), vmem_limit_bytes=VMEM_LIMIT),
        name="mla_attn",
    )(q, k, vt)


def _mlp_kernel(x_ref, ret_ref, mla_ref, mod_ref, wo_ref, ln1w_ref, ln1b_ref, wup_ref, cw_ref, cb_ref,
                wdn_ref, ln2w_ref, ln2b_ref, o_ref, carry_ref, ubuf_ref, act_ref, *, tm):
    @pl.when(pl.program_id(1) == 0)
    def _():
        carry_ref[...] = jnp.zeros_like(carry_ref)

    gate1 = mod_ref[0, 2:3, :]
    shift2 = mod_ref[0, 3:4, :]
    scale2 = mod_ref[0, 4:5, :]
    gate2 = mod_ref[0, 5:6, :]
    th = tm // 2

    def out_proj(rows):
        return (_dot(ret_ref[0, rows, :], wo_ref[0:RET_WIDTH, :])
                + _dot(mla_ref[0, rows, :], wo_ref[RET_WIDTH:2 * RET_WIDTH, :]))

    def norms(y, rows):
        x1 = _ln_plain(DN_ALPHA * x_ref[0, rows, :] + (1.0 + gate1) * y) * ln1w_ref[...] + ln1b_ref[...]
        return x1, (_ln_plain(x1) * (1.0 + scale2) + shift2).astype(BF16)

    def conv(u, col0, kind):
        outs = []
        for g in range(FF_CHUNK // LANES):
            cols = slice(col0 + g * LANES, col0 + (g + 1) * LANES)
            ug = u[:, g * LANES:(g + 1) * LANES]
            buf = ubuf_ref.at[kind, g]
            buf[0:SUBLANES, :] = carry_ref[:, cols]
            buf[SUBLANES:SUBLANES + th, :] = ug
            carry_ref[:, cols] = ug[th - SUBLANES:th, :]
            u1 = buf[SUBLANES - 1:SUBLANES - 1 + th, :]
            u2 = buf[SUBLANES - 2:SUBLANES - 2 + th, :]
            outs.append(cb_ref[:, cols] + cw_ref[2:3, cols] * ug + cw_ref[1:2, cols] * u1
                        + cw_ref[0:1, cols] * u2)
        return jnp.concatenate(outs, axis=1)

    def up_conv_act(h2, half):
        for c in range(N_FF_CHUNKS):
            gcol0 = c * FF_CHUNK
            vcol0 = D_FF + c * FF_CHUNK
            g = conv(_dot(h2, wup_ref[:, gcol0:gcol0 + FF_CHUNK]), gcol0, 0)
            val = conv(_dot(h2, wup_ref[:, vcol0:vcol0 + FF_CHUNK]), vcol0, 1)
            act_ref[half, :, gcol0:gcol0 + FF_CHUNK] = (_silu(g) * val).astype(BF16)

    def down(half):
        return _dot(act_ref[half], wdn_ref[...])

    def final_norm(x1, y2, rows):
        o_ref[0, rows, :] = _ln_plain(DN_ALPHA * x1 + (1.0 + gate2) * y2) * ln2w_ref[...] + ln2b_ref[...]

    rows_a, rows_b = slice(0, th), slice(th, tm)
    ya = out_proj(rows_a)
    yb = out_proj(rows_b)
    x1a, h2a = norms(ya, rows_a)
    up_conv_act(h2a, 0)
    x1b, h2b = norms(yb, rows_b)
    y2a = down(0)
    up_conv_act(h2b, 1)
    final_norm(x1a, y2a, rows_a)
    final_norm(x1b, down(1), rows_b)


def _mlp_call(x, ret, mla, mod, w_out, ln1w, ln1b, w_up, conv_w, conv_b, w_down, ln2w, ln2b):
    B, S, D = x.shape
    tm = ROW_TILE
    row = lambda w: pl.BlockSpec((1, tm, w), lambda b, s: (b, s, 0))
    return pl.pallas_call(
        functools.partial(_mlp_kernel, tm=tm),
        grid=(B, S // tm),
        in_specs=[
            row(D), row(RET_WIDTH), row(RET_WIDTH),
            pl.BlockSpec((1, N_MOD, D), lambda b, s: (b, 0, 0)),
            _const_spec(w_out.shape), _const_spec(ln1w.shape), _const_spec(ln1b.shape),
            _const_spec(w_up.shape), _const_spec(conv_w.shape), _const_spec(conv_b.shape),
            _const_spec(w_down.shape), _const_spec(ln2w.shape), _const_spec(ln2b.shape),
        ],
        out_specs=row(D),
        out_shape=jax.ShapeDtypeStruct((B, S, D), F32),
        scratch_shapes=[
            pltpu.VMEM((SUBLANES, 2 * D_FF), F32),
            pltpu.VMEM((2, FF_CHUNK // LANES, SUBLANES + tm // 2, LANES), F32),
            pltpu.VMEM((2, tm // 2, D_FF), BF16),
        ],
        compiler_params=pltpu.CompilerParams(
            dimension_semantics=("arbitrary", "arbitrary"), vmem_limit_bytes=VMEM_LIMIT),
        name="outproj_mlp",
    )(x, ret, mla, mod, w_out, ln1w, ln1b, w_up, conv_w, conv_b, w_down, ln2w, ln2b)


def _rope_tables(seq, half, reps):
    pos = np.arange(seq, dtype=np.float32)
    inv = np.float32(ROPE_BASE) ** (-np.arange(half, dtype=np.float32) / np.float32(half))
    ang = (pos[:, None] * inv[None, :]).astype(np.float32)
    cos = np.cos(ang).astype(np.float32)
    sin = np.sin(ang).astype(np.float32)
    cos_t = np.tile(cos, (1, 2 * reps))
    sin_t = np.concatenate([np.tile(-sin, (1, reps)), np.tile(sin, (1, reps))], axis=-1)
    return jnp.asarray(cos_t), jnp.asarray(sin_t)


def _uq_columns():
    half = MLA_ROPE_DIM // 2
    nope = [h * MLA_QK_DIM + np.arange(HEAD_DIM) for h in range(N_HEADS)]
    rope = []
    for p in range(N_HEADS // 2):
        for part in range(2):
            for h in (2 * p, 2 * p + 1):
                rope.append(h * MLA_QK_DIM + HEAD_DIM + part * half + np.arange(half))
    return np.concatenate(nope + rope)


def _ukv_columns():
    k = [h * 2 * HEAD_DIM + np.arange(HEAD_DIM) for h in range(N_HEADS)]
    v = [h * 2 * HEAD_DIM + HEAD_DIM + np.arange(HEAD_DIM) for h in range(N_HEADS)]
    return np.concatenate(k + v)


def _in_columns():
    half = MLA_ROPE_DIM // 2
    base = LAT_COL0 + 2 * MLA_RANK
    x1 = base + np.arange(half)
    x2 = base + half + np.arange(half)
    return np.concatenate([np.arange(base), x1, x1, x2, x2])


def kernel(x, c, w_ada, b_ada, w_in, ret_gn_w, mla_q_norm_w, w_uq, mla_kv_norm_w, w_ukv, w_out,
           ln1_w, ln1_b, w_up, conv_w, conv_b, w_down, ln2_w, ln2_b):
    B, S, D = x.shape
    assert D == D_MODEL and S % ROW_TILE == 0 and S % ATTN_TQ == 0 and w_ada.shape[0] == DEPTH == 1
    l = 0

    c_pad = jnp.pad(c, ((0, SUBLANES - B), (0, 0)))
    mod = _ada_call(c_pad, w_ada[l], b_ada[l][None, :])[:B].reshape(B, N_MOD, D)

    w_in_p = w_in[l][:, _in_columns()].astype(BF16)
    w_uq_p = w_uq[l][:, _uq_columns()].astype(BF16)
    w_ukv_p = w_ukv[l][:, _ukv_columns()].astype(BF16)
    cosr, sinr = _rope_tables(S, HEAD_DIM // 2, 1)
    cosm, sinm = _rope_tables(S, MLA_ROPE_DIM // 2, 2)

    ret, q, k, vt = _inproj_call(
        x, mod, w_in_p, ret_gn_w[l][None, :], mla_q_norm_w[l][None, :], mla_kv_norm_w[l][None, :],
        w_uq_p, w_ukv_p, cosr, sinr, cosm, sinm)
    mla = _attn_call(q, k, vt)
    return _mlp_call(
        x, ret, mla, mod, w_out[l].astype(BF16), ln1_w[l][None, :], ln1_b[l][None, :],
        w_up[l].astype(BF16), conv_w[l], conv_b[l][None, :], w_down[l].astype(BF16),
        ln2_w[l][None, :], ln2_b[l][None, :])
```

```python
import functools
import math

import numpy as np
import jax
import jax.numpy as jnp
from jax import lax
from jax.experimental import pallas as pl
from jax.experimental.pallas import tpu as pltpu

F32 = jnp.float32
BF16 = jnp.bfloat16

D_MODEL = 1024
DEPTH = 1
N_HEADS = 4
HEAD_DIM = 128
RET_WIDTH = N_HEADS * HEAD_DIM
LAT_COL0 = 4 * RET_WIDTH
MLA_RANK = 256
MLA_ROPE_DIM = 64
MLA_QK_DIM = HEAD_DIM + MLA_ROPE_DIM
D_FF = 2816
ROPE_BASE = 10000.0
LN_EPS = 1e-5
RMS_EPS = 1e-6
DN_ALPHA = (2.0 * DEPTH) ** 0.25
N_MOD = 6
LOG_G = [math.log1p(-(2.0 ** (-5.0 - h))) for h in range(N_HEADS)]

LANES = 128
SUBLANES = 8
MXU_DIM = 256
VMEM_LIMIT = 56 * 1024 * 1024

ROW_TILE = 512
ATTN_TQ = 1024
ATTN_TK = 512
FF_CHUNK = MXU_DIM
N_FF_CHUNKS = D_FF // FF_CHUNK
QK_PAD = 2 * LANES
BF16_ROWS = 2 * SUBLANES
ACC_ROWS = HEAD_DIM + BF16_ROWS
NEG_BIG = -1e30


def _ln_plain(x):
    mu = jnp.mean(x, axis=-1, keepdims=True)
    xc = x - mu
    var = jnp.mean(xc * xc, axis=-1, keepdims=True)
    return xc * lax.rsqrt(var + LN_EPS)


def _rms(x):
    return x * lax.rsqrt(jnp.mean(x * x, axis=-1, keepdims=True) + RMS_EPS)


def _silu(x):
    return x * jax.nn.sigmoid(x)


def _dot(a, b):
    return jnp.dot(a, b, preferred_element_type=F32)


def _dot_nt(a, b):
    return lax.dot_general(a, b, (((1,), (1,)), ((), ())), preferred_element_type=F32)


def _dot_tn(a, b):
    return lax.dot_general(a, b, (((0,), (0,)), ((), ())), preferred_element_type=F32)


def _ada_kernel(c_ref, w_ref, b_ref, o_ref):
    cond = _silu(c_ref[...])
    o_ref[...] = _dot(cond.astype(BF16), w_ref[...].astype(BF16)) + b_ref[...]


def _ada_call(c_pad, w_ada, b_ada):
    n_out = w_ada.shape[1]
    return pl.pallas_call(
        _ada_kernel,
        grid=(n_out // D_MODEL,),
        in_specs=[
            pl.BlockSpec((SUBLANES, D_MODEL), lambda j: (0, 0)),
            pl.BlockSpec((D_MODEL, D_MODEL), lambda j: (0, j)),
            pl.BlockSpec((1, D_MODEL), lambda j: (0, j)),
        ],
        out_specs=pl.BlockSpec((SUBLANES, D_MODEL), lambda j: (0, j)),
        out_shape=jax.ShapeDtypeStruct((SUBLANES, n_out), F32),
        compiler_params=pltpu.CompilerParams(dimension_semantics=("arbitrary",)),
        name="ada_mod",
    )(c_pad, w_ada, b_ada)


def _rope128(x, cos, sin_signed):
    return x * cos + pltpu.roll(x, LANES // 2, axis=1) * sin_signed


def _decay_matrix(hd, c):
    row = lax.broadcasted_iota(jnp.int32, (c, c), 0)
    col = lax.broadcasted_iota(jnp.int32, (c, c), 1)
    rel = (row - col).astype(F32)
    return jnp.where(rel >= 0.0, jnp.exp(LOG_G[hd] * jnp.maximum(rel, 0.0)), 0.0)


def _retention_head(hd, q, k, v, gate, gnw, decay_ref, state_ref, ret_ref):
    c = q.shape[0]
    lg = LOG_G[hd]
    n = lax.broadcasted_iota(jnp.int32, (c, 1), 0).astype(F32)
    q_w = jnp.exp(lg * (n + 1.0))
    k_w = jnp.exp(lg * (c - 1.0 - n))
    qb = q.astype(BF16)
    vb = v.astype(BF16)
    scores = _dot_nt(qb, k.astype(BF16)) * decay_ref[hd]
    inner = _dot(scores.astype(BF16), vb)
    state = state_ref[hd]
    cross = _dot(qb, state.astype(BF16)) * q_w
    state_ref[hd] = state * math.exp(lg * c) + _dot_tn((k * k_w).astype(BF16), vb)
    o = _ln_plain(inner + cross) * gnw
    ret_ref[0, :, hd * HEAD_DIM:(hd + 1) * HEAD_DIM] = (o * gate).astype(BF16)


def _inproj_kernel(x_ref, mod_ref, win_ref, gnw_ref, qnw_ref, kvnw_ref, wuq_ref, wukv_ref,
                   cosr_ref, sinr_ref, cosm_ref, sinm_ref,
                   ret_ref, q_ref, k_ref, vt_ref, state_ref, decay_ref, *, q_scale, rk_scale):
    @pl.when(jnp.logical_and(pl.program_id(0) == 0, pl.program_id(1) == 0))
    def _():
        for hd in range(N_HEADS):
            decay_ref[hd] = _decay_matrix(hd, decay_ref.shape[1])

    @pl.when(pl.program_id(1) == 0)
    def _():
        state_ref[...] = jnp.zeros_like(state_ref)

    x = x_ref[0]
    shift = mod_ref[0, 0:1, :]
    scale = mod_ref[0, 1:2, :]
    h = (_ln_plain(x) * (1.0 + scale) + shift).astype(BF16)

    cosr = cosr_ref[...]
    sinr = sinr_ref[...]
    cosm = cosm_ref[...]
    sinm = sinm_ref[...]

    pq = _dot(h, win_ref[:, 0:RET_WIDTH])
    pk = _dot(h, win_ref[:, RET_WIDTH:2 * RET_WIDTH])
    pv = _dot(h, win_ref[:, 2 * RET_WIDTH:3 * RET_WIDTH])
    pg = _silu(_dot(h, win_ref[:, 3 * RET_WIDTH:LAT_COL0]))
    for hd in range(N_HEADS):
        sl = slice(hd * HEAD_DIM, (hd + 1) * HEAD_DIM)
        _retention_head(hd, _rope128(pq[:, sl], cosr, sinr), _rope128(pk[:, sl], cosr, sinr) * rk_scale,
                        pv[:, sl], pg[:, sl], gnw_ref[:, sl], decay_ref, state_ref, ret_ref)

    lat = _dot(h, win_ref[:, LAT_COL0:])
    cq = (_rms(lat[:, 0:MLA_RANK]) * qnw_ref[...]).astype(BF16)
    ckv = (_rms(lat[:, MLA_RANK:2 * MLA_RANK]) * kvnw_ref[...]).astype(BF16)
    kr = _rope128(lat[:, 2 * MLA_RANK:2 * MLA_RANK + LANES], cosm, sinm)
    lane = lax.broadcasted_iota(jnp.int32, kr.shape, 1)
    first_of_pair = (lane % (LANES // 2)) < (LANES // 4)
    kr_even = jnp.where(first_of_pair, kr, 0.0).astype(BF16)
    kr_odd = jnp.where(first_of_pair, 0.0, kr).astype(BF16)

    qf = _dot(cq, wuq_ref[...])
    kvf = _dot(ckv, wukv_ref[...])
    for hd in range(N_HEADS):
        vt_ref[0, hd] = kvf[:, (2 * hd + 1) * HEAD_DIM:(2 * hd + 2) * HEAD_DIM].T.astype(BF16)
    for p in range(N_HEADS // 2):
        qr = qf[:, RET_WIDTH + p * LANES:RET_WIDTH + (p + 1) * LANES]
        qr = (_rope128(qr, cosm, sinm) * q_scale).astype(BF16)
        for hd in (2 * p, 2 * p + 1):
            sl = slice(hd * HEAD_DIM, (hd + 1) * HEAD_DIM)
            q_ref[0, :, hd * QK_PAD:hd * QK_PAD + LANES] = (qf[:, sl] * q_scale).astype(BF16)
            q_ref[0, :, hd * QK_PAD + LANES:(hd + 1) * QK_PAD] = qr
            k_ref[0, :, hd * QK_PAD:hd * QK_PAD + LANES] = kvf[:, 2 * hd * HEAD_DIM:(2 * hd + 1) * HEAD_DIM].astype(BF16)
            k_ref[0, :, hd * QK_PAD + LANES:(hd + 1) * QK_PAD] = kr_even if hd % 2 == 0 else kr_odd


def _const_spec(shape):
    nd = len(shape)
    return pl.BlockSpec(shape, lambda *_: (0,) * nd, pipeline_mode=pl.Buffered(1))


def _inproj_call(x, mod, w_in, gnw, qnw, kvnw, w_uq, w_ukv, cosr, sinr, cosm, sinm):
    B, S, D = x.shape
    tm = ROW_TILE
    q_scale = (MLA_QK_DIM ** -0.5) * math.log2(math.e)
    rk_scale = HEAD_DIM ** -0.5
    row = lambda w: pl.BlockSpec((1, tm, w), lambda b, s: (b, s, 0))
    tab = pl.BlockSpec((tm, LANES), lambda b, s: (s, 0))
    out_shapes = (
        [jax.ShapeDtypeStruct((B, S, RET_WIDTH), BF16)]
        + [jax.ShapeDtypeStruct((B, S, N_HEADS * QK_PAD), BF16)] * 2
        + [jax.ShapeDtypeStruct((B, N_HEADS, HEAD_DIM, S), BF16)]
    )
    vt_spec = pl.BlockSpec((1, N_HEADS, HEAD_DIM, tm), lambda b, s: (b, 0, 0, s))
    return pl.pallas_call(
        functools.partial(_inproj_kernel, q_scale=q_scale, rk_scale=rk_scale),
        grid=(B, S // tm),
        in_specs=[
            row(D),
            pl.BlockSpec((1, N_MOD, D), lambda b, s: (b, 0, 0)),
            _const_spec(w_in.shape), _const_spec(gnw.shape),
            _const_spec(qnw.shape), _const_spec(kvnw.shape),
            _const_spec(w_uq.shape), _const_spec(w_ukv.shape),
            tab, tab, tab, tab,
        ],
        out_specs=[row(RET_WIDTH)] + [row(N_HEADS * QK_PAD)] * 2 + [vt_spec],
        out_shape=out_shapes,
        scratch_shapes=[
            pltpu.VMEM((N_HEADS, HEAD_DIM, HEAD_DIM), F32),
            pltpu.VMEM((N_HEADS, tm, tm), F32),
        ],
        compiler_params=pltpu.CompilerParams(
            dimension_semantics=("arbitrary", "arbitrary"), vmem_limit_bytes=VMEM_LIMIT),
        name="inproj_retention",
    )(x, mod, w_in, gnw, qnw, kvnw, w_uq, w_ukv, cosr, sinr, cosm, sinm)


def _attn_kernel(q_ref, k_ref, vt_ref, o_ref, qt_ref, s_buf, p_buf, acc_ref, *, tq, tk):
    qi = pl.program_id(2)
    qt_ref[...] = q_ref[0].T

    ones = jnp.ones((BF16_ROWS, tk), BF16)

    def k_blk(j):
        return k_ref[0, pl.ds(pl.multiple_of(j * tk, tk), tk), :]

    def v_blk(j):
        return jnp.concatenate([vt_ref[0, 0, :, pl.ds(pl.multiple_of(j * tk, tk), tk)], ones], axis=0)

    def scores(j, slot):
        s_buf[slot] = _dot(k_blk(j), qt_ref[...])

    def value_update(j, slot, alpha):
        acc_ref[...] = alpha * acc_ref[...] + _dot(v_blk(j), p_buf[slot])

    def softmax(s, m):
        m_new = jnp.maximum(m, jnp.max(s, axis=0, keepdims=True))
        return m_new, jnp.exp2(m - m_new), jnp.exp2(s - m_new).astype(BF16)

    def two_blocks(u, carry):
        alpha_prev, m = carry
        t = 2 * u
        scores(t + 1, 1)
        value_update(jnp.maximum(t - 1, 0), 1, alpha_prev)
        m, alpha_even, p_buf[0] = softmax(s_buf[0], m)
        scores(t + 2, 0)
        value_update(t, 0, alpha_even)
        m, alpha_odd, p_buf[1] = softmax(s_buf[1], m)
        return alpha_odd, m

    scores(0, 0)
    p_buf[1] = jnp.zeros((tk, tq), BF16)
    acc_ref[...] = jnp.zeros_like(acc_ref)
    init = (jnp.ones((1, tq), F32), jnp.full((1, tq), NEG_BIG, F32))
    alpha_prev, m = lax.fori_loop(0, qi, two_blocks, init)

    t = 2 * qi
    key = lax.broadcasted_iota(jnp.int32, (tk, tk), 0)
    qry = lax.broadcasted_iota(jnp.int32, (tk, tk), 1)
    causal = key <= qry
    s_right = _dot(k_blk(t + 1), qt_ref[:, tk:])
    value_update(jnp.maximum(t - 1, 0), 1, alpha_prev)
    s = s_buf[0]
    s = jnp.concatenate([jnp.where(causal, s[:, :tk], NEG_BIG), s[:, tk:]], axis=1)
    m, alpha, p_buf[0] = softmax(s, m)
    value_update(t, 0, alpha)
    _, alpha_r, p_r = softmax(jnp.where(causal, s_right, NEG_BIG), m[:, tk:])
    acc_ref[:, tk:] = alpha_r * acc_ref[:, tk:] + _dot(v_blk(t + 1), p_r)
    acc = acc_ref[...]
    o_ref[0] = (acc[:HEAD_DIM] / acc[HEAD_DIM:HEAD_DIM + 1]).T.astype(BF16)


def _attn_call(q, k, vt):
    B, S, _ = q.shape
    tq, tk = ATTN_TQ, ATTN_TK
    assert tq == 2 * tk
    return pl.pallas_call(
        functools.partial(_attn_kernel, tq=tq, tk=tk),
        grid=(B, N_HEADS, S // tq),
        in_specs=[
            pl.BlockSpec((1, tq, QK_PAD), lambda b, h, i: (b, i, h)),
            pl.BlockSpec((1, S, QK_PAD), lambda b, h, i: (b, 0, h)),
            pl.BlockSpec((1, 1, HEAD_DIM, S), lambda b, h, i: (b, h, 0, 0)),
        ],
        out_specs=pl.BlockSpec((1, tq, HEAD_DIM), lambda b, h, i: (b, i, h)),
        out_shape=jax.ShapeDtypeStruct((B, S, N_HEADS * HEAD_DIM), BF16),
        scratch_shapes=[
            pltpu.VMEM((QK_PAD, tq), BF16),
            pltpu.VMEM((2, tk, tq), F32),
            pltpu.VMEM((2, tk, tq), BF16),
            pltpu.VMEM((ACC_ROWS, tq), F32),
        ],
        compiler_params=pltpu.CompilerParams(
            dimension_semantics=("arbitrary", "arbitrary", "arbitrary"), vmem_limit_bytes=VMEM_LIMIT),
        name="mla_attn",
    )(q, k, vt)


def _mlp_kernel(x_ref, ret_ref, mla_ref, mod_ref, wo_ref, ln1w_ref, ln1b_ref, wup_ref, cw_ref, cb_ref,
                wdn_ref, ln2w_ref, ln2b_ref, o_ref, carry_ref, ubuf_ref, act_ref, *, tm):
    @pl.when(pl.program_id(1) == 0)
    def _():
        carry_ref[...] = jnp.zeros_like(carry_ref)

    gate1 = mod_ref[0, 2:3, :]
    shift2 = mod_ref[0, 3:4, :]
    scale2 = mod_ref[0, 4:5, :]
    gate2 = mod_ref[0, 5:6, :]

    y = _dot(ret_ref[0], wo_ref[0:RET_WIDTH, :]) + _dot(mla_ref[0], wo_ref[RET_WIDTH:2 * RET_WIDTH, :])
    x1 = _ln_plain(DN_ALPHA * x_ref[0] + (1.0 + gate1) * y) * ln1w_ref[...] + ln1b_ref[...]
    h2 = (_ln_plain(x1) * (1.0 + scale2) + shift2).astype(BF16)

    def conv(u, col0):
        outs = []
        for g in range(u.shape[1] // LANES):
            cols = slice(col0 + g * LANES, col0 + (g + 1) * LANES)
            ug = u[:, g * LANES:(g + 1) * LANES]
            buf = ubuf_ref.at[g]
            buf[0:SUBLANES, :] = carry_ref[:, cols]
            buf[SUBLANES:SUBLANES + tm, :] = ug
            carry_ref[:, cols] = ug[tm - SUBLANES:tm, :]
            u1 = buf[SUBLANES - 1:SUBLANES - 1 + tm, :]
            u2 = buf[SUBLANES - 2:SUBLANES - 2 + tm, :]
            outs.append(cb_ref[:, cols] + cw_ref[2:3, cols] * ug + cw_ref[1:2, cols] * u1
                        + cw_ref[0:1, cols] * u2)
        return jnp.concatenate(outs, axis=1)

    for c in range(N_FF_CHUNKS):
        col0 = 2 * c * FF_CHUNK
        u = conv(_dot(h2, wup_ref[:, col0:col0 + 2 * FF_CHUNK]), col0)
        act = _silu(u[:, :FF_CHUNK]) * u[:, FF_CHUNK:]
        act_ref[:, c * FF_CHUNK:(c + 1) * FF_CHUNK] = act.astype(BF16)

    y2 = _dot(act_ref[...], wdn_ref[...])
    o_ref[0] = _ln_plain(DN_ALPHA * x1 + (1.0 + gate2) * y2) * ln2w_ref[...] + ln2b_ref[...]


def _mlp_call(x, ret, mla, mod, w_out, ln1w, ln1b, w_up, conv_w, conv_b, w_down, ln2w, ln2b):
    B, S, D = x.shape
    tm = ROW_TILE
    row = lambda w: pl.BlockSpec((1, tm, w), lambda b, s: (b, s, 0))
    return pl.pallas_call(
        functools.partial(_mlp_kernel, tm=tm),
        grid=(B, S // tm),
        in_specs=[
            row(D), row(RET_WIDTH), row(RET_WIDTH),
            pl.BlockSpec((1, N_MOD, D), lambda b, s: (b, 0, 0)),
            _const_spec(w_out.shape), _const_spec(ln1w.shape), _const_spec(ln1b.shape),
            _const_spec(w_up.shape), _const_spec(conv_w.shape), _const_spec(conv_b.shape),
            _const_spec(w_down.shape), _const_spec(ln2w.shape), _const_spec(ln2b.shape),
        ],
        out_specs=row(D),
        out_shape=jax.ShapeDtypeStruct((B, S, D), F32),
        scratch_shapes=[
            pltpu.VMEM((SUBLANES, 2 * D_FF), F32),
            pltpu.VMEM((2 * FF_CHUNK // LANES, SUBLANES + tm, LANES), F32),
            pltpu.VMEM((tm, D_FF), BF16),
        ],
        compiler_params=pltpu.CompilerParams(
            dimension_semantics=("arbitrary", "arbitrary"), vmem_limit_bytes=VMEM_LIMIT),
        name="outproj_mlp",
    )(x, ret, mla, mod, w_out, ln1w, ln1b, w_up, conv_w, conv_b, w_down, ln2w, ln2b)


def _rope_tables(seq, half, reps):
    pos = np.arange(seq, dtype=np.float32)
    inv = np.float32(ROPE_BASE) ** (-np.arange(half, dtype=np.float32) / np.float32(half))
    ang = (pos[:, None] * inv[None, :]).astype(np.float32)
    cos = np.cos(ang).astype(np.float32)
    sin = np.sin(ang).astype(np.float32)
    cos_t = np.tile(cos, (1, 2 * reps))
    sin_t = np.concatenate([np.tile(-sin, (1, reps)), np.tile(sin, (1, reps))], axis=-1)
    return jnp.asarray(cos_t), jnp.asarray(sin_t)


def _uq_columns():
    half = MLA_ROPE_DIM // 2
    nope = [h * MLA_QK_DIM + np.arange(HEAD_DIM) for h in range(N_HEADS)]
    rope = []
    for p in range(N_HEADS // 2):
        for part in range(2):
            for h in (2 * p, 2 * p + 1):
                rope.append(h * MLA_QK_DIM + HEAD_DIM + part * half + np.arange(half))
    return np.concatenate(nope + rope)


def _take_columns(w, cols):
    cols = np.asarray(cols)
    cuts = np.flatnonzero(np.diff(cols) != 1) + 1
    runs = np.split(cols, cuts)
    return jnp.concatenate([w[:, r[0]:r[-1] + 1] for r in runs], axis=1)


def _chunk_interleave(a):
    lead = a.shape[:-1]
    a = a.reshape(*lead, 2, N_FF_CHUNKS, FF_CHUNK)
    return jnp.swapaxes(a, -3, -2).reshape(*lead, 2 * D_FF)


def _in_columns():
    half = MLA_ROPE_DIM // 2
    base = LAT_COL0 + 2 * MLA_RANK
    x1 = base + np.arange(half)
    x2 = base + half + np.arange(half)
    return np.concatenate([np.arange(base), x1, x1, x2, x2])


def kernel(x, c, w_ada, b_ada, w_in, ret_gn_w, mla_q_norm_w, w_uq, mla_kv_norm_w, w_ukv, w_out,
           ln1_w, ln1_b, w_up, conv_w, conv_b, w_down, ln2_w, ln2_b):
    B, S, D = x.shape
    assert D == D_MODEL and S % ROW_TILE == 0 and S % ATTN_TQ == 0 and w_ada.shape[0] == DEPTH == 1
    l = 0

    c_pad = jnp.pad(c, ((0, SUBLANES - B), (0, 0)))
    mod = _ada_call(c_pad, w_ada[l], b_ada[l][None, :])[:B].reshape(B, N_MOD, D)

    w_in_p = _take_columns(w_in[l].astype(BF16), _in_columns())
    w_uq_p = _take_columns(w_uq[l].astype(BF16), _uq_columns())
    w_ukv_p = w_ukv[l].astype(BF16)
    w_up_p, conv_w_p, conv_b_p = (_chunk_interleave(a) for a in (w_up[l].astype(BF16), conv_w[l], conv_b[l][None, :]))
    cosr, sinr = _rope_tables(S, HEAD_DIM // 2, 1)
    cosm, sinm = _rope_tables(S, MLA_ROPE_DIM // 2, 2)

    ret, q, k, vt = _inproj_call(
        x, mod, w_in_p, ret_gn_w[l][None, :], mla_q_norm_w[l][None, :], mla_kv_norm_w[l][None, :],
        w_uq_p, w_ukv_p, cosr, sinr, cosm, sinm)
    mla = _attn_call(q, k, vt)
    return _mlp_call(
        x, ret, mla, mod, w_out[l].astype(BF16), ln1_w[l][None, :], ln1_b[l][None, :],
        w_up_p, conv_w_p, conv_b_p, w_down[l].astype(BF16), ln2_w[l][None, :], ln2_b[l][None, :])
```

```python
import functools
import math

import numpy as np
import jax
import jax.numpy as jnp
from jax import lax
from jax.experimental import pallas as pl
from jax.experimental.pallas import tpu as pltpu

F32 = jnp.float32
BF16 = jnp.bfloat16

D_MODEL = 1024
DEPTH = 1
N_HEADS = 4
HEAD_DIM = 128
RET_WIDTH = N_HEADS * HEAD_DIM
LAT_COL0 = 4 * RET_WIDTH
MLA_RANK = 256
MLA_ROPE_DIM = 64
MLA_QK_DIM = HEAD_DIM + MLA_ROPE_DIM
D_FF = 2816
ROPE_BASE = 10000.0
LN_EPS = 1e-5
RMS_EPS = 1e-6
DN_ALPHA = (2.0 * DEPTH) ** 0.25
N_MOD = 6
LOG_G = [math.log1p(-(2.0 ** (-5.0 - h))) for h in range(N_HEADS)]

LANES = 128
SUBLANES = 8
MXU_DIM = 256
VMEM_LIMIT = 56 * 1024 * 1024

ROW_TILE = 512
ATTN_TQ = 1024
ATTN_TK = 512
FF_CHUNK = MXU_DIM
N_FF_CHUNKS = D_FF // FF_CHUNK
QK_PAD = 2 * LANES
BF16_ROWS = 2 * SUBLANES
ACC_ROWS = HEAD_DIM + BF16_ROWS
NEG_BIG = -1e30


def _ln_plain(x):
    mu = jnp.mean(x, axis=-1, keepdims=True)
    xc = x - mu
    var = jnp.mean(xc * xc, axis=-1, keepdims=True)
    return xc * lax.rsqrt(var + LN_EPS)


def _rms(x):
    return x * lax.rsqrt(jnp.mean(x * x, axis=-1, keepdims=True) + RMS_EPS)


def _silu(x):
    return x * jax.nn.sigmoid(x)


def _dot(a, b):
    return jnp.dot(a, b, preferred_element_type=F32)


def _dot_nt(a, b):
    return lax.dot_general(a, b, (((1,), (1,)), ((), ())), preferred_element_type=F32)


def _dot_tn(a, b):
    return lax.dot_general(a, b, (((0,), (0,)), ((), ())), preferred_element_type=F32)


def _ada_kernel(c_ref, w_ref, b_ref, o_ref):
    cond = _silu(c_ref[...])
    o_ref[...] = _dot(cond.astype(BF16), w_ref[...].astype(BF16)) + b_ref[...]


def _ada_call(c_pad, w_ada, b_ada):
    n_out = w_ada.shape[1]
    return pl.pallas_call(
        _ada_kernel,
        grid=(n_out // D_MODEL,),
        in_specs=[
            pl.BlockSpec((SUBLANES, D_MODEL), lambda j: (0, 0)),
            pl.BlockSpec((D_MODEL, D_MODEL), lambda j: (0, j)),
            pl.BlockSpec((1, D_MODEL), lambda j: (0, j)),
        ],
        out_specs=pl.BlockSpec((SUBLANES, D_MODEL), lambda j: (0, j)),
        out_shape=jax.ShapeDtypeStruct((SUBLANES, n_out), F32),
        compiler_params=pltpu.CompilerParams(dimension_semantics=("arbitrary",)),
        name="ada_mod",
    )(c_pad, w_ada, b_ada)


def _rope128(x, cos, sin_signed):
    return x * cos + pltpu.roll(x, LANES // 2, axis=1) * sin_signed


def _decay_matrix(hd, c):
    row = lax.broadcasted_iota(jnp.int32, (c, c), 0)
    col = lax.broadcasted_iota(jnp.int32, (c, c), 1)
    rel = (row - col).astype(F32)
    return jnp.where(rel >= 0.0, jnp.exp(LOG_G[hd] * jnp.maximum(rel, 0.0)), 0.0)


def _retention_head(hd, q, k, v, gate, gnw, decay_ref, state_ref, ret_ref):
    c = q.shape[0]
    lg = LOG_G[hd]
    n = lax.broadcasted_iota(jnp.int32, (c, 1), 0).astype(F32)
    q_w = jnp.exp(lg * (n + 1.0))
    k_w = jnp.exp(lg * (c - 1.0 - n))
    qb = q.astype(BF16)
    vb = v.astype(BF16)
    scores = _dot_nt(qb, k.astype(BF16)) * decay_ref[hd]
    inner = _dot(scores.astype(BF16), vb)
    state = state_ref[hd]
    cross = _dot(qb, state.astype(BF16)) * q_w
    state_ref[hd] = state * math.exp(lg * c) + _dot_tn((k * k_w).astype(BF16), vb)
    o = _ln_plain(inner + cross) * gnw
    ret_ref[0, :, hd * HEAD_DIM:(hd + 1) * HEAD_DIM] = (o * gate).astype(BF16)


def _stage_w_in(win32_ref, win_ref):
    base = LAT_COL0 + 2 * MLA_RANK
    quarter = LANES // 4
    step = 2 * LANES
    for r0 in range(0, win32_ref.shape[0], step):
        rows = slice(r0, r0 + step)
        win_ref[rows, 0:base] = win32_ref[rows, 0:base].astype(BF16)
        tail = win32_ref[rows, base:base + MLA_ROPE_DIM]
        t2 = jnp.concatenate([tail, tail], axis=1)
        lane = lax.broadcasted_iota(jnp.int32, t2.shape, 1)
        middle = jnp.abs(2 * lane - (LANES - 1)) < 2 * quarter
        win_ref[rows, base:base + LANES] = jnp.where(middle, pltpu.roll(t2, quarter, axis=1), t2).astype(BF16)


def _inproj_kernel(x_ref, mod_ref, win32_ref, gnw_ref, qnw_ref, kvnw_ref, wuq_ref, wukv_ref,
                   cosr_ref, sinr_ref, cosm_ref, sinm_ref,
                   ret_ref, q_ref, k_ref, vt_ref, state_ref, decay_ref, win_ref, *, q_scale, rk_scale):
    @pl.when(jnp.logical_and(pl.program_id(0) == 0, pl.program_id(1) == 0))
    def _():
        _stage_w_in(win32_ref, win_ref)
        for hd in range(N_HEADS):
            decay_ref[hd] = _decay_matrix(hd, decay_ref.shape[1])

    @pl.when(pl.program_id(1) == 0)
    def _():
        state_ref[...] = jnp.zeros_like(state_ref)

    x = x_ref[0]
    shift = mod_ref[0, 0:1, :]
    scale = mod_ref[0, 1:2, :]
    h = (_ln_plain(x) * (1.0 + scale) + shift).astype(BF16)

    cosr = cosr_ref[...]
    sinr = sinr_ref[...]
    cosm = cosm_ref[...]
    sinm = sinm_ref[...]

    pq = _dot(h, win_ref[:, 0:RET_WIDTH])
    pk = _dot(h, win_ref[:, RET_WIDTH:2 * RET_WIDTH])
    pv = _dot(h, win_ref[:, 2 * RET_WIDTH:3 * RET_WIDTH])
    pg = _silu(_dot(h, win_ref[:, 3 * RET_WIDTH:LAT_COL0]))
    for hd in range(N_HEADS):
        sl = slice(hd * HEAD_DIM, (hd + 1) * HEAD_DIM)
        _retention_head(hd, _rope128(pq[:, sl], cosr, sinr), _rope128(pk[:, sl], cosr, sinr) * rk_scale,
                        pv[:, sl], pg[:, sl], gnw_ref[:, sl], decay_ref, state_ref, ret_ref)

    lat = _dot(h, win_ref[:, LAT_COL0:])
    cq = (_rms(lat[:, 0:MLA_RANK]) * qnw_ref[...]).astype(BF16)
    ckv = (_rms(lat[:, MLA_RANK:2 * MLA_RANK]) * kvnw_ref[...]).astype(BF16)
    kr = _rope128(lat[:, 2 * MLA_RANK:2 * MLA_RANK + LANES], cosm, sinm)
    lane = lax.broadcasted_iota(jnp.int32, kr.shape, 1)
    first_of_pair = (lane % (LANES // 2)) < (LANES // 4)
    kr_even = jnp.where(first_of_pair, kr, 0.0).astype(BF16)
    kr_odd = jnp.where(first_of_pair, 0.0, kr).astype(BF16)

    qf = _dot(cq, wuq_ref[...])
    kvf = _dot(ckv, wukv_ref[...])
    for hd in range(N_HEADS):
        vt_ref[0, hd] = kvf[:, (2 * hd + 1) * HEAD_DIM:(2 * hd + 2) * HEAD_DIM].T.astype(BF16)
    for p in range(N_HEADS // 2):
        qr = qf[:, RET_WIDTH + p * LANES:RET_WIDTH + (p + 1) * LANES]
        qr = (_rope128(qr, cosm, sinm) * q_scale).astype(BF16)
        for hd in (2 * p, 2 * p + 1):
            sl = slice(hd * HEAD_DIM, (hd + 1) * HEAD_DIM)
            q_ref[0, :, hd * QK_PAD:hd * QK_PAD + LANES] = (qf[:, sl] * q_scale).astype(BF16)
            q_ref[0, :, hd * QK_PAD + LANES:(hd + 1) * QK_PAD] = qr
            k_ref[0, :, hd * QK_PAD:hd * QK_PAD + LANES] = kvf[:, 2 * hd * HEAD_DIM:(2 * hd + 1) * HEAD_DIM].astype(BF16)
            k_ref[0, :, hd * QK_PAD + LANES:(hd + 1) * QK_PAD] = kr_even if hd % 2 == 0 else kr_odd


def _const_spec(shape):
    nd = len(shape)
    return pl.BlockSpec(shape, lambda *_: (0,) * nd, pipeline_mode=pl.Buffered(1))


def _inproj_call(x, mod, w_in, gnw, qnw, kvnw, w_uq, w_ukv, cosr, sinr, cosm, sinm):
    B, S, D = x.shape
    tm = ROW_TILE
    q_scale = (MLA_QK_DIM ** -0.5) * math.log2(math.e)
    rk_scale = HEAD_DIM ** -0.5
    row = lambda w: pl.BlockSpec((1, tm, w), lambda b, s: (b, s, 0))
    tab = pl.BlockSpec((tm, LANES), lambda b, s: (s, 0))
    out_shapes = (
        [jax.ShapeDtypeStruct((B, S, RET_WIDTH), BF16)]
        + [jax.ShapeDtypeStruct((B, S, N_HEADS * QK_PAD), BF16)] * 2
        + [jax.ShapeDtypeStruct((B, N_HEADS, HEAD_DIM, S), BF16)]
    )
    vt_spec = pl.BlockSpec((1, N_HEADS, HEAD_DIM, tm), lambda b, s: (b, 0, 0, s))
    return pl.pallas_call(
        functools.partial(_inproj_kernel, q_scale=q_scale, rk_scale=rk_scale),
        grid=(B, S // tm),
        in_specs=[
            row(D),
            pl.BlockSpec((1, N_MOD, D), lambda b, s: (b, 0, 0)),
            _const_spec(w_in.shape), _const_spec(gnw.shape),
            _const_spec(qnw.shape), _const_spec(kvnw.shape),
            _const_spec(w_uq.shape), _const_spec(w_ukv.shape),
            tab, tab, tab, tab,
        ],
        out_specs=[row(RET_WIDTH)] + [row(N_HEADS * QK_PAD)] * 2 + [vt_spec],
        out_shape=out_shapes,
        scratch_shapes=[
            pltpu.VMEM((N_HEADS, HEAD_DIM, HEAD_DIM), F32),
            pltpu.VMEM((N_HEADS, tm, tm), F32),
            pltpu.VMEM((D, LAT_COL0 + 2 * MLA_RANK + LANES), BF16),
        ],
        compiler_params=pltpu.CompilerParams(
            dimension_semantics=("arbitrary", "arbitrary"), vmem_limit_bytes=VMEM_LIMIT),
        name="inproj_retention",
    )(x, mod, w_in, gnw, qnw, kvnw, w_uq, w_ukv, cosr, sinr, cosm, sinm)


def _attn_kernel(q_ref, k_ref, vt_ref, o_ref, qt_ref, s_buf, p_buf, acc_ref, *, tq, tk):
    qi = pl.program_id(2)
    qt_ref[...] = q_ref[0].T

    ones = jnp.ones((BF16_ROWS, tk), BF16)

    def k_blk(j):
        return k_ref[0, pl.ds(pl.multiple_of(j * tk, tk), tk), :]

    def v_blk(j):
        return jnp.concatenate([vt_ref[0, 0, :, pl.ds(pl.multiple_of(j * tk, tk), tk)], ones], axis=0)

    def scores(j, slot):
        s_buf[slot] = _dot(k_blk(j), qt_ref[...])

    def value_update(j, slot, alpha):
        acc_ref[...] = alpha * acc_ref[...] + _dot(v_blk(j), p_buf[slot])

    def softmax(s, m):
        m_new = jnp.maximum(m, jnp.max(s, axis=0, keepdims=True))
        return m_new, jnp.exp2(m - m_new), jnp.exp2(s - m_new).astype(BF16)

    def two_blocks(u, carry):
        alpha_prev, m = carry
        t = 2 * u
        scores(t + 1, 1)
        value_update(jnp.maximum(t - 1, 0), 1, alpha_prev)
        m, alpha_even, p_buf[0] = softmax(s_buf[0], m)
        scores(t + 2, 0)
        value_update(t, 0, alpha_even)
        m, alpha_odd, p_buf[1] = softmax(s_buf[1], m)
        return alpha_odd, m

    scores(0, 0)
    p_buf[1] = jnp.zeros((tk, tq), BF16)
    acc_ref[...] = jnp.zeros_like(acc_ref)
    init = (jnp.ones((1, tq), F32), jnp.full((1, tq), NEG_BIG, F32))
    alpha_prev, m = lax.fori_loop(0, qi, two_blocks, init)

    t = 2 * qi
    key = lax.broadcasted_iota(jnp.int32, (tk, tk), 0)
    qry = lax.broadcasted_iota(jnp.int32, (tk, tk), 1)
    causal = key <= qry
    s_right = _dot(k_blk(t + 1), qt_ref[:, tk:])
    value_update(jnp.maximum(t - 1, 0), 1, alpha_prev)
    s = s_buf[0]
    s = jnp.concatenate([jnp.where(causal, s[:, :tk], NEG_BIG), s[:, tk:]], axis=1)
    m, alpha, p_buf[0] = softmax(s, m)
    value_update(t, 0, alpha)
    _, alpha_r, p_r = softmax(jnp.where(causal, s_right, NEG_BIG), m[:, tk:])
    acc_ref[:, tk:] = alpha_r * acc_ref[:, tk:] + _dot(v_blk(t + 1), p_r)
    acc = acc_ref[...]
    o_ref[0] = (acc[:HEAD_DIM] / acc[HEAD_DIM:HEAD_DIM + 1]).T.astype(BF16)


def _attn_call(q, k, vt):
    B, S, _ = q.shape
    tq, tk = ATTN_TQ, ATTN_TK
    assert tq == 2 * tk
    return pl.pallas_call(
        functools.partial(_attn_kernel, tq=tq, tk=tk),
        grid=(B, N_HEADS, S // tq),
        in_specs=[
            pl.BlockSpec((1, tq, QK_PAD), lambda b, h, i: (b, i, h)),
            pl.BlockSpec((1, S, QK_PAD), lambda b, h, i: (b, 0, h)),
            pl.BlockSpec((1, 1, HEAD_DIM, S), lambda b, h, i: (b, h, 0, 0)),
        ],
        out_specs=pl.BlockSpec((1, tq, HEAD_DIM), lambda b, h, i: (b, i, h)),
        out_shape=jax.ShapeDtypeStruct((B, S, N_HEADS * HEAD_DIM), BF16),
        scratch_shapes=[
            pltpu.VMEM((QK_PAD, tq), BF16),
            pltpu.VMEM((2, tk, tq), F32),
            pltpu.VMEM((2, tk, tq), BF16),
            pltpu.VMEM((ACC_ROWS, tq), F32),
        ],
        compiler_params=pltpu.CompilerParams(
            dimension_semantics=("arbitrary", "arbitrary", "arbitrary"), vmem_limit_bytes=VMEM_LIMIT),
        name="mla_attn",
    )(q, k, vt)


def _mlp_kernel(x_ref, ret_ref, mla_ref, mod_ref, wo_ref, ln1w_ref, ln1b_ref, wup_ref, cw_ref, cb_ref,
                wdn_ref, ln2w_ref, ln2b_ref, o_ref, carry_ref, ubuf_ref, act_ref, *, tm):
    @pl.when(pl.program_id(1) == 0)
    def _():
        carry_ref[...] = jnp.zeros_like(carry_ref)

    gate1 = mod_ref[0, 2:3, :]
    shift2 = mod_ref[0, 3:4, :]
    scale2 = mod_ref[0, 4:5, :]
    gate2 = mod_ref[0, 5:6, :]

    y = _dot(ret_ref[0], wo_ref[0:RET_WIDTH, :]) + _dot(mla_ref[0], wo_ref[RET_WIDTH:2 * RET_WIDTH, :])
    x1 = _ln_plain(DN_ALPHA * x_ref[0] + (1.0 + gate1) * y) * ln1w_ref[...] + ln1b_ref[...]
    h2 = (_ln_plain(x1) * (1.0 + scale2) + shift2).astype(BF16)

    def conv(u, col0, kind):
        outs = []
        for g in range(FF_CHUNK // LANES):
            cols = slice(col0 + g * LANES, col0 + (g + 1) * LANES)
            ug = u[:, g * LANES:(g + 1) * LANES]
            buf = ubuf_ref.at[kind, g]
            buf[0:SUBLANES, :] = carry_ref[:, cols]
            buf[SUBLANES:SUBLANES + tm, :] = ug
            carry_ref[:, cols] = ug[tm - SUBLANES:tm, :]
            u1 = buf[SUBLANES - 1:SUBLANES - 1 + tm, :]
            u2 = buf[SUBLANES - 2:SUBLANES - 2 + tm, :]
            outs.append(cb_ref[:, cols] + cw_ref[2:3, cols] * ug + cw_ref[1:2, cols] * u1
                        + cw_ref[0:1, cols] * u2)
        return jnp.concatenate(outs, axis=1)

    for c in range(N_FF_CHUNKS):
        gcol0 = c * FF_CHUNK
        vcol0 = D_FF + c * FF_CHUNK
        g = conv(_dot(h2, wup_ref[:, gcol0:gcol0 + FF_CHUNK]), gcol0, 0)
        val = conv(_dot(h2, wup_ref[:, vcol0:vcol0 + FF_CHUNK]), vcol0, 1)
        act_ref[:, gcol0:gcol0 + FF_CHUNK] = (_silu(g) * val).astype(BF16)

    y2 = _dot(act_ref[...], wdn_ref[...])
    o_ref[0] = _ln_plain(DN_ALPHA * x1 + (1.0 + gate2) * y2) * ln2w_ref[...] + ln2b_ref[...]


def _mlp_call(x, ret, mla, mod, w_out, ln1w, ln1b, w_up, conv_w, conv_b, w_down, ln2w, ln2b):
    B, S, D = x.shape
    tm = ROW_TILE
    row = lambda w: pl.BlockSpec((1, tm, w), lambda b, s: (b, s, 0))
    return pl.pallas_call(
        functools.partial(_mlp_kernel, tm=tm),
        grid=(B, S // tm),
        in_specs=[
            row(D), row(RET_WIDTH), row(RET_WIDTH),
            pl.BlockSpec((1, N_MOD, D), lambda b, s: (b, 0, 0)),
            _const_spec(w_out.shape), _const_spec(ln1w.shape), _const_spec(ln1b.shape),
            _const_spec(w_up.shape), _const_spec(conv_w.shape), _const_spec(conv_b.shape),
            _const_spec(w_down.shape), _const_spec(ln2w.shape), _const_spec(ln2b.shape),
        ],
        out_specs=row(D),
        out_shape=jax.ShapeDtypeStruct((B, S, D), F32),
        scratch_shapes=[
            pltpu.VMEM((SUBLANES, 2 * D_FF), F32),
            pltpu.VMEM((2, FF_CHUNK // LANES, SUBLANES + tm, LANES), F32),
            pltpu.VMEM((tm, D_FF), BF16),
        ],
        compiler_params=pltpu.CompilerParams(
            dimension_semantics=("arbitrary", "arbitrary"), vmem_limit_bytes=VMEM_LIMIT),
        name="outproj_mlp",
    )(x, ret, mla, mod, w_out, ln1w, ln1b, w_up, conv_w, conv_b, w_down, ln2w, ln2b)


def _rope_tables(seq, half, reps):
    pos = np.arange(seq, dtype=np.float32)
    inv = np.float32(ROPE_BASE) ** (-np.arange(half, dtype=np.float32) / np.float32(half))
    ang = (pos[:, None] * inv[None, :]).astype(np.float32)
    cos = np.cos(ang).astype(np.float32)
    sin = np.sin(ang).astype(np.float32)
    cos_t = np.tile(cos, (1, 2 * reps))
    sin_t = np.concatenate([np.tile(-sin, (1, reps)), np.tile(sin, (1, reps))], axis=-1)
    return jnp.asarray(cos_t), jnp.asarray(sin_t)


def _uq_columns():
    half = MLA_ROPE_DIM // 2
    nope = [h * MLA_QK_DIM + np.arange(HEAD_DIM) for h in range(N_HEADS)]
    rope = []
    for p in range(N_HEADS // 2):
        for part in range(2):
            for h in (2 * p, 2 * p + 1):
                rope.append(h * MLA_QK_DIM + HEAD_DIM + part * half + np.arange(half))
    return np.concatenate(nope + rope)


def _take_columns(w, cols):
    cols = np.asarray(cols)
    cuts = np.flatnonzero(np.diff(cols) != 1) + 1
    runs = np.split(cols, cuts)
    return jnp.concatenate([w[:, r[0]:r[-1] + 1] for r in runs], axis=1)


def kernel(x, c, w_ada, b_ada, w_in, ret_gn_w, mla_q_norm_w, w_uq, mla_kv_norm_w, w_ukv, w_out,
           ln1_w, ln1_b, w_up, conv_w, conv_b, w_down, ln2_w, ln2_b):
    B, S, D = x.shape
    assert D == D_MODEL and S % ROW_TILE == 0 and S % ATTN_TQ == 0 and w_ada.shape[0] == DEPTH == 1
    l = 0

    c_pad = jnp.pad(c, ((0, SUBLANES - B), (0, 0)))
    mod = _ada_call(c_pad, w_ada[l], b_ada[l][None, :])[:B].reshape(B, N_MOD, D)

    w_uq_p = _take_columns(w_uq[l].astype(BF16), _uq_columns())
    w_ukv_p = w_ukv[l].astype(BF16)
    cosr, sinr = _rope_tables(S, HEAD_DIM // 2, 1)
    cosm, sinm = _rope_tables(S, MLA_ROPE_DIM // 2, 2)

    ret, q, k, vt = _inproj_call(
        x, mod, w_in[l], ret_gn_w[l][None, :], mla_q_norm_w[l][None, :], mla_kv_norm_w[l][None, :],
        w_uq_p, w_ukv_p, cosr, sinr, cosm, sinm)
    mla = _attn_call(q, k, vt)
    return _mlp_call(
        x, ret, mla, mod, w_out[l].astype(BF16), ln1_w[l][None, :], ln1_b[l][None, :],
        w_up[l].astype(BF16), conv_w[l], conv_b[l][None, :], w_down[l].astype(BF16),
        ln2_w[l][None, :], ln2_b[l][None, :])
```

```python
import functools
import math

import numpy as np
import jax
import jax.numpy as jnp
from jax import lax
from jax.experimental import pallas as pl
from jax.experimental.pallas import tpu as pltpu

F32 = jnp.float32
BF16 = jnp.bfloat16

D_MODEL = 1024
DEPTH = 1
N_HEADS = 4
HEAD_DIM = 128
RET_WIDTH = N_HEADS * HEAD_DIM
LAT_COL0 = 4 * RET_WIDTH
MLA_RANK = 256
MLA_ROPE_DIM = 64
MLA_QK_DIM = HEAD_DIM + MLA_ROPE_DIM
D_FF = 2816
ROPE_BASE = 10000.0
LN_EPS = 1e-5
RMS_EPS = 1e-6
DN_ALPHA = (2.0 * DEPTH) ** 0.25
N_MOD = 6
LOG_G = [math.log1p(-(2.0 ** (-5.0 - h))) for h in range(N_HEADS)]

LANES = 128
SUBLANES = 8
MXU_DIM = 256
VMEM_LIMIT = 56 * 1024 * 1024

ROW_TILE = 512
ATTN_TQ = 1024
ATTN_TK = 512
FF_CHUNK = MXU_DIM
N_FF_CHUNKS = D_FF // FF_CHUNK
STAGE_ROWS_SQ = 256
STAGE_ROWS_UP = 64
QK_PAD = 2 * LANES
BF16_ROWS = 2 * SUBLANES
ACC_ROWS = HEAD_DIM + BF16_ROWS
NEG_BIG = -1e30


def _ln_plain(x):
    mu = jnp.mean(x, axis=-1, keepdims=True)
    xc = x - mu
    var = jnp.mean(xc * xc, axis=-1, keepdims=True)
    return xc * lax.rsqrt(var + LN_EPS)


def _rms(x):
    return x * lax.rsqrt(jnp.mean(x * x, axis=-1, keepdims=True) + RMS_EPS)


def _silu(x):
    return x * jax.nn.sigmoid(x)


def _dot(a, b):
    return jnp.dot(a, b, preferred_element_type=F32)


def _dot_nt(a, b):
    return lax.dot_general(a, b, (((1,), (1,)), ((), ())), preferred_element_type=F32)


def _dot_tn(a, b):
    return lax.dot_general(a, b, (((0,), (0,)), ((), ())), preferred_element_type=F32)


def _ada_kernel(c_ref, w_ref, b_ref, o_ref):
    cond = _silu(c_ref[...])
    o_ref[...] = _dot(cond.astype(BF16), w_ref[...].astype(BF16)) + b_ref[...]


def _ada_call(c_pad, w_ada, b_ada):
    n_out = w_ada.shape[1]
    return pl.pallas_call(
        _ada_kernel,
        grid=(n_out // D_MODEL,),
        in_specs=[
            pl.BlockSpec((SUBLANES, D_MODEL), lambda j: (0, 0)),
            pl.BlockSpec((D_MODEL, D_MODEL), lambda j: (0, j)),
            pl.BlockSpec((1, D_MODEL), lambda j: (0, j)),
        ],
        out_specs=pl.BlockSpec((SUBLANES, D_MODEL), lambda j: (0, j)),
        out_shape=jax.ShapeDtypeStruct((SUBLANES, n_out), F32),
        compiler_params=pltpu.CompilerParams(dimension_semantics=("arbitrary",)),
        name="ada_mod",
    )(c_pad, w_ada, b_ada)


def _rope128(x, cos, sin_signed):
    return x * cos + pltpu.roll(x, LANES // 2, axis=1) * sin_signed


def _decay_matrix(hd, c):
    row = lax.broadcasted_iota(jnp.int32, (c, c), 0)
    col = lax.broadcasted_iota(jnp.int32, (c, c), 1)
    rel = (row - col).astype(F32)
    return jnp.where(rel >= 0.0, jnp.exp(LOG_G[hd] * jnp.maximum(rel, 0.0)), 0.0)


def _retention_head(hd, q, k, v, gate, gnw, decay_ref, state_ref, ret_ref):
    c = q.shape[0]
    lg = LOG_G[hd]
    n = lax.broadcasted_iota(jnp.int32, (c, 1), 0).astype(F32)
    q_w = jnp.exp(lg * (n + 1.0))
    k_w = jnp.exp(lg * (c - 1.0 - n))
    qb = q.astype(BF16)
    vb = v.astype(BF16)
    scores = _dot_nt(qb, k.astype(BF16)) * decay_ref[hd]
    inner = _dot(scores.astype(BF16), vb)
    state = state_ref[hd]
    cross = _dot(qb, state.astype(BF16)) * q_w
    state_ref[hd] = state * math.exp(lg * c) + _dot_tn((k * k_w).astype(BF16), vb)
    o = _ln_plain(inner + cross) * gnw
    ret_ref[0, :, hd * HEAD_DIM:(hd + 1) * HEAD_DIM] = (o * gate).astype(BF16)


def _stage_w_in(win32_ref, win_ref):
    base = LAT_COL0 + 2 * MLA_RANK
    quarter = LANES // 4
    step = 2 * LANES
    for r0 in range(0, win32_ref.shape[0], step):
        rows = slice(r0, r0 + step)
        win_ref[rows, 0:base] = win32_ref[rows, 0:base].astype(BF16)
        tail = win32_ref[rows, base:base + MLA_ROPE_DIM]
        t2 = jnp.concatenate([tail, tail], axis=1)
        lane = lax.broadcasted_iota(jnp.int32, t2.shape, 1)
        middle = jnp.abs(2 * lane - (LANES - 1)) < 2 * quarter
        win_ref[rows, base:base + LANES] = jnp.where(middle, pltpu.roll(t2, quarter, axis=1), t2).astype(BF16)


def _inproj_kernel(x_ref, mod_ref, win32_ref, gnw_ref, qnw_ref, kvnw_ref, wuq_ref, wukv_ref,
                   cosr_ref, sinr_ref, cosm_ref, sinm_ref,
                   ret_ref, q_ref, k_ref, vt_ref, state_ref, decay_ref, win_ref, *, q_scale, rk_scale):
    @pl.when(jnp.logical_and(pl.program_id(0) == 0, pl.program_id(1) == 0))
    def _():
        _stage_w_in(win32_ref.at[0], win_ref)
        for hd in range(N_HEADS):
            decay_ref[hd] = _decay_matrix(hd, decay_ref.shape[1])

    @pl.when(pl.program_id(1) == 0)
    def _():
        state_ref[...] = jnp.zeros_like(state_ref)

    x = x_ref[0]
    shift = mod_ref[0, 0:1, :]
    scale = mod_ref[0, 1:2, :]
    h = (_ln_plain(x) * (1.0 + scale) + shift).astype(BF16)

    cosr = cosr_ref[...]
    sinr = sinr_ref[...]
    cosm = cosm_ref[...]
    sinm = sinm_ref[...]

    pq = _dot(h, win_ref[:, 0:RET_WIDTH])
    pk = _dot(h, win_ref[:, RET_WIDTH:2 * RET_WIDTH])
    pv = _dot(h, win_ref[:, 2 * RET_WIDTH:3 * RET_WIDTH])
    pg = _silu(_dot(h, win_ref[:, 3 * RET_WIDTH:LAT_COL0]))
    for hd in range(N_HEADS):
        sl = slice(hd * HEAD_DIM, (hd + 1) * HEAD_DIM)
        _retention_head(hd, _rope128(pq[:, sl], cosr, sinr), _rope128(pk[:, sl], cosr, sinr) * rk_scale,
                        pv[:, sl], pg[:, sl], gnw_ref[:, sl], decay_ref, state_ref, ret_ref)

    lat = _dot(h, win_ref[:, LAT_COL0:])
    cq = (_rms(lat[:, 0:MLA_RANK]) * qnw_ref[...]).astype(BF16)
    ckv = (_rms(lat[:, MLA_RANK:2 * MLA_RANK]) * kvnw_ref[...]).astype(BF16)
    kr = _rope128(lat[:, 2 * MLA_RANK:2 * MLA_RANK + LANES], cosm, sinm)
    lane = lax.broadcasted_iota(jnp.int32, kr.shape, 1)
    first_of_pair = (lane % (LANES // 2)) < (LANES // 4)
    kr_even = jnp.where(first_of_pair, kr, 0.0).astype(BF16)
    kr_odd = jnp.where(first_of_pair, 0.0, kr).astype(BF16)

    qf = _dot(cq, wuq_ref[...])
    kvf = _dot(ckv, wukv_ref[...])
    for hd in range(N_HEADS):
        vt_ref[0, hd] = kvf[:, (2 * hd + 1) * HEAD_DIM:(2 * hd + 2) * HEAD_DIM].T.astype(BF16)
    for p in range(N_HEADS // 2):
        qr = qf[:, RET_WIDTH + p * LANES:RET_WIDTH + (p + 1) * LANES]
        qr = (_rope128(qr, cosm, sinm) * q_scale).astype(BF16)
        for hd in (2 * p, 2 * p + 1):
            sl = slice(hd * HEAD_DIM, (hd + 1) * HEAD_DIM)
            q_ref[0, :, hd * QK_PAD:hd * QK_PAD + LANES] = (qf[:, sl] * q_scale).astype(BF16)
            q_ref[0, :, hd * QK_PAD + LANES:(hd + 1) * QK_PAD] = qr
            k_ref[0, :, hd * QK_PAD:hd * QK_PAD + LANES] = kvf[:, 2 * hd * HEAD_DIM:(2 * hd + 1) * HEAD_DIM].astype(BF16)
            k_ref[0, :, hd * QK_PAD + LANES:(hd + 1) * QK_PAD] = kr_even if hd % 2 == 0 else kr_odd


def _const_spec(shape):
    nd = len(shape)
    return pl.BlockSpec(shape, lambda *_: (0,) * nd, pipeline_mode=pl.Buffered(1))


def _inproj_call(x, mod, w_in, gnw, qnw, kvnw, w_uq, w_ukv, cosr, sinr, cosm, sinm):
    B, S, D = x.shape
    tm = ROW_TILE
    q_scale = (MLA_QK_DIM ** -0.5) * math.log2(math.e)
    rk_scale = HEAD_DIM ** -0.5
    row = lambda w: pl.BlockSpec((1, tm, w), lambda b, s: (b, s, 0))
    tab = pl.BlockSpec((tm, LANES), lambda b, s: (s, 0))
    out_shapes = (
        [jax.ShapeDtypeStruct((B, S, RET_WIDTH), BF16)]
        + [jax.ShapeDtypeStruct((B, S, N_HEADS * QK_PAD), BF16)] * 2
        + [jax.ShapeDtypeStruct((B, N_HEADS, HEAD_DIM, S), BF16)]
    )
    vt_spec = pl.BlockSpec((1, N_HEADS, HEAD_DIM, tm), lambda b, s: (b, 0, 0, s))
    return pl.pallas_call(
        functools.partial(_inproj_kernel, q_scale=q_scale, rk_scale=rk_scale),
        grid=(B, S // tm),
        in_specs=[
            row(D),
            pl.BlockSpec((1, N_MOD, D), lambda b, s: (b, 0, 0)),
            _const_spec(w_in.shape), _const_spec(gnw.shape),
            _const_spec(qnw.shape), _const_spec(kvnw.shape),
            _const_spec(w_uq.shape), _const_spec(w_ukv.shape),
            tab, tab, tab, tab,
        ],
        out_specs=[row(RET_WIDTH)] + [row(N_HEADS * QK_PAD)] * 2 + [vt_spec],
        out_shape=out_shapes,
        scratch_shapes=[
            pltpu.VMEM((N_HEADS, HEAD_DIM, HEAD_DIM), F32),
            pltpu.VMEM((N_HEADS, tm, tm), F32),
            pltpu.VMEM((D, LAT_COL0 + 2 * MLA_RANK + LANES), BF16),
        ],
        compiler_params=pltpu.CompilerParams(
            dimension_semantics=("arbitrary", "arbitrary"), vmem_limit_bytes=VMEM_LIMIT),
        name="inproj_retention",
    )(x, mod, w_in, gnw, qnw, kvnw, w_uq, w_ukv, cosr, sinr, cosm, sinm)


def _attn_kernel(q_ref, k_ref, vt_ref, o_ref, qt_ref, s_buf, p_buf, acc_ref, *, tq, tk):
    qi = pl.program_id(2)
    qt_ref[...] = q_ref[0].T

    ones = jnp.ones((BF16_ROWS, tk), BF16)

    def k_blk(j):
        return k_ref[0, pl.ds(pl.multiple_of(j * tk, tk), tk), :]

    def v_blk(j):
        return jnp.concatenate([vt_ref[0, 0, :, pl.ds(pl.multiple_of(j * tk, tk), tk)], ones], axis=0)

    def scores(j, slot):
        s_buf[slot] = _dot(k_blk(j), qt_ref[...])

    def value_update(j, slot, alpha):
        acc_ref[...] = alpha * acc_ref[...] + _dot(v_blk(j), p_buf[slot])

    def softmax(s, m):
        m_new = jnp.maximum(m, jnp.max(s, axis=0, keepdims=True))
        return m_new, jnp.exp2(m - m_new), jnp.exp2(s - m_new).astype(BF16)

    def two_blocks(u, carry):
        alpha_prev, m = carry
        t = 2 * u
        scores(t + 1, 1)
        value_update(jnp.maximum(t - 1, 0), 1, alpha_prev)
        m, alpha_even, p_buf[0] = softmax(s_buf[0], m)
        scores(t + 2, 0)
        value_update(t, 0, alpha_even)
        m, alpha_odd, p_buf[1] = softmax(s_buf[1], m)
        return alpha_odd, m

    scores(0, 0)
    p_buf[1] = jnp.zeros((tk, tq), BF16)
    acc_ref[...] = jnp.zeros_like(acc_ref)
    init = (jnp.ones((1, tq), F32), jnp.full((1, tq), NEG_BIG, F32))
    alpha_prev, m = lax.fori_loop(0, qi, two_blocks, init)

    t = 2 * qi
    key = lax.broadcasted_iota(jnp.int32, (tk, tk), 0)
    qry = lax.broadcasted_iota(jnp.int32, (tk, tk), 1)
    causal = key <= qry
    s_right = _dot(k_blk(t + 1), qt_ref[:, tk:])
    value_update(jnp.maximum(t - 1, 0), 1, alpha_prev)
    s = s_buf[0]
    s = jnp.concatenate([jnp.where(causal, s[:, :tk], NEG_BIG), s[:, tk:]], axis=1)
    m, alpha, p_buf[0] = softmax(s, m)
    value_update(t, 0, alpha)
    _, alpha_r, p_r = softmax(jnp.where(causal, s_right, NEG_BIG), m[:, tk:])
    acc_ref[:, tk:] = alpha_r * acc_ref[:, tk:] + _dot(v_blk(t + 1), p_r)
    acc = acc_ref[...]
    o_ref[0] = (acc[:HEAD_DIM] / acc[HEAD_DIM:HEAD_DIM + 1]).T.astype(BF16)


def _attn_call(q, k, vt):
    B, S, _ = q.shape
    tq, tk = ATTN_TQ, ATTN_TK
    assert tq == 2 * tk
    return pl.pallas_call(
        functools.partial(_attn_kernel, tq=tq, tk=tk),
        grid=(B, N_HEADS, S // tq),
        in_specs=[
            pl.BlockSpec((1, tq, QK_PAD), lambda b, h, i: (b, i, h)),
            pl.BlockSpec((1, S, QK_PAD), lambda b, h, i: (b, 0, h)),
            pl.BlockSpec((1, 1, HEAD_DIM, S), lambda b, h, i: (b, h, 0, 0)),
        ],
        out_specs=pl.BlockSpec((1, tq, HEAD_DIM), lambda b, h, i: (b, i, h)),
        out_shape=jax.ShapeDtypeStruct((B, S, N_HEADS * HEAD_DIM), BF16),
        scratch_shapes=[
            pltpu.VMEM((QK_PAD, tq), BF16),
            pltpu.VMEM((2, tk, tq), F32),
            pltpu.VMEM((2, tk, tq), BF16),
            pltpu.VMEM((ACC_ROWS, tq), F32),
        ],
        compiler_params=pltpu.CompilerParams(
            dimension_semantics=("arbitrary", "arbitrary", "arbitrary"), vmem_limit_bytes=VMEM_LIMIT),
        name="mla_attn",
    )(q, k, vt)


def _stream_cast(src_hbm, dst_ref, stage_ref, sem):
    rows = stage_ref.shape[1]
    n_chunks = src_hbm.shape[0] // rows
    assert n_chunks * rows == src_hbm.shape[0]

    def copy(k, slot):
        return pltpu.make_async_copy(
            src_hbm.at[pl.ds(pl.multiple_of(k * rows, rows), rows), :], stage_ref.at[slot], sem.at[slot])

    copy(0, 0).start()

    def body(k, _):
        slot = k % 2

        @pl.when(k + 1 < n_chunks)
        def _():
            copy(k + 1, 1 - slot).start()

        copy(k, slot).wait()
        dst_ref[pl.ds(pl.multiple_of(k * rows, rows), rows), :] = stage_ref[slot].astype(BF16)
        return 0

    lax.fori_loop(0, n_chunks, body, 0)


def _mlp_kernel(x_ref, ret_ref, mla_ref, mod_ref, wo32_hbm, ln1w_ref, ln1b_ref, wup32_hbm, cw_ref, cb_ref,
                wdn32_hbm, ln2w_ref, ln2b_ref, o_ref, carry_ref, ubuf_ref, act_ref,
                wo_ref, wup_ref, wdn_ref, stage_sq_ref, stage_up_ref, sem, *, tm):
    @pl.when(jnp.logical_and(pl.program_id(0) == 0, pl.program_id(1) == 0))
    def _():
        _stream_cast(wo32_hbm.at[0], wo_ref, stage_sq_ref, sem)
        _stream_cast(wup32_hbm.at[0], wup_ref, stage_up_ref, sem)
        _stream_cast(wdn32_hbm.at[0], wdn_ref, stage_sq_ref, sem)

    @pl.when(pl.program_id(1) == 0)
    def _():
        carry_ref[...] = jnp.zeros_like(carry_ref)

    gate1 = mod_ref[0, 2:3, :]
    shift2 = mod_ref[0, 3:4, :]
    scale2 = mod_ref[0, 4:5, :]
    gate2 = mod_ref[0, 5:6, :]

    y = _dot(ret_ref[0], wo_ref[0:RET_WIDTH, :]) + _dot(mla_ref[0], wo_ref[RET_WIDTH:2 * RET_WIDTH, :])
    x1 = _ln_plain(DN_ALPHA * x_ref[0] + (1.0 + gate1) * y) * ln1w_ref[...] + ln1b_ref[...]
    h2 = (_ln_plain(x1) * (1.0 + scale2) + shift2).astype(BF16)

    def conv(u, col0, kind):
        outs = []
        for g in range(FF_CHUNK // LANES):
            cols = slice(col0 + g * LANES, col0 + (g + 1) * LANES)
            ug = u[:, g * LANES:(g + 1) * LANES]
            buf = ubuf_ref.at[kind, g]
            buf[0:SUBLANES, :] = carry_ref[:, cols]
            buf[SUBLANES:SUBLANES + tm, :] = ug
            carry_ref[:, cols] = ug[tm - SUBLANES:tm, :]
            u1 = buf[SUBLANES - 1:SUBLANES - 1 + tm, :]
            u2 = buf[SUBLANES - 2:SUBLANES - 2 + tm, :]
            outs.append(cb_ref[:, cols] + cw_ref[2:3, cols] * ug + cw_ref[1:2, cols] * u1
                        + cw_ref[0:1, cols] * u2)
        return jnp.concatenate(outs, axis=1)

    for c in range(N_FF_CHUNKS):
        gcol0 = c * FF_CHUNK
        vcol0 = D_FF + c * FF_CHUNK
        g = conv(_dot(h2, wup_ref[:, gcol0:gcol0 + FF_CHUNK]), gcol0, 0)
        val = conv(_dot(h2, wup_ref[:, vcol0:vcol0 + FF_CHUNK]), vcol0, 1)
        act_ref[:, gcol0:gcol0 + FF_CHUNK] = (_silu(g) * val).astype(BF16)

    y2 = _dot(act_ref[...], wdn_ref[...])
    o_ref[0] = _ln_plain(DN_ALPHA * x1 + (1.0 + gate2) * y2) * ln2w_ref[...] + ln2b_ref[...]


def _mlp_call(x, ret, mla, mod, w_out, ln1w, ln1b, w_up, conv_w, conv_b, w_down, ln2w, ln2b):
    B, S, D = x.shape
    tm = ROW_TILE
    row = lambda w: pl.BlockSpec((1, tm, w), lambda b, s: (b, s, 0))
    hbm = pl.BlockSpec(memory_space=pl.ANY)
    return pl.pallas_call(
        functools.partial(_mlp_kernel, tm=tm),
        grid=(B, S // tm),
        in_specs=[
            row(D), row(RET_WIDTH), row(RET_WIDTH),
            pl.BlockSpec((1, N_MOD, D), lambda b, s: (b, 0, 0)),
            hbm, _const_spec(ln1w.shape), _const_spec(ln1b.shape),
            hbm, _const_spec(conv_w.shape), _const_spec(conv_b.shape),
            hbm, _const_spec(ln2w.shape), _const_spec(ln2b.shape),
        ],
        out_specs=row(D),
        out_shape=jax.ShapeDtypeStruct((B, S, D), F32),
        scratch_shapes=[
            pltpu.VMEM((SUBLANES, 2 * D_FF), F32),
            pltpu.VMEM((2, FF_CHUNK // LANES, SUBLANES + tm, LANES), F32),
            pltpu.VMEM((tm, D_FF), BF16),
            pltpu.VMEM(w_out.shape[1:], BF16),
            pltpu.VMEM(w_up.shape[1:], BF16),
            pltpu.VMEM(w_down.shape[1:], BF16),
            pltpu.VMEM((2, STAGE_ROWS_SQ, D), F32),
            pltpu.VMEM((2, STAGE_ROWS_UP, 2 * D_FF), F32),
            pltpu.SemaphoreType.DMA((2,)),
        ],
        compiler_params=pltpu.CompilerParams(
            dimension_semantics=("arbitrary", "arbitrary"), vmem_limit_bytes=VMEM_LIMIT),
        name="outproj_mlp",
    )(x, ret, mla, mod, w_out, ln1w, ln1b, w_up, conv_w, conv_b, w_down, ln2w, ln2b)


def _rope_tables(seq, half, reps):
    pos = np.arange(seq, dtype=np.float32)
    inv = np.float32(ROPE_BASE) ** (-np.arange(half, dtype=np.float32) / np.float32(half))
    ang = (pos[:, None] * inv[None, :]).astype(np.float32)
    cos = np.cos(ang).astype(np.float32)
    sin = np.sin(ang).astype(np.float32)
    cos_t = np.tile(cos, (1, 2 * reps))
    sin_t = np.concatenate([np.tile(-sin, (1, reps)), np.tile(sin, (1, reps))], axis=-1)
    return jnp.asarray(cos_t), jnp.asarray(sin_t)


def _uq_columns():
    half = MLA_ROPE_DIM // 2
    nope = [h * MLA_QK_DIM + np.arange(HEAD_DIM) for h in range(N_HEADS)]
    rope = []
    for p in range(N_HEADS // 2):
        for part in range(2):
            for h in (2 * p, 2 * p + 1):
                rope.append(h * MLA_QK_DIM + HEAD_DIM + part * half + np.arange(half))
    return np.concatenate(nope + rope)


def _take_columns(w, cols):
    cols = np.asarray(cols)
    cuts = np.flatnonzero(np.diff(cols) != 1) + 1
    runs = np.split(cols, cuts)
    return jnp.concatenate([w[:, r[0]:r[-1] + 1] for r in runs], axis=1)


def kernel(x, c, w_ada, b_ada, w_in, ret_gn_w, mla_q_norm_w, w_uq, mla_kv_norm_w, w_ukv, w_out,
           ln1_w, ln1_b, w_up, conv_w, conv_b, w_down, ln2_w, ln2_b):
    B, S, D = x.shape
    assert D == D_MODEL and S % ROW_TILE == 0 and S % ATTN_TQ == 0 and w_ada.shape[0] == DEPTH == 1
    l = 0

    c_pad = jnp.pad(c, ((0, SUBLANES - B), (0, 0)))
    mod = _ada_call(c_pad, w_ada[l], b_ada[l][None, :])[:B].reshape(B, N_MOD, D)

    w_uq_p = _take_columns(w_uq[l].astype(BF16), _uq_columns())
    w_ukv_p = w_ukv[l].astype(BF16)
    cosr, sinr = _rope_tables(S, HEAD_DIM // 2, 1)
    cosm, sinm = _rope_tables(S, MLA_ROPE_DIM // 2, 2)

    ret, q, k, vt = _inproj_call(
        x, mod, w_in, ret_gn_w[l][None, :], mla_q_norm_w[l][None, :], mla_kv_norm_w[l][None, :],
        w_uq_p, w_ukv_p, cosr, sinr, cosm, sinm)
    mla = _attn_call(q, k, vt)
    return _mlp_call(
        x, ret, mla, mod, w_out, ln1_w[l][None, :], ln1_b[l][None, :],
        w_up, conv_w[l], conv_b[l][None, :], w_down, ln2_w[l][None, :], ln2_b[l][None, :])
```

```python
import functools
import math

import numpy as np
import jax
import jax.numpy as jnp
from jax import lax
from jax.experimental import pallas as pl
from jax.experimental.pallas import tpu as pltpu

F32 = jnp.float32
BF16 = jnp.bfloat16

D_MODEL = 1024
DEPTH = 1
N_HEADS = 4
HEAD_DIM = 128
RET_WIDTH = N_HEADS * HEAD_DIM
LAT_COL0 = 4 * RET_WIDTH
MLA_RANK = 256
MLA_ROPE_DIM = 64
MLA_QK_DIM = HEAD_DIM + MLA_ROPE_DIM
D_FF = 2816
ROPE_BASE = 10000.0
LN_EPS = 1e-5
RMS_EPS = 1e-6
DN_ALPHA = (2.0 * DEPTH) ** 0.25
N_MOD = 6
LOG_G = [math.log1p(-(2.0 ** (-5.0 - h))) for h in range(N_HEADS)]

LANES = 128
SUBLANES = 8
MXU_DIM = 256
VMEM_LIMIT = 56 * 1024 * 1024

ROW_TILE = 512
ATTN_TQ = 1024
ATTN_TK = 512
FF_CHUNK = MXU_DIM
N_FF_CHUNKS = D_FF // FF_CHUNK
STAGE_ROWS_SQ = 256
STAGE_ROWS_UP = 64
STAGE_ROWS_IN = 128
STAGE_SLOTS = 4
QK_PAD = 2 * LANES
BF16_ROWS = 2 * SUBLANES
ACC_ROWS = HEAD_DIM + BF16_ROWS
NEG_BIG = -1e30


def _ln_plain(x):
    mu = jnp.mean(x, axis=-1, keepdims=True)
    xc = x - mu
    var = jnp.mean(xc * xc, axis=-1, keepdims=True)
    return xc * lax.rsqrt(var + LN_EPS)


def _rms(x):
    return x * lax.rsqrt(jnp.mean(x * x, axis=-1, keepdims=True) + RMS_EPS)


def _silu(x):
    return x * jax.nn.sigmoid(x)


def _dot(a, b):
    return jnp.dot(a, b, preferred_element_type=F32)


def _dot_nt(a, b):
    return lax.dot_general(a, b, (((1,), (1,)), ((), ())), preferred_element_type=F32)


def _dot_tn(a, b):
    return lax.dot_general(a, b, (((0,), (0,)), ((), ())), preferred_element_type=F32)


def _ada_kernel(c_ref, w_ref, b_ref, o_ref):
    cond = _silu(c_ref[...])
    o_ref[...] = _dot(cond.astype(BF16), w_ref[...].astype(BF16)) + b_ref[...]


def _ada_call(c_pad, w_ada, b_ada):
    n_out = w_ada.shape[1]
    return pl.pallas_call(
        _ada_kernel,
        grid=(n_out // D_MODEL,),
        in_specs=[
            pl.BlockSpec((SUBLANES, D_MODEL), lambda j: (0, 0)),
            pl.BlockSpec((D_MODEL, D_MODEL), lambda j: (0, j)),
            pl.BlockSpec((1, D_MODEL), lambda j: (0, j)),
        ],
        out_specs=pl.BlockSpec((SUBLANES, D_MODEL), lambda j: (0, j)),
        out_shape=jax.ShapeDtypeStruct((SUBLANES, n_out), F32),
        compiler_params=pltpu.CompilerParams(dimension_semantics=("arbitrary",)),
        name="ada_mod",
    )(c_pad, w_ada, b_ada)


def _rope128(x, cos, sin_signed):
    return x * cos + pltpu.roll(x, LANES // 2, axis=1) * sin_signed


def _decay_matrix(hd, c):
    row = lax.broadcasted_iota(jnp.int32, (c, c), 0)
    col = lax.broadcasted_iota(jnp.int32, (c, c), 1)
    rel = (row - col).astype(F32)
    return jnp.where(rel >= 0.0, jnp.exp(LOG_G[hd] * jnp.maximum(rel, 0.0)), 0.0)


def _retention_head(hd, q, k, v, gate, gnw, decay_ref, state_ref, ret_ref):
    c = q.shape[0]
    lg = LOG_G[hd]
    n = lax.broadcasted_iota(jnp.int32, (c, 1), 0).astype(F32)
    q_w = jnp.exp(lg * (n + 1.0))
    k_w = jnp.exp(lg * (c - 1.0 - n))
    qb = q.astype(BF16)
    vb = v.astype(BF16)
    scores = _dot_nt(qb, k.astype(BF16)) * decay_ref[hd]
    inner = _dot(scores.astype(BF16), vb)
    state = state_ref[hd]
    cross = _dot(qb, state.astype(BF16)) * q_w
    state_ref[hd] = state * math.exp(lg * c) + _dot_tn((k * k_w).astype(BF16), vb)
    o = _ln_plain(inner + cross) * gnw
    ret_ref[0, :, hd * HEAD_DIM:(hd + 1) * HEAD_DIM] = (o * gate).astype(BF16)


def _stream_cast(src_hbm, stage_ref, sem, store):
    slots, rows = stage_ref.shape[0], stage_ref.shape[1]
    n_chunks = src_hbm.shape[0] // rows
    assert n_chunks * rows == src_hbm.shape[0] and n_chunks >= slots

    def copy(k):
        slot = k % slots
        return pltpu.make_async_copy(
            src_hbm.at[pl.ds(pl.multiple_of(k * rows, rows), rows), :], stage_ref.at[slot], sem.at[slot])

    for k in range(slots - 1):
        copy(k).start()

    def body(k, _):
        @pl.when(k + slots - 1 < n_chunks)
        def _():
            copy(k + slots - 1).start()

        copy(k).wait()
        store(pl.ds(pl.multiple_of(k * rows, rows), rows), stage_ref[k % slots])
        return 0

    lax.fori_loop(0, n_chunks, body, 0)


def _store_bf16(dst_ref):
    def store(rows, chunk):
        dst_ref[rows, :] = chunk.astype(BF16)
    return store


def _store_w_in(win_ref):
    base = LAT_COL0 + 2 * MLA_RANK
    quarter = LANES // 4

    def store(rows, chunk):
        win_ref[rows, 0:base] = chunk[:, 0:base].astype(BF16)
        tail = chunk[:, base:base + MLA_ROPE_DIM]
        t2 = jnp.concatenate([tail, tail], axis=1)
        lane = lax.broadcasted_iota(jnp.int32, t2.shape, 1)
        middle = jnp.abs(2 * lane - (LANES - 1)) < 2 * quarter
        win_ref[rows, base:base + LANES] = jnp.where(middle, pltpu.roll(t2, quarter, axis=1), t2).astype(BF16)
    return store


def _inproj_kernel(x_ref, mod_ref, win32_hbm, gnw_ref, qnw_ref, kvnw_ref, wuq_ref, wukv_ref,
                   cosr_ref, sinr_ref, cosm_ref, sinm_ref,
                   ret_ref, q_ref, k_ref, vt_ref, state_ref, decay_ref, win_ref, stage_ref, sem,
                   *, q_scale, rk_scale):
    @pl.when(jnp.logical_and(pl.program_id(0) == 0, pl.program_id(1) == 0))
    def _():
        _stream_cast(win32_hbm.at[0], stage_ref, sem, _store_w_in(win_ref))
        for hd in range(N_HEADS):
            decay_ref[hd] = _decay_matrix(hd, decay_ref.shape[1])

    @pl.when(pl.program_id(1) == 0)
    def _():
        state_ref[...] = jnp.zeros_like(state_ref)

    x = x_ref[0]
    shift = mod_ref[0, 0:1, :]
    scale = mod_ref[0, 1:2, :]
    h = (_ln_plain(x) * (1.0 + scale) + shift).astype(BF16)

    cosr = cosr_ref[...]
    sinr = sinr_ref[...]
    cosm = cosm_ref[...]
    sinm = sinm_ref[...]

    pq = _dot(h, win_ref[:, 0:RET_WIDTH])
    pk = _dot(h, win_ref[:, RET_WIDTH:2 * RET_WIDTH])
    pv = _dot(h, win_ref[:, 2 * RET_WIDTH:3 * RET_WIDTH])
    pg = _silu(_dot(h, win_ref[:, 3 * RET_WIDTH:LAT_COL0]))
    for hd in range(N_HEADS):
        sl = slice(hd * HEAD_DIM, (hd + 1) * HEAD_DIM)
        _retention_head(hd, _rope128(pq[:, sl], cosr, sinr), _rope128(pk[:, sl], cosr, sinr) * rk_scale,
                        pv[:, sl], pg[:, sl], gnw_ref[:, sl], decay_ref, state_ref, ret_ref)

    lat = _dot(h, win_ref[:, LAT_COL0:])
    cq = (_rms(lat[:, 0:MLA_RANK]) * qnw_ref[...]).astype(BF16)
    ckv = (_rms(lat[:, MLA_RANK:2 * MLA_RANK]) * kvnw_ref[...]).astype(BF16)
    kr = _rope128(lat[:, 2 * MLA_RANK:2 * MLA_RANK + LANES], cosm, sinm)
    lane = lax.broadcasted_iota(jnp.int32, kr.shape, 1)
    first_of_pair = (lane % (LANES // 2)) < (LANES // 4)
    kr_even = jnp.where(first_of_pair, kr, 0.0).astype(BF16)
    kr_odd = jnp.where(first_of_pair, 0.0, kr).astype(BF16)

    qf = _dot(cq, wuq_ref[...])
    kvf = _dot(ckv, wukv_ref[...])
    for hd in range(N_HEADS):
        vt_ref[0, hd] = kvf[:, (2 * hd + 1) * HEAD_DIM:(2 * hd + 2) * HEAD_DIM].T.astype(BF16)
    for p in range(N_HEADS // 2):
        qr = qf[:, RET_WIDTH + p * LANES:RET_WIDTH + (p + 1) * LANES]
        qr = (_rope128(qr, cosm, sinm) * q_scale).astype(BF16)
        for hd in (2 * p, 2 * p + 1):
            sl = slice(hd * HEAD_DIM, (hd + 1) * HEAD_DIM)
            q_ref[0, :, hd * QK_PAD:hd * QK_PAD + LANES] = (qf[:, sl] * q_scale).astype(BF16)
            q_ref[0, :, hd * QK_PAD + LANES:(hd + 1) * QK_PAD] = qr
            k_ref[0, :, hd * QK_PAD:hd * QK_PAD + LANES] = kvf[:, 2 * hd * HEAD_DIM:(2 * hd + 1) * HEAD_DIM].astype(BF16)
            k_ref[0, :, hd * QK_PAD + LANES:(hd + 1) * QK_PAD] = kr_even if hd % 2 == 0 else kr_odd


def _const_spec(shape):
    nd = len(shape)
    return pl.BlockSpec(shape, lambda *_: (0,) * nd, pipeline_mode=pl.Buffered(1))


def _inproj_call(x, mod, w_in, gnw, qnw, kvnw, w_uq, w_ukv, cosr, sinr, cosm, sinm):
    B, S, D = x.shape
    tm = ROW_TILE
    q_scale = (MLA_QK_DIM ** -0.5) * math.log2(math.e)
    rk_scale = HEAD_DIM ** -0.5
    row = lambda w: pl.BlockSpec((1, tm, w), lambda b, s: (b, s, 0))
    tab = pl.BlockSpec((tm, LANES), lambda b, s: (s, 0))
    out_shapes = (
        [jax.ShapeDtypeStruct((B, S, RET_WIDTH), BF16)]
        + [jax.ShapeDtypeStruct((B, S, N_HEADS * QK_PAD), BF16)] * 2
        + [jax.ShapeDtypeStruct((B, N_HEADS, HEAD_DIM, S), BF16)]
    )
    vt_spec = pl.BlockSpec((1, N_HEADS, HEAD_DIM, tm), lambda b, s: (b, 0, 0, s))
    return pl.pallas_call(
        functools.partial(_inproj_kernel, q_scale=q_scale, rk_scale=rk_scale),
        grid=(B, S // tm),
        in_specs=[
            row(D),
            pl.BlockSpec((1, N_MOD, D), lambda b, s: (b, 0, 0)),
            pl.BlockSpec(memory_space=pl.ANY), _const_spec(gnw.shape),
            _const_spec(qnw.shape), _const_spec(kvnw.shape),
            _const_spec(w_uq.shape), _const_spec(w_ukv.shape),
            tab, tab, tab, tab,
        ],
        out_specs=[row(RET_WIDTH)] + [row(N_HEADS * QK_PAD)] * 2 + [vt_spec],
        out_shape=out_shapes,
        scratch_shapes=[
            pltpu.VMEM((N_HEADS, HEAD_DIM, HEAD_DIM), F32),
            pltpu.VMEM((N_HEADS, tm, tm), F32),
            pltpu.VMEM((D, LAT_COL0 + 2 * MLA_RANK + LANES), BF16),
            pltpu.VMEM((STAGE_SLOTS, STAGE_ROWS_IN, w_in.shape[-1]), F32),
            pltpu.SemaphoreType.DMA((STAGE_SLOTS,)),
        ],
        compiler_params=pltpu.CompilerParams(
            dimension_semantics=("arbitrary", "arbitrary"), vmem_limit_bytes=VMEM_LIMIT),
        name="inproj_retention",
    )(x, mod, w_in, gnw, qnw, kvnw, w_uq, w_ukv, cosr, sinr, cosm, sinm)


def _attn_kernel(q_ref, k_ref, vt_ref, o_ref, qt_ref, s_buf, p_buf, acc_ref, *, tq, tk):
    qi = pl.program_id(2)
    qt_ref[...] = q_ref[0].T

    ones = jnp.ones((BF16_ROWS, tk), BF16)

    def k_blk(j):
        return k_ref[0, pl.ds(pl.multiple_of(j * tk, tk), tk), :]

    def v_blk(j):
        return jnp.concatenate([vt_ref[0, 0, :, pl.ds(pl.multiple_of(j * tk, tk), tk)], ones], axis=0)

    def scores(j, slot):
        s_buf[slot] = _dot(k_blk(j), qt_ref[...])

    def value_update(j, slot, alpha):
        acc_ref[...] = alpha * acc_ref[...] + _dot(v_blk(j), p_buf[slot])

    def softmax(s, m):
        m_new = jnp.maximum(m, jnp.max(s, axis=0, keepdims=True))
        return m_new, jnp.exp2(m - m_new), jnp.exp2(s - m_new).astype(BF16)

    def two_blocks(u, carry):
        alpha_prev, m = carry
        t = 2 * u
        scores(t + 1, 1)
        value_update(jnp.maximum(t - 1, 0), 1, alpha_prev)
        m, alpha_even, p_buf[0] = softmax(s_buf[0], m)
        scores(t + 2, 0)
        value_update(t, 0, alpha_even)
        m, alpha_odd, p_buf[1] = softmax(s_buf[1], m)
        return alpha_odd, m

    scores(0, 0)
    p_buf[1] = jnp.zeros((tk, tq), BF16)
    acc_ref[...] = jnp.zeros_like(acc_ref)
    init = (jnp.ones((1, tq), F32), jnp.full((1, tq), NEG_BIG, F32))
    alpha_prev, m = lax.fori_loop(0, qi, two_blocks, init)

    t = 2 * qi
    key = lax.broadcasted_iota(jnp.int32, (tk, tk), 0)
    qry = lax.broadcasted_iota(jnp.int32, (tk, tk), 1)
    causal = key <= qry
    s_right = _dot(k_blk(t + 1), qt_ref[:, tk:])
    value_update(jnp.maximum(t - 1, 0), 1, alpha_prev)
    s = s_buf[0]
    s = jnp.concatenate([jnp.where(causal, s[:, :tk], NEG_BIG), s[:, tk:]], axis=1)
    m, alpha, p_buf[0] = softmax(s, m)
    value_update(t, 0, alpha)
    _, alpha_r, p_r = softmax(jnp.where(causal, s_right, NEG_BIG), m[:, tk:])
    acc_ref[:, tk:] = alpha_r * acc_ref[:, tk:] + _dot(v_blk(t + 1), p_r)
    acc = acc_ref[...]
    o_ref[0] = (acc[:HEAD_DIM] / acc[HEAD_DIM:HEAD_DIM + 1]).T.astype(BF16)


def _attn_call(q, k, vt):
    B, S, _ = q.shape
    tq, tk = ATTN_TQ, ATTN_TK
    assert tq == 2 * tk
    return pl.pallas_call(
        functools.partial(_attn_kernel, tq=tq, tk=tk),
        grid=(B, N_HEADS, S // tq),
        in_specs=[
            pl.BlockSpec((1, tq, QK_PAD), lambda b, h, i: (b, i, h)),
            pl.BlockSpec((1, S, QK_PAD), lambda b, h, i: (b, 0, h)),
            pl.BlockSpec((1, 1, HEAD_DIM, S), lambda b, h, i: (b, h, 0, 0)),
        ],
        out_specs=pl.BlockSpec((1, tq, HEAD_DIM), lambda b, h, i: (b, i, h)),
        out_shape=jax.ShapeDtypeStruct((B, S, N_HEADS * HEAD_DIM), BF16),
        scratch_shapes=[
            pltpu.VMEM((QK_PAD, tq), BF16),
            pltpu.VMEM((2, tk, tq), F32),
            pltpu.VMEM((2, tk, tq), BF16),
            pltpu.VMEM((ACC_ROWS, tq), F32),
        ],
        compiler_params=pltpu.CompilerParams(
            dimension_semantics=("arbitrary", "arbitrary", "arbitrary"), vmem_limit_bytes=VMEM_LIMIT),
        name="mla_attn",
    )(q, k, vt)


def _mlp_kernel(x_ref, ret_ref, mla_ref, mod_ref, wo32_hbm, ln1w_ref, ln1b_ref, wup32_hbm, cw_ref, cb_ref,
                wdn32_hbm, ln2w_ref, ln2b_ref, o_ref, carry_ref, ubuf_ref, act_ref,
                wo_ref, wup_ref, wdn_ref, stage_sq_ref, stage_up_ref, sem, *, tm):
    @pl.when(jnp.logical_and(pl.program_id(0) == 0, pl.program_id(1) == 0))
    def _():
        _stream_cast(wo32_hbm.at[0], stage_sq_ref, sem, _store_bf16(wo_ref))
        _stream_cast(wup32_hbm.at[0], stage_up_ref, sem, _store_bf16(wup_ref))
        _stream_cast(wdn32_hbm.at[0], stage_sq_ref, sem, _store_bf16(wdn_ref))

    @pl.when(pl.program_id(1) == 0)
    def _():
        carry_ref[...] = jnp.zeros_like(carry_ref)

    gate1 = mod_ref[0, 2:3, :]
    shift2 = mod_ref[0, 3:4, :]
    scale2 = mod_ref[0, 4:5, :]
    gate2 = mod_ref[0, 5:6, :]

    y = _dot(ret_ref[0], wo_ref[0:RET_WIDTH, :]) + _dot(mla_ref[0], wo_ref[RET_WIDTH:2 * RET_WIDTH, :])
    x1 = _ln_plain(DN_ALPHA * x_ref[0] + (1.0 + gate1) * y) * ln1w_ref[...] + ln1b_ref[...]
    h2 = (_ln_plain(x1) * (1.0 + scale2) + shift2).astype(BF16)

    def conv(u, col0, kind):
        outs = []
        for g in range(FF_CHUNK // LANES):
            cols = slice(col0 + g * LANES, col0 + (g + 1) * LANES)
            ug = u[:, g * LANES:(g + 1) * LANES]
            buf = ubuf_ref.at[kind, g]
            buf[0:SUBLANES, :] = carry_ref[:, cols]
            buf[SUBLANES:SUBLANES + tm, :] = ug
            carry_ref[:, cols] = ug[tm - SUBLANES:tm, :]
            u1 = buf[SUBLANES - 1:SUBLANES - 1 + tm, :]
            u2 = buf[SUBLANES - 2:SUBLANES - 2 + tm, :]
            outs.append(cb_ref[:, cols] + cw_ref[2:3, cols] * ug + cw_ref[1:2, cols] * u1
                        + cw_ref[0:1, cols] * u2)
        return jnp.concatenate(outs, axis=1)

    for c in range(N_FF_CHUNKS):
        gcol0 = c * FF_CHUNK
        vcol0 = D_FF + c * FF_CHUNK
        g = conv(_dot(h2, wup_ref[:, gcol0:gcol0 + FF_CHUNK]), gcol0, 0)
        val = conv(_dot(h2, wup_ref[:, vcol0:vcol0 + FF_CHUNK]), vcol0, 1)
        act_ref[:, gcol0:gcol0 + FF_CHUNK] = (_silu(g) * val).astype(BF16)

    y2 = _dot(act_ref[...], wdn_ref[...])
    o_ref[0] = _ln_plain(DN_ALPHA * x1 + (1.0 + gate2) * y2) * ln2w_ref[...] + ln2b_ref[...]


def _mlp_call(x, ret, mla, mod, w_out, ln1w, ln1b, w_up, conv_w, conv_b, w_down, ln2w, ln2b):
    B, S, D = x.shape
    tm = ROW_TILE
    row = lambda w: pl.BlockSpec((1, tm, w), lambda b, s: (b, s, 0))
    hbm = pl.BlockSpec(memory_space=pl.ANY)
    return pl.pallas_call(
        functools.partial(_mlp_kernel, tm=tm),
        grid=(B, S // tm),
        in_specs=[
            row(D), row(RET_WIDTH), row(RET_WIDTH),
            pl.BlockSpec((1, N_MOD, D), lambda b, s: (b, 0, 0)),
            hbm, _const_spec(ln1w.shape), _const_spec(ln1b.shape),
            hbm, _const_spec(conv_w.shape), _const_spec(conv_b.shape),
            hbm, _const_spec(ln2w.shape), _const_spec(ln2b.shape),
        ],
        out_specs=row(D),
        out_shape=jax.ShapeDtypeStruct((B, S, D), F32),
        scratch_shapes=[
            pltpu.VMEM((SUBLANES, 2 * D_FF), F32),
            pltpu.VMEM((2, FF_CHUNK // LANES, SUBLANES + tm, LANES), F32),
            pltpu.VMEM((tm, D_FF), BF16),
            pltpu.VMEM(w_out.shape[1:], BF16),
            pltpu.VMEM(w_up.shape[1:], BF16),
            pltpu.VMEM(w_down.shape[1:], BF16),
            pltpu.VMEM((STAGE_SLOTS, STAGE_ROWS_SQ, D), F32),
            pltpu.VMEM((STAGE_SLOTS, STAGE_ROWS_UP, 2 * D_FF), F32),
            pltpu.SemaphoreType.DMA((STAGE_SLOTS,)),
        ],
        compiler_params=pltpu.CompilerParams(
            dimension_semantics=("arbitrary", "arbitrary"), vmem_limit_bytes=VMEM_LIMIT),
        name="outproj_mlp",
    )(x, ret, mla, mod, w_out, ln1w, ln1b, w_up, conv_w, conv_b, w_down, ln2w, ln2b)


def _rope_tables(seq, half, reps):
    pos = np.arange(seq, dtype=np.float32)
    inv = np.float32(ROPE_BASE) ** (-np.arange(half, dtype=np.float32) / np.float32(half))
    ang = (pos[:, None] * inv[None, :]).astype(np.float32)
    cos = np.cos(ang).astype(np.float32)
    sin = np.sin(ang).astype(np.float32)
    cos_t = np.tile(cos, (1, 2 * reps))
    sin_t = np.concatenate([np.tile(-sin, (1, reps)), np.tile(sin, (1, reps))], axis=-1)
    return jnp.asarray(cos_t), jnp.asarray(sin_t)


def _uq_columns():
    half = MLA_ROPE_DIM // 2
    nope = [h * MLA_QK_DIM + np.arange(HEAD_DIM) for h in range(N_HEADS)]
    rope = []
    for p in range(N_HEADS // 2):
        for part in range(2):
            for h in (2 * p, 2 * p + 1):
                rope.append(h * MLA_QK_DIM + HEAD_DIM + part * half + np.arange(half))
    return np.concatenate(nope + rope)


def _take_columns(w, cols):
    cols = np.asarray(cols)
    cuts = np.flatnonzero(np.diff(cols) != 1) + 1
    runs = np.split(cols, cuts)
    return jnp.concatenate([w[:, r[0]:r[-1] + 1] for r in runs], axis=1)


def kernel(x, c, w_ada, b_ada, w_in, ret_gn_w, mla_q_norm_w, w_uq, mla_kv_norm_w, w_ukv, w_out,
           ln1_w, ln1_b, w_up, conv_w, conv_b, w_down, ln2_w, ln2_b):
    B, S, D = x.shape
    assert D == D_MODEL and S % ROW_TILE == 0 and S % ATTN_TQ == 0 and w_ada.shape[0] == DEPTH == 1
    l = 0

    c_pad = jnp.pad(c, ((0, SUBLANES - B), (0, 0)))
    mod = _ada_call(c_pad, w_ada[l], b_ada[l][None, :])[:B].reshape(B, N_MOD, D)

    w_uq_p = _take_columns(w_uq[l].astype(BF16), _uq_columns())
    w_ukv_p = w_ukv[l].astype(BF16)
    cosr, sinr = _rope_tables(S, HEAD_DIM // 2, 1)
    cosm, sinm = _rope_tables(S, MLA_ROPE_DIM // 2, 2)

    ret, q, k, vt = _inproj_call(
        x, mod, w_in, ret_gn_w[l][None, :], mla_q_norm_w[l][None, :], mla_kv_norm_w[l][None, :],
        w_uq_p, w_ukv_p, cosr, sinr, cosm, sinm)
    mla = _attn_call(q, k, vt)
    return _mlp_call(
        x, ret, mla, mod, w_out, ln1_w[l][None, :], ln1_b[l][None, :],
        w_up, conv_w[l], conv_b[l][None, :], w_down, ln2_w[l][None, :], ln2_b[l][None, :])
```

```python
import functools
import math

import numpy as np
import jax
import jax.numpy as jnp
from jax import lax
from jax.experimental import pallas as pl
from jax.experimental.pallas import tpu as pltpu

F32 = jnp.float32
BF16 = jnp.bfloat16

D_MODEL = 1024
DEPTH = 1
N_HEADS = 4
HEAD_DIM = 128
RET_WIDTH = N_HEADS * HEAD_DIM
LAT_COL0 = 4 * RET_WIDTH
MLA_RANK = 256
MLA_ROPE_DIM = 64
MLA_QK_DIM = HEAD_DIM + MLA_ROPE_DIM
D_FF = 2816
ROPE_BASE = 10000.0
LN_EPS = 1e-5
RMS_EPS = 1e-6
DN_ALPHA = (2.0 * DEPTH) ** 0.25
N_MOD = 6
LOG_G = [math.log1p(-(2.0 ** (-5.0 - h))) for h in range(N_HEADS)]

LANES = 128
SUBLANES = 8
MXU_DIM = 256
VMEM_LIMIT = 56 * 1024 * 1024

ROW_TILE = 512
ATTN_TQ = 1024
ATTN_TK = 512
FF_CHUNK = MXU_DIM
N_FF_CHUNKS = D_FF // FF_CHUNK
STAGE_ROWS_SQ = 256
STAGE_ROWS_UP = 64
STAGE_ROWS_IN = 128
STAGE_SLOTS = 4
QK_PAD = 2 * LANES
BF16_ROWS = 2 * SUBLANES
ACC_ROWS = HEAD_DIM + BF16_ROWS
NEG_BIG = -1e30


def _ln_plain(x):
    mu = jnp.mean(x, axis=-1, keepdims=True)
    xc = x - mu
    var = jnp.mean(xc * xc, axis=-1, keepdims=True)
    return xc * lax.rsqrt(var + LN_EPS)


def _rms(x):
    return x * lax.rsqrt(jnp.mean(x * x, axis=-1, keepdims=True) + RMS_EPS)


def _silu(x):
    return x * jax.nn.sigmoid(x)


def _dot(a, b):
    return jnp.dot(a, b, preferred_element_type=F32)


def _dot_nt(a, b):
    return lax.dot_general(a, b, (((1,), (1,)), ((), ())), preferred_element_type=F32)


def _dot_tn(a, b):
    return lax.dot_general(a, b, (((0,), (0,)), ((), ())), preferred_element_type=F32)


def _ada_kernel(c_ref, w_ref, b_ref, o_ref):
    cond = _silu(c_ref[...])
    o_ref[...] = _dot(cond.astype(BF16), w_ref[...].astype(BF16)) + b_ref[...]


def _ada_call(c_pad, w_ada, b_ada):
    n_out = w_ada.shape[1]
    return pl.pallas_call(
        _ada_kernel,
        grid=(n_out // D_MODEL,),
        in_specs=[
            pl.BlockSpec((SUBLANES, D_MODEL), lambda j: (0, 0)),
            pl.BlockSpec((D_MODEL, D_MODEL), lambda j: (0, j)),
            pl.BlockSpec((1, D_MODEL), lambda j: (0, j)),
        ],
        out_specs=pl.BlockSpec((SUBLANES, D_MODEL), lambda j: (0, j)),
        out_shape=jax.ShapeDtypeStruct((SUBLANES, n_out), F32),
        compiler_params=pltpu.CompilerParams(dimension_semantics=("arbitrary",)),
        name="ada_mod",
    )(c_pad, w_ada, b_ada)


def _rope128(x, cos, sin_signed):
    return x * cos + pltpu.roll(x, LANES // 2, axis=1) * sin_signed


def _decay_matrix(hd, c):
    row = lax.broadcasted_iota(jnp.int32, (c, c), 0)
    col = lax.broadcasted_iota(jnp.int32, (c, c), 1)
    rel = (row - col).astype(F32)
    return jnp.where(rel >= 0.0, jnp.exp(LOG_G[hd] * jnp.maximum(rel, 0.0)), 0.0)


def _retention_head(hd, q, k, v, gate, gnw, decay_ref, state_ref, ret_ref):
    c = q.shape[0]
    lg = LOG_G[hd]
    n = lax.broadcasted_iota(jnp.int32, (c, 1), 0).astype(F32)
    q_w = jnp.exp(lg * (n + 1.0))
    k_w = jnp.exp(lg * (c - 1.0 - n))
    qb = q.astype(BF16)
    vb = v.astype(BF16)
    scores = _dot_nt(qb, k.astype(BF16)) * decay_ref[hd]
    inner = _dot(scores.astype(BF16), vb)
    state = state_ref[hd]
    cross = _dot(qb, state.astype(BF16)) * q_w
    state_ref[hd] = state * math.exp(lg * c) + _dot_tn((k * k_w).astype(BF16), vb)
    o = _ln_plain(inner + cross) * gnw
    ret_ref[0, :, hd * HEAD_DIM:(hd + 1) * HEAD_DIM] = (o * gate).astype(BF16)


def _stream_cast(src_hbm, stage_ref, sem, store, n_rows=None):
    slots, rows = stage_ref.shape[0], stage_ref.shape[1]
    n_rows = src_hbm.shape[0] if n_rows is None else n_rows
    n_chunks = n_rows // rows
    assert n_chunks * rows == n_rows and n_chunks >= slots

    def copy(k):
        slot = k % slots
        return pltpu.make_async_copy(
            src_hbm.at[pl.ds(pl.multiple_of(k * rows, rows), rows), :], stage_ref.at[slot], sem.at[slot])

    for k in range(slots - 1):
        copy(k).start()

    def body(k, _):
        @pl.when(k + slots - 1 < n_chunks)
        def _():
            copy(k + slots - 1).start()

        copy(k).wait()
        store(pl.ds(pl.multiple_of(k * rows, rows), rows), stage_ref[k % slots])
        return 0

    lax.fori_loop(0, n_chunks, body, 0)


def _store_bf16(dst_ref):
    def store(rows, chunk):
        dst_ref[rows, :] = chunk.astype(BF16)
    return store


def _stage_w_in(wt_hbm, win_ref, stage_ref, sem):
    base = LAT_COL0 + 2 * MLA_RANK
    quarter = LANES // 4

    def store(cols, chunk):
        win_ref[:, cols] = chunk.T.astype(BF16)

    _stream_cast(wt_hbm, stage_ref, sem, store, n_rows=base)

    tail_copy = pltpu.make_async_copy(
        wt_hbm.at[pl.ds(base, MLA_ROPE_DIM), :], stage_ref.at[0, pl.ds(0, MLA_ROPE_DIM), :], sem.at[0])
    tail_copy.start()
    tail_copy.wait()
    tail = stage_ref[0, 0:MLA_ROPE_DIM, :].T
    t2 = jnp.concatenate([tail, tail], axis=1)
    lane = lax.broadcasted_iota(jnp.int32, t2.shape, 1)
    middle = jnp.abs(2 * lane - (LANES - 1)) < 2 * quarter
    win_ref[:, base:base + LANES] = jnp.where(middle, pltpu.roll(t2, quarter, axis=1), t2).astype(BF16)


def _inproj_kernel(x_ref, mod_ref, wint_hbm, gnw_ref, qnw_ref, kvnw_ref, wuq_ref, wukv_ref,
                   cosr_ref, sinr_ref, cosm_ref, sinm_ref,
                   ret_ref, q_ref, k_ref, vt_ref, state_ref, decay_ref, win_ref, stage_ref, sem,
                   *, q_scale, rk_scale):
    @pl.when(jnp.logical_and(pl.program_id(0) == 0, pl.program_id(1) == 0))
    def _():
        _stage_w_in(wint_hbm.at[0], win_ref, stage_ref, sem)
        for hd in range(N_HEADS):
            decay_ref[hd] = _decay_matrix(hd, decay_ref.shape[1])

    @pl.when(pl.program_id(1) == 0)
    def _():
        state_ref[...] = jnp.zeros_like(state_ref)

    x = x_ref[0]
    shift = mod_ref[0, 0:1, :]
    scale = mod_ref[0, 1:2, :]
    h = (_ln_plain(x) * (1.0 + scale) + shift).astype(BF16)

    cosr = cosr_ref[...]
    sinr = sinr_ref[...]
    cosm = cosm_ref[...]
    sinm = sinm_ref[...]

    pq = _dot(h, win_ref[:, 0:RET_WIDTH])
    pk = _dot(h, win_ref[:, RET_WIDTH:2 * RET_WIDTH])
    pv = _dot(h, win_ref[:, 2 * RET_WIDTH:3 * RET_WIDTH])
    pg = _silu(_dot(h, win_ref[:, 3 * RET_WIDTH:LAT_COL0]))
    for hd in range(N_HEADS):
        sl = slice(hd * HEAD_DIM, (hd + 1) * HEAD_DIM)
        _retention_head(hd, _rope128(pq[:, sl], cosr, sinr), _rope128(pk[:, sl], cosr, sinr) * rk_scale,
                        pv[:, sl], pg[:, sl], gnw_ref[:, sl], decay_ref, state_ref, ret_ref)

    lat = _dot(h, win_ref[:, LAT_COL0:])
    cq = (_rms(lat[:, 0:MLA_RANK]) * qnw_ref[...]).astype(BF16)
    ckv = (_rms(lat[:, MLA_RANK:2 * MLA_RANK]) * kvnw_ref[...]).astype(BF16)
    kr = _rope128(lat[:, 2 * MLA_RANK:2 * MLA_RANK + LANES], cosm, sinm)
    lane = lax.broadcasted_iota(jnp.int32, kr.shape, 1)
    first_of_pair = (lane % (LANES // 2)) < (LANES // 4)
    kr_even = jnp.where(first_of_pair, kr, 0.0).astype(BF16)
    kr_odd = jnp.where(first_of_pair, 0.0, kr).astype(BF16)

    qf = _dot(cq, wuq_ref[...])
    kvf = _dot(ckv, wukv_ref[...])
    for hd in range(N_HEADS):
        vt_ref[0, hd] = kvf[:, (2 * hd + 1) * HEAD_DIM:(2 * hd + 2) * HEAD_DIM].T.astype(BF16)
    for p in range(N_HEADS // 2):
        qr = qf[:, RET_WIDTH + p * LANES:RET_WIDTH + (p + 1) * LANES]
        qr = (_rope128(qr, cosm, sinm) * q_scale).astype(BF16)
        for hd in (2 * p, 2 * p + 1):
            sl = slice(hd * HEAD_DIM, (hd + 1) * HEAD_DIM)
            q_ref[0, :, hd * QK_PAD:hd * QK_PAD + LANES] = (qf[:, sl] * q_scale).astype(BF16)
            q_ref[0, :, hd * QK_PAD + LANES:(hd + 1) * QK_PAD] = qr
            k_ref[0, :, hd * QK_PAD:hd * QK_PAD + LANES] = kvf[:, 2 * hd * HEAD_DIM:(2 * hd + 1) * HEAD_DIM].astype(BF16)
            k_ref[0, :, hd * QK_PAD + LANES:(hd + 1) * QK_PAD] = kr_even if hd % 2 == 0 else kr_odd


def _const_spec(shape):
    nd = len(shape)
    return pl.BlockSpec(shape, lambda *_: (0,) * nd, pipeline_mode=pl.Buffered(1))


def _inproj_call(x, mod, w_in, gnw, qnw, kvnw, w_uq, w_ukv, cosr, sinr, cosm, sinm):
    B, S, D = x.shape
    tm = ROW_TILE
    q_scale = (MLA_QK_DIM ** -0.5) * math.log2(math.e)
    rk_scale = HEAD_DIM ** -0.5
    row = lambda w: pl.BlockSpec((1, tm, w), lambda b, s: (b, s, 0))
    tab = pl.BlockSpec((tm, LANES), lambda b, s: (s, 0))
    out_shapes = (
        [jax.ShapeDtypeStruct((B, S, RET_WIDTH), BF16)]
        + [jax.ShapeDtypeStruct((B, S, N_HEADS * QK_PAD), BF16)] * 2
        + [jax.ShapeDtypeStruct((B, N_HEADS, HEAD_DIM, S), BF16)]
    )
    vt_spec = pl.BlockSpec((1, N_HEADS, HEAD_DIM, tm), lambda b, s: (b, 0, 0, s))
    return pl.pallas_call(
        functools.partial(_inproj_kernel, q_scale=q_scale, rk_scale=rk_scale),
        grid=(B, S // tm),
        in_specs=[
            row(D),
            pl.BlockSpec((1, N_MOD, D), lambda b, s: (b, 0, 0)),
            pl.BlockSpec(memory_space=pl.ANY), _const_spec(gnw.shape),
            _const_spec(qnw.shape), _const_spec(kvnw.shape),
            _const_spec(w_uq.shape), _const_spec(w_ukv.shape),
            tab, tab, tab, tab,
        ],
        out_specs=[row(RET_WIDTH)] + [row(N_HEADS * QK_PAD)] * 2 + [vt_spec],
        out_shape=out_shapes,
        scratch_shapes=[
            pltpu.VMEM((N_HEADS, HEAD_DIM, HEAD_DIM), F32),
            pltpu.VMEM((N_HEADS, tm, tm), F32),
            pltpu.VMEM((D, LAT_COL0 + 2 * MLA_RANK + LANES), BF16),
            pltpu.VMEM((STAGE_SLOTS, STAGE_ROWS_IN, w_in.shape[-1]), F32),
            pltpu.SemaphoreType.DMA((STAGE_SLOTS,)),
        ],
        compiler_params=pltpu.CompilerParams(
            dimension_semantics=("arbitrary", "arbitrary"), vmem_limit_bytes=VMEM_LIMIT),
        name="inproj_retention",
    )(x, mod, w_in, gnw, qnw, kvnw, w_uq, w_ukv, cosr, sinr, cosm, sinm)


def _attn_kernel(q_ref, k_ref, vt_ref, o_ref, qt_ref, s_buf, p_buf, acc_ref, *, tq, tk):
    qi = pl.program_id(2)
    qt_ref[...] = q_ref[0].T

    ones = jnp.ones((BF16_ROWS, tk), BF16)

    def k_blk(j):
        return k_ref[0, pl.ds(pl.multiple_of(j * tk, tk), tk), :]

    def v_blk(j):
        return jnp.concatenate([vt_ref[0, 0, :, pl.ds(pl.multiple_of(j * tk, tk), tk)], ones], axis=0)

    def scores(j, slot):
        s_buf[slot] = _dot(k_blk(j), qt_ref[...])

    def value_update(j, slot, alpha):
        acc_ref[...] = alpha * acc_ref[...] + _dot(v_blk(j), p_buf[slot])

    def softmax(s, m):
        m_new = jnp.maximum(m, jnp.max(s, axis=0, keepdims=True))
        return m_new, jnp.exp2(m - m_new), jnp.exp2(s - m_new).astype(BF16)

    def two_blocks(u, carry):
        alpha_prev, m = carry
        t = 2 * u
        scores(t + 1, 1)
        value_update(jnp.maximum(t - 1, 0), 1, alpha_prev)
        m, alpha_even, p_buf[0] = softmax(s_buf[0], m)
        scores(t + 2, 0)
        value_update(t, 0, alpha_even)
        m, alpha_odd, p_buf[1] = softmax(s_buf[1], m)
        return alpha_odd, m

    scores(0, 0)
    p_buf[1] = jnp.zeros((tk, tq), BF16)
    acc_ref[...] = jnp.zeros_like(acc_ref)
    init = (jnp.ones((1, tq), F32), jnp.full((1, tq), NEG_BIG, F32))
    alpha_prev, m = lax.fori_loop(0, qi, two_blocks, init)

    t = 2 * qi
    key = lax.broadcasted_iota(jnp.int32, (tk, tk), 0)
    qry = lax.broadcasted_iota(jnp.int32, (tk, tk), 1)
    causal = key <= qry
    s_right = _dot(k_blk(t + 1), qt_ref[:, tk:])
    value_update(jnp.maximum(t - 1, 0), 1, alpha_prev)
    s = s_buf[0]
    s = jnp.concatenate([jnp.where(causal, s[:, :tk], NEG_BIG), s[:, tk:]], axis=1)
    m, alpha, p_buf[0] = softmax(s, m)
    value_update(t, 0, alpha)
    _, alpha_r, p_r = softmax(jnp.where(causal, s_right, NEG_BIG), m[:, tk:])
    acc_ref[:, tk:] = alpha_r * acc_ref[:, tk:] + _dot(v_blk(t + 1), p_r)
    acc = acc_ref[...]
    o_ref[0] = (acc[:HEAD_DIM] / acc[HEAD_DIM:HEAD_DIM + 1]).T.astype(BF16)


def _attn_call(q, k, vt):
    B, S, _ = q.shape
    tq, tk = ATTN_TQ, ATTN_TK
    assert tq == 2 * tk
    return pl.pallas_call(
        functools.partial(_attn_kernel, tq=tq, tk=tk),
        grid=(B, N_HEADS, S // tq),
        in_specs=[
            pl.BlockSpec((1, tq, QK_PAD), lambda b, h, i: (b, i, h)),
            pl.BlockSpec((1, S, QK_PAD), lambda b, h, i: (b, 0, h)),
            pl.BlockSpec((1, 1, HEAD_DIM, S), lambda b, h, i: (b, h, 0, 0)),
        ],
        out_specs=pl.BlockSpec((1, tq, HEAD_DIM), lambda b, h, i: (b, i, h)),
        out_shape=jax.ShapeDtypeStruct((B, S, N_HEADS * HEAD_DIM), BF16),
        scratch_shapes=[
            pltpu.VMEM((QK_PAD, tq), BF16),
            pltpu.VMEM((2, tk, tq), F32),
            pltpu.VMEM((2, tk, tq), BF16),
            pltpu.VMEM((ACC_ROWS, tq), F32),
        ],
        compiler_params=pltpu.CompilerParams(
            dimension_semantics=("arbitrary", "arbitrary", "arbitrary"), vmem_limit_bytes=VMEM_LIMIT),
        name="mla_attn",
    )(q, k, vt)


def _mlp_kernel(x_ref, ret_ref, mla_ref, mod_ref, wo32_hbm, ln1w_ref, ln1b_ref, wup32_hbm, cw_ref, cb_ref,
                wdn32_hbm, ln2w_ref, ln2b_ref, o_ref, carry_ref, ubuf_ref, act_ref,
                wo_ref, wup_ref, wdn_ref, stage_sq_ref, stage_up_ref, sem, *, tm):
    @pl.when(jnp.logical_and(pl.program_id(0) == 0, pl.program_id(1) == 0))
    def _():
        _stream_cast(wo32_hbm.at[0], stage_sq_ref, sem, _store_bf16(wo_ref))
        _stream_cast(wup32_hbm.at[0], stage_up_ref, sem, _store_bf16(wup_ref))
        _stream_cast(wdn32_hbm.at[0], stage_sq_ref, sem, _store_bf16(wdn_ref))

    @pl.when(pl.program_id(1) == 0)
    def _():
        carry_ref[...] = jnp.zeros_like(carry_ref)

    gate1 = mod_ref[0, 2:3, :]
    shift2 = mod_ref[0, 3:4, :]
    scale2 = mod_ref[0, 4:5, :]
    gate2 = mod_ref[0, 5:6, :]

    y = _dot(ret_ref[0], wo_ref[0:RET_WIDTH, :]) + _dot(mla_ref[0], wo_ref[RET_WIDTH:2 * RET_WIDTH, :])
    x1 = _ln_plain(DN_ALPHA * x_ref[0] + (1.0 + gate1) * y) * ln1w_ref[...] + ln1b_ref[...]
    h2 = (_ln_plain(x1) * (1.0 + scale2) + shift2).astype(BF16)

    def conv(u, col0, kind):
        outs = []
        for g in range(FF_CHUNK // LANES):
            cols = slice(col0 + g * LANES, col0 + (g + 1) * LANES)
            ug = u[:, g * LANES:(g + 1) * LANES]
            buf = ubuf_ref.at[kind, g]
            buf[0:SUBLANES, :] = carry_ref[:, cols]
            buf[SUBLANES:SUBLANES + tm, :] = ug
            carry_ref[:, cols] = ug[tm - SUBLANES:tm, :]
            u1 = buf[SUBLANES - 1:SUBLANES - 1 + tm, :]
            u2 = buf[SUBLANES - 2:SUBLANES - 2 + tm, :]
            outs.append(cb_ref[:, cols] + cw_ref[2:3, cols] * ug + cw_ref[1:2, cols] * u1
                        + cw_ref[0:1, cols] * u2)
        return jnp.concatenate(outs, axis=1)

    for c in range(N_FF_CHUNKS):
        gcol0 = c * FF_CHUNK
        vcol0 = D_FF + c * FF_CHUNK
        g = conv(_dot(h2, wup_ref[:, gcol0:gcol0 + FF_CHUNK]), gcol0, 0)
        val = conv(_dot(h2, wup_ref[:, vcol0:vcol0 + FF_CHUNK]), vcol0, 1)
        act_ref[:, gcol0:gcol0 + FF_CHUNK] = (_silu(g) * val).astype(BF16)

    y2 = _dot(act_ref[...], wdn_ref[...])
    o_ref[0] = _ln_plain(DN_ALPHA * x1 + (1.0 + gate2) * y2) * ln2w_ref[...] + ln2b_ref[...]


def _mlp_call(x, ret, mla, mod, w_out, ln1w, ln1b, w_up, conv_w, conv_b, w_down, ln2w, ln2b):
    B, S, D = x.shape
    tm = ROW_TILE
    row = lambda w: pl.BlockSpec((1, tm, w), lambda b, s: (b, s, 0))
    hbm = pl.BlockSpec(memory_space=pl.ANY)
    return pl.pallas_call(
        functools.partial(_mlp_kernel, tm=tm),
        grid=(B, S // tm),
        in_specs=[
            row(D), row(RET_WIDTH), row(RET_WIDTH),
            pl.BlockSpec((1, N_MOD, D), lambda b, s: (b, 0, 0)),
            hbm, _const_spec(ln1w.shape), _const_spec(ln1b.shape),
            hbm, _const_spec(conv_w.shape), _const_spec(conv_b.shape),
            hbm, _const_spec(ln2w.shape), _const_spec(ln2b.shape),
        ],
        out_specs=row(D),
        out_shape=jax.ShapeDtypeStruct((B, S, D), F32),
        scratch_shapes=[
            pltpu.VMEM((SUBLANES, 2 * D_FF), F32),
            pltpu.VMEM((2, FF_CHUNK // LANES, SUBLANES + tm, LANES), F32),
            pltpu.VMEM((tm, D_FF), BF16),
            pltpu.VMEM(w_out.shape[1:], BF16),
            pltpu.VMEM(w_up.shape[1:], BF16),
            pltpu.VMEM(w_down.shape[1:], BF16),
            pltpu.VMEM((STAGE_SLOTS, STAGE_ROWS_SQ, D), F32),
            pltpu.VMEM((STAGE_SLOTS, STAGE_ROWS_UP, 2 * D_FF), F32),
            pltpu.SemaphoreType.DMA((STAGE_SLOTS,)),
        ],
        compiler_params=pltpu.CompilerParams(
            dimension_semantics=("arbitrary", "arbitrary"), vmem_limit_bytes=VMEM_LIMIT),
        name="outproj_mlp",
    )(x, ret, mla, mod, w_out, ln1w, ln1b, w_up, conv_w, conv_b, w_down, ln2w, ln2b)


def _rope_tables(seq, half, reps):
    pos = np.arange(seq, dtype=np.float32)
    inv = np.float32(ROPE_BASE) ** (-np.arange(half, dtype=np.float32) / np.float32(half))
    ang = (pos[:, None] * inv[None, :]).astype(np.float32)
    cos = np.cos(ang).astype(np.float32)
    sin = np.sin(ang).astype(np.float32)
    cos_t = np.tile(cos, (1, 2 * reps))
    sin_t = np.concatenate([np.tile(-sin, (1, reps)), np.tile(sin, (1, reps))], axis=-1)
    return jnp.asarray(cos_t), jnp.asarray(sin_t)


def _uq_columns():
    half = MLA_ROPE_DIM // 2
    nope = [h * MLA_QK_DIM + np.arange(HEAD_DIM) for h in range(N_HEADS)]
    rope = []
    for p in range(N_HEADS // 2):
        for part in range(2):
            for h in (2 * p, 2 * p + 1):
                rope.append(h * MLA_QK_DIM + HEAD_DIM + part * half + np.arange(half))
    return np.concatenate(nope + rope)


def _take_columns(w, cols):
    cols = np.asarray(cols)
    cuts = np.flatnonzero(np.diff(cols) != 1) + 1
    runs = np.split(cols, cuts)
    return jnp.concatenate([w[:, r[0]:r[-1] + 1] for r in runs], axis=1)


def kernel(x, c, w_ada, b_ada, w_in, ret_gn_w, mla_q_norm_w, w_uq, mla_kv_norm_w, w_ukv, w_out,
           ln1_w, ln1_b, w_up, conv_w, conv_b, w_down, ln2_w, ln2_b):
    B, S, D = x.shape
    assert D == D_MODEL and S % ROW_TILE == 0 and S % ATTN_TQ == 0 and w_ada.shape[0] == DEPTH == 1
    l = 0

    c_pad = jnp.pad(c, ((0, SUBLANES - B), (0, 0)))
    mod = _ada_call(c_pad, w_ada[l], b_ada[l][None, :])[:B].reshape(B, N_MOD, D)

    w_uq_p = _take_columns(w_uq[l].astype(BF16), _uq_columns())
    w_ukv_p = w_ukv[l].astype(BF16)
    cosr, sinr = _rope_tables(S, HEAD_DIM // 2, 1)
    cosm, sinm = _rope_tables(S, MLA_ROPE_DIM // 2, 2)

    ret, q, k, vt = _inproj_call(
        x, mod, jnp.swapaxes(w_in, 1, 2), ret_gn_w[l][None, :], mla_q_norm_w[l][None, :], mla_kv_norm_w[l][None, :],
        w_uq_p, w_ukv_p, cosr, sinr, cosm, sinm)
    mla = _attn_call(q, k, vt)
    return _mlp_call(
        x, ret, mla, mod, w_out, ln1_w[l][None, :], ln1_b[l][None, :],
        w_up, conv_w[l], conv_b[l][None, :], w_down, ln2_w[l][None, :], ln2_b[l][None, :])
```

```python
import functools
import math

import numpy as np
import jax
import jax.numpy as jnp
from jax import lax
from jax.experimental import pallas as pl
from jax.experimental.pallas import tpu as pltpu

F32 = jnp.float32
BF16 = jnp.bfloat16

D_MODEL = 1024
DEPTH = 1
N_HEADS = 4
HEAD_DIM = 128
RET_WIDTH = N_HEADS * HEAD_DIM
LAT_COL0 = 4 * RET_WIDTH
MLA_RANK = 256
MLA_ROPE_DIM = 64
MLA_QK_DIM = HEAD_DIM + MLA_ROPE_DIM
D_FF = 2816
ROPE_BASE = 10000.0
LN_EPS = 1e-5
RMS_EPS = 1e-6
DN_ALPHA = (2.0 * DEPTH) ** 0.25
N_MOD = 6
LOG_G = [math.log1p(-(2.0 ** (-5.0 - h))) for h in range(N_HEADS)]

LANES = 128
SUBLANES = 8
MXU_DIM = 256
VMEM_LIMIT = 56 * 1024 * 1024

ADA_ROWS = 256
ROW_TILE = 512
RET_CHUNK = MXU_DIM
ATTN_TQ = 1024
ATTN_TK = 512
FF_CHUNK = MXU_DIM
N_FF_CHUNKS = D_FF // FF_CHUNK
STAGE_ROWS_SQ = 256
STAGE_ROWS_UP = 64
STAGE_ROWS_IN = 128
STAGE_SLOTS = 4
QK_PAD = 2 * LANES
BF16_ROWS = 2 * SUBLANES
ACC_ROWS = HEAD_DIM + BF16_ROWS
MASKED = -float("inf")


def _ln_plain(x):
    mu = jnp.mean(x, axis=-1, keepdims=True)
    xc = x - mu
    var = jnp.mean(xc * xc, axis=-1, keepdims=True)
    return xc * lax.rsqrt(var + LN_EPS)


def _rms(x):
    return x * lax.rsqrt(jnp.mean(x * x, axis=-1, keepdims=True) + RMS_EPS)


def _silu(x):
    return x * jax.nn.sigmoid(x)


def _dot(a, b):
    return jnp.dot(a, b, preferred_element_type=F32)


def _dot_nt(a, b):
    return lax.dot_general(a, b, (((1,), (1,)), ((), ())), preferred_element_type=F32)


def _dot_tn(a, b):
    return lax.dot_general(a, b, (((0,), (0,)), ((), ())), preferred_element_type=F32)


def _ada_kernel(c_ref, w_ref, b_ref, o_ref):
    @pl.when(pl.program_id(0) == 0)
    def _():
        o_ref[...] = jnp.broadcast_to(b_ref[...], o_ref.shape)

    cond = _silu(c_ref[...])
    o_ref[...] += _dot(cond.astype(BF16), w_ref[0].astype(BF16))


def _ada_call(c, w_ada, b_ada):
    n_batch = c.shape[0]
    _, d_in, n_out = w_ada.shape
    return pl.pallas_call(
        _ada_kernel,
        grid=(d_in // ADA_ROWS,),
        in_specs=[
            pl.BlockSpec((n_batch, ADA_ROWS), lambda j: (0, j)),
            pl.BlockSpec((1, ADA_ROWS, n_out), lambda j: (0, j, 0)),
            pl.BlockSpec((1, n_out), lambda j: (0, 0)),
        ],
        out_specs=pl.BlockSpec((n_batch, n_out), lambda j: (0, 0)),
        out_shape=jax.ShapeDtypeStruct((n_batch, n_out), F32),
        compiler_params=pltpu.CompilerParams(dimension_semantics=("arbitrary",), vmem_limit_bytes=VMEM_LIMIT),
        name="ada_mod",
    )(c, w_ada, b_ada)


def _rope128(x, cos, sin_signed):
    return x * cos + pltpu.roll(x, LANES // 2, axis=1) * sin_signed


def _decay_matrix(hd, c):
    row = lax.broadcasted_iota(jnp.int32, (c, c), 0)
    col = lax.broadcasted_iota(jnp.int32, (c, c), 1)
    rel = (row - col).astype(F32)
    return jnp.where(rel >= 0.0, jnp.exp(LOG_G[hd] * jnp.maximum(rel, 0.0)), 0.0)


def _retention_head(hd, q, k, v, gate, gnw, decay_ref, state_ref, ret_ref):
    c = decay_ref.shape[1]
    lg = LOG_G[hd]
    n = lax.broadcasted_iota(jnp.int32, (c, 1), 0).astype(F32)
    q_w = jnp.exp(lg * (n + 1.0))
    k_w = jnp.exp(lg * (c - 1.0 - n))
    for i in range(q.shape[0] // c):
        rows = slice(i * c, (i + 1) * c)
        qb = q[rows].astype(BF16)
        vb = v[rows].astype(BF16)
        scores = _dot_nt(qb, k[rows].astype(BF16)) * decay_ref[hd]
        inner = _dot(scores.astype(BF16), vb)
        state = state_ref[hd]
        cross = _dot(qb, state.astype(BF16)) * q_w
        state_ref[hd] = state * math.exp(lg * c) + _dot_tn((k[rows] * k_w).astype(BF16), vb)
        o = _ln_plain(inner + cross) * gnw
        ret_ref[0, rows, hd * HEAD_DIM:(hd + 1) * HEAD_DIM] = (o * gate[rows]).astype(BF16)


def _stream_cast(src_hbm, stage_ref, sem, store, n_rows=None):
    slots, rows = stage_ref.shape[0], stage_ref.shape[1]
    n_rows = src_hbm.shape[0] if n_rows is None else n_rows
    n_chunks = n_rows // rows
    assert n_chunks * rows == n_rows and n_chunks >= slots

    def copy(k):
        slot = k % slots
        return pltpu.make_async_copy(
            src_hbm.at[pl.ds(pl.multiple_of(k * rows, rows), rows), :], stage_ref.at[slot], sem.at[slot])

    for k in range(slots - 1):
        copy(k).start()

    def body(k, _):
        @pl.when(k + slots - 1 < n_chunks)
        def _():
            copy(k + slots - 1).start()

        copy(k).wait()
        store(pl.ds(pl.multiple_of(k * rows, rows), rows), stage_ref[k % slots])
        return 0

    lax.fori_loop(0, n_chunks, body, 0)


def _store_bf16(dst_ref):
    def store(rows, chunk):
        dst_ref[rows, :] = chunk.astype(BF16)
    return store


def _stage_w_in(wt_hbm, win_ref, stage_ref, sem):
    base = LAT_COL0 + 2 * MLA_RANK
    quarter = LANES // 4

    def store(cols, chunk):
        win_ref[:, cols] = chunk.T.astype(BF16)

    _stream_cast(wt_hbm, stage_ref, sem, store, n_rows=base)

    tail_copy = pltpu.make_async_copy(
        wt_hbm.at[pl.ds(base, MLA_ROPE_DIM), :], stage_ref.at[0, pl.ds(0, MLA_ROPE_DIM), :], sem.at[0])
    tail_copy.start()
    tail_copy.wait()
    tail = stage_ref[0, 0:MLA_ROPE_DIM, :].T
    t2 = jnp.concatenate([tail, tail], axis=1)
    lane = lax.broadcasted_iota(jnp.int32, t2.shape, 1)
    middle = jnp.abs(2 * lane - (LANES - 1)) < 2 * quarter
    win_ref[:, base:base + LANES] = jnp.where(middle, pltpu.roll(t2, quarter, axis=1), t2).astype(BF16)


def _inproj_kernel(x_ref, mod_ref, wint_hbm, gnw_ref, qnw_ref, kvnw_ref, wuq_ref, wukv_ref,
                   cosr_ref, sinr_ref, cosm_ref, sinm_ref,
                   ret_ref, q_ref, k_ref, vt_ref, state_ref, decay_ref, win_ref, stage_ref, sem,
                   *, q_scale, rk_scale):
    @pl.when(jnp.logical_and(pl.program_id(0) == 0, pl.program_id(1) == 0))
    def _():
        _stage_w_in(wint_hbm.at[0], win_ref, stage_ref, sem)
        for hd in range(N_HEADS):
            decay_ref[hd] = _decay_matrix(hd, decay_ref.shape[1])

    @pl.when(pl.program_id(1) == 0)
    def _():
        state_ref[...] = jnp.zeros_like(state_ref)

    x = x_ref[0]
    shift = mod_ref[0, 0:1, :]
    scale = mod_ref[0, 1:2, :]
    h = (_ln_plain(x) * (1.0 + scale) + shift).astype(BF16)

    cosr = cosr_ref[...]
    sinr = sinr_ref[...]
    cosm = cosm_ref[...]
    sinm = sinm_ref[...]

    pq = _dot(h, win_ref[:, 0:RET_WIDTH])
    pk = _dot(h, win_ref[:, RET_WIDTH:2 * RET_WIDTH])
    pv = _dot(h, win_ref[:, 2 * RET_WIDTH:3 * RET_WIDTH])
    pg = _silu(_dot(h, win_ref[:, 3 * RET_WIDTH:LAT_COL0]))
    for hd in range(N_HEADS):
        sl = slice(hd * HEAD_DIM, (hd + 1) * HEAD_DIM)
        _retention_head(hd, _rope128(pq[:, sl], cosr, sinr), _rope128(pk[:, sl], cosr, sinr) * rk_scale,
                        pv[:, sl], pg[:, sl], gnw_ref[:, sl], decay_ref, state_ref, ret_ref)

    lat = _dot(h, win_ref[:, LAT_COL0:])
    cq = (_rms(lat[:, 0:MLA_RANK]) * qnw_ref[...]).astype(BF16)
    ckv = (_rms(lat[:, MLA_RANK:2 * MLA_RANK]) * kvnw_ref[...]).astype(BF16)
    kr = _rope128(lat[:, 2 * MLA_RANK:2 * MLA_RANK + LANES], cosm, sinm)
    lane = lax.broadcasted_iota(jnp.int32, kr.shape, 1)
    first_of_pair = (lane % (LANES // 2)) < (LANES // 4)
    kr_even = jnp.where(first_of_pair, kr, 0.0).astype(BF16)
    kr_odd = jnp.where(first_of_pair, 0.0, kr).astype(BF16)

    qf = _dot(cq, wuq_ref[...])
    kvf = _dot(ckv, wukv_ref[...])
    for hd in range(N_HEADS):
        vt_ref[0, hd] = kvf[:, (2 * hd + 1) * HEAD_DIM:(2 * hd + 2) * HEAD_DIM].T.astype(BF16)
    for p in range(N_HEADS // 2):
        qr = qf[:, RET_WIDTH + p * LANES:RET_WIDTH + (p + 1) * LANES]
        qr = (_rope128(qr, cosm, sinm) * q_scale).astype(BF16)
        for hd in (2 * p, 2 * p + 1):
            sl = slice(hd * HEAD_DIM, (hd + 1) * HEAD_DIM)
            q_ref[0, :, hd * QK_PAD:hd * QK_PAD + LANES] = (qf[:, sl] * q_scale).astype(BF16)
            q_ref[0, :, hd * QK_PAD + LANES:(hd + 1) * QK_PAD] = qr
            k_ref[0, :, hd * QK_PAD:hd * QK_PAD + LANES] = kvf[:, 2 * hd * HEAD_DIM:(2 * hd + 1) * HEAD_DIM].astype(BF16)
            k_ref[0, :, hd * QK_PAD + LANES:(hd + 1) * QK_PAD] = kr_even if hd % 2 == 0 else kr_odd


def _const_spec(shape):
    nd = len(shape)
    return pl.BlockSpec(shape, lambda *_: (0,) * nd, pipeline_mode=pl.Buffered(1))


def _inproj_call(x, mod, w_in, gnw, qnw, kvnw, w_uq, w_ukv, cosr, sinr, cosm, sinm):
    B, S, D = x.shape
    tm = ROW_TILE
    q_scale = (MLA_QK_DIM ** -0.5) * math.log2(math.e)
    rk_scale = HEAD_DIM ** -0.5
    row = lambda w: pl.BlockSpec((1, tm, w), lambda b, s: (b, s, 0))
    tab = pl.BlockSpec((tm, LANES), lambda b, s: (s, 0))
    out_shapes = (
        [jax.ShapeDtypeStruct((B, S, RET_WIDTH), BF16)]
        + [jax.ShapeDtypeStruct((B, S, N_HEADS * QK_PAD), BF16)] * 2
        + [jax.ShapeDtypeStruct((B, N_HEADS, HEAD_DIM, S), BF16)]
    )
    vt_spec = pl.BlockSpec((1, N_HEADS, HEAD_DIM, tm), lambda b, s: (b, 0, 0, s))
    return pl.pallas_call(
        functools.partial(_inproj_kernel, q_scale=q_scale, rk_scale=rk_scale),
        grid=(B, S // tm),
        in_specs=[
            row(D),
            pl.BlockSpec((1, N_MOD, D), lambda b, s: (b, 0, 0)),
            pl.BlockSpec(memory_space=pl.ANY), _const_spec(gnw.shape),
            _const_spec(qnw.shape), _const_spec(kvnw.shape),
            _const_spec(w_uq.shape), _const_spec(w_ukv.shape),
            tab, tab, tab, tab,
        ],
        out_specs=[row(RET_WIDTH)] + [row(N_HEADS * QK_PAD)] * 2 + [vt_spec],
        out_shape=out_shapes,
        scratch_shapes=[
            pltpu.VMEM((N_HEADS, HEAD_DIM, HEAD_DIM), F32),
            pltpu.VMEM((N_HEADS, RET_CHUNK, RET_CHUNK), F32),
            pltpu.VMEM((D, LAT_COL0 + 2 * MLA_RANK + LANES), BF16),
            pltpu.VMEM((STAGE_SLOTS, STAGE_ROWS_IN, w_in.shape[-1]), F32),
            pltpu.SemaphoreType.DMA((STAGE_SLOTS,)),
        ],
        compiler_params=pltpu.CompilerParams(
            dimension_semantics=("arbitrary", "arbitrary"), vmem_limit_bytes=VMEM_LIMIT),
        name="inproj_retention",
    )(x, mod, w_in, gnw, qnw, kvnw, w_uq, w_ukv, cosr, sinr, cosm, sinm)


def _attn_kernel(q_ref, k_ref, vt_ref, o_ref, qt_ref, s_buf, p_buf, acc_ref, *, tq, tk):
    qi = pl.program_id(2)
    qt_ref[...] = q_ref[0].T

    ones = jnp.ones((BF16_ROWS, tk), BF16)

    def k_blk(j):
        return k_ref[0, pl.ds(pl.multiple_of(j * tk, tk), tk), :]

    def v_blk(j):
        return jnp.concatenate([vt_ref[0, 0, :, pl.ds(pl.multiple_of(j * tk, tk), tk)], ones], axis=0)

    def scores(j, slot):
        s_buf[slot] = _dot(k_blk(j), qt_ref[...])

    def value_update(j, slot, alpha):
        acc_ref[...] = alpha * acc_ref[...] + _dot(v_blk(j), p_buf[slot])

    def softmax(s, m):
        m_new = jnp.maximum(m, jnp.max(s, axis=0, keepdims=True))
        return m_new, jnp.exp2(m - m_new), jnp.exp2(s - m_new).astype(BF16)

    def two_blocks(u, carry):
        alpha_prev, m = carry
        t = 2 * u
        scores(t + 1, 1)
        value_update(jnp.maximum(t - 1, 0), 1, alpha_prev)
        m, alpha_even, p_buf[0] = softmax(s_buf[0], m)
        scores(t + 2, 0)
        value_update(t, 0, alpha_even)
        m, alpha_odd, p_buf[1] = softmax(s_buf[1], m)
        return alpha_odd, m

    scores(0, 0)
    p_buf[1] = jnp.zeros((tk, tq), BF16)
    acc_ref[...] = jnp.zeros_like(acc_ref)
    init = (jnp.ones((1, tq), F32), jnp.full((1, tq), MASKED, F32))
    alpha_prev, m = lax.fori_loop(0, qi, two_blocks, init)

    t = 2 * qi
    key = lax.broadcasted_iota(jnp.int32, (tk, tk), 0)
    qry = lax.broadcasted_iota(jnp.int32, (tk, tk), 1)
    causal = key <= qry
    s_right = _dot(k_blk(t + 1), qt_ref[:, tk:])
    value_update(jnp.maximum(t - 1, 0), 1, alpha_prev)
    s = s_buf[0]
    s = jnp.concatenate([jnp.where(causal, s[:, :tk], MASKED), s[:, tk:]], axis=1)
    m, alpha, p_buf[0] = softmax(s, m)
    value_update(t, 0, alpha)
    _, alpha_r, p_r = softmax(jnp.where(causal, s_right, MASKED), m[:, tk:])
    acc_ref[:, tk:] = alpha_r * acc_ref[:, tk:] + _dot(v_blk(t + 1), p_r)
    acc = acc_ref[...]
    o_ref[0] = (acc[:HEAD_DIM] / acc[HEAD_DIM:HEAD_DIM + 1]).T.astype(BF16)


def _attn_call(q, k, vt):
    B, S, _ = q.shape
    tq, tk = ATTN_TQ, ATTN_TK
    assert tq == 2 * tk
    return pl.pallas_call(
        functools.partial(_attn_kernel, tq=tq, tk=tk),
        grid=(B, N_HEADS, S // tq),
        in_specs=[
            pl.BlockSpec((1, tq, QK_PAD), lambda b, h, i: (b, i, h)),
            pl.BlockSpec((1, S, QK_PAD), lambda b, h, i: (b, 0, h)),
            pl.BlockSpec((1, 1, HEAD_DIM, S), lambda b, h, i: (b, h, 0, 0)),
        ],
        out_specs=pl.BlockSpec((1, tq, HEAD_DIM), lambda b, h, i: (b, i, h)),
        out_shape=jax.ShapeDtypeStruct((B, S, N_HEADS * HEAD_DIM), BF16),
        scratch_shapes=[
            pltpu.VMEM((QK_PAD, tq), BF16),
            pltpu.VMEM((2, tk, tq), F32),
            pltpu.VMEM((2, tk, tq), BF16),
            pltpu.VMEM((ACC_ROWS, tq), F32),
        ],
        compiler_params=pltpu.CompilerParams(
            dimension_semantics=("arbitrary", "arbitrary", "arbitrary"), vmem_limit_bytes=VMEM_LIMIT),
        name="mla_attn",
    )(q, k, vt)


def _mlp_kernel(x_ref, ret_ref, mla_ref, mod_ref, wo32_hbm, ln1w_ref, ln1b_ref, wup32_hbm, cw_ref, cb_ref,
                wdn32_hbm, ln2w_ref, ln2b_ref, o_ref, carry_ref, ubuf_ref, act_ref,
                wo_ref, wup_ref, wdn_ref, stage_sq_ref, stage_up_ref, sem, *, tm):
    @pl.when(jnp.logical_and(pl.program_id(0) == 0, pl.program_id(1) == 0))
    def _():
        _stream_cast(wo32_hbm.at[0], stage_sq_ref, sem, _store_bf16(wo_ref))
        _stream_cast(wup32_hbm.at[0], stage_up_ref, sem, _store_bf16(wup_ref))
        _stream_cast(wdn32_hbm.at[0], stage_sq_ref, sem, _store_bf16(wdn_ref))

    @pl.when(pl.program_id(1) == 0)
    def _():
        carry_ref[...] = jnp.zeros_like(carry_ref)

    gate1 = mod_ref[0, 2:3, :]
    shift2 = mod_ref[0, 3:4, :]
    scale2 = mod_ref[0, 4:5, :]
    gate2 = mod_ref[0, 5:6, :]

    y = _dot(ret_ref[0], wo_ref[0:RET_WIDTH, :]) + _dot(mla_ref[0], wo_ref[RET_WIDTH:2 * RET_WIDTH, :])
    x1 = _ln_plain(DN_ALPHA * x_ref[0] + (1.0 + gate1) * y) * ln1w_ref[...] + ln1b_ref[...]
    h2 = (_ln_plain(x1) * (1.0 + scale2) + shift2).astype(BF16)

    def conv(u, col0, kind):
        outs = []
        for g in range(FF_CHUNK // LANES):
            cols = slice(col0 + g * LANES, col0 + (g + 1) * LANES)
            ug = u[:, g * LANES:(g + 1) * LANES]
            buf = ubuf_ref.at[kind, g]
            buf[0:SUBLANES, :] = carry_ref[:, cols]
            buf[SUBLANES:SUBLANES + tm, :] = ug
            carry_ref[:, cols] = ug[tm - SUBLANES:tm, :]
            u1 = buf[SUBLANES - 1:SUBLANES - 1 + tm, :]
            u2 = buf[SUBLANES - 2:SUBLANES - 2 + tm, :]
            outs.append(cb_ref[:, cols] + cw_ref[2:3, cols] * ug + cw_ref[1:2, cols] * u1
                        + cw_ref[0:1, cols] * u2)
        return jnp.concatenate(outs, axis=1)

    for c in range(N_FF_CHUNKS):
        gcol0 = c * FF_CHUNK
        vcol0 = D_FF + c * FF_CHUNK
        g = conv(_dot(h2, wup_ref[:, gcol0:gcol0 + FF_CHUNK]), gcol0, 0)
        val = conv(_dot(h2, wup_ref[:, vcol0:vcol0 + FF_CHUNK]), vcol0, 1)
        act_ref[:, gcol0:gcol0 + FF_CHUNK] = (_silu(g) * val).astype(BF16)

    y2 = _dot(act_ref[...], wdn_ref[...])
    o_ref[0] = _ln_plain(DN_ALPHA * x1 + (1.0 + gate2) * y2) * ln2w_ref[...] + ln2b_ref[...]


def _mlp_call(x, ret, mla, mod, w_out, ln1w, ln1b, w_up, conv_w, conv_b, w_down, ln2w, ln2b):
    B, S, D = x.shape
    tm = ROW_TILE
    row = lambda w: pl.BlockSpec((1, tm, w), lambda b, s: (b, s, 0))
    hbm = pl.BlockSpec(memory_space=pl.ANY)
    return pl.pallas_call(
        functools.partial(_mlp_kernel, tm=tm),
        grid=(B, S // tm),
        in_specs=[
            row(D), row(RET_WIDTH), row(RET_WIDTH),
            pl.BlockSpec((1, N_MOD, D), lambda b, s: (b, 0, 0)),
            hbm, _const_spec(ln1w.shape), _const_spec(ln1b.shape),
            hbm, _const_spec(conv_w.shape), _const_spec(conv_b.shape),
            hbm, _const_spec(ln2w.shape), _const_spec(ln2b.shape),
        ],
        out_specs=row(D),
        out_shape=jax.ShapeDtypeStruct((B, S, D), F32),
        scratch_shapes=[
            pltpu.VMEM((SUBLANES, 2 * D_FF), F32),
            pltpu.VMEM((2, FF_CHUNK // LANES, SUBLANES + tm, LANES), F32),
            pltpu.VMEM((tm, D_FF), BF16),
            pltpu.VMEM(w_out.shape[1:], BF16),
            pltpu.VMEM(w_up.shape[1:], BF16),
            pltpu.VMEM(w_down.shape[1:], BF16),
            pltpu.VMEM((STAGE_SLOTS, STAGE_ROWS_SQ, D), F32),
            pltpu.VMEM((STAGE_SLOTS, STAGE_ROWS_UP, 2 * D_FF), F32),
            pltpu.SemaphoreType.DMA((STAGE_SLOTS,)),
        ],
        compiler_params=pltpu.CompilerParams(
            dimension_semantics=("arbitrary", "arbitrary"), vmem_limit_bytes=VMEM_LIMIT),
        name="outproj_mlp",
    )(x, ret, mla, mod, w_out, ln1w, ln1b, w_up, conv_w, conv_b, w_down, ln2w, ln2b)


def _rope_tables(seq, half, reps):
    pos = np.arange(seq, dtype=np.float32)
    inv = np.float32(ROPE_BASE) ** (-np.arange(half, dtype=np.float32) / np.float32(half))
    ang = (pos[:, None] * inv[None, :]).astype(np.float32)
    cos = np.cos(ang).astype(np.float32)
    sin = np.sin(ang).astype(np.float32)
    cos_t = np.tile(cos, (1, 2 * reps))
    sin_t = np.concatenate([np.tile(-sin, (1, reps)), np.tile(sin, (1, reps))], axis=-1)
    return jnp.asarray(cos_t), jnp.asarray(sin_t)


def _uq_columns():
    half = MLA_ROPE_DIM // 2
    nope = [h * MLA_QK_DIM + np.arange(HEAD_DIM) for h in range(N_HEADS)]
    rope = []
    for p in range(N_HEADS // 2):
        for part in range(2):
            for h in (2 * p, 2 * p + 1):
                rope.append(h * MLA_QK_DIM + HEAD_DIM + part * half + np.arange(half))
    return np.concatenate(nope + rope)


def _take_columns(w, cols):
    cols = np.asarray(cols)
    cuts = np.flatnonzero(np.diff(cols) != 1) + 1
    runs = np.split(cols, cuts)
    return jnp.concatenate([w[:, r[0]:r[-1] + 1] for r in runs], axis=1)


def kernel(x, c, w_ada, b_ada, w_in, ret_gn_w, mla_q_norm_w, w_uq, mla_kv_norm_w, w_ukv, w_out,
           ln1_w, ln1_b, w_up, conv_w, conv_b, w_down, ln2_w, ln2_b):
    B, S, D = x.shape
    assert D == D_MODEL and S % ROW_TILE == 0 and S % ATTN_TQ == 0 and w_ada.shape[0] == DEPTH == 1
    l = 0

    mod = _ada_call(c, w_ada, b_ada[l][None, :]).reshape(B, N_MOD, D)

    w_uq_p = _take_columns(w_uq[l].astype(BF16), _uq_columns())
    w_ukv_p = w_ukv[l].astype(BF16)
    cosr, sinr = _rope_tables(S, HEAD_DIM // 2, 1)
    cosm, sinm = _rope_tables(S, MLA_ROPE_DIM // 2, 2)

    ret, q, k, vt = _inproj_call(
        x, mod, jnp.swapaxes(w_in, 1, 2), ret_gn_w[l][None, :], mla_q_norm_w[l][None, :], mla_kv_norm_w[l][None, :],
        w_uq_p, w_ukv_p, cosr, sinr, cosm, sinm)
    mla = _attn_call(q, k, vt)
    return _mlp_call(
        x, ret, mla, mod, w_out, ln1_w[l][None, :], ln1_b[l][None, :],
        w_up, conv_w[l], conv_b[l][None, :], w_down, ln2_w[l][None, :], ln2_b[l][None, :])
```

```python
import functools
import math

import numpy as np
import jax
import jax.numpy as jnp
from jax import lax
from jax.experimental import pallas as pl
from jax.experimental.pallas import tpu as pltpu

F32 = jnp.float32
BF16 = jnp.bfloat16

D_MODEL = 1024
DEPTH = 1
N_HEADS = 4
HEAD_DIM = 128
RET_WIDTH = N_HEADS * HEAD_DIM
LAT_COL0 = 4 * RET_WIDTH
MLA_RANK = 256
MLA_ROPE_DIM = 64
MLA_QK_DIM = HEAD_DIM + MLA_ROPE_DIM
D_FF = 2816
ROPE_BASE = 10000.0
LN_EPS = 1e-5
RMS_EPS = 1e-6
DN_ALPHA = (2.0 * DEPTH) ** 0.25
N_MOD = 6
LOG_G = [math.log1p(-(2.0 ** (-5.0 - h))) for h in range(N_HEADS)]

LANES = 128
SUBLANES = 8
MXU_DIM = 256
VMEM_BYTES = 64 * 1024 * 1024
VMEM_LIMIT = VMEM_BYTES * 7 // 8

ADA_ROWS = 256
ROW_TILE = 512
RET_CHUNK = MXU_DIM
ATTN_TQ = 1024
ATTN_TK = 512
FF_CHUNK = MXU_DIM
N_FF_CHUNKS = D_FF // FF_CHUNK
STAGE_ROWS_SQ = 256
STAGE_ROWS_UP = 64
STAGE_ROWS_IN = 128
STAGE_SLOTS = 4
QK_PAD = 2 * LANES
BF16_ROWS = 2 * SUBLANES
ACC_ROWS = HEAD_DIM + BF16_ROWS
MASKED = -float("inf")


def _ln_plain(x):
    mu = jnp.mean(x, axis=-1, keepdims=True)
    xc = x - mu
    var = jnp.mean(xc * xc, axis=-1, keepdims=True)
    return xc * lax.rsqrt(var + LN_EPS)


def _rms(x):
    return x * lax.rsqrt(jnp.mean(x * x, axis=-1, keepdims=True) + RMS_EPS)


def _silu(x):
    return x * jax.nn.sigmoid(x)


def _dot(a, b):
    return jnp.dot(a, b, preferred_element_type=F32)


def _dot_nt(a, b):
    return lax.dot_general(a, b, (((1,), (1,)), ((), ())), preferred_element_type=F32)


def _dot_tn(a, b):
    return lax.dot_general(a, b, (((0,), (0,)), ((), ())), preferred_element_type=F32)


def _ada_kernel(c_ref, w_ref, b_ref, o_ref):
    @pl.when(pl.program_id(0) == 0)
    def _():
        o_ref[...] = jnp.broadcast_to(b_ref[...], o_ref.shape)

    cond = _silu(c_ref[...])
    o_ref[...] += _dot(cond.astype(BF16), w_ref[0].astype(BF16))


def _ada_call(c, w_ada, b_ada):
    n_batch = c.shape[0]
    _, d_in, n_out = w_ada.shape
    return pl.pallas_call(
        _ada_kernel,
        grid=(d_in // ADA_ROWS,),
        in_specs=[
            pl.BlockSpec((n_batch, ADA_ROWS), lambda j: (0, j)),
            pl.BlockSpec((1, ADA_ROWS, n_out), lambda j: (0, j, 0)),
            pl.BlockSpec((1, n_out), lambda j: (0, 0)),
        ],
        out_specs=pl.BlockSpec((n_batch, n_out), lambda j: (0, 0)),
        out_shape=jax.ShapeDtypeStruct((n_batch, n_out), F32),
        compiler_params=pltpu.CompilerParams(dimension_semantics=("arbitrary",), vmem_limit_bytes=VMEM_LIMIT),
        name="ada_mod",
    )(c, w_ada, b_ada)


def _rope128(x, cos, sin_signed):
    return x * cos + pltpu.roll(x, LANES // 2, axis=1) * sin_signed


def _decay_matrix(hd, c):
    row = lax.broadcasted_iota(jnp.int32, (c, c), 0)
    col = lax.broadcasted_iota(jnp.int32, (c, c), 1)
    rel = (row - col).astype(F32)
    return jnp.where(rel >= 0.0, jnp.exp(LOG_G[hd] * jnp.maximum(rel, 0.0)), 0.0)


def _retention_head(hd, q, k, v, gate, gnw, decay_ref, state_ref, ret_ref):
    c = decay_ref.shape[1]
    lg = LOG_G[hd]
    n = lax.broadcasted_iota(jnp.int32, (c, 1), 0).astype(F32)
    q_w = jnp.exp(lg * (n + 1.0))
    k_w = jnp.exp(lg * (c - 1.0 - n))
    for i in range(q.shape[0] // c):
        rows = slice(i * c, (i + 1) * c)
        qb = q[rows].astype(BF16)
        vb = v[rows].astype(BF16)
        scores = _dot_nt(qb, k[rows].astype(BF16)) * decay_ref[hd]
        inner = _dot(scores.astype(BF16), vb)
        state = state_ref[hd]
        cross = _dot(qb, state.astype(BF16)) * q_w
        state_ref[hd] = state * math.exp(lg * c) + _dot_tn((k[rows] * k_w).astype(BF16), vb)
        o = _ln_plain(inner + cross) * gnw
        ret_ref[0, rows, hd * HEAD_DIM:(hd + 1) * HEAD_DIM] = (o * gate[rows]).astype(BF16)


def _stream_cast(src_hbm, stage_ref, sem, store, n_rows=None):
    slots, rows = stage_ref.shape[0], stage_ref.shape[1]
    n_rows = src_hbm.shape[0] if n_rows is None else n_rows
    n_chunks = n_rows // rows
    assert n_chunks * rows == n_rows and n_chunks >= slots

    def copy(k):
        slot = k % slots
        return pltpu.make_async_copy(
            src_hbm.at[pl.ds(pl.multiple_of(k * rows, rows), rows), :], stage_ref.at[slot], sem.at[slot])

    for k in range(slots - 1):
        copy(k).start()

    def body(k, _):
        @pl.when(k + slots - 1 < n_chunks)
        def _():
            copy(k + slots - 1).start()

        copy(k).wait()
        store(pl.ds(pl.multiple_of(k * rows, rows), rows), stage_ref[k % slots])
        return 0

    lax.fori_loop(0, n_chunks, body, 0)


def _store_bf16(dst_ref):
    def store(rows, chunk):
        dst_ref[rows, :] = chunk.astype(BF16)
    return store


def _stage_w_in(wt_hbm, win_ref, stage_ref, sem):
    base = LAT_COL0 + 2 * MLA_RANK
    quarter = LANES // 4

    def store(cols, chunk):
        win_ref[:, cols] = chunk.T.astype(BF16)

    _stream_cast(wt_hbm, stage_ref, sem, store, n_rows=base)

    tail_copy = pltpu.make_async_copy(
        wt_hbm.at[pl.ds(base, MLA_ROPE_DIM), :], stage_ref.at[0, pl.ds(0, MLA_ROPE_DIM), :], sem.at[0])
    tail_copy.start()
    tail_copy.wait()
    tail = stage_ref[0, 0:MLA_ROPE_DIM, :].T
    t2 = jnp.concatenate([tail, tail], axis=1)
    lane = lax.broadcasted_iota(jnp.int32, t2.shape, 1)
    middle = jnp.abs(2 * lane - (LANES - 1)) < 2 * quarter
    win_ref[:, base:base + LANES] = jnp.where(middle, pltpu.roll(t2, quarter, axis=1), t2).astype(BF16)


def _inproj_kernel(x_ref, mod_ref, wint_hbm, gnw_ref, qnw_ref, kvnw_ref, wuq_ref, wukv_ref,
                   cosr_ref, sinr_ref, cosm_ref, sinm_ref,
                   ret_ref, q_ref, k_ref, vt_ref, state_ref, decay_ref, win_ref, stage_ref, sem,
                   *, q_scale, rk_scale):
    @pl.when(jnp.logical_and(pl.program_id(0) == 0, pl.program_id(1) == 0))
    def _():
        _stage_w_in(wint_hbm.at[0], win_ref, stage_ref, sem)
        for hd in range(N_HEADS):
            decay_ref[hd] = _decay_matrix(hd, decay_ref.shape[1])

    @pl.when(pl.program_id(1) == 0)
    def _():
        state_ref[...] = jnp.zeros_like(state_ref)

    x = x_ref[0]
    shift = mod_ref[0, 0:1, :]
    scale = mod_ref[0, 1:2, :]
    h = (_ln_plain(x) * (1.0 + scale) + shift).astype(BF16)

    cosr = cosr_ref[...]
    sinr = sinr_ref[...]
    cosm = cosm_ref[...]
    sinm = sinm_ref[...]

    pq = _dot(h, win_ref[:, 0:RET_WIDTH])
    pk = _dot(h, win_ref[:, RET_WIDTH:2 * RET_WIDTH])
    pv = _dot(h, win_ref[:, 2 * RET_WIDTH:3 * RET_WIDTH])
    pg = _silu(_dot(h, win_ref[:, 3 * RET_WIDTH:LAT_COL0]))
    for hd in range(N_HEADS):
        sl = slice(hd * HEAD_DIM, (hd + 1) * HEAD_DIM)
        _retention_head(hd, _rope128(pq[:, sl], cosr, sinr), _rope128(pk[:, sl], cosr, sinr) * rk_scale,
                        pv[:, sl], pg[:, sl], gnw_ref[:, sl], decay_ref, state_ref, ret_ref)

    lat = _dot(h, win_ref[:, LAT_COL0:])
    cq = (_rms(lat[:, 0:MLA_RANK]) * qnw_ref[...]).astype(BF16)
    ckv = (_rms(lat[:, MLA_RANK:2 * MLA_RANK]) * kvnw_ref[...]).astype(BF16)
    kr = _rope128(lat[:, 2 * MLA_RANK:2 * MLA_RANK + LANES], cosm, sinm)
    lane = lax.broadcasted_iota(jnp.int32, kr.shape, 1)
    first_of_pair = (lane % (LANES // 2)) < (LANES // 4)
    kr_even = jnp.where(first_of_pair, kr, 0.0).astype(BF16)
    kr_odd = jnp.where(first_of_pair, 0.0, kr).astype(BF16)

    qf = _dot(cq, wuq_ref[...])
    kvf = _dot(ckv, wukv_ref[...])
    for hd in range(N_HEADS):
        vt_ref[0, hd] = kvf[:, (2 * hd + 1) * HEAD_DIM:(2 * hd + 2) * HEAD_DIM].T.astype(BF16)
    for p in range(N_HEADS // 2):
        qr = qf[:, RET_WIDTH + p * LANES:RET_WIDTH + (p + 1) * LANES]
        qr = (_rope128(qr, cosm, sinm) * q_scale).astype(BF16)
        for hd in (2 * p, 2 * p + 1):
            sl = slice(hd * HEAD_DIM, (hd + 1) * HEAD_DIM)
            q_ref[0, :, hd * QK_PAD:hd * QK_PAD + LANES] = (qf[:, sl] * q_scale).astype(BF16)
            q_ref[0, :, hd * QK_PAD + LANES:(hd + 1) * QK_PAD] = qr
            k_ref[0, :, hd * QK_PAD:hd * QK_PAD + LANES] = kvf[:, 2 * hd * HEAD_DIM:(2 * hd + 1) * HEAD_DIM].astype(BF16)
            k_ref[0, :, hd * QK_PAD + LANES:(hd + 1) * QK_PAD] = kr_even if hd % 2 == 0 else kr_odd


def _const_spec(shape):
    nd = len(shape)
    return pl.BlockSpec(shape, lambda *_: (0,) * nd, pipeline_mode=pl.Buffered(1))


def _inproj_call(x, mod, w_in, gnw, qnw, kvnw, w_uq, w_ukv, cosr, sinr, cosm, sinm):
    B, S, D = x.shape
    tm = ROW_TILE
    q_scale = (MLA_QK_DIM ** -0.5) * math.log2(math.e)
    rk_scale = HEAD_DIM ** -0.5
    row = lambda w: pl.BlockSpec((1, tm, w), lambda b, s: (b, s, 0))
    tab = pl.BlockSpec((tm, LANES), lambda b, s: (s, 0))
    out_shapes = (
        [jax.ShapeDtypeStruct((B, S, RET_WIDTH), BF16)]
        + [jax.ShapeDtypeStruct((B, S, N_HEADS * QK_PAD), BF16)] * 2
        + [jax.ShapeDtypeStruct((B, N_HEADS, HEAD_DIM, S), BF16)]
    )
    vt_spec = pl.BlockSpec((1, N_HEADS, HEAD_DIM, tm), lambda b, s: (b, 0, 0, s))
    return pl.pallas_call(
        functools.partial(_inproj_kernel, q_scale=q_scale, rk_scale=rk_scale),
        grid=(B, S // tm),
        in_specs=[
            row(D),
            pl.BlockSpec((1, N_MOD, D), lambda b, s: (b, 0, 0)),
            pl.BlockSpec(memory_space=pl.ANY), _const_spec(gnw.shape),
            _const_spec(qnw.shape), _const_spec(kvnw.shape),
            _const_spec(w_uq.shape), _const_spec(w_ukv.shape),
            tab, tab, tab, tab,
        ],
        out_specs=[row(RET_WIDTH)] + [row(N_HEADS * QK_PAD)] * 2 + [vt_spec],
        out_shape=out_shapes,
        scratch_shapes=[
            pltpu.VMEM((N_HEADS, HEAD_DIM, HEAD_DIM), F32),
            pltpu.VMEM((N_HEADS, RET_CHUNK, RET_CHUNK), F32),
            pltpu.VMEM((D, LAT_COL0 + 2 * MLA_RANK + LANES), BF16),
            pltpu.VMEM((STAGE_SLOTS, STAGE_ROWS_IN, w_in.shape[-1]), F32),
            pltpu.SemaphoreType.DMA((STAGE_SLOTS,)),
        ],
        compiler_params=pltpu.CompilerParams(
            dimension_semantics=("arbitrary", "arbitrary"), vmem_limit_bytes=VMEM_LIMIT),
        name="inproj_retention",
    )(x, mod, w_in, gnw, qnw, kvnw, w_uq, w_ukv, cosr, sinr, cosm, sinm)


def _attn_kernel(q_ref, k_ref, vt_ref, o_ref, qt_ref, s_buf, p_buf, acc_ref, *, tq, tk):
    qi = pl.program_id(2)
    qt_ref[...] = q_ref[0].T

    ones = jnp.ones((BF16_ROWS, tk), BF16)

    def k_blk(j):
        return k_ref[0, pl.ds(pl.multiple_of(j * tk, tk), tk), :]

    def v_blk(j):
        return jnp.concatenate([vt_ref[0, 0, :, pl.ds(pl.multiple_of(j * tk, tk), tk)], ones], axis=0)

    def scores(j, slot):
        s_buf[slot] = _dot(k_blk(j), qt_ref[...])

    def value_update(j, slot, alpha):
        acc_ref[...] = alpha * acc_ref[...] + _dot(v_blk(j), p_buf[slot])

    def softmax(s, m):
        m_new = jnp.maximum(m, jnp.max(s, axis=0, keepdims=True))
        return m_new, jnp.exp2(m - m_new), jnp.exp2(s - m_new).astype(BF16)

    def two_blocks(u, carry):
        alpha_prev, m = carry
        t = 2 * u
        scores(t + 1, 1)
        value_update(jnp.maximum(t - 1, 0), 1, alpha_prev)
        m, alpha_even, p_buf[0] = softmax(s_buf[0], m)
        scores(t + 2, 0)
        value_update(t, 0, alpha_even)
        m, alpha_odd, p_buf[1] = softmax(s_buf[1], m)
        return alpha_odd, m

    scores(0, 0)
    p_buf[1] = jnp.zeros((tk, tq), BF16)
    acc_ref[...] = jnp.zeros_like(acc_ref)
    init = (jnp.ones((1, tq), F32), jnp.full((1, tq), MASKED, F32))
    alpha_prev, m = lax.fori_loop(0, qi, two_blocks, init)

    t = 2 * qi
    key = lax.broadcasted_iota(jnp.int32, (tk, tk), 0)
    qry = lax.broadcasted_iota(jnp.int32, (tk, tk), 1)
    causal = key <= qry
    s_right = _dot(k_blk(t + 1), qt_ref[:, tk:])
    value_update(jnp.maximum(t - 1, 0), 1, alpha_prev)
    s = s_buf[0]
    s = jnp.concatenate([jnp.where(causal, s[:, :tk], MASKED), s[:, tk:]], axis=1)
    m, alpha, p_buf[0] = softmax(s, m)
    value_update(t, 0, alpha)
    _, alpha_r, p_r = softmax(jnp.where(causal, s_right, MASKED), m[:, tk:])
    acc_ref[:, tk:] = alpha_r * acc_ref[:, tk:] + _dot(v_blk(t + 1), p_r)
    acc = acc_ref[...]
    o_ref[0] = (acc[:HEAD_DIM] / acc[HEAD_DIM:HEAD_DIM + 1]).T.astype(BF16)


def _attn_call(q, k, vt):
    B, S, _ = q.shape
    tq, tk = ATTN_TQ, ATTN_TK
    assert tq == 2 * tk
    return pl.pallas_call(
        functools.partial(_attn_kernel, tq=tq, tk=tk),
        grid=(B, N_HEADS, S // tq),
        in_specs=[
            pl.BlockSpec((1, tq, QK_PAD), lambda b, h, i: (b, i, h)),
            pl.BlockSpec((1, S, QK_PAD), lambda b, h, i: (b, 0, h)),
            pl.BlockSpec((1, 1, HEAD_DIM, S), lambda b, h, i: (b, h, 0, 0)),
        ],
        out_specs=pl.BlockSpec((1, tq, HEAD_DIM), lambda b, h, i: (b, i, h)),
        out_shape=jax.ShapeDtypeStruct((B, S, N_HEADS * HEAD_DIM), BF16),
        scratch_shapes=[
            pltpu.VMEM((QK_PAD, tq), BF16),
            pltpu.VMEM((2, tk, tq), F32),
            pltpu.VMEM((2, tk, tq), BF16),
            pltpu.VMEM((ACC_ROWS, tq), F32),
        ],
        compiler_params=pltpu.CompilerParams(
            dimension_semantics=("arbitrary", "arbitrary", "arbitrary"), vmem_limit_bytes=VMEM_LIMIT),
        name="mla_attn",
    )(q, k, vt)


def _mlp_kernel(x_ref, ret_ref, mla_ref, mod_ref, wo32_hbm, ln1w_ref, ln1b_ref, wup32_hbm, cw_ref, cb_ref,
                wdn32_hbm, ln2w_ref, ln2b_ref, o_ref, carry_ref, ubuf_ref, act_ref,
                wo_ref, wup_ref, wdn_ref, stage_sq_ref, stage_up_ref, sem, *, tm):
    @pl.when(jnp.logical_and(pl.program_id(0) == 0, pl.program_id(1) == 0))
    def _():
        _stream_cast(wo32_hbm.at[0], stage_sq_ref, sem, _store_bf16(wo_ref))
        _stream_cast(wup32_hbm.at[0], stage_up_ref, sem, _store_bf16(wup_ref))
        _stream_cast(wdn32_hbm.at[0], stage_sq_ref, sem, _store_bf16(wdn_ref))

    @pl.when(pl.program_id(1) == 0)
    def _():
        carry_ref[...] = jnp.zeros_like(carry_ref)

    gate1 = mod_ref[0, 2:3, :]
    shift2 = mod_ref[0, 3:4, :]
    scale2 = mod_ref[0, 4:5, :]
    gate2 = mod_ref[0, 5:6, :]

    y = _dot(ret_ref[0], wo_ref[0:RET_WIDTH, :]) + _dot(mla_ref[0], wo_ref[RET_WIDTH:2 * RET_WIDTH, :])
    x1 = _ln_plain(DN_ALPHA * x_ref[0] + (1.0 + gate1) * y) * ln1w_ref[...] + ln1b_ref[...]
    h2 = (_ln_plain(x1) * (1.0 + scale2) + shift2).astype(BF16)

    def conv(u, col0, kind):
        outs = []
        for g in range(FF_CHUNK // LANES):
            cols = slice(col0 + g * LANES, col0 + (g + 1) * LANES)
            ug = u[:, g * LANES:(g + 1) * LANES]
            buf = ubuf_ref.at[kind, g]
            buf[0:SUBLANES, :] = carry_ref[:, cols]
            buf[SUBLANES:SUBLANES + tm, :] = ug
            carry_ref[:, cols] = ug[tm - SUBLANES:tm, :]
            u1 = buf[SUBLANES - 1:SUBLANES - 1 + tm, :]
            u2 = buf[SUBLANES - 2:SUBLANES - 2 + tm, :]
            outs.append(cb_ref[:, cols] + cw_ref[2:3, cols] * ug + cw_ref[1:2, cols] * u1
                        + cw_ref[0:1, cols] * u2)
        return jnp.concatenate(outs, axis=1)

    for c in range(N_FF_CHUNKS):
        gcol0 = c * FF_CHUNK
        vcol0 = D_FF + c * FF_CHUNK
        g = conv(_dot(h2, wup_ref[:, gcol0:gcol0 + FF_CHUNK]), gcol0, 0)
        val = conv(_dot(h2, wup_ref[:, vcol0:vcol0 + FF_CHUNK]), vcol0, 1)
        act_ref[:, gcol0:gcol0 + FF_CHUNK] = (_silu(g) * val).astype(BF16)

    y2 = _dot(act_ref[...], wdn_ref[...])
    o_ref[0] = _ln_plain(DN_ALPHA * x1 + (1.0 + gate2) * y2) * ln2w_ref[...] + ln2b_ref[...]


def _mlp_call(x, ret, mla, mod, w_out, ln1w, ln1b, w_up, conv_w, conv_b, w_down, ln2w, ln2b):
    B, S, D = x.shape
    tm = ROW_TILE
    row = lambda w: pl.BlockSpec((1, tm, w), lambda b, s: (b, s, 0))
    hbm = pl.BlockSpec(memory_space=pl.ANY)
    return pl.pallas_call(
        functools.partial(_mlp_kernel, tm=tm),
        grid=(B, S // tm),
        in_specs=[
            row(D), row(RET_WIDTH), row(RET_WIDTH),
            pl.BlockSpec((1, N_MOD, D), lambda b, s: (b, 0, 0)),
            hbm, _const_spec(ln1w.shape), _const_spec(ln1b.shape),
            hbm, _const_spec(conv_w.shape), _const_spec(conv_b.shape),
            hbm, _const_spec(ln2w.shape), _const_spec(ln2b.shape),
        ],
        out_specs=row(D),
        out_shape=jax.ShapeDtypeStruct((B, S, D), F32),
        scratch_shapes=[
            pltpu.VMEM((SUBLANES, 2 * D_FF), F32),
            pltpu.VMEM((2, FF_CHUNK // LANES, SUBLANES + tm, LANES), F32),
            pltpu.VMEM((tm, D_FF), BF16),
            pltpu.VMEM(w_out.shape[1:], BF16),
            pltpu.VMEM(w_up.shape[1:], BF16),
            pltpu.VMEM(w_down.shape[1:], BF16),
            pltpu.VMEM((STAGE_SLOTS, STAGE_ROWS_SQ, D), F32),
            pltpu.VMEM((STAGE_SLOTS, STAGE_ROWS_UP, 2 * D_FF), F32),
            pltpu.SemaphoreType.DMA((STAGE_SLOTS,)),
        ],
        compiler_params=pltpu.CompilerParams(
            dimension_semantics=("arbitrary", "arbitrary"), vmem_limit_bytes=VMEM_LIMIT),
        name="outproj_mlp",
    )(x, ret, mla, mod, w_out, ln1w, ln1b, w_up, conv_w, conv_b, w_down, ln2w, ln2b)


def _rope_tables(seq, half, reps):
    pos = np.arange(seq, dtype=np.float32)
    inv = np.float32(ROPE_BASE) ** (-np.arange(half, dtype=np.float32) / np.float32(half))
    ang = (pos[:, None] * inv[None, :]).astype(np.float32)
    cos = np.cos(ang).astype(np.float32)
    sin = np.sin(ang).astype(np.float32)
    cos_t = np.tile(cos, (1, 2 * reps))
    sin_t = np.concatenate([np.tile(-sin, (1, reps)), np.tile(sin, (1, reps))], axis=-1)
    return jnp.asarray(cos_t), jnp.asarray(sin_t)


def _uq_columns():
    half = MLA_ROPE_DIM // 2
    nope = [h * MLA_QK_DIM + np.arange(HEAD_DIM) for h in range(N_HEADS)]
    rope = []
    for p in range(N_HEADS // 2):
        for part in range(2):
            for h in (2 * p, 2 * p + 1):
                rope.append(h * MLA_QK_DIM + HEAD_DIM + part * half + np.arange(half))
    return np.concatenate(nope + rope)


def _take_columns(w, cols):
    cols = np.asarray(cols)
    cuts = np.flatnonzero(np.diff(cols) != 1) + 1
    runs = np.split(cols, cuts)
    return jnp.concatenate([w[:, r[0]:r[-1] + 1] for r in runs], axis=1)


def kernel(x, c, w_ada, b_ada, w_in, ret_gn_w, mla_q_norm_w, w_uq, mla_kv_norm_w, w_ukv, w_out,
           ln1_w, ln1_b, w_up, conv_w, conv_b, w_down, ln2_w, ln2_b):
    B, S, D = x.shape
    assert D == D_MODEL and S % ROW_TILE == 0 and S % ATTN_TQ == 0 and w_ada.shape[0] == DEPTH == 1
    l = 0

    mod = _ada_call(c, w_ada, b_ada[l][None, :]).reshape(B, N_MOD, D)

    w_uq_p = _take_columns(w_uq[l].astype(BF16), _uq_columns())
    w_ukv_p = w_ukv[l].astype(BF16)
    cosr, sinr = _rope_tables(S, HEAD_DIM // 2, 1)
    cosm, sinm = _rope_tables(S, MLA_ROPE_DIM // 2, 2)

    ret, q, k, vt = _inproj_call(
        x, mod, jnp.swapaxes(w_in, 1, 2), ret_gn_w[l][None, :], mla_q_norm_w[l][None, :], mla_kv_norm_w[l][None, :],
        w_uq_p, w_ukv_p, cosr, sinr, cosm, sinm)
    mla = _attn_call(q, k, vt)
    return _mlp_call(
        x, ret, mla, mod, w_out, ln1_w[l][None, :], ln1_b[l][None, :],
        w_up, conv_w[l], conv_b[l][None, :], w_down, ln2_w[l][None, :], ln2_b[l][None, :])
```

```python
import functools
import math

import numpy as np
import jax
import jax.numpy as jnp
from jax import lax
from jax.experimental import pallas as pl
from jax.experimental.pallas import tpu as pltpu

F32 = jnp.float32
BF16 = jnp.bfloat16

D_MODEL = 1024
DEPTH = 1
N_HEADS = 4
HEAD_DIM = 128
RET_WIDTH = N_HEADS * HEAD_DIM
LAT_COL0 = 4 * RET_WIDTH
MLA_RANK = 256
MLA_ROPE_DIM = 64
MLA_QK_DIM = HEAD_DIM + MLA_ROPE_DIM
D_FF = 2816
ROPE_BASE = 10000.0
LN_EPS = 1e-5
RMS_EPS = 1e-6
DN_ALPHA = (2.0 * DEPTH) ** 0.25
N_MOD = 6
LOG_G = [math.log1p(-(2.0 ** (-5.0 - h))) for h in range(N_HEADS)]

LANES = 128
SUBLANES = 8
MXU_DIM = 256
VMEM_BYTES = 64 * 1024 * 1024
VMEM_LIMIT = VMEM_BYTES * 7 // 8

ADA_ROWS = 256
ROW_TILE = 512
RET_CHUNK = MXU_DIM
ATTN_TQ = 1024
ATTN_TK = 512
FF_CHUNK = MXU_DIM
N_FF_CHUNKS = D_FF // FF_CHUNK
STAGE_ROWS_SQ = 256
STAGE_ROWS_UP = 64
STAGE_ROWS_IN = 128
STAGE_SLOTS = 4
QK_PAD = 2 * LANES
BF16_ROWS = 2 * SUBLANES
ACC_ROWS = HEAD_DIM + BF16_ROWS
MASKED = -float("inf")


def _ln_plain(x):
    mu = jnp.mean(x, axis=-1, keepdims=True)
    xc = x - mu
    var = jnp.mean(xc * xc, axis=-1, keepdims=True)
    return xc * lax.rsqrt(var + LN_EPS)


def _rms(x):
    return x * lax.rsqrt(jnp.mean(x * x, axis=-1, keepdims=True) + RMS_EPS)


def _silu(x):
    return x * jax.nn.sigmoid(x)


def _dot(a, b):
    return jnp.dot(a, b, preferred_element_type=F32)


def _dot_nt(a, b):
    return lax.dot_general(a, b, (((1,), (1,)), ((), ())), preferred_element_type=F32)


def _dot_tn(a, b):
    return lax.dot_general(a, b, (((0,), (0,)), ((), ())), preferred_element_type=F32)


def _ada_kernel(c_ref, w_ref, b_ref, o_ref):
    @pl.when(pl.program_id(0) == 0)
    def _():
        o_ref[...] = jnp.broadcast_to(b_ref[...], o_ref.shape)

    cond = _silu(c_ref[...])
    o_ref[...] += _dot(cond.astype(BF16), w_ref[0].astype(BF16))


def _ada_call(c, w_ada, b_ada):
    n_batch = c.shape[0]
    _, d_in, n_out = w_ada.shape
    return pl.pallas_call(
        _ada_kernel,
        grid=(d_in // ADA_ROWS,),
        in_specs=[
            pl.BlockSpec((n_batch, ADA_ROWS), lambda j: (0, j)),
            pl.BlockSpec((1, ADA_ROWS, n_out), lambda j: (0, j, 0)),
            pl.BlockSpec((1, n_out), lambda j: (0, 0)),
        ],
        out_specs=pl.BlockSpec((n_batch, n_out), lambda j: (0, 0)),
        out_shape=jax.ShapeDtypeStruct((n_batch, n_out), F32),
        compiler_params=pltpu.CompilerParams(dimension_semantics=("arbitrary",), vmem_limit_bytes=VMEM_LIMIT),
        name="ada_mod",
    )(c, w_ada, b_ada)


def _rope128(x, cos, sin_signed):
    return x * cos + pltpu.roll(x, LANES // 2, axis=1) * sin_signed


def _decay_matrix(hd, c):
    row = lax.broadcasted_iota(jnp.int32, (c, c), 0)
    col = lax.broadcasted_iota(jnp.int32, (c, c), 1)
    rel = (row - col).astype(F32)
    return jnp.where(rel >= 0.0, jnp.exp(LOG_G[hd] * jnp.maximum(rel, 0.0)), 0.0)


def _retention_head(hd, q, k, v, gate, gnw, decay_ref, state_ref, ret_ref):
    c = decay_ref.shape[1]
    lg = LOG_G[hd]
    n = lax.broadcasted_iota(jnp.int32, (c, 1), 0).astype(F32)
    q_w = jnp.exp(lg * (n + 1.0))
    k_w = jnp.exp(lg * (c - 1.0 - n))
    for i in range(q.shape[0] // c):
        rows = slice(i * c, (i + 1) * c)
        qb = q[rows].astype(BF16)
        vb = v[rows].astype(BF16)
        scores = _dot_nt(qb, k[rows].astype(BF16)) * decay_ref[hd]
        inner = _dot(scores.astype(BF16), vb)
        state = state_ref[hd]
        cross = _dot(qb, state.astype(BF16)) * q_w
        state_ref[hd] = state * math.exp(lg * c) + _dot_tn((k[rows] * k_w).astype(BF16), vb)
        o = _ln_plain(inner + cross) * gnw
        ret_ref[0, rows, hd * HEAD_DIM:(hd + 1) * HEAD_DIM] = (o * gate[rows]).astype(BF16)


def _stream_cast(src_hbm, stage_ref, sem, store, n_rows=None):
    slots, rows = stage_ref.shape[0], stage_ref.shape[1]
    n_rows = src_hbm.shape[0] if n_rows is None else n_rows
    n_chunks = n_rows // rows
    assert n_chunks * rows == n_rows and n_chunks >= slots

    def copy(k):
        slot = k % slots
        return pltpu.make_async_copy(
            src_hbm.at[pl.ds(pl.multiple_of(k * rows, rows), rows), :], stage_ref.at[slot], sem.at[slot])

    for k in range(slots - 1):
        copy(k).start()

    def body(k, _):
        @pl.when(k + slots - 1 < n_chunks)
        def _():
            copy(k + slots - 1).start()

        copy(k).wait()
        store(pl.ds(pl.multiple_of(k * rows, rows), rows), stage_ref[k % slots])
        return 0

    lax.fori_loop(0, n_chunks, body, 0)


def _store_bf16(dst_ref):
    def store(rows, chunk):
        dst_ref[rows, :] = chunk.astype(BF16)
    return store


def _stage_w_in(wt_hbm, win_ref, stage_ref, sem):
    base = LAT_COL0 + 2 * MLA_RANK
    quarter = LANES // 4

    def store(cols, chunk):
        win_ref[:, cols] = chunk.T.astype(BF16)

    _stream_cast(wt_hbm, stage_ref, sem, store, n_rows=base)

    tail_copy = pltpu.make_async_copy(
        wt_hbm.at[pl.ds(base, MLA_ROPE_DIM), :], stage_ref.at[0, pl.ds(0, MLA_ROPE_DIM), :], sem.at[0])
    tail_copy.start()
    tail_copy.wait()
    tail = stage_ref[0, 0:MLA_ROPE_DIM, :].T
    t2 = jnp.concatenate([tail, tail], axis=1)
    lane = lax.broadcasted_iota(jnp.int32, t2.shape, 1)
    middle = jnp.abs(2 * lane - (LANES - 1)) < 2 * quarter
    win_ref[:, base:base + LANES] = jnp.where(middle, pltpu.roll(t2, quarter, axis=1), t2).astype(BF16)


def _inproj_kernel(x_ref, mod_ref, wint_hbm, gnw_ref, qnw_ref, kvnw_ref, wuq_ref, wukv_ref,
                   cosr_ref, sinr_ref, cosm_ref, sinm_ref,
                   ret_ref, q_ref, k_ref, vt_ref, state_ref, decay_ref, win_ref, stage_ref, sem,
                   *, q_scale, rk_scale):
    @pl.when(jnp.logical_and(pl.program_id(0) == 0, pl.program_id(1) == 0))
    def _():
        _stage_w_in(wint_hbm.at[0], win_ref, stage_ref, sem)
        for hd in range(N_HEADS):
            decay_ref[hd] = _decay_matrix(hd, decay_ref.shape[1])

    @pl.when(pl.program_id(1) == 0)
    def _():
        state_ref[...] = jnp.zeros_like(state_ref)

    x = x_ref[0]
    shift = mod_ref[0, 0:1, :]
    scale = mod_ref[0, 1:2, :]
    h = (_ln_plain(x) * (1.0 + scale) + shift).astype(BF16)

    cosr = cosr_ref[...]
    sinr = sinr_ref[...]
    cosm = cosm_ref[...]
    sinm = sinm_ref[...]

    pq = _dot(h, win_ref[:, 0:RET_WIDTH])
    pk = _dot(h, win_ref[:, RET_WIDTH:2 * RET_WIDTH])
    pv = _dot(h, win_ref[:, 2 * RET_WIDTH:3 * RET_WIDTH])
    pg = _silu(_dot(h, win_ref[:, 3 * RET_WIDTH:LAT_COL0]))
    for hd in range(N_HEADS):
        sl = slice(hd * HEAD_DIM, (hd + 1) * HEAD_DIM)
        _retention_head(hd, _rope128(pq[:, sl], cosr, sinr), _rope128(pk[:, sl], cosr, sinr) * rk_scale,
                        pv[:, sl], pg[:, sl], gnw_ref[:, sl], decay_ref, state_ref, ret_ref)

    lat = _dot(h, win_ref[:, LAT_COL0:])
    cq = (_rms(lat[:, 0:MLA_RANK]) * qnw_ref[...]).astype(BF16)
    ckv = (_rms(lat[:, MLA_RANK:2 * MLA_RANK]) * kvnw_ref[...]).astype(BF16)
    kr = _rope128(lat[:, 2 * MLA_RANK:2 * MLA_RANK + LANES], cosm, sinm)
    lane = lax.broadcasted_iota(jnp.int32, kr.shape, 1)
    first_of_pair = (lane % (LANES // 2)) < (LANES // 4)
    kr_even = jnp.where(first_of_pair, kr, 0.0).astype(BF16)
    kr_odd = jnp.where(first_of_pair, 0.0, kr).astype(BF16)

    qf = _dot(cq, wuq_ref[...])
    kvf = _dot(ckv, wukv_ref[...])
    for hd in range(N_HEADS):
        vt_ref[0, hd] = kvf[:, (2 * hd + 1) * HEAD_DIM:(2 * hd + 2) * HEAD_DIM].T.astype(BF16)
    for p in range(N_HEADS // 2):
        qr = qf[:, RET_WIDTH + p * LANES:RET_WIDTH + (p + 1) * LANES]
        qr = (_rope128(qr, cosm, sinm) * q_scale).astype(BF16)
        for hd in (2 * p, 2 * p + 1):
            sl = slice(hd * HEAD_DIM, (hd + 1) * HEAD_DIM)
            q_ref[0, :, hd * QK_PAD:hd * QK_PAD + LANES] = (qf[:, sl] * q_scale).astype(BF16)
            q_ref[0, :, hd * QK_PAD + LANES:(hd + 1) * QK_PAD] = qr
            k_ref[0, :, hd * QK_PAD:hd * QK_PAD + LANES] = kvf[:, 2 * hd * HEAD_DIM:(2 * hd + 1) * HEAD_DIM].astype(BF16)
            k_ref[0, :, hd * QK_PAD + LANES:(hd + 1) * QK_PAD] = kr_even if hd % 2 == 0 else kr_odd


def _const_spec(shape):
    nd = len(shape)
    return pl.BlockSpec(shape, lambda *_: (0,) * nd, pipeline_mode=pl.Buffered(1))


def _inproj_call(x, mod, w_in, gnw, qnw, kvnw, w_uq, w_ukv, cosr, sinr, cosm, sinm):
    B, S, D = x.shape
    tm = ROW_TILE
    q_scale = (MLA_QK_DIM ** -0.5) * math.log2(math.e)
    rk_scale = HEAD_DIM ** -0.5
    row = lambda w: pl.BlockSpec((1, tm, w), lambda b, s: (b, s, 0))
    tab = pl.BlockSpec((tm, LANES), lambda b, s: (s, 0))
    out_shapes = (
        [jax.ShapeDtypeStruct((B, S, RET_WIDTH), BF16)]
        + [jax.ShapeDtypeStruct((B, S, N_HEADS * QK_PAD), BF16)] * 2
        + [jax.ShapeDtypeStruct((B, N_HEADS, HEAD_DIM, S), BF16)]
    )
    vt_spec = pl.BlockSpec((1, N_HEADS, HEAD_DIM, tm), lambda b, s: (b, 0, 0, s))
    return pl.pallas_call(
        functools.partial(_inproj_kernel, q_scale=q_scale, rk_scale=rk_scale),
        grid=(B, S // tm),
        in_specs=[
            row(D),
            pl.BlockSpec((1, N_MOD, D), lambda b, s: (b, 0, 0)),
            pl.BlockSpec(memory_space=pl.ANY), _const_spec(gnw.shape),
            _const_spec(qnw.shape), _const_spec(kvnw.shape),
            _const_spec(w_uq.shape), _const_spec(w_ukv.shape),
            tab, tab, tab, tab,
        ],
        out_specs=[row(RET_WIDTH)] + [row(N_HEADS * QK_PAD)] * 2 + [vt_spec],
        out_shape=out_shapes,
        scratch_shapes=[
            pltpu.VMEM((N_HEADS, HEAD_DIM, HEAD_DIM), F32),
            pltpu.VMEM((N_HEADS, RET_CHUNK, RET_CHUNK), F32),
            pltpu.VMEM((D, LAT_COL0 + 2 * MLA_RANK + LANES), BF16),
            pltpu.VMEM((STAGE_SLOTS, STAGE_ROWS_IN, w_in.shape[-1]), F32),
            pltpu.SemaphoreType.DMA((STAGE_SLOTS,)),
        ],
        compiler_params=pltpu.CompilerParams(
            dimension_semantics=("arbitrary", "arbitrary"), vmem_limit_bytes=VMEM_LIMIT),
        name="inproj_retention",
    )(x, mod, w_in, gnw, qnw, kvnw, w_uq, w_ukv, cosr, sinr, cosm, sinm)


def _attn_kernel(q_ref, k_ref, vt_ref, o_ref, qt_ref, s_buf, p_buf, acc_ref, *, tq, tk):
    qi = pl.program_id(2)
    qt_ref[...] = q_ref[0].T

    ones = jnp.ones((BF16_ROWS, tk), BF16)

    def k_blk(j):
        return k_ref[0, pl.ds(pl.multiple_of(j * tk, tk), tk), :]

    def v_blk(j):
        return jnp.concatenate([vt_ref[0, 0, :, pl.ds(pl.multiple_of(j * tk, tk), tk)], ones], axis=0)

    def scores(j, slot):
        s = _dot(k_blk(j), qt_ref[...])
        s_buf[slot] = s
        return jnp.max(s, axis=0, keepdims=True)

    def value_update(j, slot, alpha):
        acc_ref[...] = alpha * acc_ref[...] + _dot(v_blk(j), p_buf[slot])

    def softmax(s, m, block_max=None):
        if block_max is None:
            block_max = jnp.max(s, axis=0, keepdims=True)
        m_new = jnp.maximum(m, block_max)
        return m_new, jnp.exp2(m - m_new), jnp.exp2(s - m_new).astype(BF16)

    def two_blocks(u, carry):
        alpha_prev, m, max_even = carry
        t = 2 * u
        max_odd = scores(t + 1, 1)
        value_update(jnp.maximum(t - 1, 0), 1, alpha_prev)
        m, alpha_even, p_buf[0] = softmax(s_buf[0], m, max_even)
        max_next = scores(t + 2, 0)
        value_update(t, 0, alpha_even)
        m, alpha_odd, p_buf[1] = softmax(s_buf[1], m, max_odd)
        return alpha_odd, m, max_next

    max_0 = scores(0, 0)
    p_buf[1] = jnp.zeros((tk, tq), BF16)
    acc_ref[...] = jnp.zeros_like(acc_ref)
    init = (jnp.ones((1, tq), F32), jnp.full((1, tq), MASKED, F32), max_0)
    alpha_prev, m, _ = lax.fori_loop(0, qi, two_blocks, init)

    t = 2 * qi
    key = lax.broadcasted_iota(jnp.int32, (tk, tk), 0)
    qry = lax.broadcasted_iota(jnp.int32, (tk, tk), 1)
    causal = key <= qry
    s_right = _dot(k_blk(t + 1), qt_ref[:, tk:])
    value_update(jnp.maximum(t - 1, 0), 1, alpha_prev)
    s = s_buf[0]
    s = jnp.concatenate([jnp.where(causal, s[:, :tk], MASKED), s[:, tk:]], axis=1)
    m, alpha, p_buf[0] = softmax(s, m)
    value_update(t, 0, alpha)
    _, alpha_r, p_r = softmax(jnp.where(causal, s_right, MASKED), m[:, tk:])
    acc_ref[:, tk:] = alpha_r * acc_ref[:, tk:] + _dot(v_blk(t + 1), p_r)
    acc = acc_ref[...]
    o_ref[0] = (acc[:HEAD_DIM] / acc[HEAD_DIM:HEAD_DIM + 1]).T.astype(BF16)


def _attn_call(q, k, vt):
    B, S, _ = q.shape
    tq, tk = ATTN_TQ, ATTN_TK
    assert tq == 2 * tk
    return pl.pallas_call(
        functools.partial(_attn_kernel, tq=tq, tk=tk),
        grid=(B, N_HEADS, S // tq),
        in_specs=[
            pl.BlockSpec((1, tq, QK_PAD), lambda b, h, i: (b, i, h)),
            pl.BlockSpec((1, S, QK_PAD), lambda b, h, i: (b, 0, h)),
            pl.BlockSpec((1, 1, HEAD_DIM, S), lambda b, h, i: (b, h, 0, 0)),
        ],
        out_specs=pl.BlockSpec((1, tq, HEAD_DIM), lambda b, h, i: (b, i, h)),
        out_shape=jax.ShapeDtypeStruct((B, S, N_HEADS * HEAD_DIM), BF16),
        scratch_shapes=[
            pltpu.VMEM((QK_PAD, tq), BF16),
            pltpu.VMEM((2, tk, tq), F32),
            pltpu.VMEM((2, tk, tq), BF16),
            pltpu.VMEM((ACC_ROWS, tq), F32),
        ],
        compiler_params=pltpu.CompilerParams(
            dimension_semantics=("arbitrary", "arbitrary", "arbitrary"), vmem_limit_bytes=VMEM_LIMIT),
        name="mla_attn",
    )(q, k, vt)


def _mlp_kernel(x_ref, ret_ref, mla_ref, mod_ref, wo32_hbm, ln1w_ref, ln1b_ref, wup32_hbm, cw_ref, cb_ref,
                wdn32_hbm, ln2w_ref, ln2b_ref, o_ref, carry_ref, ubuf_ref, act_ref,
                wo_ref, wup_ref, wdn_ref, stage_sq_ref, stage_up_ref, sem, *, tm):
    @pl.when(jnp.logical_and(pl.program_id(0) == 0, pl.program_id(1) == 0))
    def _():
        _stream_cast(wo32_hbm.at[0], stage_sq_ref, sem, _store_bf16(wo_ref))
        _stream_cast(wup32_hbm.at[0], stage_up_ref, sem, _store_bf16(wup_ref))
        _stream_cast(wdn32_hbm.at[0], stage_sq_ref, sem, _store_bf16(wdn_ref))

    @pl.when(pl.program_id(1) == 0)
    def _():
        carry_ref[...] = jnp.zeros_like(carry_ref)

    gate1 = mod_ref[0, 2:3, :]
    shift2 = mod_ref[0, 3:4, :]
    scale2 = mod_ref[0, 4:5, :]
    gate2 = mod_ref[0, 5:6, :]

    y = _dot(ret_ref[0], wo_ref[0:RET_WIDTH, :]) + _dot(mla_ref[0], wo_ref[RET_WIDTH:2 * RET_WIDTH, :])
    x1 = _ln_plain(DN_ALPHA * x_ref[0] + (1.0 + gate1) * y) * ln1w_ref[...] + ln1b_ref[...]
    h2 = (_ln_plain(x1) * (1.0 + scale2) + shift2).astype(BF16)

    def conv(u, col0, kind):
        outs = []
        for g in range(FF_CHUNK // LANES):
            cols = slice(col0 + g * LANES, col0 + (g + 1) * LANES)
            ug = u[:, g * LANES:(g + 1) * LANES]
            buf = ubuf_ref.at[kind, g]
            buf[0:SUBLANES, :] = carry_ref[:, cols]
            buf[SUBLANES:SUBLANES + tm, :] = ug
            carry_ref[:, cols] = ug[tm - SUBLANES:tm, :]
            u1 = buf[SUBLANES - 1:SUBLANES - 1 + tm, :]
            u2 = buf[SUBLANES - 2:SUBLANES - 2 + tm, :]
            outs.append(cb_ref[:, cols] + cw_ref[2:3, cols] * ug + cw_ref[1:2, cols] * u1
                        + cw_ref[0:1, cols] * u2)
        return jnp.concatenate(outs, axis=1)

    for c in range(N_FF_CHUNKS):
        gcol0 = c * FF_CHUNK
        vcol0 = D_FF + c * FF_CHUNK
        g = conv(_dot(h2, wup_ref[:, gcol0:gcol0 + FF_CHUNK]), gcol0, 0)
        val = conv(_dot(h2, wup_ref[:, vcol0:vcol0 + FF_CHUNK]), vcol0, 1)
        act_ref[:, gcol0:gcol0 + FF_CHUNK] = (_silu(g) * val).astype(BF16)

    y2 = _dot(act_ref[...], wdn_ref[...])
    o_ref[0] = _ln_plain(DN_ALPHA * x1 + (1.0 + gate2) * y2) * ln2w_ref[...] + ln2b_ref[...]


def _mlp_call(x, ret, mla, mod, w_out, ln1w, ln1b, w_up, conv_w, conv_b, w_down, ln2w, ln2b):
    B, S, D = x.shape
    tm = ROW_TILE
    row = lambda w: pl.BlockSpec((1, tm, w), lambda b, s: (b, s, 0))
    hbm = pl.BlockSpec(memory_space=pl.ANY)
    return pl.pallas_call(
        functools.partial(_mlp_kernel, tm=tm),
        grid=(B, S // tm),
        in_specs=[
            row(D), row(RET_WIDTH), row(RET_WIDTH),
            pl.BlockSpec((1, N_MOD, D), lambda b, s: (b, 0, 0)),
            hbm, _const_spec(ln1w.shape), _const_spec(ln1b.shape),
            hbm, _const_spec(conv_w.shape), _const_spec(conv_b.shape),
            hbm, _const_spec(ln2w.shape), _const_spec(ln2b.shape),
        ],
        out_specs=row(D),
        out_shape=jax.ShapeDtypeStruct((B, S, D), F32),
        scratch_shapes=[
            pltpu.VMEM((SUBLANES, 2 * D_FF), F32),
            pltpu.VMEM((2, FF_CHUNK // LANES, SUBLANES + tm, LANES), F32),
            pltpu.VMEM((tm, D_FF), BF16),
            pltpu.VMEM(w_out.shape[1:], BF16),
            pltpu.VMEM(w_up.shape[1:], BF16),
            pltpu.VMEM(w_down.shape[1:], BF16),
            pltpu.VMEM((STAGE_SLOTS, STAGE_ROWS_SQ, D), F32),
            pltpu.VMEM((STAGE_SLOTS, STAGE_ROWS_UP, 2 * D_FF), F32),
            pltpu.SemaphoreType.DMA((STAGE_SLOTS,)),
        ],
        compiler_params=pltpu.CompilerParams(
            dimension_semantics=("arbitrary", "arbitrary"), vmem_limit_bytes=VMEM_LIMIT),
        name="outproj_mlp",
    )(x, ret, mla, mod, w_out, ln1w, ln1b, w_up, conv_w, conv_b, w_down, ln2w, ln2b)


def _rope_tables(seq, half, reps):
    pos = np.arange(seq, dtype=np.float32)
    inv = np.float32(ROPE_BASE) ** (-np.arange(half, dtype=np.float32) / np.float32(half))
    ang = (pos[:, None] * inv[None, :]).astype(np.float32)
    cos = np.cos(ang).astype(np.float32)
    sin = np.sin(ang).astype(np.float32)
    cos_t = np.tile(cos, (1, 2 * reps))
    sin_t = np.concatenate([np.tile(-sin, (1, reps)), np.tile(sin, (1, reps))], axis=-1)
    return jnp.asarray(cos_t), jnp.asarray(sin_t)


def _uq_columns():
    half = MLA_ROPE_DIM // 2
    nope = [h * MLA_QK_DIM + np.arange(HEAD_DIM) for h in range(N_HEADS)]
    rope = []
    for p in range(N_HEADS // 2):
        for part in range(2):
            for h in (2 * p, 2 * p + 1):
                rope.append(h * MLA_QK_DIM + HEAD_DIM + part * half + np.arange(half))
    return np.concatenate(nope + rope)


def _take_columns(w, cols):
    cols = np.asarray(cols)
    cuts = np.flatnonzero(np.diff(cols) != 1) + 1
    runs = np.split(cols, cuts)
    return jnp.concatenate([w[:, r[0]:r[-1] + 1] for r in runs], axis=1)


def kernel(x, c, w_ada, b_ada, w_in, ret_gn_w, mla_q_norm_w, w_uq, mla_kv_norm_w, w_ukv, w_out,
           ln1_w, ln1_b, w_up, conv_w, conv_b, w_down, ln2_w, ln2_b):
    B, S, D = x.shape
    assert D == D_MODEL and S % ROW_TILE == 0 and S % ATTN_TQ == 0 and w_ada.shape[0] == DEPTH == 1
    l = 0

    mod = _ada_call(c, w_ada, b_ada[l][None, :]).reshape(B, N_MOD, D)

    w_uq_p = _take_columns(w_uq[l].astype(BF16), _uq_columns())
    w_ukv_p = w_ukv[l].astype(BF16)
    cosr, sinr = _rope_tables(S, HEAD_DIM // 2, 1)
    cosm, sinm = _rope_tables(S, MLA_ROPE_DIM // 2, 2)

    ret, q, k, vt = _inproj_call(
        x, mod, jnp.swapaxes(w_in, 1, 2), ret_gn_w[l][None, :], mla_q_norm_w[l][None, :], mla_kv_norm_w[l][None, :],
        w_uq_p, w_ukv_p, cosr, sinr, cosm, sinm)
    mla = _attn_call(q, k, vt)
    return _mlp_call(
        x, ret, mla, mod, w_out, ln1_w[l][None, :], ln1_b[l][None, :],
        w_up, conv_w[l], conv_b[l][None, :], w_down, ln2_w[l][None, :], ln2_b[l][None, :])
```

```python
import functools
import math

import numpy as np
import jax
import jax.numpy as jnp
from jax import lax
from jax.experimental import pallas as pl
from jax.experimental.pallas import tpu as pltpu

F32 = jnp.float32
BF16 = jnp.bfloat16

D_MODEL = 1024
DEPTH = 1
N_HEADS = 4
HEAD_DIM = 128
RET_WIDTH = N_HEADS * HEAD_DIM
LAT_COL0 = 4 * RET_WIDTH
MLA_RANK = 256
MLA_ROPE_DIM = 64
MLA_QK_DIM = HEAD_DIM + MLA_ROPE_DIM
D_FF = 2816
ROPE_BASE = 10000.0
LN_EPS = 1e-5
RMS_EPS = 1e-6
DN_ALPHA = (2.0 * DEPTH) ** 0.25
N_MOD = 6
LOG_G = [math.log1p(-(2.0 ** (-5.0 - h))) for h in range(N_HEADS)]

LANES = 128
SUBLANES = 8
MXU_DIM = 256
VMEM_BYTES = 64 * 1024 * 1024
VMEM_LIMIT = VMEM_BYTES * 7 // 8

ADA_ROWS = 256
IN_ROW_TILE = 1024
ROW_TILE = 512
RET_CHUNK = MXU_DIM
ATTN_TQ = 1024
ATTN_TK = 512
FF_CHUNK = MXU_DIM
N_FF_CHUNKS = D_FF // FF_CHUNK
STAGE_ROWS_SQ = 256
STAGE_ROWS_UP = 64
STAGE_ROWS_IN = 128
STAGE_SLOTS = 4
QK_PAD = 2 * LANES
BF16_ROWS = 2 * SUBLANES
ACC_ROWS = HEAD_DIM + BF16_ROWS
MASKED = -float("inf")


def _ln_plain(x):
    mu = jnp.mean(x, axis=-1, keepdims=True)
    xc = x - mu
    var = jnp.mean(xc * xc, axis=-1, keepdims=True)
    return xc * lax.rsqrt(var + LN_EPS)


def _rms(x):
    return x * lax.rsqrt(jnp.mean(x * x, axis=-1, keepdims=True) + RMS_EPS)


def _silu(x):
    return x * jax.nn.sigmoid(x)


def _dot(a, b):
    return jnp.dot(a, b, preferred_element_type=F32)


def _dot_nt(a, b):
    return lax.dot_general(a, b, (((1,), (1,)), ((), ())), preferred_element_type=F32)


def _dot_tn(a, b):
    return lax.dot_general(a, b, (((0,), (0,)), ((), ())), preferred_element_type=F32)


def _ada_kernel(c_ref, w_ref, b_ref, o_ref):
    @pl.when(pl.program_id(0) == 0)
    def _():
        o_ref[...] = jnp.broadcast_to(b_ref[...], o_ref.shape)

    cond = _silu(c_ref[...])
    o_ref[...] += _dot(cond.astype(BF16), w_ref[0].astype(BF16))


def _ada_call(c, w_ada, b_ada):
    n_batch = c.shape[0]
    _, d_in, n_out = w_ada.shape
    return pl.pallas_call(
        _ada_kernel,
        grid=(d_in // ADA_ROWS,),
        in_specs=[
            pl.BlockSpec((n_batch, ADA_ROWS), lambda j: (0, j)),
            pl.BlockSpec((1, ADA_ROWS, n_out), lambda j: (0, j, 0)),
            pl.BlockSpec((1, n_out), lambda j: (0, 0)),
        ],
        out_specs=pl.BlockSpec((n_batch, n_out), lambda j: (0, 0)),
        out_shape=jax.ShapeDtypeStruct((n_batch, n_out), F32),
        compiler_params=pltpu.CompilerParams(dimension_semantics=("arbitrary",), vmem_limit_bytes=VMEM_LIMIT),
        name="ada_mod",
    )(c, w_ada, b_ada)


def _rope128(x, cos, sin_signed):
    return x * cos + pltpu.roll(x, LANES // 2, axis=1) * sin_signed


def _decay_matrix(hd, c):
    row = lax.broadcasted_iota(jnp.int32, (c, c), 0)
    col = lax.broadcasted_iota(jnp.int32, (c, c), 1)
    rel = (row - col).astype(F32)
    return jnp.where(rel >= 0.0, jnp.exp(LOG_G[hd] * jnp.maximum(rel, 0.0)), 0.0)


def _retention_head(hd, q, k, v, gate, gnw, decay_ref, state_ref, ret_ref):
    c = decay_ref.shape[1]
    lg = LOG_G[hd]
    n = lax.broadcasted_iota(jnp.int32, (c, 1), 0).astype(F32)
    q_w = jnp.exp(lg * (n + 1.0))
    k_w = jnp.exp(lg * (c - 1.0 - n))
    for i in range(q.shape[0] // c):
        rows = slice(i * c, (i + 1) * c)
        qb = q[rows].astype(BF16)
        vb = v[rows].astype(BF16)
        scores = _dot_nt(qb, k[rows].astype(BF16)) * decay_ref[hd]
        inner = _dot(scores.astype(BF16), vb)
        state = state_ref[hd]
        cross = _dot(qb, state.astype(BF16)) * q_w
        state_ref[hd] = state * math.exp(lg * c) + _dot_tn((k[rows] * k_w).astype(BF16), vb)
        o = _ln_plain(inner + cross) * gnw
        ret_ref[0, rows, hd * HEAD_DIM:(hd + 1) * HEAD_DIM] = (o * gate[rows]).astype(BF16)


def _stream_cast(src_hbm, stage_ref, sem, store, n_rows=None):
    slots, rows = stage_ref.shape[0], stage_ref.shape[1]
    n_rows = src_hbm.shape[0] if n_rows is None else n_rows
    n_chunks = n_rows // rows
    assert n_chunks * rows == n_rows and n_chunks >= slots

    def copy(k):
        slot = k % slots
        return pltpu.make_async_copy(
            src_hbm.at[pl.ds(pl.multiple_of(k * rows, rows), rows), :], stage_ref.at[slot], sem.at[slot])

    for k in range(slots - 1):
        copy(k).start()

    def body(k, _):
        @pl.when(k + slots - 1 < n_chunks)
        def _():
            copy(k + slots - 1).start()

        copy(k).wait()
        store(pl.ds(pl.multiple_of(k * rows, rows), rows), stage_ref[k % slots])
        return 0

    lax.fori_loop(0, n_chunks, body, 0)


def _store_bf16(dst_ref):
    def store(rows, chunk):
        dst_ref[rows, :] = chunk.astype(BF16)
    return store


def _stage_w_in(wt_hbm, win_ref, stage_ref, sem):
    base = LAT_COL0 + 2 * MLA_RANK
    quarter = LANES // 4

    def store(cols, chunk):
        win_ref[:, cols] = chunk.T.astype(BF16)

    _stream_cast(wt_hbm, stage_ref, sem, store, n_rows=base)

    tail_copy = pltpu.make_async_copy(
        wt_hbm.at[pl.ds(base, MLA_ROPE_DIM), :], stage_ref.at[0, pl.ds(0, MLA_ROPE_DIM), :], sem.at[0])
    tail_copy.start()
    tail_copy.wait()
    tail = stage_ref[0, 0:MLA_ROPE_DIM, :].T
    t2 = jnp.concatenate([tail, tail], axis=1)
    lane = lax.broadcasted_iota(jnp.int32, t2.shape, 1)
    middle = jnp.abs(2 * lane - (LANES - 1)) < 2 * quarter
    win_ref[:, base:base + LANES] = jnp.where(middle, pltpu.roll(t2, quarter, axis=1), t2).astype(BF16)


def _inproj_kernel(x_ref, mod_ref, wint_hbm, gnw_ref, qnw_ref, kvnw_ref, wuq_ref, wukv_ref,
                   cosr_ref, sinr_ref, cosm_ref, sinm_ref,
                   ret_ref, q_ref, k_ref, vt_ref, state_ref, decay_ref, win_ref, stage_ref, sem,
                   *, q_scale, rk_scale):
    @pl.when(jnp.logical_and(pl.program_id(0) == 0, pl.program_id(1) == 0))
    def _():
        _stage_w_in(wint_hbm.at[0], win_ref, stage_ref, sem)
        for hd in range(N_HEADS):
            decay_ref[hd] = _decay_matrix(hd, decay_ref.shape[1])

    @pl.when(pl.program_id(1) == 0)
    def _():
        state_ref[...] = jnp.zeros_like(state_ref)

    x = x_ref[0]
    shift = mod_ref[0, 0:1, :]
    scale = mod_ref[0, 1:2, :]
    h = (_ln_plain(x) * (1.0 + scale) + shift).astype(BF16)

    cosr = cosr_ref[...]
    sinr = sinr_ref[...]
    cosm = cosm_ref[...]
    sinm = sinm_ref[...]

    pq = _dot(h, win_ref[:, 0:RET_WIDTH])
    pk = _dot(h, win_ref[:, RET_WIDTH:2 * RET_WIDTH])
    pv = _dot(h, win_ref[:, 2 * RET_WIDTH:3 * RET_WIDTH])
    pg = _silu(_dot(h, win_ref[:, 3 * RET_WIDTH:LAT_COL0]))
    for hd in range(N_HEADS):
        sl = slice(hd * HEAD_DIM, (hd + 1) * HEAD_DIM)
        _retention_head(hd, _rope128(pq[:, sl], cosr, sinr), _rope128(pk[:, sl], cosr, sinr) * rk_scale,
                        pv[:, sl], pg[:, sl], gnw_ref[:, sl], decay_ref, state_ref, ret_ref)

    lat = _dot(h, win_ref[:, LAT_COL0:])
    cq = (_rms(lat[:, 0:MLA_RANK]) * qnw_ref[...]).astype(BF16)
    ckv = (_rms(lat[:, MLA_RANK:2 * MLA_RANK]) * kvnw_ref[...]).astype(BF16)
    kr = _rope128(lat[:, 2 * MLA_RANK:2 * MLA_RANK + LANES], cosm, sinm)
    lane = lax.broadcasted_iota(jnp.int32, kr.shape, 1)
    first_of_pair = (lane % (LANES // 2)) < (LANES // 4)
    kr_even = jnp.where(first_of_pair, kr, 0.0).astype(BF16)
    kr_odd = jnp.where(first_of_pair, 0.0, kr).astype(BF16)

    qf = _dot(cq, wuq_ref[...])
    kvf = _dot(ckv, wukv_ref[...])
    for hd in range(N_HEADS):
        vt_ref[0, hd] = kvf[:, (2 * hd + 1) * HEAD_DIM:(2 * hd + 2) * HEAD_DIM].T.astype(BF16)
    for p in range(N_HEADS // 2):
        qr = qf[:, RET_WIDTH + p * LANES:RET_WIDTH + (p + 1) * LANES]
        qr = (_rope128(qr, cosm, sinm) * q_scale).astype(BF16)
        for hd in (2 * p, 2 * p + 1):
            sl = slice(hd * HEAD_DIM, (hd + 1) * HEAD_DIM)
            q_ref[0, :, hd * QK_PAD:hd * QK_PAD + LANES] = (qf[:, sl] * q_scale).astype(BF16)
            q_ref[0, :, hd * QK_PAD + LANES:(hd + 1) * QK_PAD] = qr
            k_ref[0, :, hd * QK_PAD:hd * QK_PAD + LANES] = kvf[:, 2 * hd * HEAD_DIM:(2 * hd + 1) * HEAD_DIM].astype(BF16)
            k_ref[0, :, hd * QK_PAD + LANES:(hd + 1) * QK_PAD] = kr_even if hd % 2 == 0 else kr_odd


def _const_spec(shape):
    nd = len(shape)
    return pl.BlockSpec(shape, lambda *_: (0,) * nd, pipeline_mode=pl.Buffered(1))


def _inproj_call(x, mod, w_in, gnw, qnw, kvnw, w_uq, w_ukv, cosr, sinr, cosm, sinm):
    B, S, D = x.shape
    tm = IN_ROW_TILE
    q_scale = (MLA_QK_DIM ** -0.5) * math.log2(math.e)
    rk_scale = HEAD_DIM ** -0.5
    row = lambda w: pl.BlockSpec((1, tm, w), lambda b, s: (b, s, 0))
    tab = pl.BlockSpec((tm, LANES), lambda b, s: (s, 0))
    out_shapes = (
        [jax.ShapeDtypeStruct((B, S, RET_WIDTH), BF16)]
        + [jax.ShapeDtypeStruct((B, S, N_HEADS * QK_PAD), BF16)] * 2
        + [jax.ShapeDtypeStruct((B, N_HEADS, HEAD_DIM, S), BF16)]
    )
    vt_spec = pl.BlockSpec((1, N_HEADS, HEAD_DIM, tm), lambda b, s: (b, 0, 0, s))
    return pl.pallas_call(
        functools.partial(_inproj_kernel, q_scale=q_scale, rk_scale=rk_scale),
        grid=(B, S // tm),
        in_specs=[
            row(D),
            pl.BlockSpec((1, N_MOD, D), lambda b, s: (b, 0, 0)),
            pl.BlockSpec(memory_space=pl.ANY), _const_spec(gnw.shape),
            _const_spec(qnw.shape), _const_spec(kvnw.shape),
            _const_spec(w_uq.shape), _const_spec(w_ukv.shape),
            tab, tab, tab, tab,
        ],
        out_specs=[row(RET_WIDTH)] + [row(N_HEADS * QK_PAD)] * 2 + [vt_spec],
        out_shape=out_shapes,
        scratch_shapes=[
            pltpu.VMEM((N_HEADS, HEAD_DIM, HEAD_DIM), F32),
            pltpu.VMEM((N_HEADS, RET_CHUNK, RET_CHUNK), F32),
            pltpu.VMEM((D, LAT_COL0 + 2 * MLA_RANK + LANES), BF16),
            pltpu.VMEM((STAGE_SLOTS, STAGE_ROWS_IN, w_in.shape[-1]), F32),
            pltpu.SemaphoreType.DMA((STAGE_SLOTS,)),
        ],
        compiler_params=pltpu.CompilerParams(
            dimension_semantics=("arbitrary", "arbitrary"), vmem_limit_bytes=VMEM_LIMIT),
        name="inproj_retention",
    )(x, mod, w_in, gnw, qnw, kvnw, w_uq, w_ukv, cosr, sinr, cosm, sinm)


def _attn_kernel(q_ref, k_ref, vt_ref, o_ref, qt_ref, s_buf, p_buf, acc_ref, *, tq, tk):
    qi = pl.program_id(2)
    qt_ref[...] = q_ref[0].T

    ones = jnp.ones((BF16_ROWS, tk), BF16)

    def k_blk(j):
        return k_ref[0, pl.ds(pl.multiple_of(j * tk, tk), tk), :]

    def v_blk(j):
        return jnp.concatenate([vt_ref[0, 0, :, pl.ds(pl.multiple_of(j * tk, tk), tk)], ones], axis=0)

    def scores(j, slot):
        s = _dot(k_blk(j), qt_ref[...])
        s_buf[slot] = s
        return jnp.max(s, axis=0, keepdims=True)

    def value_update(j, slot, alpha):
        acc_ref[...] = alpha * acc_ref[...] + _dot(v_blk(j), p_buf[slot])

    def softmax(s, m, block_max=None):
        if block_max is None:
            block_max = jnp.max(s, axis=0, keepdims=True)
        m_new = jnp.maximum(m, block_max)
        return m_new, jnp.exp2(m - m_new), jnp.exp2(s - m_new).astype(BF16)

    def two_blocks(u, carry):
        alpha_prev, m, max_even = carry
        t = 2 * u
        max_odd = scores(t + 1, 1)
        value_update(jnp.maximum(t - 1, 0), 1, alpha_prev)
        m, alpha_even, p_buf[0] = softmax(s_buf[0], m, max_even)
        max_next = scores(t + 2, 0)
        value_update(t, 0, alpha_even)
        m, alpha_odd, p_buf[1] = softmax(s_buf[1], m, max_odd)
        return alpha_odd, m, max_next

    max_0 = scores(0, 0)
    p_buf[1] = jnp.zeros((tk, tq), BF16)
    acc_ref[...] = jnp.zeros_like(acc_ref)
    init = (jnp.ones((1, tq), F32), jnp.full((1, tq), MASKED, F32), max_0)
    alpha_prev, m, _ = lax.fori_loop(0, qi, two_blocks, init)

    t = 2 * qi
    key = lax.broadcasted_iota(jnp.int32, (tk, tk), 0)
    qry = lax.broadcasted_iota(jnp.int32, (tk, tk), 1)
    causal = key <= qry
    s_right = _dot(k_blk(t + 1), qt_ref[:, tk:])
    value_update(jnp.maximum(t - 1, 0), 1, alpha_prev)
    s = s_buf[0]
    s = jnp.concatenate([jnp.where(causal, s[:, :tk], MASKED), s[:, tk:]], axis=1)
    m, alpha, p_buf[0] = softmax(s, m)
    value_update(t, 0, alpha)
    _, alpha_r, p_r = softmax(jnp.where(causal, s_right, MASKED), m[:, tk:])
    acc_ref[:, tk:] = alpha_r * acc_ref[:, tk:] + _dot(v_blk(t + 1), p_r)
    acc = acc_ref[...]
    o_ref[0] = (acc[:HEAD_DIM] / acc[HEAD_DIM:HEAD_DIM + 1]).T.astype(BF16)


def _attn_call(q, k, vt):
    B, S, _ = q.shape
    tq, tk = ATTN_TQ, ATTN_TK
    assert tq == 2 * tk
    return pl.pallas_call(
        functools.partial(_attn_kernel, tq=tq, tk=tk),
        grid=(B, N_HEADS, S // tq),
        in_specs=[
            pl.BlockSpec((1, tq, QK_PAD), lambda b, h, i: (b, i, h)),
            pl.BlockSpec((1, S, QK_PAD), lambda b, h, i: (b, 0, h)),
            pl.BlockSpec((1, 1, HEAD_DIM, S), lambda b, h, i: (b, h, 0, 0)),
        ],
        out_specs=pl.BlockSpec((1, tq, HEAD_DIM), lambda b, h, i: (b, i, h)),
        out_shape=jax.ShapeDtypeStruct((B, S, N_HEADS * HEAD_DIM), BF16),
        scratch_shapes=[
            pltpu.VMEM((QK_PAD, tq), BF16),
            pltpu.VMEM((2, tk, tq), F32),
            pltpu.VMEM((2, tk, tq), BF16),
            pltpu.VMEM((ACC_ROWS, tq), F32),
        ],
        compiler_params=pltpu.CompilerParams(
            dimension_semantics=("arbitrary", "arbitrary", "arbitrary"), vmem_limit_bytes=VMEM_LIMIT),
        name="mla_attn",
    )(q, k, vt)


def _mlp_kernel(x_ref, ret_ref, mla_ref, mod_ref, wo32_hbm, ln1w_ref, ln1b_ref, wup32_hbm, cw_ref, cb_ref,
                wdn32_hbm, ln2w_ref, ln2b_ref, o_ref, carry_ref, ubuf_ref, act_ref,
                wo_ref, wup_ref, wdn_ref, stage_sq_ref, stage_up_ref, sem, *, tm):
    @pl.when(jnp.logical_and(pl.program_id(0) == 0, pl.program_id(1) == 0))
    def _():
        _stream_cast(wo32_hbm.at[0], stage_sq_ref, sem, _store_bf16(wo_ref))
        _stream_cast(wup32_hbm.at[0], stage_up_ref, sem, _store_bf16(wup_ref))
        _stream_cast(wdn32_hbm.at[0], stage_sq_ref, sem, _store_bf16(wdn_ref))

    @pl.when(pl.program_id(1) == 0)
    def _():
        carry_ref[...] = jnp.zeros_like(carry_ref)

    gate1 = mod_ref[0, 2:3, :]
    shift2 = mod_ref[0, 3:4, :]
    scale2 = mod_ref[0, 4:5, :]
    gate2 = mod_ref[0, 5:6, :]

    y = _dot(ret_ref[0], wo_ref[0:RET_WIDTH, :]) + _dot(mla_ref[0], wo_ref[RET_WIDTH:2 * RET_WIDTH, :])
    x1 = _ln_plain(DN_ALPHA * x_ref[0] + (1.0 + gate1) * y) * ln1w_ref[...] + ln1b_ref[...]
    h2 = (_ln_plain(x1) * (1.0 + scale2) + shift2).astype(BF16)

    def conv(u, col0, kind):
        outs = []
        for g in range(FF_CHUNK // LANES):
            cols = slice(col0 + g * LANES, col0 + (g + 1) * LANES)
            ug = u[:, g * LANES:(g + 1) * LANES]
            buf = ubuf_ref.at[kind, g]
            buf[0:SUBLANES, :] = carry_ref[:, cols]
            buf[SUBLANES:SUBLANES + tm, :] = ug
            carry_ref[:, cols] = ug[tm - SUBLANES:tm, :]
            u1 = buf[SUBLANES - 1:SUBLANES - 1 + tm, :]
            u2 = buf[SUBLANES - 2:SUBLANES - 2 + tm, :]
            outs.append(cb_ref[:, cols] + cw_ref[2:3, cols] * ug + cw_ref[1:2, cols] * u1
                        + cw_ref[0:1, cols] * u2)
        return jnp.concatenate(outs, axis=1)

    for c in range(N_FF_CHUNKS):
        gcol0 = c * FF_CHUNK
        vcol0 = D_FF + c * FF_CHUNK
        g = conv(_dot(h2, wup_ref[:, gcol0:gcol0 + FF_CHUNK]), gcol0, 0)
        val = conv(_dot(h2, wup_ref[:, vcol0:vcol0 + FF_CHUNK]), vcol0, 1)
        act_ref[:, gcol0:gcol0 + FF_CHUNK] = (_silu(g) * val).astype(BF16)

    y2 = _dot(act_ref[...], wdn_ref[...])
    o_ref[0] = _ln_plain(DN_ALPHA * x1 + (1.0 + gate2) * y2) * ln2w_ref[...] + ln2b_ref[...]


def _mlp_call(x, ret, mla, mod, w_out, ln1w, ln1b, w_up, conv_w, conv_b, w_down, ln2w, ln2b):
    B, S, D = x.shape
    tm = ROW_TILE
    row = lambda w: pl.BlockSpec((1, tm, w), lambda b, s: (b, s, 0))
    hbm = pl.BlockSpec(memory_space=pl.ANY)
    return pl.pallas_call(
        functools.partial(_mlp_kernel, tm=tm),
        grid=(B, S // tm),
        in_specs=[
            row(D), row(RET_WIDTH), row(RET_WIDTH),
            pl.BlockSpec((1, N_MOD, D), lambda b, s: (b, 0, 0)),
            hbm, _const_spec(ln1w.shape), _const_spec(ln1b.shape),
            hbm, _const_spec(conv_w.shape), _const_spec(conv_b.shape),
            hbm, _const_spec(ln2w.shape), _const_spec(ln2b.shape),
        ],
        out_specs=row(D),
        out_shape=jax.ShapeDtypeStruct((B, S, D), F32),
        scratch_shapes=[
            pltpu.VMEM((SUBLANES, 2 * D_FF), F32),
            pltpu.VMEM((2, FF_CHUNK // LANES, SUBLANES + tm, LANES), F32),
            pltpu.VMEM((tm, D_FF), BF16),
            pltpu.VMEM(w_out.shape[1:], BF16),
            pltpu.VMEM(w_up.shape[1:], BF16),
            pltpu.VMEM(w_down.shape[1:], BF16),
            pltpu.VMEM((STAGE_SLOTS, STAGE_ROWS_SQ, D), F32),
            pltpu.VMEM((STAGE_SLOTS, STAGE_ROWS_UP, 2 * D_FF), F32),
            pltpu.SemaphoreType.DMA((STAGE_SLOTS,)),
        ],
        compiler_params=pltpu.CompilerParams(
            dimension_semantics=("arbitrary", "arbitrary"), vmem_limit_bytes=VMEM_LIMIT),
        name="outproj_mlp",
    )(x, ret, mla, mod, w_out, ln1w, ln1b, w_up, conv_w, conv_b, w_down, ln2w, ln2b)


def _rope_tables(seq, half, reps):
    pos = np.arange(seq, dtype=np.float32)
    inv = np.float32(ROPE_BASE) ** (-np.arange(half, dtype=np.float32) / np.float32(half))
    ang = (pos[:, None] * inv[None, :]).astype(np.float32)
    cos = np.cos(ang).astype(np.float32)
    sin = np.sin(ang).astype(np.float32)
    cos_t = np.tile(cos, (1, 2 * reps))
    sin_t = np.concatenate([np.tile(-sin, (1, reps)), np.tile(sin, (1, reps))], axis=-1)
    return jnp.asarray(cos_t), jnp.asarray(sin_t)


def _uq_columns():
    half = MLA_ROPE_DIM // 2
    nope = [h * MLA_QK_DIM + np.arange(HEAD_DIM) for h in range(N_HEADS)]
    rope = []
    for p in range(N_HEADS // 2):
        for part in range(2):
            for h in (2 * p, 2 * p + 1):
                rope.append(h * MLA_QK_DIM + HEAD_DIM + part * half + np.arange(half))
    return np.concatenate(nope + rope)


def _take_columns(w, cols):
    cols = np.asarray(cols)
    cuts = np.flatnonzero(np.diff(cols) != 1) + 1
    runs = np.split(cols, cuts)
    return jnp.concatenate([w[:, r[0]:r[-1] + 1] for r in runs], axis=1)


def kernel(x, c, w_ada, b_ada, w_in, ret_gn_w, mla_q_norm_w, w_uq, mla_kv_norm_w, w_ukv, w_out,
           ln1_w, ln1_b, w_up, conv_w, conv_b, w_down, ln2_w, ln2_b):
    B, S, D = x.shape
    assert D == D_MODEL and S % IN_ROW_TILE == 0 and S % ROW_TILE == 0 and S % ATTN_TQ == 0
    assert w_ada.shape[0] == DEPTH == 1
    l = 0

    mod = _ada_call(c, w_ada, b_ada[l][None, :]).reshape(B, N_MOD, D)

    w_uq_p = _take_columns(w_uq[l].astype(BF16), _uq_columns())
    w_ukv_p = w_ukv[l].astype(BF16)
    cosr, sinr = _rope_tables(S, HEAD_DIM // 2, 1)
    cosm, sinm = _rope_tables(S, MLA_ROPE_DIM // 2, 2)

    ret, q, k, vt = _inproj_call(
        x, mod, jnp.swapaxes(w_in, 1, 2), ret_gn_w[l][None, :], mla_q_norm_w[l][None, :], mla_kv_norm_w[l][None, :],
        w_uq_p, w_ukv_p, cosr, sinr, cosm, sinm)
    mla = _attn_call(q, k, vt)
    return _mlp_call(
        x, ret, mla, mod, w_out, ln1_w[l][None, :], ln1_b[l][None, :],
        w_up, conv_w[l], conv_b[l][None, :], w_down, ln2_w[l][None, :], ln2_b[l][None, :])
```

```python
import functools
import math

import numpy as np
import jax
import jax.numpy as jnp
from jax import lax
from jax.experimental import pallas as pl
from jax.experimental.pallas import tpu as pltpu

F32 = jnp.float32
BF16 = jnp.bfloat16

D_MODEL = 1024
DEPTH = 1
N_HEADS = 4
HEAD_DIM = 128
RET_WIDTH = N_HEADS * HEAD_DIM
LAT_COL0 = 4 * RET_WIDTH
MLA_RANK = 256
MLA_ROPE_DIM = 64
MLA_QK_DIM = HEAD_DIM + MLA_ROPE_DIM
D_FF = 2816
ROPE_BASE = 10000.0
LN_EPS = 1e-5
RMS_EPS = 1e-6
DN_ALPHA = (2.0 * DEPTH) ** 0.25
N_MOD = 6
LOG_G = [math.log1p(-(2.0 ** (-5.0 - h))) for h in range(N_HEADS)]

LANES = 128
SUBLANES = 8
MXU_DIM = 256
VMEM_BYTES = 64 * 1024 * 1024
VMEM_LIMIT = VMEM_BYTES * 7 // 8

ADA_ROWS = 256
IN_ROW_TILE = 1024
ROW_TILE = 512
RET_CHUNK = MXU_DIM
ATTN_TQ = 2048
ATTN_TK = 1024
FF_CHUNK = MXU_DIM
N_FF_CHUNKS = D_FF // FF_CHUNK
STAGE_ROWS_SQ = 256
STAGE_ROWS_UP = 64
STAGE_ROWS_IN = 128
STAGE_SLOTS = 4
QK_PAD = 2 * LANES
BF16_ROWS = 2 * SUBLANES
ACC_ROWS = HEAD_DIM + BF16_ROWS
MASKED = -float("inf")


def _ln_plain(x):
    mu = jnp.mean(x, axis=-1, keepdims=True)
    xc = x - mu
    var = jnp.mean(xc * xc, axis=-1, keepdims=True)
    return xc * lax.rsqrt(var + LN_EPS)


def _rms(x):
    return x * lax.rsqrt(jnp.mean(x * x, axis=-1, keepdims=True) + RMS_EPS)


def _silu(x):
    return x * jax.nn.sigmoid(x)


def _dot(a, b):
    return jnp.dot(a, b, preferred_element_type=F32)


def _dot_nt(a, b):
    return lax.dot_general(a, b, (((1,), (1,)), ((), ())), preferred_element_type=F32)


def _dot_tn(a, b):
    return lax.dot_general(a, b, (((0,), (0,)), ((), ())), preferred_element_type=F32)


def _ada_kernel(c_ref, w_ref, b_ref, o_ref):
    @pl.when(pl.program_id(0) == 0)
    def _():
        o_ref[...] = jnp.broadcast_to(b_ref[...], o_ref.shape)

    cond = _silu(c_ref[...])
    o_ref[...] += _dot(cond.astype(BF16), w_ref[0].astype(BF16))


def _ada_call(c, w_ada, b_ada):
    n_batch = c.shape[0]
    _, d_in, n_out = w_ada.shape
    return pl.pallas_call(
        _ada_kernel,
        grid=(d_in // ADA_ROWS,),
        in_specs=[
            pl.BlockSpec((n_batch, ADA_ROWS), lambda j: (0, j)),
            pl.BlockSpec((1, ADA_ROWS, n_out), lambda j: (0, j, 0)),
            pl.BlockSpec((1, n_out), lambda j: (0, 0)),
        ],
        out_specs=pl.BlockSpec((n_batch, n_out), lambda j: (0, 0)),
        out_shape=jax.ShapeDtypeStruct((n_batch, n_out), F32),
        compiler_params=pltpu.CompilerParams(dimension_semantics=("arbitrary",), vmem_limit_bytes=VMEM_LIMIT),
        name="ada_mod",
    )(c, w_ada, b_ada)


def _rope128(x, cos, sin_signed):
    return x * cos + pltpu.roll(x, LANES // 2, axis=1) * sin_signed


def _decay_matrix(hd, c):
    row = lax.broadcasted_iota(jnp.int32, (c, c), 0)
    col = lax.broadcasted_iota(jnp.int32, (c, c), 1)
    rel = (row - col).astype(F32)
    return jnp.where(rel >= 0.0, jnp.exp(LOG_G[hd] * jnp.maximum(rel, 0.0)), 0.0)


def _retention_head(hd, q, k, v, gate, gnw, decay_ref, state_ref, ret_ref):
    c = decay_ref.shape[1]
    lg = LOG_G[hd]
    n = lax.broadcasted_iota(jnp.int32, (c, 1), 0).astype(F32)
    q_w = jnp.exp(lg * (n + 1.0))
    k_w = jnp.exp(lg * (c - 1.0 - n))
    for i in range(q.shape[0] // c):
        rows = slice(i * c, (i + 1) * c)
        qb = q[rows].astype(BF16)
        vb = v[rows].astype(BF16)
        scores = _dot_nt(qb, k[rows].astype(BF16)) * decay_ref[hd]
        inner = _dot(scores.astype(BF16), vb)
        state = state_ref[hd]
        cross = _dot(qb, state.astype(BF16)) * q_w
        state_ref[hd] = state * math.exp(lg * c) + _dot_tn((k[rows] * k_w).astype(BF16), vb)
        o = _ln_plain(inner + cross) * gnw
        ret_ref[0, rows, hd * HEAD_DIM:(hd + 1) * HEAD_DIM] = (o * gate[rows]).astype(BF16)


def _stream_cast(src_hbm, stage_ref, sem, store, n_rows=None):
    slots, rows = stage_ref.shape[0], stage_ref.shape[1]
    n_rows = src_hbm.shape[0] if n_rows is None else n_rows
    n_chunks = n_rows // rows
    assert n_chunks * rows == n_rows and n_chunks >= slots

    def copy(k):
        slot = k % slots
        return pltpu.make_async_copy(
            src_hbm.at[pl.ds(pl.multiple_of(k * rows, rows), rows), :], stage_ref.at[slot], sem.at[slot])

    for k in range(slots - 1):
        copy(k).start()

    def body(k, _):
        @pl.when(k + slots - 1 < n_chunks)
        def _():
            copy(k + slots - 1).start()

        copy(k).wait()
        store(pl.ds(pl.multiple_of(k * rows, rows), rows), stage_ref[k % slots])
        return 0

    lax.fori_loop(0, n_chunks, body, 0)


def _store_bf16(dst_ref):
    def store(rows, chunk):
        dst_ref[rows, :] = chunk.astype(BF16)
    return store


def _stage_w_in(wt_hbm, win_ref, stage_ref, sem):
    base = LAT_COL0 + 2 * MLA_RANK
    quarter = LANES // 4

    def store(cols, chunk):
        win_ref[:, cols] = chunk.T.astype(BF16)

    _stream_cast(wt_hbm, stage_ref, sem, store, n_rows=base)

    tail_copy = pltpu.make_async_copy(
        wt_hbm.at[pl.ds(base, MLA_ROPE_DIM), :], stage_ref.at[0, pl.ds(0, MLA_ROPE_DIM), :], sem.at[0])
    tail_copy.start()
    tail_copy.wait()
    tail = stage_ref[0, 0:MLA_ROPE_DIM, :].T
    t2 = jnp.concatenate([tail, tail], axis=1)
    lane = lax.broadcasted_iota(jnp.int32, t2.shape, 1)
    middle = jnp.abs(2 * lane - (LANES - 1)) < 2 * quarter
    win_ref[:, base:base + LANES] = jnp.where(middle, pltpu.roll(t2, quarter, axis=1), t2).astype(BF16)


def _inproj_kernel(x_ref, mod_ref, wint_hbm, gnw_ref, qnw_ref, kvnw_ref, wuq_ref, wukv_ref,
                   cosr_ref, sinr_ref, cosm_ref, sinm_ref,
                   ret_ref, q_ref, k_ref, vt_ref, state_ref, decay_ref, win_ref, stage_ref, sem,
                   *, q_scale, rk_scale):
    @pl.when(jnp.logical_and(pl.program_id(0) == 0, pl.program_id(1) == 0))
    def _():
        _stage_w_in(wint_hbm.at[0], win_ref, stage_ref, sem)
        for hd in range(N_HEADS):
            decay_ref[hd] = _decay_matrix(hd, decay_ref.shape[1])

    @pl.when(pl.program_id(1) == 0)
    def _():
        state_ref[...] = jnp.zeros_like(state_ref)

    x = x_ref[0]
    shift = mod_ref[0, 0:1, :]
    scale = mod_ref[0, 1:2, :]
    h = (_ln_plain(x) * (1.0 + scale) + shift).astype(BF16)

    cosr = cosr_ref[...]
    sinr = sinr_ref[...]
    cosm = cosm_ref[...]
    sinm = sinm_ref[...]

    pq = _dot(h, win_ref[:, 0:RET_WIDTH])
    pk = _dot(h, win_ref[:, RET_WIDTH:2 * RET_WIDTH])
    pv = _dot(h, win_ref[:, 2 * RET_WIDTH:3 * RET_WIDTH])
    pg = _silu(_dot(h, win_ref[:, 3 * RET_WIDTH:LAT_COL0]))
    for hd in range(N_HEADS):
        sl = slice(hd * HEAD_DIM, (hd + 1) * HEAD_DIM)
        _retention_head(hd, _rope128(pq[:, sl], cosr, sinr), _rope128(pk[:, sl], cosr, sinr) * rk_scale,
                        pv[:, sl], pg[:, sl], gnw_ref[:, sl], decay_ref, state_ref, ret_ref)

    lat = _dot(h, win_ref[:, LAT_COL0:])
    cq = (_rms(lat[:, 0:MLA_RANK]) * qnw_ref[...]).astype(BF16)
    ckv = (_rms(lat[:, MLA_RANK:2 * MLA_RANK]) * kvnw_ref[...]).astype(BF16)
    kr = _rope128(lat[:, 2 * MLA_RANK:2 * MLA_RANK + LANES], cosm, sinm)
    lane = lax.broadcasted_iota(jnp.int32, kr.shape, 1)
    first_of_pair = (lane % (LANES // 2)) < (LANES // 4)
    kr_even = jnp.where(first_of_pair, kr, 0.0).astype(BF16)
    kr_odd = jnp.where(first_of_pair, 0.0, kr).astype(BF16)

    qf = _dot(cq, wuq_ref[...])
    kvf = _dot(ckv, wukv_ref[...])
    for hd in range(N_HEADS):
        vt_ref[0, hd] = kvf[:, (2 * hd + 1) * HEAD_DIM:(2 * hd + 2) * HEAD_DIM].T.astype(BF16)
    for p in range(N_HEADS // 2):
        qr = qf[:, RET_WIDTH + p * LANES:RET_WIDTH + (p + 1) * LANES]
        qr = (_rope128(qr, cosm, sinm) * q_scale).astype(BF16)
        for hd in (2 * p, 2 * p + 1):
            sl = slice(hd * HEAD_DIM, (hd + 1) * HEAD_DIM)
            q_ref[0, :, hd * QK_PAD:hd * QK_PAD + LANES] = (qf[:, sl] * q_scale).astype(BF16)
            q_ref[0, :, hd * QK_PAD + LANES:(hd + 1) * QK_PAD] = qr
            k_ref[0, :, hd * QK_PAD:hd * QK_PAD + LANES] = kvf[:, 2 * hd * HEAD_DIM:(2 * hd + 1) * HEAD_DIM].astype(BF16)
            k_ref[0, :, hd * QK_PAD + LANES:(hd + 1) * QK_PAD] = kr_even if hd % 2 == 0 else kr_odd


def _const_spec(shape):
    nd = len(shape)
    return pl.BlockSpec(shape, lambda *_: (0,) * nd, pipeline_mode=pl.Buffered(1))


def _inproj_call(x, mod, w_in, gnw, qnw, kvnw, w_uq, w_ukv, cosr, sinr, cosm, sinm):
    B, S, D = x.shape
    tm = IN_ROW_TILE
    q_scale = (MLA_QK_DIM ** -0.5) * math.log2(math.e)
    rk_scale = HEAD_DIM ** -0.5
    row = lambda w: pl.BlockSpec((1, tm, w), lambda b, s: (b, s, 0))
    tab = pl.BlockSpec((tm, LANES), lambda b, s: (s, 0))
    out_shapes = (
        [jax.ShapeDtypeStruct((B, S, RET_WIDTH), BF16)]
        + [jax.ShapeDtypeStruct((B, S, N_HEADS * QK_PAD), BF16)] * 2
        + [jax.ShapeDtypeStruct((B, N_HEADS, HEAD_DIM, S), BF16)]
    )
    vt_spec = pl.BlockSpec((1, N_HEADS, HEAD_DIM, tm), lambda b, s: (b, 0, 0, s))
    return pl.pallas_call(
        functools.partial(_inproj_kernel, q_scale=q_scale, rk_scale=rk_scale),
        grid=(B, S // tm),
        in_specs=[
            row(D),
            pl.BlockSpec((1, N_MOD, D), lambda b, s: (b, 0, 0)),
            pl.BlockSpec(memory_space=pl.ANY), _const_spec(gnw.shape),
            _const_spec(qnw.shape), _const_spec(kvnw.shape),
            _const_spec(w_uq.shape), _const_spec(w_ukv.shape),
            tab, tab, tab, tab,
        ],
        out_specs=[row(RET_WIDTH)] + [row(N_HEADS * QK_PAD)] * 2 + [vt_spec],
        out_shape=out_shapes,
        scratch_shapes=[
            pltpu.VMEM((N_HEADS, HEAD_DIM, HEAD_DIM), F32),
            pltpu.VMEM((N_HEADS, RET_CHUNK, RET_CHUNK), F32),
            pltpu.VMEM((D, LAT_COL0 + 2 * MLA_RANK + LANES), BF16),
            pltpu.VMEM((STAGE_SLOTS, STAGE_ROWS_IN, w_in.shape[-1]), F32),
            pltpu.SemaphoreType.DMA((STAGE_SLOTS,)),
        ],
        compiler_params=pltpu.CompilerParams(
            dimension_semantics=("arbitrary", "arbitrary"), vmem_limit_bytes=VMEM_LIMIT),
        name="inproj_retention",
    )(x, mod, w_in, gnw, qnw, kvnw, w_uq, w_ukv, cosr, sinr, cosm, sinm)


def _attn_kernel(q_ref, k_ref, vt_ref, o_ref, qt_ref, s_buf, p_buf, acc_ref, *, tq, tk):
    qi = pl.program_id(2)
    qt_ref[...] = q_ref[0].T

    ones = jnp.ones((BF16_ROWS, tk), BF16)

    def k_blk(j):
        return k_ref[0, pl.ds(pl.multiple_of(j * tk, tk), tk), :]

    def v_blk(j):
        return jnp.concatenate([vt_ref[0, 0, :, pl.ds(pl.multiple_of(j * tk, tk), tk)], ones], axis=0)

    def scores(j, slot):
        s = _dot(k_blk(j), qt_ref[...])
        s_buf[slot] = s
        return jnp.max(s, axis=0, keepdims=True)

    def value_update(j, slot, alpha):
        acc_ref[...] = alpha * acc_ref[...] + _dot(v_blk(j), p_buf[slot])

    def softmax(s, m, block_max=None):
        if block_max is None:
            block_max = jnp.max(s, axis=0, keepdims=True)
        m_new = jnp.maximum(m, block_max)
        return m_new, jnp.exp2(m - m_new), jnp.exp2(s - m_new).astype(BF16)

    def two_blocks(u, carry):
        alpha_prev, m, max_even = carry
        t = 2 * u
        max_odd = scores(t + 1, 1)
        value_update(jnp.maximum(t - 1, 0), 1, alpha_prev)
        m, alpha_even, p_buf[0] = softmax(s_buf[0], m, max_even)
        max_next = scores(t + 2, 0)
        value_update(t, 0, alpha_even)
        m, alpha_odd, p_buf[1] = softmax(s_buf[1], m, max_odd)
        return alpha_odd, m, max_next

    max_0 = scores(0, 0)
    p_buf[1] = jnp.zeros((tk, tq), BF16)
    acc_ref[...] = jnp.zeros_like(acc_ref)
    init = (jnp.ones((1, tq), F32), jnp.full((1, tq), MASKED, F32), max_0)
    alpha_prev, m, _ = lax.fori_loop(0, qi, two_blocks, init)

    t = 2 * qi
    key = lax.broadcasted_iota(jnp.int32, (tk, tk), 0)
    qry = lax.broadcasted_iota(jnp.int32, (tk, tk), 1)
    causal = key <= qry
    s_right = _dot(k_blk(t + 1), qt_ref[:, tk:])
    value_update(jnp.maximum(t - 1, 0), 1, alpha_prev)
    s = s_buf[0]
    s = jnp.concatenate([jnp.where(causal, s[:, :tk], MASKED), s[:, tk:]], axis=1)
    m, alpha, p_buf[0] = softmax(s, m)
    value_update(t, 0, alpha)
    _, alpha_r, p_r = softmax(jnp.where(causal, s_right, MASKED), m[:, tk:])
    acc_ref[:, tk:] = alpha_r * acc_ref[:, tk:] + _dot(v_blk(t + 1), p_r)
    acc = acc_ref[...]
    o_ref[0] = (acc[:HEAD_DIM] / acc[HEAD_DIM:HEAD_DIM + 1]).T.astype(BF16)


def _attn_call(q, k, vt):
    B, S, _ = q.shape
    tq, tk = ATTN_TQ, ATTN_TK
    assert tq == 2 * tk
    return pl.pallas_call(
        functools.partial(_attn_kernel, tq=tq, tk=tk),
        grid=(B, N_HEADS, S // tq),
        in_specs=[
            pl.BlockSpec((1, tq, QK_PAD), lambda b, h, i: (b, i, h)),
            pl.BlockSpec((1, S, QK_PAD), lambda b, h, i: (b, 0, h)),
            pl.BlockSpec((1, 1, HEAD_DIM, S), lambda b, h, i: (b, h, 0, 0)),
        ],
        out_specs=pl.BlockSpec((1, tq, HEAD_DIM), lambda b, h, i: (b, i, h)),
        out_shape=jax.ShapeDtypeStruct((B, S, N_HEADS * HEAD_DIM), BF16),
        scratch_shapes=[
            pltpu.VMEM((QK_PAD, tq), BF16),
            pltpu.VMEM((2, tk, tq), F32),
            pltpu.VMEM((2, tk, tq), BF16),
            pltpu.VMEM((ACC_ROWS, tq), F32),
        ],
        compiler_params=pltpu.CompilerParams(
            dimension_semantics=("arbitrary", "arbitrary", "arbitrary"), vmem_limit_bytes=VMEM_LIMIT),
        name="mla_attn",
    )(q, k, vt)


def _mlp_kernel(x_ref, ret_ref, mla_ref, mod_ref, wo32_hbm, ln1w_ref, ln1b_ref, wup32_hbm, cw_ref, cb_ref,
                wdn32_hbm, ln2w_ref, ln2b_ref, o_ref, carry_ref, ubuf_ref, act_ref,
                wo_ref, wup_ref, wdn_ref, stage_sq_ref, stage_up_ref, sem, *, tm):
    @pl.when(jnp.logical_and(pl.program_id(0) == 0, pl.program_id(1) == 0))
    def _():
        _stream_cast(wo32_hbm.at[0], stage_sq_ref, sem, _store_bf16(wo_ref))
        _stream_cast(wup32_hbm.at[0], stage_up_ref, sem, _store_bf16(wup_ref))
        _stream_cast(wdn32_hbm.at[0], stage_sq_ref, sem, _store_bf16(wdn_ref))

    @pl.when(pl.program_id(1) == 0)
    def _():
        carry_ref[...] = jnp.zeros_like(carry_ref)

    gate1 = mod_ref[0, 2:3, :]
    shift2 = mod_ref[0, 3:4, :]
    scale2 = mod_ref[0, 4:5, :]
    gate2 = mod_ref[0, 5:6, :]

    y = _dot(ret_ref[0], wo_ref[0:RET_WIDTH, :]) + _dot(mla_ref[0], wo_ref[RET_WIDTH:2 * RET_WIDTH, :])
    x1 = _ln_plain(DN_ALPHA * x_ref[0] + (1.0 + gate1) * y) * ln1w_ref[...] + ln1b_ref[...]
    h2 = (_ln_plain(x1) * (1.0 + scale2) + shift2).astype(BF16)

    def conv(u, col0, kind):
        outs = []
        for g in range(FF_CHUNK // LANES):
            cols = slice(col0 + g * LANES, col0 + (g + 1) * LANES)
            ug = u[:, g * LANES:(g + 1) * LANES]
            buf = ubuf_ref.at[kind, g]
            buf[0:SUBLANES, :] = carry_ref[:, cols]
            buf[SUBLANES:SUBLANES + tm, :] = ug
            carry_ref[:, cols] = ug[tm - SUBLANES:tm, :]
            u1 = buf[SUBLANES - 1:SUBLANES - 1 + tm, :]
            u2 = buf[SUBLANES - 2:SUBLANES - 2 + tm, :]
            outs.append(cb_ref[:, cols] + cw_ref[2:3, cols] * ug + cw_ref[1:2, cols] * u1
                        + cw_ref[0:1, cols] * u2)
        return jnp.concatenate(outs, axis=1)

    for c in range(N_FF_CHUNKS):
        gcol0 = c * FF_CHUNK
        vcol0 = D_FF + c * FF_CHUNK
        g = conv(_dot(h2, wup_ref[:, gcol0:gcol0 + FF_CHUNK]), gcol0, 0)
        val = conv(_dot(h2, wup_ref[:, vcol0:vcol0 + FF_CHUNK]), vcol0, 1)
        act_ref[:, gcol0:gcol0 + FF_CHUNK] = (_silu(g) * val).astype(BF16)

    y2 = _dot(act_ref[...], wdn_ref[...])
    o_ref[0] = _ln_plain(DN_ALPHA * x1 + (1.0 + gate2) * y2) * ln2w_ref[...] + ln2b_ref[...]


def _mlp_call(x, ret, mla, mod, w_out, ln1w, ln1b, w_up, conv_w, conv_b, w_down, ln2w, ln2b):
    B, S, D = x.shape
    tm = ROW_TILE
    row = lambda w: pl.BlockSpec((1, tm, w), lambda b, s: (b, s, 0))
    hbm = pl.BlockSpec(memory_space=pl.ANY)
    return pl.pallas_call(
        functools.partial(_mlp_kernel, tm=tm),
        grid=(B, S // tm),
        in_specs=[
            row(D), row(RET_WIDTH), row(RET_WIDTH),
            pl.BlockSpec((1, N_MOD, D), lambda b, s: (b, 0, 0)),
            hbm, _const_spec(ln1w.shape), _const_spec(ln1b.shape),
            hbm, _const_spec(conv_w.shape), _const_spec(conv_b.shape),
            hbm, _const_spec(ln2w.shape), _const_spec(ln2b.shape),
        ],
        out_specs=row(D),
        out_shape=jax.ShapeDtypeStruct((B, S, D), F32),
        scratch_shapes=[
            pltpu.VMEM((SUBLANES, 2 * D_FF), F32),
            pltpu.VMEM((2, FF_CHUNK // LANES, SUBLANES + tm, LANES), F32),
            pltpu.VMEM((tm, D_FF), BF16),
            pltpu.VMEM(w_out.shape[1:], BF16),
            pltpu.VMEM(w_up.shape[1:], BF16),
            pltpu.VMEM(w_down.shape[1:], BF16),
            pltpu.VMEM((STAGE_SLOTS, STAGE_ROWS_SQ, D), F32),
            pltpu.VMEM((STAGE_SLOTS, STAGE_ROWS_UP, 2 * D_FF), F32),
            pltpu.SemaphoreType.DMA((STAGE_SLOTS,)),
        ],
        compiler_params=pltpu.CompilerParams(
            dimension_semantics=("arbitrary", "arbitrary"), vmem_limit_bytes=VMEM_LIMIT),
        name="outproj_mlp",
    )(x, ret, mla, mod, w_out, ln1w, ln1b, w_up, conv_w, conv_b, w_down, ln2w, ln2b)


def _rope_tables(seq, half, reps):
    pos = np.arange(seq, dtype=np.float32)
    inv = np.float32(ROPE_BASE) ** (-np.arange(half, dtype=np.float32) / np.float32(half))
    ang = (pos[:, None] * inv[None, :]).astype(np.float32)
    cos = np.cos(ang).astype(np.float32)
    sin = np.sin(ang).astype(np.float32)
    cos_t = np.tile(cos, (1, 2 * reps))
    sin_t = np.concatenate([np.tile(-sin, (1, reps)), np.tile(sin, (1, reps))], axis=-1)
    return jnp.asarray(cos_t), jnp.asarray(sin_t)


def _uq_columns():
    half = MLA_ROPE_DIM // 2
    nope = [h * MLA_QK_DIM + np.arange(HEAD_DIM) for h in range(N_HEADS)]
    rope = []
    for p in range(N_HEADS // 2):
        for part in range(2):
            for h in (2 * p, 2 * p + 1):
                rope.append(h * MLA_QK_DIM + HEAD_DIM + part * half + np.arange(half))
    return np.concatenate(nope + rope)


def _take_columns(w, cols):
    cols = np.asarray(cols)
    cuts = np.flatnonzero(np.diff(cols) != 1) + 1
    runs = np.split(cols, cuts)
    return jnp.concatenate([w[:, r[0]:r[-1] + 1] for r in runs], axis=1)


def kernel(x, c, w_ada, b_ada, w_in, ret_gn_w, mla_q_norm_w, w_uq, mla_kv_norm_w, w_ukv, w_out,
           ln1_w, ln1_b, w_up, conv_w, conv_b, w_down, ln2_w, ln2_b):
    B, S, D = x.shape
    assert D == D_MODEL and S % IN_ROW_TILE == 0 and S % ROW_TILE == 0 and S % ATTN_TQ == 0
    assert w_ada.shape[0] == DEPTH == 1
    l = 0

    mod = _ada_call(c, w_ada, b_ada[l][None, :]).reshape(B, N_MOD, D)

    w_uq_p = _take_columns(w_uq[l].astype(BF16), _uq_columns())
    w_ukv_p = w_ukv[l].astype(BF16)
    cosr, sinr = _rope_tables(S, HEAD_DIM // 2, 1)
    cosm, sinm = _rope_tables(S, MLA_ROPE_DIM // 2, 2)

    ret, q, k, vt = _inproj_call(
        x, mod, jnp.swapaxes(w_in, 1, 2), ret_gn_w[l][None, :], mla_q_norm_w[l][None, :], mla_kv_norm_w[l][None, :],
        w_uq_p, w_ukv_p, cosr, sinr, cosm, sinm)
    mla = _attn_call(q, k, vt)
    return _mlp_call(
        x, ret, mla, mod, w_out, ln1_w[l][None, :], ln1_b[l][None, :],
        w_up, conv_w[l], conv_b[l][None, :], w_down, ln2_w[l][None, :], ln2_b[l][None, :])
```

```python
import functools
import math

import numpy as np
import jax
import jax.numpy as jnp
from jax import lax
from jax.experimental import pallas as pl
from jax.experimental.pallas import tpu as pltpu

F32 = jnp.float32
BF16 = jnp.bfloat16

D_MODEL = 1024
DEPTH = 1
N_HEADS = 4
HEAD_DIM = 128
RET_WIDTH = N_HEADS * HEAD_DIM
LAT_COL0 = 4 * RET_WIDTH
MLA_RANK = 256
MLA_ROPE_DIM = 64
MLA_QK_DIM = HEAD_DIM + MLA_ROPE_DIM
D_FF = 2816
ROPE_BASE = 10000.0
LN_EPS = 1e-5
RMS_EPS = 1e-6
DN_ALPHA = (2.0 * DEPTH) ** 0.25
N_MOD = 6
LOG_G = [math.log1p(-(2.0 ** (-5.0 - h))) for h in range(N_HEADS)]

LANES = 128
SUBLANES = 8
MXU_DIM = 256
VMEM_BYTES = 64 * 1024 * 1024
VMEM_LIMIT = VMEM_BYTES * 7 // 8

ADA_ROWS = 256
IN_ROW_TILE = 1024
ROW_TILE = 512
RET_CHUNK = MXU_DIM
ATTN_TQ = 2048
ATTN_TK = 1024
FF_CHUNK = MXU_DIM
N_FF_CHUNKS = D_FF // FF_CHUNK
STAGE_ROWS_SQ = 256
STAGE_ROWS_UP = 64
STAGE_ROWS_IN = 128
STAGE_SLOTS = 4
QK_PAD = 2 * LANES
BF16_ROWS = 2 * SUBLANES
ACC_ROWS = HEAD_DIM + BF16_ROWS
MASKED = -float("inf")


def _ln_plain(x):
    mu = jnp.mean(x, axis=-1, keepdims=True)
    xc = x - mu
    var = jnp.mean(xc * xc, axis=-1, keepdims=True)
    return xc * lax.rsqrt(var + LN_EPS)


def _rms(x):
    return x * lax.rsqrt(jnp.mean(x * x, axis=-1, keepdims=True) + RMS_EPS)


def _silu(x):
    return x * jax.nn.sigmoid(x)


def _dot(a, b):
    return jnp.dot(a, b, preferred_element_type=F32)


def _dot_nt(a, b):
    return lax.dot_general(a, b, (((1,), (1,)), ((), ())), preferred_element_type=F32)


def _dot_tn(a, b):
    return lax.dot_general(a, b, (((0,), (0,)), ((), ())), preferred_element_type=F32)


def _ada_kernel(c_ref, w_ref, b_ref, o_ref):
    @pl.when(pl.program_id(0) == 0)
    def _():
        o_ref[...] = jnp.broadcast_to(b_ref[...], o_ref.shape)

    cond = _silu(c_ref[...])
    o_ref[...] += _dot(cond.astype(BF16), w_ref[0].astype(BF16))


def _ada_call(c, w_ada, b_ada):
    n_batch = c.shape[0]
    _, d_in, n_out = w_ada.shape
    return pl.pallas_call(
        _ada_kernel,
        grid=(d_in // ADA_ROWS,),
        in_specs=[
            pl.BlockSpec((n_batch, ADA_ROWS), lambda j: (0, j)),
            pl.BlockSpec((1, ADA_ROWS, n_out), lambda j: (0, j, 0)),
            pl.BlockSpec((1, n_out), lambda j: (0, 0)),
        ],
        out_specs=pl.BlockSpec((n_batch, n_out), lambda j: (0, 0)),
        out_shape=jax.ShapeDtypeStruct((n_batch, n_out), F32),
        compiler_params=pltpu.CompilerParams(dimension_semantics=("arbitrary",), vmem_limit_bytes=VMEM_LIMIT),
        name="ada_mod",
    )(c, w_ada, b_ada)


def _rope128(x, cos, sin_signed):
    return x * cos + pltpu.roll(x, LANES // 2, axis=1) * sin_signed


def _decay_matrix(hd, c):
    row = lax.broadcasted_iota(jnp.int32, (c, c), 0)
    col = lax.broadcasted_iota(jnp.int32, (c, c), 1)
    rel = (row - col).astype(F32)
    return jnp.where(rel >= 0.0, jnp.exp(LOG_G[hd] * jnp.maximum(rel, 0.0)), 0.0)


def _retention_head(hd, q, k, v, gate, gnw, decay_ref, state_ref, ret_ref):
    c = decay_ref.shape[1]
    lg = LOG_G[hd]
    n = lax.broadcasted_iota(jnp.int32, (c, 1), 0).astype(F32)
    q_w = jnp.exp(lg * (n + 1.0))
    k_w = jnp.exp(lg * (c - 1.0 - n))
    for i in range(q.shape[0] // c):
        rows = slice(i * c, (i + 1) * c)
        qb = q[rows].astype(BF16)
        vb = v[rows].astype(BF16)
        scores = _dot_nt(qb, k[rows].astype(BF16)) * decay_ref[hd]
        inner = _dot(scores.astype(BF16), vb)
        state = state_ref[hd]
        cross = _dot(qb, state.astype(BF16)) * q_w
        state_ref[hd] = state * math.exp(lg * c) + _dot_tn((k[rows] * k_w).astype(BF16), vb)
        o = _ln_plain(inner + cross) * gnw
        ret_ref[0, rows, hd * HEAD_DIM:(hd + 1) * HEAD_DIM] = (o * gate[rows]).astype(BF16)


def _stream_cast(src_hbm, stage_ref, sem, store, n_rows=None):
    slots, rows = stage_ref.shape[0], stage_ref.shape[1]
    n_rows = src_hbm.shape[0] if n_rows is None else n_rows
    n_chunks = n_rows // rows
    assert n_chunks * rows == n_rows and n_chunks >= slots

    def copy(k):
        slot = k % slots
        return pltpu.make_async_copy(
            src_hbm.at[pl.ds(pl.multiple_of(k * rows, rows), rows), :], stage_ref.at[slot], sem.at[slot])

    for k in range(slots - 1):
        copy(k).start()

    def body(k, _):
        @pl.when(k + slots - 1 < n_chunks)
        def _():
            copy(k + slots - 1).start()

        copy(k).wait()
        store(pl.ds(pl.multiple_of(k * rows, rows), rows), stage_ref[k % slots])
        return 0

    lax.fori_loop(0, n_chunks, body, 0)


def _store_bf16(dst_ref):
    def store(rows, chunk):
        dst_ref[rows, :] = chunk.astype(BF16)
    return store


def _stage_w_in(wt_hbm, win_ref, stage_ref, sem):
    base = LAT_COL0 + 2 * MLA_RANK
    quarter = LANES // 4

    def store(cols, chunk):
        win_ref[:, cols] = chunk.T.astype(BF16)

    _stream_cast(wt_hbm, stage_ref, sem, store, n_rows=base)

    tail_copy = pltpu.make_async_copy(
        wt_hbm.at[pl.ds(base, MLA_ROPE_DIM), :], stage_ref.at[0, pl.ds(0, MLA_ROPE_DIM), :], sem.at[0])
    tail_copy.start()
    tail_copy.wait()
    tail = stage_ref[0, 0:MLA_ROPE_DIM, :].T
    t2 = jnp.concatenate([tail, tail], axis=1)
    lane = lax.broadcasted_iota(jnp.int32, t2.shape, 1)
    middle = jnp.abs(2 * lane - (LANES - 1)) < 2 * quarter
    win_ref[:, base:base + LANES] = jnp.where(middle, pltpu.roll(t2, quarter, axis=1), t2).astype(BF16)


def _inproj_kernel(x_ref, mod_ref, wint_hbm, gnw_ref, qnw_ref, kvnw_ref, wuq_ref, wukv_ref,
                   cosr_ref, sinr_ref, cosm_ref, sinm_ref,
                   ret_ref, qt_ref, k_ref, vt_ref, state_ref, decay_ref, win_ref, stage_ref, sem,
                   *, q_scale, rk_scale):
    @pl.when(jnp.logical_and(pl.program_id(0) == 0, pl.program_id(1) == 0))
    def _():
        _stage_w_in(wint_hbm.at[0], win_ref, stage_ref, sem)
        for hd in range(N_HEADS):
            decay_ref[hd] = _decay_matrix(hd, decay_ref.shape[1])

    @pl.when(pl.program_id(1) == 0)
    def _():
        state_ref[...] = jnp.zeros_like(state_ref)

    x = x_ref[0]
    shift = mod_ref[0, 0:1, :]
    scale = mod_ref[0, 1:2, :]
    h = (_ln_plain(x) * (1.0 + scale) + shift).astype(BF16)

    cosr = cosr_ref[...]
    sinr = sinr_ref[...]
    cosm = cosm_ref[...]
    sinm = sinm_ref[...]

    pq = _dot(h, win_ref[:, 0:RET_WIDTH])
    pk = _dot(h, win_ref[:, RET_WIDTH:2 * RET_WIDTH])
    pv = _dot(h, win_ref[:, 2 * RET_WIDTH:3 * RET_WIDTH])
    pg = _silu(_dot(h, win_ref[:, 3 * RET_WIDTH:LAT_COL0]))
    for hd in range(N_HEADS):
        sl = slice(hd * HEAD_DIM, (hd + 1) * HEAD_DIM)
        _retention_head(hd, _rope128(pq[:, sl], cosr, sinr), _rope128(pk[:, sl], cosr, sinr) * rk_scale,
                        pv[:, sl], pg[:, sl], gnw_ref[:, sl], decay_ref, state_ref, ret_ref)

    lat = _dot(h, win_ref[:, LAT_COL0:])
    cq = (_rms(lat[:, 0:MLA_RANK]) * qnw_ref[...]).astype(BF16)
    ckv = (_rms(lat[:, MLA_RANK:2 * MLA_RANK]) * kvnw_ref[...]).astype(BF16)
    kr = _rope128(lat[:, 2 * MLA_RANK:2 * MLA_RANK + LANES], cosm, sinm)
    lane = lax.broadcasted_iota(jnp.int32, kr.shape, 1)
    first_of_pair = (lane % (LANES // 2)) < (LANES // 4)
    kr_even = jnp.where(first_of_pair, kr, 0.0).astype(BF16)
    kr_odd = jnp.where(first_of_pair, 0.0, kr).astype(BF16)

    qf = _dot(cq, wuq_ref[...])
    kvf = _dot(ckv, wukv_ref[...])
    for hd in range(N_HEADS):
        vt_ref[0, hd] = kvf[:, (2 * hd + 1) * HEAD_DIM:(2 * hd + 2) * HEAD_DIM].T.astype(BF16)
    for p in range(N_HEADS // 2):
        qr = qf[:, RET_WIDTH + p * LANES:RET_WIDTH + (p + 1) * LANES]
        qr_t = (_rope128(qr, cosm, sinm) * q_scale).T.astype(BF16)
        for hd in (2 * p, 2 * p + 1):
            sl = slice(hd * HEAD_DIM, (hd + 1) * HEAD_DIM)
            qt_ref[0, hd, 0:LANES, :] = (qf[:, sl] * q_scale).T.astype(BF16)
            qt_ref[0, hd, LANES:QK_PAD, :] = qr_t
            k_ref[0, :, hd * QK_PAD:hd * QK_PAD + LANES] = kvf[:, 2 * hd * HEAD_DIM:(2 * hd + 1) * HEAD_DIM].astype(BF16)
            k_ref[0, :, hd * QK_PAD + LANES:(hd + 1) * QK_PAD] = kr_even if hd % 2 == 0 else kr_odd


def _const_spec(shape):
    nd = len(shape)
    return pl.BlockSpec(shape, lambda *_: (0,) * nd, pipeline_mode=pl.Buffered(1))


def _inproj_call(x, mod, w_in, gnw, qnw, kvnw, w_uq, w_ukv, cosr, sinr, cosm, sinm):
    B, S, D = x.shape
    tm = IN_ROW_TILE
    q_scale = (MLA_QK_DIM ** -0.5) * math.log2(math.e)
    rk_scale = HEAD_DIM ** -0.5
    row = lambda w: pl.BlockSpec((1, tm, w), lambda b, s: (b, s, 0))
    tab = pl.BlockSpec((tm, LANES), lambda b, s: (s, 0))
    out_shapes = [
        jax.ShapeDtypeStruct((B, S, RET_WIDTH), BF16),
        jax.ShapeDtypeStruct((B, N_HEADS, QK_PAD, S), BF16),
        jax.ShapeDtypeStruct((B, S, N_HEADS * QK_PAD), BF16),
        jax.ShapeDtypeStruct((B, N_HEADS, HEAD_DIM, S), BF16),
    ]
    head_t = lambda rows: pl.BlockSpec((1, N_HEADS, rows, tm), lambda b, s: (b, 0, 0, s))
    return pl.pallas_call(
        functools.partial(_inproj_kernel, q_scale=q_scale, rk_scale=rk_scale),
        grid=(B, S // tm),
        in_specs=[
            row(D),
            pl.BlockSpec((1, N_MOD, D), lambda b, s: (b, 0, 0)),
            pl.BlockSpec(memory_space=pl.ANY), _const_spec(gnw.shape),
            _const_spec(qnw.shape), _const_spec(kvnw.shape),
            _const_spec(w_uq.shape), _const_spec(w_ukv.shape),
            tab, tab, tab, tab,
        ],
        out_specs=[row(RET_WIDTH), head_t(QK_PAD), row(N_HEADS * QK_PAD), head_t(HEAD_DIM)],
        out_shape=out_shapes,
        scratch_shapes=[
            pltpu.VMEM((N_HEADS, HEAD_DIM, HEAD_DIM), F32),
            pltpu.VMEM((N_HEADS, RET_CHUNK, RET_CHUNK), F32),
            pltpu.VMEM((D, LAT_COL0 + 2 * MLA_RANK + LANES), BF16),
            pltpu.VMEM((STAGE_SLOTS, STAGE_ROWS_IN, w_in.shape[-1]), F32),
            pltpu.SemaphoreType.DMA((STAGE_SLOTS,)),
        ],
        compiler_params=pltpu.CompilerParams(
            dimension_semantics=("arbitrary", "arbitrary"), vmem_limit_bytes=VMEM_LIMIT),
        name="inproj_retention",
    )(x, mod, w_in, gnw, qnw, kvnw, w_uq, w_ukv, cosr, sinr, cosm, sinm)


def _attn_kernel(qt_ref, k_ref, vt_ref, o_ref, s_buf, p_buf, acc_ref, *, tq, tk):
    qi = pl.program_id(2)
    ones = jnp.ones((BF16_ROWS, tk), BF16)

    def k_blk(j):
        return k_ref[0, pl.ds(pl.multiple_of(j * tk, tk), tk), :]

    def v_blk(j):
        return jnp.concatenate([vt_ref[0, 0, :, pl.ds(pl.multiple_of(j * tk, tk), tk)], ones], axis=0)

    def scores(j, slot):
        s = _dot(k_blk(j), qt_ref[0, 0])
        s_buf[slot] = s
        return jnp.max(s, axis=0, keepdims=True)

    def value_update(j, slot, alpha):
        acc_ref[...] = alpha * acc_ref[...] + _dot(v_blk(j), p_buf[slot])

    def softmax(s, m, block_max=None):
        if block_max is None:
            block_max = jnp.max(s, axis=0, keepdims=True)
        m_new = jnp.maximum(m, block_max)
        return m_new, jnp.exp2(m - m_new), jnp.exp2(s - m_new).astype(BF16)

    def two_blocks(u, carry):
        alpha_prev, m, max_even = carry
        t = 2 * u
        max_odd = scores(t + 1, 1)
        value_update(jnp.maximum(t - 1, 0), 1, alpha_prev)
        m, alpha_even, p_buf[0] = softmax(s_buf[0], m, max_even)
        max_next = scores(t + 2, 0)
        value_update(t, 0, alpha_even)
        m, alpha_odd, p_buf[1] = softmax(s_buf[1], m, max_odd)
        return alpha_odd, m, max_next

    def reset():
        p_buf[1] = jnp.zeros((tk, tq), BF16)
        acc_ref[...] = jnp.zeros_like(acc_ref)

    @pl.when((pl.program_id(0) == 0) & (pl.program_id(1) == 0) & (qi == 0))
    def _():
        reset()

    max_0 = scores(0, 0)
    init = (jnp.ones((1, tq), F32), jnp.full((1, tq), MASKED, F32), max_0)
    alpha_prev, m, _ = lax.fori_loop(0, qi, two_blocks, init)

    t = 2 * qi
    key = lax.broadcasted_iota(jnp.int32, (tk, tk), 0)
    qry = lax.broadcasted_iota(jnp.int32, (tk, tk), 1)
    causal = key <= qry
    s_right = _dot(k_blk(t + 1), qt_ref[0, 0, :, tk:])
    value_update(jnp.maximum(t - 1, 0), 1, alpha_prev)
    s = s_buf[0]
    s = jnp.concatenate([jnp.where(causal, s[:, :tk], MASKED), s[:, tk:]], axis=1)
    m, alpha, p_buf[0] = softmax(s, m)
    value_update(t, 0, alpha)
    _, alpha_r, p_r = softmax(jnp.where(causal, s_right, MASKED), m[:, tk:])
    acc_ref[:, tk:] = alpha_r * acc_ref[:, tk:] + _dot(v_blk(t + 1), p_r)
    acc = acc_ref[...]
    o_ref[0] = (acc[:HEAD_DIM] / acc[HEAD_DIM:HEAD_DIM + 1]).T.astype(BF16)
    reset()


def _attn_call(qt, k, vt):
    B, S, _ = k.shape
    tq, tk = ATTN_TQ, ATTN_TK
    assert tq == 2 * tk
    return pl.pallas_call(
        functools.partial(_attn_kernel, tq=tq, tk=tk),
        grid=(B, N_HEADS, S // tq),
        in_specs=[
            pl.BlockSpec((1, 1, QK_PAD, tq), lambda b, h, i: (b, h, 0, i)),
            pl.BlockSpec((1, S, QK_PAD), lambda b, h, i: (b, 0, h)),
            pl.BlockSpec((1, 1, HEAD_DIM, S), lambda b, h, i: (b, h, 0, 0)),
        ],
        out_specs=pl.BlockSpec((1, tq, HEAD_DIM), lambda b, h, i: (b, i, h)),
        out_shape=jax.ShapeDtypeStruct((B, S, N_HEADS * HEAD_DIM), BF16),
        scratch_shapes=[
            pltpu.VMEM((2, tk, tq), F32),
            pltpu.VMEM((2, tk, tq), BF16),
            pltpu.VMEM((ACC_ROWS, tq), F32),
        ],
        compiler_params=pltpu.CompilerParams(
            dimension_semantics=("arbitrary", "arbitrary", "arbitrary"), vmem_limit_bytes=VMEM_LIMIT),
        name="mla_attn",
    )(qt, k, vt)


def _mlp_kernel(x_ref, ret_ref, mla_ref, mod_ref, wo32_hbm, ln1w_ref, ln1b_ref, wup32_hbm, cw_ref, cb_ref,
                wdn32_hbm, ln2w_ref, ln2b_ref, o_ref, carry_ref, ubuf_ref, act_ref,
                wo_ref, wup_ref, wdn_ref, stage_sq_ref, stage_up_ref, sem, *, tm):
    @pl.when(jnp.logical_and(pl.program_id(0) == 0, pl.program_id(1) == 0))
    def _():
        _stream_cast(wo32_hbm.at[0], stage_sq_ref, sem, _store_bf16(wo_ref))
        _stream_cast(wup32_hbm.at[0], stage_up_ref, sem, _store_bf16(wup_ref))
        _stream_cast(wdn32_hbm.at[0], stage_sq_ref, sem, _store_bf16(wdn_ref))

    @pl.when(pl.program_id(1) == 0)
    def _():
        carry_ref[...] = jnp.zeros_like(carry_ref)

    gate1 = mod_ref[0, 2:3, :]
    shift2 = mod_ref[0, 3:4, :]
    scale2 = mod_ref[0, 4:5, :]
    gate2 = mod_ref[0, 5:6, :]

    y = _dot(ret_ref[0], wo_ref[0:RET_WIDTH, :]) + _dot(mla_ref[0], wo_ref[RET_WIDTH:2 * RET_WIDTH, :])
    x1 = _ln_plain(DN_ALPHA * x_ref[0] + (1.0 + gate1) * y) * ln1w_ref[...] + ln1b_ref[...]
    h2 = (_ln_plain(x1) * (1.0 + scale2) + shift2).astype(BF16)

    def conv(u, col0, kind):
        outs = []
        for g in range(FF_CHUNK // LANES):
            cols = slice(col0 + g * LANES, col0 + (g + 1) * LANES)
            ug = u[:, g * LANES:(g + 1) * LANES]
            buf = ubuf_ref.at[kind, g]
            buf[0:SUBLANES, :] = carry_ref[:, cols]
            buf[SUBLANES:SUBLANES + tm, :] = ug
            carry_ref[:, cols] = ug[tm - SUBLANES:tm, :]
            u1 = buf[SUBLANES - 1:SUBLANES - 1 + tm, :]
            u2 = buf[SUBLANES - 2:SUBLANES - 2 + tm, :]
            outs.append(cb_ref[:, cols] + cw_ref[2:3, cols] * ug + cw_ref[1:2, cols] * u1
                        + cw_ref[0:1, cols] * u2)
        return jnp.concatenate(outs, axis=1)

    for c in range(N_FF_CHUNKS):
        gcol0 = c * FF_CHUNK
        vcol0 = D_FF + c * FF_CHUNK
        g = conv(_dot(h2, wup_ref[:, gcol0:gcol0 + FF_CHUNK]), gcol0, 0)
        val = conv(_dot(h2, wup_ref[:, vcol0:vcol0 + FF_CHUNK]), vcol0, 1)
        act_ref[:, gcol0:gcol0 + FF_CHUNK] = (_silu(g) * val).astype(BF16)

    y2 = _dot(act_ref[...], wdn_ref[...])
    o_ref[0] = _ln_plain(DN_ALPHA * x1 + (1.0 + gate2) * y2) * ln2w_ref[...] + ln2b_ref[...]


def _mlp_call(x, ret, mla, mod, w_out, ln1w, ln1b, w_up, conv_w, conv_b, w_down, ln2w, ln2b):
    B, S, D = x.shape
    tm = ROW_TILE
    row = lambda w: pl.BlockSpec((1, tm, w), lambda b, s: (b, s, 0))
    hbm = pl.BlockSpec(memory_space=pl.ANY)
    return pl.pallas_call(
        functools.partial(_mlp_kernel, tm=tm),
        grid=(B, S // tm),
        in_specs=[
            row(D), row(RET_WIDTH), row(RET_WIDTH),
            pl.BlockSpec((1, N_MOD, D), lambda b, s: (b, 0, 0)),
            hbm, _const_spec(ln1w.shape), _const_spec(ln1b.shape),
            hbm, _const_spec(conv_w.shape), _const_spec(conv_b.shape),
            hbm, _const_spec(ln2w.shape), _const_spec(ln2b.shape),
        ],
        out_specs=row(D),
        out_shape=jax.ShapeDtypeStruct((B, S, D), F32),
        scratch_shapes=[
            pltpu.VMEM((SUBLANES, 2 * D_FF), F32),
            pltpu.VMEM((2, FF_CHUNK // LANES, SUBLANES + tm, LANES), F32),
            pltpu.VMEM((tm, D_FF), BF16),
            pltpu.VMEM(w_out.shape[1:], BF16),
            pltpu.VMEM(w_up.shape[1:], BF16),
            pltpu.VMEM(w_down.shape[1:], BF16),
            pltpu.VMEM((STAGE_SLOTS, STAGE_ROWS_SQ, D), F32),
            pltpu.VMEM((STAGE_SLOTS, STAGE_ROWS_UP, 2 * D_FF), F32),
            pltpu.SemaphoreType.DMA((STAGE_SLOTS,)),
        ],
        compiler_params=pltpu.CompilerParams(
            dimension_semantics=("arbitrary", "arbitrary"), vmem_limit_bytes=VMEM_LIMIT),
        name="outproj_mlp",
    )(x, ret, mla, mod, w_out, ln1w, ln1b, w_up, conv_w, conv_b, w_down, ln2w, ln2b)


def _rope_tables(seq, half, reps):
    pos = np.arange(seq, dtype=np.float32)
    inv = np.float32(ROPE_BASE) ** (-np.arange(half, dtype=np.float32) / np.float32(half))
    ang = (pos[:, None] * inv[None, :]).astype(np.float32)
    cos = np.cos(ang).astype(np.float32)
    sin = np.sin(ang).astype(np.float32)
    cos_t = np.tile(cos, (1, 2 * reps))
    sin_t = np.concatenate([np.tile(-sin, (1, reps)), np.tile(sin, (1, reps))], axis=-1)
    return jnp.asarray(cos_t), jnp.asarray(sin_t)


def _uq_columns():
    half = MLA_ROPE_DIM // 2
    nope = [h * MLA_QK_DIM + np.arange(HEAD_DIM) for h in range(N_HEADS)]
    rope = []
    for p in range(N_HEADS // 2):
        for part in range(2):
            for h in (2 * p, 2 * p + 1):
                rope.append(h * MLA_QK_DIM + HEAD_DIM + part * half + np.arange(half))
    return np.concatenate(nope + rope)


def _take_columns(w, cols):
    cols = np.asarray(cols)
    cuts = np.flatnonzero(np.diff(cols) != 1) + 1
    runs = np.split(cols, cuts)
    return jnp.concatenate([w[:, r[0]:r[-1] + 1] for r in runs], axis=1)


def kernel(x, c, w_ada, b_ada, w_in, ret_gn_w, mla_q_norm_w, w_uq, mla_kv_norm_w, w_ukv, w_out,
           ln1_w, ln1_b, w_up, conv_w, conv_b, w_down, ln2_w, ln2_b):
    B, S, D = x.shape
    assert D == D_MODEL and S % IN_ROW_TILE == 0 and S % ROW_TILE == 0 and S % ATTN_TQ == 0
    assert w_ada.shape[0] == DEPTH == 1
    l = 0

    mod = _ada_call(c, w_ada, b_ada[l][None, :]).reshape(B, N_MOD, D)

    w_uq_p = _take_columns(w_uq[l].astype(BF16), _uq_columns())
    w_ukv_p = w_ukv[l].astype(BF16)
    cosr, sinr = _rope_tables(S, HEAD_DIM // 2, 1)
    cosm, sinm = _rope_tables(S, MLA_ROPE_DIM // 2, 2)

    ret, qt, k, vt = _inproj_call(
        x, mod, jnp.swapaxes(w_in, 1, 2), ret_gn_w[l][None, :], mla_q_norm_w[l][None, :], mla_kv_norm_w[l][None, :],
        w_uq_p, w_ukv_p, cosr, sinr, cosm, sinm)
    mla = _attn_call(qt, k, vt)
    return _mlp_call(
        x, ret, mla, mod, w_out, ln1_w[l][None, :], ln1_b[l][None, :],
        w_up, conv_w[l], conv_b[l][None, :], w_down, ln2_w[l][None, :], ln2_b[l][None, :])
```

```python
import functools
import math

import numpy as np
import jax
import jax.numpy as jnp
from jax import lax
from jax.experimental import pallas as pl
from jax.experimental.pallas import tpu as pltpu

F32 = jnp.float32
BF16 = jnp.bfloat16

D_MODEL = 1024
DEPTH = 1
N_HEADS = 4
HEAD_DIM = 128
RET_WIDTH = N_HEADS * HEAD_DIM
LAT_COL0 = 4 * RET_WIDTH
MLA_RANK = 256
MLA_ROPE_DIM = 64
MLA_QK_DIM = HEAD_DIM + MLA_ROPE_DIM
D_FF = 2816
ROPE_BASE = 10000.0
LN_EPS = 1e-5
RMS_EPS = 1e-6
DN_ALPHA = (2.0 * DEPTH) ** 0.25
N_MOD = 6
LOG_G = [math.log1p(-(2.0 ** (-5.0 - h))) for h in range(N_HEADS)]

LANES = 128
SUBLANES = 8
MXU_DIM = 256
VMEM_BYTES = 64 * 1024 * 1024
VMEM_LIMIT = VMEM_BYTES * 7 // 8

ADA_ROWS = 256
IN_ROW_TILE = 1024
ROW_TILE = 512
RET_CHUNK = MXU_DIM
ATTN_TQ = 2048
ATTN_TK = 1024
FF_CHUNK = MXU_DIM
N_FF_CHUNKS = D_FF // FF_CHUNK
STAGE_ROWS_SQ = 256
STAGE_ROWS_UP = 64
STAGE_ROWS_IN = 128
STAGE_SLOTS = 4
QK_PAD = 2 * LANES
BF16_ROWS = 2 * SUBLANES
ACC_ROWS = HEAD_DIM + BF16_ROWS
MASKED = -float("inf")


def _ln_plain(x):
    mu = jnp.mean(x, axis=-1, keepdims=True)
    xc = x - mu
    var = jnp.mean(xc * xc, axis=-1, keepdims=True)
    return xc * lax.rsqrt(var + LN_EPS)


def _rms(x):
    return x * lax.rsqrt(jnp.mean(x * x, axis=-1, keepdims=True) + RMS_EPS)


def _silu(x):
    return x * jax.nn.sigmoid(x)


def _dot(a, b):
    return jnp.dot(a, b, preferred_element_type=F32)


def _dot_nt(a, b):
    return lax.dot_general(a, b, (((1,), (1,)), ((), ())), preferred_element_type=F32)


def _dot_tn(a, b):
    return lax.dot_general(a, b, (((0,), (0,)), ((), ())), preferred_element_type=F32)


def _ada_kernel(c_ref, w_ref, b_ref, o_ref):
    @pl.when(pl.program_id(0) == 0)
    def _():
        o_ref[...] = jnp.broadcast_to(b_ref[...], o_ref.shape)

    cond = _silu(c_ref[...])
    o_ref[...] += _dot(cond.astype(BF16), w_ref[0].astype(BF16))


def _ada_call(c, w_ada, b_ada):
    n_batch = c.shape[0]
    _, d_in, n_out = w_ada.shape
    return pl.pallas_call(
        _ada_kernel,
        grid=(d_in // ADA_ROWS,),
        in_specs=[
            pl.BlockSpec((n_batch, ADA_ROWS), lambda j: (0, j)),
            pl.BlockSpec((1, ADA_ROWS, n_out), lambda j: (0, j, 0)),
            pl.BlockSpec((1, n_out), lambda j: (0, 0)),
        ],
        out_specs=pl.BlockSpec((n_batch, n_out), lambda j: (0, 0)),
        out_shape=jax.ShapeDtypeStruct((n_batch, n_out), F32),
        compiler_params=pltpu.CompilerParams(dimension_semantics=("arbitrary",), vmem_limit_bytes=VMEM_LIMIT),
        name="ada_mod",
    )(c, w_ada, b_ada)


def _rope128(x, cos, sin_signed):
    return x * cos + pltpu.roll(x, LANES // 2, axis=1) * sin_signed


def _decay_matrix(hd, c):
    row = lax.broadcasted_iota(jnp.int32, (c, c), 0)
    col = lax.broadcasted_iota(jnp.int32, (c, c), 1)
    rel = (row - col).astype(F32)
    return jnp.where(rel >= 0.0, jnp.exp(LOG_G[hd] * jnp.maximum(rel, 0.0)), 0.0)


def _retention_head(hd, q, k, v, gate, gnw, decay_ref, state_ref, ret_ref):
    c = decay_ref.shape[1]
    lg = LOG_G[hd]
    n = lax.broadcasted_iota(jnp.int32, (c, 1), 0).astype(F32)
    q_w = jnp.exp(lg * (n + 1.0))
    k_w = jnp.exp(lg * (c - 1.0 - n))
    for i in range(q.shape[0] // c):
        rows = slice(i * c, (i + 1) * c)
        qb = q[rows].astype(BF16)
        vb = v[rows].astype(BF16)
        scores = _dot_nt(qb, k[rows].astype(BF16)) * decay_ref[hd]
        inner = _dot(scores.astype(BF16), vb)
        state = state_ref[hd]
        cross = _dot(qb, state.astype(BF16)) * q_w
        state_ref[hd] = state * math.exp(lg * c) + _dot_tn((k[rows] * k_w).astype(BF16), vb)
        o = _ln_plain(inner + cross) * gnw
        ret_ref[0, rows, hd * HEAD_DIM:(hd + 1) * HEAD_DIM] = (o * gate[rows]).astype(BF16)


def _stream_cast(src_hbm, stage_ref, sem, store, n_rows=None):
    slots, rows = stage_ref.shape[0], stage_ref.shape[1]
    n_rows = src_hbm.shape[0] if n_rows is None else n_rows
    n_chunks = n_rows // rows
    assert n_chunks * rows == n_rows and n_chunks >= slots

    def copy(k):
        slot = k % slots
        return pltpu.make_async_copy(
            src_hbm.at[pl.ds(pl.multiple_of(k * rows, rows), rows), :], stage_ref.at[slot], sem.at[slot])

    for k in range(slots - 1):
        copy(k).start()

    def body(k, _):
        @pl.when(k + slots - 1 < n_chunks)
        def _():
            copy(k + slots - 1).start()

        copy(k).wait()
        store(pl.ds(pl.multiple_of(k * rows, rows), rows), stage_ref[k % slots])
        return 0

    lax.fori_loop(0, n_chunks, body, 0)


def _store_bf16(dst_ref):
    def store(rows, chunk):
        dst_ref[rows, :] = chunk.astype(BF16)
    return store


def _stage_w_in(wt_hbm, win_ref, stage_ref, sem):
    base = LAT_COL0 + 2 * MLA_RANK
    quarter = LANES // 4

    def store(cols, chunk):
        win_ref[:, cols] = chunk.T.astype(BF16)

    _stream_cast(wt_hbm, stage_ref, sem, store, n_rows=base)

    tail_copy = pltpu.make_async_copy(
        wt_hbm.at[pl.ds(base, MLA_ROPE_DIM), :], stage_ref.at[0, pl.ds(0, MLA_ROPE_DIM), :], sem.at[0])
    tail_copy.start()
    tail_copy.wait()
    tail = stage_ref[0, 0:MLA_ROPE_DIM, :].T
    t2 = jnp.concatenate([tail, tail], axis=1)
    lane = lax.broadcasted_iota(jnp.int32, t2.shape, 1)
    middle = jnp.abs(2 * lane - (LANES - 1)) < 2 * quarter
    win_ref[:, base:base + LANES] = jnp.where(middle, pltpu.roll(t2, quarter, axis=1), t2).astype(BF16)


def _inproj_kernel(x_ref, mod_ref, wint_hbm, gnw_ref, qnw_ref, kvnw_ref, wuq_ref, wukv_ref,
                   cosr_ref, sinr_ref, cosm_ref, sinm_ref,
                   ret_ref, q_ref, k_ref, vt_ref, state_ref, decay_ref, win_ref, stage_ref, sem,
                   *, q_scale, rk_scale):
    @pl.when(jnp.logical_and(pl.program_id(0) == 0, pl.program_id(1) == 0))
    def _():
        _stage_w_in(wint_hbm.at[0], win_ref, stage_ref, sem)
        for hd in range(N_HEADS):
            decay_ref[hd] = _decay_matrix(hd, decay_ref.shape[1])

    @pl.when(pl.program_id(1) == 0)
    def _():
        state_ref[...] = jnp.zeros_like(state_ref)

    x = x_ref[0]
    shift = mod_ref[0, 0:1, :]
    scale = mod_ref[0, 1:2, :]
    h = (_ln_plain(x) * (1.0 + scale) + shift).astype(BF16)

    cosr = cosr_ref[...]
    sinr = sinr_ref[...]
    cosm = cosm_ref[...]
    sinm = sinm_ref[...]

    pq = _dot(h, win_ref[:, 0:RET_WIDTH])
    pk = _dot(h, win_ref[:, RET_WIDTH:2 * RET_WIDTH])
    pv = _dot(h, win_ref[:, 2 * RET_WIDTH:3 * RET_WIDTH])
    pg = _silu(_dot(h, win_ref[:, 3 * RET_WIDTH:LAT_COL0]))
    for hd in range(N_HEADS):
        sl = slice(hd * HEAD_DIM, (hd + 1) * HEAD_DIM)
        _retention_head(hd, _rope128(pq[:, sl], cosr, sinr), _rope128(pk[:, sl], cosr, sinr) * rk_scale,
                        pv[:, sl], pg[:, sl], gnw_ref[:, sl], decay_ref, state_ref, ret_ref)

    lat = _dot(h, win_ref[:, LAT_COL0:])
    cq = (_rms(lat[:, 0:MLA_RANK]) * qnw_ref[...]).astype(BF16)
    ckv = (_rms(lat[:, MLA_RANK:2 * MLA_RANK]) * kvnw_ref[...]).astype(BF16)
    kr = _rope128(lat[:, 2 * MLA_RANK:2 * MLA_RANK + LANES], cosm, sinm)
    lane = lax.broadcasted_iota(jnp.int32, kr.shape, 1)
    first_of_pair = (lane % (LANES // 2)) < (LANES // 4)
    kr_even = jnp.where(first_of_pair, kr, 0.0).astype(BF16)
    kr_odd = jnp.where(first_of_pair, 0.0, kr).astype(BF16)

    qf = _dot(cq, wuq_ref[...])
    kvf = _dot(ckv, wukv_ref[...])
    for hd in range(N_HEADS):
        vt_ref[0, hd] = kvf[:, (2 * hd + 1) * HEAD_DIM:(2 * hd + 2) * HEAD_DIM].T.astype(BF16)
    for p in range(N_HEADS // 2):
        qr = qf[:, RET_WIDTH + p * LANES:RET_WIDTH + (p + 1) * LANES]
        qr = (_rope128(qr, cosm, sinm) * q_scale).astype(BF16)
        for hd in (2 * p, 2 * p + 1):
            sl = slice(hd * HEAD_DIM, (hd + 1) * HEAD_DIM)
            q_ref[0, :, hd * QK_PAD:hd * QK_PAD + LANES] = (qf[:, sl] * q_scale).astype(BF16)
            q_ref[0, :, hd * QK_PAD + LANES:(hd + 1) * QK_PAD] = qr
            k_ref[0, :, hd * QK_PAD:hd * QK_PAD + LANES] = kvf[:, 2 * hd * HEAD_DIM:(2 * hd + 1) * HEAD_DIM].astype(BF16)
            k_ref[0, :, hd * QK_PAD + LANES:(hd + 1) * QK_PAD] = kr_even if hd % 2 == 0 else kr_odd


def _const_spec(shape):
    nd = len(shape)
    return pl.BlockSpec(shape, lambda *_: (0,) * nd, pipeline_mode=pl.Buffered(1))


def _inproj_call(x, mod, w_in, gnw, qnw, kvnw, w_uq, w_ukv, cosr, sinr, cosm, sinm):
    B, S, D = x.shape
    tm = IN_ROW_TILE
    q_scale = (MLA_QK_DIM ** -0.5) * math.log2(math.e)
    rk_scale = HEAD_DIM ** -0.5
    row = lambda w: pl.BlockSpec((1, tm, w), lambda b, s: (b, s, 0))
    tab = pl.BlockSpec((tm, LANES), lambda b, s: (s, 0))
    out_shapes = (
        [jax.ShapeDtypeStruct((B, S, RET_WIDTH), BF16)]
        + [jax.ShapeDtypeStruct((B, S, N_HEADS * QK_PAD), BF16)] * 2
        + [jax.ShapeDtypeStruct((B, N_HEADS, HEAD_DIM, S), BF16)]
    )
    vt_spec = pl.BlockSpec((1, N_HEADS, HEAD_DIM, tm), lambda b, s: (b, 0, 0, s))
    return pl.pallas_call(
        functools.partial(_inproj_kernel, q_scale=q_scale, rk_scale=rk_scale),
        grid=(B, S // tm),
        in_specs=[
            row(D),
            pl.BlockSpec((1, N_MOD, D), lambda b, s: (b, 0, 0)),
            pl.BlockSpec(memory_space=pl.ANY), _const_spec(gnw.shape),
            _const_spec(qnw.shape), _const_spec(kvnw.shape),
            _const_spec(w_uq.shape), _const_spec(w_ukv.shape),
            tab, tab, tab, tab,
        ],
        out_specs=[row(RET_WIDTH)] + [row(N_HEADS * QK_PAD)] * 2 + [vt_spec],
        out_shape=out_shapes,
        scratch_shapes=[
            pltpu.VMEM((N_HEADS, HEAD_DIM, HEAD_DIM), F32),
            pltpu.VMEM((N_HEADS, RET_CHUNK, RET_CHUNK), F32),
            pltpu.VMEM((D, LAT_COL0 + 2 * MLA_RANK + LANES), BF16),
            pltpu.VMEM((STAGE_SLOTS, STAGE_ROWS_IN, w_in.shape[-1]), F32),
            pltpu.SemaphoreType.DMA((STAGE_SLOTS,)),
        ],
        compiler_params=pltpu.CompilerParams(
            dimension_semantics=("arbitrary", "arbitrary"), vmem_limit_bytes=VMEM_LIMIT),
        name="inproj_retention",
    )(x, mod, w_in, gnw, qnw, kvnw, w_uq, w_ukv, cosr, sinr, cosm, sinm)


def _attn_kernel(q_ref, k_ref, vt_ref, o_ref, qt_ref, s_buf, p_buf, acc_ref, *, tq, tk):
    qi = pl.program_id(2)
    qt_ref[...] = q_ref[0].T

    ones = jnp.ones((BF16_ROWS, tk), BF16)

    def k_blk(j):
        return k_ref[0, pl.ds(pl.multiple_of(j * tk, tk), tk), :]

    def v_blk(j):
        return jnp.concatenate([vt_ref[0, 0, :, pl.ds(pl.multiple_of(j * tk, tk), tk)], ones], axis=0)

    def scores(j, slot):
        s = _dot(k_blk(j), qt_ref[...])
        s_buf[slot] = s
        return jnp.max(s, axis=0, keepdims=True)

    def value_update(j, slot, alpha):
        acc_ref[...] = alpha * acc_ref[...] + _dot(v_blk(j), p_buf[slot])

    def softmax(s, m, block_max=None):
        if block_max is None:
            block_max = jnp.max(s, axis=0, keepdims=True)
        m_new = jnp.maximum(m, block_max)
        return m_new, jnp.exp2(m - m_new), jnp.exp2(s - m_new).astype(BF16)

    def two_blocks(u, carry):
        alpha_prev, m, max_even = carry
        t = 2 * u
        max_odd = scores(t + 1, 1)
        value_update(jnp.maximum(t - 1, 0), 1, alpha_prev)
        m, alpha_even, p_buf[0] = softmax(s_buf[0], m, max_even)
        max_next = scores(t + 2, 0)
        value_update(t, 0, alpha_even)
        m, alpha_odd, p_buf[1] = softmax(s_buf[1], m, max_odd)
        return alpha_odd, m, max_next

    max_0 = scores(0, 0)
    p_buf[1] = jnp.zeros((tk, tq), BF16)
    acc_ref[...] = jnp.zeros_like(acc_ref)
    init = (jnp.ones((1, tq), F32), jnp.full((1, tq), MASKED, F32), max_0)
    alpha_prev, m, _ = lax.fori_loop(0, qi, two_blocks, init)

    t = 2 * qi
    key = lax.broadcasted_iota(jnp.int32, (tk, tk), 0)
    qry = lax.broadcasted_iota(jnp.int32, (tk, tk), 1)
    causal = key <= qry
    s_right = _dot(k_blk(t + 1), qt_ref[:, tk:])
    value_update(jnp.maximum(t - 1, 0), 1, alpha_prev)
    s = s_buf[0]
    s = jnp.concatenate([jnp.where(causal, s[:, :tk], MASKED), s[:, tk:]], axis=1)
    m, alpha, p_buf[0] = softmax(s, m)
    value_update(t, 0, alpha)
    _, alpha_r, p_r = softmax(jnp.where(causal, s_right, MASKED), m[:, tk:])
    acc_ref[:, tk:] = alpha_r * acc_ref[:, tk:] + _dot(v_blk(t + 1), p_r)
    acc = acc_ref[...]
    o_ref[0] = (acc[:HEAD_DIM] / acc[HEAD_DIM:HEAD_DIM + 1]).T.astype(BF16)


def _attn_call(q, k, vt):
    B, S, _ = q.shape
    tq, tk = ATTN_TQ, ATTN_TK
    assert tq == 2 * tk
    return pl.pallas_call(
        functools.partial(_attn_kernel, tq=tq, tk=tk),
        grid=(B, N_HEADS, S // tq),
        in_specs=[
            pl.BlockSpec((1, tq, QK_PAD), lambda b, h, i: (b, i, h)),
            pl.BlockSpec((1, S, QK_PAD), lambda b, h, i: (b, 0, h)),
            pl.BlockSpec((1, 1, HEAD_DIM, S), lambda b, h, i: (b, h, 0, 0)),
        ],
        out_specs=pl.BlockSpec((1, tq, HEAD_DIM), lambda b, h, i: (b, i, h)),
        out_shape=jax.ShapeDtypeStruct((B, S, N_HEADS * HEAD_DIM), BF16),
        scratch_shapes=[
            pltpu.VMEM((QK_PAD, tq), BF16),
            pltpu.VMEM((2, tk, tq), F32),
            pltpu.VMEM((2, tk, tq), BF16),
            pltpu.VMEM((ACC_ROWS, tq), F32),
        ],
        compiler_params=pltpu.CompilerParams(
            dimension_semantics=("arbitrary", "arbitrary", "arbitrary"), vmem_limit_bytes=VMEM_LIMIT),
        name="mla_attn",
    )(q, k, vt)


def _mlp_kernel(x_ref, ret_ref, mla_ref, mod_ref, wo32_hbm, ln1w_ref, ln1b_ref, wup32_hbm, cw_ref, cb_ref,
                wdn32_hbm, ln2w_ref, ln2b_ref, o_ref, carry_ref, ubuf_ref, act_ref,
                wo_ref, wup_ref, wdn_ref, stage_sq_ref, stage_up_ref, sem, *, tm):
    @pl.when(jnp.logical_and(pl.program_id(0) == 0, pl.program_id(1) == 0))
    def _():
        _stream_cast(wo32_hbm.at[0], stage_sq_ref, sem, _store_bf16(wo_ref))
        _stream_cast(wup32_hbm.at[0], stage_up_ref, sem, _store_bf16(wup_ref))
        _stream_cast(wdn32_hbm.at[0], stage_sq_ref, sem, _store_bf16(wdn_ref))

    @pl.when(pl.program_id(1) == 0)
    def _():
        carry_ref[...] = jnp.zeros_like(carry_ref)

    gate1 = mod_ref[0, 2:3, :]
    shift2 = mod_ref[0, 3:4, :]
    scale2 = mod_ref[0, 4:5, :]
    gate2 = mod_ref[0, 5:6, :]

    halves = (slice(0, tm // 2), slice(tm // 2, tm))

    def out_proj(rows):
        return (_dot(ret_ref[0, rows, :], wo_ref[0:RET_WIDTH, :])
                + _dot(mla_ref[0, rows, :], wo_ref[RET_WIDTH:2 * RET_WIDTH, :]))

    def norms(y, rows):
        x1 = _ln_plain(DN_ALPHA * x_ref[0, rows, :] + (1.0 + gate1) * y) * ln1w_ref[...] + ln1b_ref[...]
        return x1, (_ln_plain(x1) * (1.0 + scale2) + shift2).astype(BF16)

    ys = [out_proj(rows) for rows in halves]
    x1s, h2s = zip(*[norms(y, rows) for y, rows in zip(ys, halves)])
    h2 = jnp.concatenate(h2s, axis=0)

    def conv(u, col0, kind):
        outs = []
        for g in range(FF_CHUNK // LANES):
            cols = slice(col0 + g * LANES, col0 + (g + 1) * LANES)
            ug = u[:, g * LANES:(g + 1) * LANES]
            buf = ubuf_ref.at[kind, g]
            buf[0:SUBLANES, :] = carry_ref[:, cols]
            buf[SUBLANES:SUBLANES + tm, :] = ug
            carry_ref[:, cols] = ug[tm - SUBLANES:tm, :]
            u1 = buf[SUBLANES - 1:SUBLANES - 1 + tm, :]
            u2 = buf[SUBLANES - 2:SUBLANES - 2 + tm, :]
            outs.append(cb_ref[:, cols] + cw_ref[2:3, cols] * ug + cw_ref[1:2, cols] * u1
                        + cw_ref[0:1, cols] * u2)
        return jnp.concatenate(outs, axis=1)

    for c in range(N_FF_CHUNKS):
        gcol0 = c * FF_CHUNK
        vcol0 = D_FF + c * FF_CHUNK
        g = conv(_dot(h2, wup_ref[:, gcol0:gcol0 + FF_CHUNK]), gcol0, 0)
        val = conv(_dot(h2, wup_ref[:, vcol0:vcol0 + FF_CHUNK]), vcol0, 1)
        act_ref[:, gcol0:gcol0 + FF_CHUNK] = (_silu(g) * val).astype(BF16)

    y2s = [_dot(act_ref[rows, :], wdn_ref[...]) for rows in halves]
    for x1, y2, rows in zip(x1s, y2s, halves):
        o_ref[0, rows, :] = _ln_plain(DN_ALPHA * x1 + (1.0 + gate2) * y2) * ln2w_ref[...] + ln2b_ref[...]


def _mlp_call(x, ret, mla, mod, w_out, ln1w, ln1b, w_up, conv_w, conv_b, w_down, ln2w, ln2b):
    B, S, D = x.shape
    tm = ROW_TILE
    row = lambda w: pl.BlockSpec((1, tm, w), lambda b, s: (b, s, 0))
    hbm = pl.BlockSpec(memory_space=pl.ANY)
    return pl.pallas_call(
        functools.partial(_mlp_kernel, tm=tm),
        grid=(B, S // tm),
        in_specs=[
            row(D), row(RET_WIDTH), row(RET_WIDTH),
            pl.BlockSpec((1, N_MOD, D), lambda b, s: (b, 0, 0)),
            hbm, _const_spec(ln1w.shape), _const_spec(ln1b.shape),
            hbm, _const_spec(conv_w.shape), _const_spec(conv_b.shape),
            hbm, _const_spec(ln2w.shape), _const_spec(ln2b.shape),
        ],
        out_specs=row(D),
        out_shape=jax.ShapeDtypeStruct((B, S, D), F32),
        scratch_shapes=[
            pltpu.VMEM((SUBLANES, 2 * D_FF), F32),
            pltpu.VMEM((2, FF_CHUNK // LANES, SUBLANES + tm, LANES), F32),
            pltpu.VMEM((tm, D_FF), BF16),
            pltpu.VMEM(w_out.shape[1:], BF16),
            pltpu.VMEM(w_up.shape[1:], BF16),
            pltpu.VMEM(w_down.shape[1:], BF16),
            pltpu.VMEM((STAGE_SLOTS, STAGE_ROWS_SQ, D), F32),
            pltpu.VMEM((STAGE_SLOTS, STAGE_ROWS_UP, 2 * D_FF), F32),
            pltpu.SemaphoreType.DMA((STAGE_SLOTS,)),
        ],
        compiler_params=pltpu.CompilerParams(
            dimension_semantics=("arbitrary", "arbitrary"), vmem_limit_bytes=VMEM_LIMIT),
        name="outproj_mlp",
    )(x, ret, mla, mod, w_out, ln1w, ln1b, w_up, conv_w, conv_b, w_down, ln2w, ln2b)


def _rope_tables(seq, half, reps):
    pos = np.arange(seq, dtype=np.float32)
    inv = np.float32(ROPE_BASE) ** (-np.arange(half, dtype=np.float32) / np.float32(half))
    ang = (pos[:, None] * inv[None, :]).astype(np.float32)
    cos = np.cos(ang).astype(np.float32)
    sin = np.sin(ang).astype(np.float32)
    cos_t = np.tile(cos, (1, 2 * reps))
    sin_t = np.concatenate([np.tile(-sin, (1, reps)), np.tile(sin, (1, reps))], axis=-1)
    return jnp.asarray(cos_t), jnp.asarray(sin_t)


def _uq_columns():
    half = MLA_ROPE_DIM // 2
    nope = [h * MLA_QK_DIM + np.arange(HEAD_DIM) for h in range(N_HEADS)]
    rope = []
    for p in range(N_HEADS // 2):
        for part in range(2):
            for h in (2 * p, 2 * p + 1):
                rope.append(h * MLA_QK_DIM + HEAD_DIM + part * half + np.arange(half))
    return np.concatenate(nope + rope)


def _take_columns(w, cols):
    cols = np.asarray(cols)
    cuts = np.flatnonzero(np.diff(cols) != 1) + 1
    runs = np.split(cols, cuts)
    return jnp.concatenate([w[:, r[0]:r[-1] + 1] for r in runs], axis=1)


def kernel(x, c, w_ada, b_ada, w_in, ret_gn_w, mla_q_norm_w, w_uq, mla_kv_norm_w, w_ukv, w_out,
           ln1_w, ln1_b, w_up, conv_w, conv_b, w_down, ln2_w, ln2_b):
    B, S, D = x.shape
    assert D == D_MODEL and S % IN_ROW_TILE == 0 and S % ROW_TILE == 0 and S % ATTN_TQ == 0
    assert w_ada.shape[0] == DEPTH == 1
    l = 0

    mod = _ada_call(c, w_ada, b_ada[l][None, :]).reshape(B, N_MOD, D)

    w_uq_p = _take_columns(w_uq[l].astype(BF16), _uq_columns())
    w_ukv_p = w_ukv[l].astype(BF16)
    cosr, sinr = _rope_tables(S, HEAD_DIM // 2, 1)
    cosm, sinm = _rope_tables(S, MLA_ROPE_DIM // 2, 2)

    ret, q, k, vt = _inproj_call(
        x, mod, jnp.swapaxes(w_in, 1, 2), ret_gn_w[l][None, :], mla_q_norm_w[l][None, :], mla_kv_norm_w[l][None, :],
        w_uq_p, w_ukv_p, cosr, sinr, cosm, sinm)
    mla = _attn_call(q, k, vt)
    return _mlp_call(
        x, ret, mla, mod, w_out, ln1_w[l][None, :], ln1_b[l][None, :],
        w_up, conv_w[l], conv_b[l][None, :], w_down, ln2_w[l][None, :], ln2_b[l][None, :])
```

```python
import functools
import math

import numpy as np
import jax
import jax.numpy as jnp
from jax import lax
from jax.experimental import pallas as pl
from jax.experimental.pallas import tpu as pltpu

F32 = jnp.float32
BF16 = jnp.bfloat16

D_MODEL = 1024
DEPTH = 1
N_HEADS = 4
HEAD_DIM = 128
RET_WIDTH = N_HEADS * HEAD_DIM
LAT_COL0 = 4 * RET_WIDTH
MLA_RANK = 256
MLA_ROPE_DIM = 64
MLA_QK_DIM = HEAD_DIM + MLA_ROPE_DIM
D_FF = 2816
ROPE_BASE = 10000.0
LN_EPS = 1e-5
RMS_EPS = 1e-6
DN_ALPHA = (2.0 * DEPTH) ** 0.25
N_MOD = 6
LOG_G = [math.log1p(-(2.0 ** (-5.0 - h))) for h in range(N_HEADS)]

LANES = 128
SUBLANES = 8
MXU_DIM = 256
VMEM_BYTES = 64 * 1024 * 1024
VMEM_LIMIT = VMEM_BYTES * 7 // 8

ADA_ROWS = 256
IN_ROW_TILE = 1024
ROW_TILE = 512
RET_CHUNK = MXU_DIM
ATTN_TQ = 2048
ATTN_TK = 1024
MLP_IN_PARTS = 2
MLP_OUT_PARTS = 2
FF_CHUNK = MXU_DIM
N_FF_CHUNKS = D_FF // FF_CHUNK
STAGE_ROWS_SQ = 256
STAGE_ROWS_UP = 64
STAGE_ROWS_IN = 128
STAGE_SLOTS = 4
QK_PAD = 2 * LANES
BF16_ROWS = 2 * SUBLANES
ACC_ROWS = HEAD_DIM + BF16_ROWS
MASKED = -float("inf")


def _ln_plain(x):
    mu = jnp.mean(x, axis=-1, keepdims=True)
    xc = x - mu
    var = jnp.mean(xc * xc, axis=-1, keepdims=True)
    return xc * lax.rsqrt(var + LN_EPS)


def _rms(x):
    return x * lax.rsqrt(jnp.mean(x * x, axis=-1, keepdims=True) + RMS_EPS)


def _silu(x):
    return x * jax.nn.sigmoid(x)


def _dot(a, b):
    return jnp.dot(a, b, preferred_element_type=F32)


def _dot_nt(a, b):
    return lax.dot_general(a, b, (((1,), (1,)), ((), ())), preferred_element_type=F32)


def _dot_tn(a, b):
    return lax.dot_general(a, b, (((0,), (0,)), ((), ())), preferred_element_type=F32)


def _ada_kernel(c_ref, w_ref, b_ref, o_ref):
    @pl.when(pl.program_id(0) == 0)
    def _():
        o_ref[...] = jnp.broadcast_to(b_ref[...], o_ref.shape)

    cond = _silu(c_ref[...])
    o_ref[...] += _dot(cond.astype(BF16), w_ref[0].astype(BF16))


def _ada_call(c, w_ada, b_ada):
    n_batch = c.shape[0]
    _, d_in, n_out = w_ada.shape
    return pl.pallas_call(
        _ada_kernel,
        grid=(d_in // ADA_ROWS,),
        in_specs=[
            pl.BlockSpec((n_batch, ADA_ROWS), lambda j: (0, j)),
            pl.BlockSpec((1, ADA_ROWS, n_out), lambda j: (0, j, 0)),
            pl.BlockSpec((1, n_out), lambda j: (0, 0)),
        ],
        out_specs=pl.BlockSpec((n_batch, n_out), lambda j: (0, 0)),
        out_shape=jax.ShapeDtypeStruct((n_batch, n_out), F32),
        compiler_params=pltpu.CompilerParams(dimension_semantics=("arbitrary",), vmem_limit_bytes=VMEM_LIMIT),
        name="ada_mod",
    )(c, w_ada, b_ada)


def _rope128(x, cos, sin_signed):
    return x * cos + pltpu.roll(x, LANES // 2, axis=1) * sin_signed


def _decay_matrix(hd, c):
    row = lax.broadcasted_iota(jnp.int32, (c, c), 0)
    col = lax.broadcasted_iota(jnp.int32, (c, c), 1)
    rel = (row - col).astype(F32)
    return jnp.where(rel >= 0.0, jnp.exp(LOG_G[hd] * jnp.maximum(rel, 0.0)), 0.0)


def _retention_head(hd, q, k, v, gate, gnw, decay_ref, state_ref, ret_ref):
    c = decay_ref.shape[1]
    lg = LOG_G[hd]
    n = lax.broadcasted_iota(jnp.int32, (c, 1), 0).astype(F32)
    q_w = jnp.exp(lg * (n + 1.0))
    k_w = jnp.exp(lg * (c - 1.0 - n))
    for i in range(q.shape[0] // c):
        rows = slice(i * c, (i + 1) * c)
        qb = q[rows].astype(BF16)
        vb = v[rows].astype(BF16)
        scores = _dot_nt(qb, k[rows].astype(BF16)) * decay_ref[hd]
        inner = _dot(scores.astype(BF16), vb)
        state = state_ref[hd]
        cross = _dot(qb, state.astype(BF16)) * q_w
        state_ref[hd] = state * math.exp(lg * c) + _dot_tn((k[rows] * k_w).astype(BF16), vb)
        o = _ln_plain(inner + cross) * gnw
        ret_ref[0, rows, hd * HEAD_DIM:(hd + 1) * HEAD_DIM] = (o * gate[rows]).astype(BF16)


def _stream_cast(src_hbm, stage_ref, sem, store, n_rows=None):
    slots, rows = stage_ref.shape[0], stage_ref.shape[1]
    n_rows = src_hbm.shape[0] if n_rows is None else n_rows
    n_chunks = n_rows // rows
    assert n_chunks * rows == n_rows and n_chunks >= slots

    def copy(k):
        slot = k % slots
        return pltpu.make_async_copy(
            src_hbm.at[pl.ds(pl.multiple_of(k * rows, rows), rows), :], stage_ref.at[slot], sem.at[slot])

    for k in range(slots - 1):
        copy(k).start()

    def body(k, _):
        @pl.when(k + slots - 1 < n_chunks)
        def _():
            copy(k + slots - 1).start()

        copy(k).wait()
        store(pl.ds(pl.multiple_of(k * rows, rows), rows), stage_ref[k % slots])
        return 0

    lax.fori_loop(0, n_chunks, body, 0)


def _store_bf16(dst_ref):
    def store(rows, chunk):
        dst_ref[rows, :] = chunk.astype(BF16)
    return store


def _stage_w_in(wt_hbm, win_ref, stage_ref, sem):
    base = LAT_COL0 + 2 * MLA_RANK
    quarter = LANES // 4

    def store(cols, chunk):
        win_ref[:, cols] = chunk.T.astype(BF16)

    _stream_cast(wt_hbm, stage_ref, sem, store, n_rows=base)

    tail_copy = pltpu.make_async_copy(
        wt_hbm.at[pl.ds(base, MLA_ROPE_DIM), :], stage_ref.at[0, pl.ds(0, MLA_ROPE_DIM), :], sem.at[0])
    tail_copy.start()
    tail_copy.wait()
    tail = stage_ref[0, 0:MLA_ROPE_DIM, :].T
    t2 = jnp.concatenate([tail, tail], axis=1)
    lane = lax.broadcasted_iota(jnp.int32, t2.shape, 1)
    middle = jnp.abs(2 * lane - (LANES - 1)) < 2 * quarter
    win_ref[:, base:base + LANES] = jnp.where(middle, pltpu.roll(t2, quarter, axis=1), t2).astype(BF16)


def _inproj_kernel(x_ref, mod_ref, wint_hbm, gnw_ref, qnw_ref, kvnw_ref, wuq_ref, wukv_ref,
                   cosr_ref, sinr_ref, cosm_ref, sinm_ref,
                   ret_ref, q_ref, k_ref, vt_ref, state_ref, decay_ref, win_ref, stage_ref, sem,
                   *, q_scale, rk_scale):
    @pl.when(jnp.logical_and(pl.program_id(0) == 0, pl.program_id(1) == 0))
    def _():
        _stage_w_in(wint_hbm.at[0], win_ref, stage_ref, sem)
        for hd in range(N_HEADS):
            decay_ref[hd] = _decay_matrix(hd, decay_ref.shape[1])

    @pl.when(pl.program_id(1) == 0)
    def _():
        state_ref[...] = jnp.zeros_like(state_ref)

    shift = mod_ref[0, 0:1, :]
    scale = mod_ref[0, 1:2, :]
    tm = x_ref.shape[1]
    hs, pqs = [], []
    for rows in (slice(0, tm // 2), slice(tm // 2, tm)):
        hs.append((_ln_plain(x_ref[0, rows, :]) * (1.0 + scale) + shift).astype(BF16))
        pqs.append(_dot(hs[-1], win_ref[:, 0:RET_WIDTH]))
    h = jnp.concatenate(hs, axis=0)

    cosr = cosr_ref[...]
    sinr = sinr_ref[...]
    cosm = cosm_ref[...]
    sinm = sinm_ref[...]

    pq = jnp.concatenate(pqs, axis=0)
    pk = _dot(h, win_ref[:, RET_WIDTH:2 * RET_WIDTH])
    pv = _dot(h, win_ref[:, 2 * RET_WIDTH:3 * RET_WIDTH])
    pg = _silu(_dot(h, win_ref[:, 3 * RET_WIDTH:LAT_COL0]))
    for hd in range(N_HEADS):
        sl = slice(hd * HEAD_DIM, (hd + 1) * HEAD_DIM)
        _retention_head(hd, _rope128(pq[:, sl], cosr, sinr), _rope128(pk[:, sl], cosr, sinr) * rk_scale,
                        pv[:, sl], pg[:, sl], gnw_ref[:, sl], decay_ref, state_ref, ret_ref)

    lat = _dot(h, win_ref[:, LAT_COL0:])
    cq = (_rms(lat[:, 0:MLA_RANK]) * qnw_ref[...]).astype(BF16)
    ckv = (_rms(lat[:, MLA_RANK:2 * MLA_RANK]) * kvnw_ref[...]).astype(BF16)
    kr = _rope128(lat[:, 2 * MLA_RANK:2 * MLA_RANK + LANES], cosm, sinm)
    lane = lax.broadcasted_iota(jnp.int32, kr.shape, 1)
    first_of_pair = (lane % (LANES // 2)) < (LANES // 4)
    kr_even = jnp.where(first_of_pair, kr, 0.0).astype(BF16)
    kr_odd = jnp.where(first_of_pair, 0.0, kr).astype(BF16)

    qf = _dot(cq, wuq_ref[...])
    kvf = _dot(ckv, wukv_ref[...])
    for hd in range(N_HEADS):
        vt_ref[0, hd] = kvf[:, (2 * hd + 1) * HEAD_DIM:(2 * hd + 2) * HEAD_DIM].T.astype(BF16)
    for p in range(N_HEADS // 2):
        qr = qf[:, RET_WIDTH + p * LANES:RET_WIDTH + (p + 1) * LANES]
        qr = (_rope128(qr, cosm, sinm) * q_scale).astype(BF16)
        for hd in (2 * p, 2 * p + 1):
            sl = slice(hd * HEAD_DIM, (hd + 1) * HEAD_DIM)
            q_ref[0, :, hd * QK_PAD:hd * QK_PAD + LANES] = (qf[:, sl] * q_scale).astype(BF16)
            q_ref[0, :, hd * QK_PAD + LANES:(hd + 1) * QK_PAD] = qr
            k_ref[0, :, hd * QK_PAD:hd * QK_PAD + LANES] = kvf[:, 2 * hd * HEAD_DIM:(2 * hd + 1) * HEAD_DIM].astype(BF16)
            k_ref[0, :, hd * QK_PAD + LANES:(hd + 1) * QK_PAD] = kr_even if hd % 2 == 0 else kr_odd


def _const_spec(shape):
    nd = len(shape)
    return pl.BlockSpec(shape, lambda *_: (0,) * nd, pipeline_mode=pl.Buffered(1))


def _inproj_call(x, mod, w_in, gnw, qnw, kvnw, w_uq, w_ukv, cosr, sinr, cosm, sinm):
    B, S, D = x.shape
    tm = IN_ROW_TILE
    q_scale = (MLA_QK_DIM ** -0.5) * math.log2(math.e)
    rk_scale = HEAD_DIM ** -0.5
    row = lambda w: pl.BlockSpec((1, tm, w), lambda b, s: (b, s, 0))
    tab = pl.BlockSpec((tm, LANES), lambda b, s: (s, 0))
    out_shapes = (
        [jax.ShapeDtypeStruct((B, S, RET_WIDTH), BF16)]
        + [jax.ShapeDtypeStruct((B, S, N_HEADS * QK_PAD), BF16)] * 2
        + [jax.ShapeDtypeStruct((B, N_HEADS, HEAD_DIM, S), BF16)]
    )
    vt_spec = pl.BlockSpec((1, N_HEADS, HEAD_DIM, tm), lambda b, s: (b, 0, 0, s))
    return pl.pallas_call(
        functools.partial(_inproj_kernel, q_scale=q_scale, rk_scale=rk_scale),
        grid=(B, S // tm),
        in_specs=[
            row(D),
            pl.BlockSpec((1, N_MOD, D), lambda b, s: (b, 0, 0)),
            pl.BlockSpec(memory_space=pl.ANY), _const_spec(gnw.shape),
            _const_spec(qnw.shape), _const_spec(kvnw.shape),
            _const_spec(w_uq.shape), _const_spec(w_ukv.shape),
            tab, tab, tab, tab,
        ],
        out_specs=[row(RET_WIDTH)] + [row(N_HEADS * QK_PAD)] * 2 + [vt_spec],
        out_shape=out_shapes,
        scratch_shapes=[
            pltpu.VMEM((N_HEADS, HEAD_DIM, HEAD_DIM), F32),
            pltpu.VMEM((N_HEADS, RET_CHUNK, RET_CHUNK), F32),
            pltpu.VMEM((D, LAT_COL0 + 2 * MLA_RANK + LANES), BF16),
            pltpu.VMEM((STAGE_SLOTS, STAGE_ROWS_IN, w_in.shape[-1]), F32),
            pltpu.SemaphoreType.DMA((STAGE_SLOTS,)),
        ],
        compiler_params=pltpu.CompilerParams(
            dimension_semantics=("arbitrary", "arbitrary"), vmem_limit_bytes=VMEM_LIMIT),
        name="inproj_retention",
    )(x, mod, w_in, gnw, qnw, kvnw, w_uq, w_ukv, cosr, sinr, cosm, sinm)


def _attn_kernel(q_ref, k_ref, vt_ref, o_ref, qt_ref, s_buf, p_buf, acc_ref, *, tq, tk):
    qi = pl.program_id(2)
    qt_ref[...] = q_ref[0].T

    ones = jnp.ones((BF16_ROWS, tk), BF16)

    def k_blk(j):
        return k_ref[0, pl.ds(pl.multiple_of(j * tk, tk), tk), :]

    def v_blk(j):
        return jnp.concatenate([vt_ref[0, 0, :, pl.ds(pl.multiple_of(j * tk, tk), tk)], ones], axis=0)

    def scores(j, slot):
        s = _dot(k_blk(j), qt_ref[...])
        s_buf[slot] = s
        return jnp.max(s, axis=0, keepdims=True)

    def value_update(j, slot, alpha):
        acc_ref[...] = alpha * acc_ref[...] + _dot(v_blk(j), p_buf[slot])

    def softmax(s, m, block_max=None):
        if block_max is None:
            block_max = jnp.max(s, axis=0, keepdims=True)
        m_new = jnp.maximum(m, block_max)
        return m_new, jnp.exp2(m - m_new), jnp.exp2(s - m_new).astype(BF16)

    def two_blocks(u, carry):
        alpha_prev, m, max_even = carry
        t = 2 * u
        max_odd = scores(t + 1, 1)
        value_update(jnp.maximum(t - 1, 0), 1, alpha_prev)
        m, alpha_even, p_buf[0] = softmax(s_buf[0], m, max_even)
        max_next = scores(t + 2, 0)
        value_update(t, 0, alpha_even)
        m, alpha_odd, p_buf[1] = softmax(s_buf[1], m, max_odd)
        return alpha_odd, m, max_next

    max_0 = scores(0, 0)
    p_buf[1] = jnp.zeros((tk, tq), BF16)
    acc_ref[...] = jnp.zeros_like(acc_ref)
    init = (jnp.ones((1, tq), F32), jnp.full((1, tq), MASKED, F32), max_0)
    alpha_prev, m, _ = lax.fori_loop(0, qi, two_blocks, init)

    t = 2 * qi
    key = lax.broadcasted_iota(jnp.int32, (tk, tk), 0)
    qry = lax.broadcasted_iota(jnp.int32, (tk, tk), 1)
    causal = key <= qry
    s_right = _dot(k_blk(t + 1), qt_ref[:, tk:])
    value_update(jnp.maximum(t - 1, 0), 1, alpha_prev)
    s = s_buf[0]
    s = jnp.concatenate([jnp.where(causal, s[:, :tk], MASKED), s[:, tk:]], axis=1)
    m, alpha, p_buf[0] = softmax(s, m)
    value_update(t, 0, alpha)
    _, alpha_r, p_r = softmax(jnp.where(causal, s_right, MASKED), m[:, tk:])
    acc_ref[:, tk:] = alpha_r * acc_ref[:, tk:] + _dot(v_blk(t + 1), p_r)
    acc = acc_ref[...]
    o_ref[0] = (acc[:HEAD_DIM] / acc[HEAD_DIM:HEAD_DIM + 1]).T.astype(BF16)


def _attn_call(q, k, vt):
    B, S, _ = q.shape
    tq, tk = ATTN_TQ, ATTN_TK
    assert tq == 2 * tk
    return pl.pallas_call(
        functools.partial(_attn_kernel, tq=tq, tk=tk),
        grid=(B, N_HEADS, S // tq),
        in_specs=[
            pl.BlockSpec((1, tq, QK_PAD), lambda b, h, i: (b, i, h)),
            pl.BlockSpec((1, S, QK_PAD), lambda b, h, i: (b, 0, h)),
            pl.BlockSpec((1, 1, HEAD_DIM, S), lambda b, h, i: (b, h, 0, 0)),
        ],
        out_specs=pl.BlockSpec((1, tq, HEAD_DIM), lambda b, h, i: (b, i, h)),
        out_shape=jax.ShapeDtypeStruct((B, S, N_HEADS * HEAD_DIM), BF16),
        scratch_shapes=[
            pltpu.VMEM((QK_PAD, tq), BF16),
            pltpu.VMEM((2, tk, tq), F32),
            pltpu.VMEM((2, tk, tq), BF16),
            pltpu.VMEM((ACC_ROWS, tq), F32),
        ],
        compiler_params=pltpu.CompilerParams(
            dimension_semantics=("arbitrary", "arbitrary", "arbitrary"), vmem_limit_bytes=VMEM_LIMIT),
        name="mla_attn",
    )(q, k, vt)


def _mlp_kernel(x_ref, ret_ref, mla_ref, mod_ref, wo32_hbm, ln1w_ref, ln1b_ref, wup32_hbm, cw_ref, cb_ref,
                wdn32_hbm, ln2w_ref, ln2b_ref, o_ref, carry_ref, ubuf_ref, act_ref,
                wo_ref, wup_ref, wdn_ref, stage_sq_ref, stage_up_ref, sem, *, tm):
    @pl.when(jnp.logical_and(pl.program_id(0) == 0, pl.program_id(1) == 0))
    def _():
        _stream_cast(wo32_hbm.at[0], stage_sq_ref, sem, _store_bf16(wo_ref))
        _stream_cast(wup32_hbm.at[0], stage_up_ref, sem, _store_bf16(wup_ref))
        _stream_cast(wdn32_hbm.at[0], stage_sq_ref, sem, _store_bf16(wdn_ref))

    @pl.when(pl.program_id(1) == 0)
    def _():
        carry_ref[...] = jnp.zeros_like(carry_ref)

    gate1 = mod_ref[0, 2:3, :]
    shift2 = mod_ref[0, 3:4, :]
    scale2 = mod_ref[0, 4:5, :]
    gate2 = mod_ref[0, 5:6, :]

    def row_parts(n):
        return [slice(i * tm // n, (i + 1) * tm // n) for i in range(n)]

    def out_proj(rows):
        return (_dot(ret_ref[0, rows, :], wo_ref[0:RET_WIDTH, :])
                + _dot(mla_ref[0, rows, :], wo_ref[RET_WIDTH:2 * RET_WIDTH, :]))

    def norms(y, rows):
        x1 = _ln_plain(DN_ALPHA * x_ref[0, rows, :] + (1.0 + gate1) * y) * ln1w_ref[...] + ln1b_ref[...]
        return x1, (_ln_plain(x1) * (1.0 + scale2) + shift2).astype(BF16)

    ys = [out_proj(rows) for rows in row_parts(MLP_IN_PARTS)]
    x1s, h2s = zip(*[norms(y, rows) for y, rows in zip(ys, row_parts(MLP_IN_PARTS))])
    x1 = jnp.concatenate(x1s, axis=0)
    h2 = jnp.concatenate(h2s, axis=0)

    def conv(u, col0, kind):
        outs = []
        for g in range(FF_CHUNK // LANES):
            cols = slice(col0 + g * LANES, col0 + (g + 1) * LANES)
            ug = u[:, g * LANES:(g + 1) * LANES]
            buf = ubuf_ref.at[kind, g]
            buf[0:SUBLANES, :] = carry_ref[:, cols]
            buf[SUBLANES:SUBLANES + tm, :] = ug
            carry_ref[:, cols] = ug[tm - SUBLANES:tm, :]
            u1 = buf[SUBLANES - 1:SUBLANES - 1 + tm, :]
            u2 = buf[SUBLANES - 2:SUBLANES - 2 + tm, :]
            outs.append(cb_ref[:, cols] + cw_ref[2:3, cols] * ug + cw_ref[1:2, cols] * u1
                        + cw_ref[0:1, cols] * u2)
        return jnp.concatenate(outs, axis=1)

    for c in range(N_FF_CHUNKS):
        gcol0 = c * FF_CHUNK
        vcol0 = D_FF + c * FF_CHUNK
        g = conv(_dot(h2, wup_ref[:, gcol0:gcol0 + FF_CHUNK]), gcol0, 0)
        val = conv(_dot(h2, wup_ref[:, vcol0:vcol0 + FF_CHUNK]), vcol0, 1)
        act_ref[:, gcol0:gcol0 + FF_CHUNK] = (_silu(g) * val).astype(BF16)

    y2s = [_dot(act_ref[rows, :], wdn_ref[...]) for rows in row_parts(MLP_OUT_PARTS)]
    for y2, rows in zip(y2s, row_parts(MLP_OUT_PARTS)):
        o_ref[0, rows, :] = _ln_plain(DN_ALPHA * x1[rows] + (1.0 + gate2) * y2) * ln2w_ref[...] + ln2b_ref[...]


def _mlp_call(x, ret, mla, mod, w_out, ln1w, ln1b, w_up, conv_w, conv_b, w_down, ln2w, ln2b):
    B, S, D = x.shape
    tm = ROW_TILE
    row = lambda w: pl.BlockSpec((1, tm, w), lambda b, s: (b, s, 0))
    hbm = pl.BlockSpec(memory_space=pl.ANY)
    return pl.pallas_call(
        functools.partial(_mlp_kernel, tm=tm),
        grid=(B, S // tm),
        in_specs=[
            row(D), row(RET_WIDTH), row(RET_WIDTH),
            pl.BlockSpec((1, N_MOD, D), lambda b, s: (b, 0, 0)),
            hbm, _const_spec(ln1w.shape), _const_spec(ln1b.shape),
            hbm, _const_spec(conv_w.shape), _const_spec(conv_b.shape),
            hbm, _const_spec(ln2w.shape), _const_spec(ln2b.shape),
        ],
        out_specs=row(D),
        out_shape=jax.ShapeDtypeStruct((B, S, D), F32),
        scratch_shapes=[
            pltpu.VMEM((SUBLANES, 2 * D_FF), F32),
            pltpu.VMEM((2, FF_CHUNK // LANES, SUBLANES + tm, LANES), F32),
            pltpu.VMEM((tm, D_FF), BF16),
            pltpu.VMEM(w_out.shape[1:], BF16),
            pltpu.VMEM(w_up.shape[1:], BF16),
            pltpu.VMEM(w_down.shape[1:], BF16),
            pltpu.VMEM((STAGE_SLOTS, STAGE_ROWS_SQ, D), F32),
            pltpu.VMEM((STAGE_SLOTS, STAGE_ROWS_UP, 2 * D_FF), F32),
            pltpu.SemaphoreType.DMA((STAGE_SLOTS,)),
        ],
        compiler_params=pltpu.CompilerParams(
            dimension_semantics=("arbitrary", "arbitrary"), vmem_limit_bytes=VMEM_LIMIT),
        name="outproj_mlp",
    )(x, ret, mla, mod, w_out, ln1w, ln1b, w_up, conv_w, conv_b, w_down, ln2w, ln2b)


def _rope_tables(seq, half, reps):
    pos = np.arange(seq, dtype=np.float32)
    inv = np.float32(ROPE_BASE) ** (-np.arange(half, dtype=np.float32) / np.float32(half))
    ang = (pos[:, None] * inv[None, :]).astype(np.float32)
    cos = np.cos(ang).astype(np.float32)
    sin = np.sin(ang).astype(np.float32)
    cos_t = np.tile(cos, (1, 2 * reps))
    sin_t = np.concatenate([np.tile(-sin, (1, reps)), np.tile(sin, (1, reps))], axis=-1)
    return jnp.asarray(cos_t), jnp.asarray(sin_t)


def _uq_columns():
    half = MLA_ROPE_DIM // 2
    nope = [h * MLA_QK_DIM + np.arange(HEAD_DIM) for h in range(N_HEADS)]
    rope = []
    for p in range(N_HEADS // 2):
        for part in range(2):
            for h in (2 * p, 2 * p + 1):
                rope.append(h * MLA_QK_DIM + HEAD_DIM + part * half + np.arange(half))
    return np.concatenate(nope + rope)


def _take_columns(w, cols):
    cols = np.asarray(cols)
    cuts = np.flatnonzero(np.diff(cols) != 1) + 1
    runs = np.split(cols, cuts)
    return jnp.concatenate([w[:, r[0]:r[-1] + 1] for r in runs], axis=1)


def kernel(x, c, w_ada, b_ada, w_in, ret_gn_w, mla_q_norm_w, w_uq, mla_kv_norm_w, w_ukv, w_out,
           ln1_w, ln1_b, w_up, conv_w, conv_b, w_down, ln2_w, ln2_b):
    B, S, D = x.shape
    assert D == D_MODEL and S % IN_ROW_TILE == 0 and S % ROW_TILE == 0 and S % ATTN_TQ == 0
    assert w_ada.shape[0] == DEPTH == 1
    l = 0

    mod = _ada_call(c, w_ada, b_ada[l][None, :]).reshape(B, N_MOD, D)

    w_uq_p = _take_columns(w_uq[l].astype(BF16), _uq_columns())
    w_ukv_p = w_ukv[l].astype(BF16)
    cosr, sinr = _rope_tables(S, HEAD_DIM // 2, 1)
    cosm, sinm = _rope_tables(S, MLA_ROPE_DIM // 2, 2)

    ret, q, k, vt = _inproj_call(
        x, mod, jnp.swapaxes(w_in, 1, 2), ret_gn_w[l][None, :], mla_q_norm_w[l][None, :], mla_kv_norm_w[l][None, :],
        w_uq_p, w_ukv_p, cosr, sinr, cosm, sinm)
    mla = _attn_call(q, k, vt)
    return _mlp_call(
        x, ret, mla, mod, w_out, ln1_w[l][None, :], ln1_b[l][None, :],
        w_up, conv_w[l], conv_b[l][None, :], w_down, ln2_w[l][None, :], ln2_b[l][None, :])
```

```python
import functools
import math

import numpy as np
import jax
import jax.numpy as jnp
from jax import lax
from jax.experimental import pallas as pl
from jax.experimental.pallas import tpu as pltpu

F32 = jnp.float32
BF16 = jnp.bfloat16

D_MODEL = 1024
DEPTH = 1
N_HEADS = 4
HEAD_DIM = 128
RET_WIDTH = N_HEADS * HEAD_DIM
LAT_COL0 = 4 * RET_WIDTH
MLA_RANK = 256
MLA_ROPE_DIM = 64
MLA_QK_DIM = HEAD_DIM + MLA_ROPE_DIM
D_FF = 2816
ROPE_BASE = 10000.0
LN_EPS = 1e-5
RMS_EPS = 1e-6
DN_ALPHA = (2.0 * DEPTH) ** 0.25
N_MOD = 6
LOG_G = [math.log1p(-(2.0 ** (-5.0 - h))) for h in range(N_HEADS)]

LANES = 128
SUBLANES = 8
MXU_DIM = 256
VMEM_BYTES = 64 * 1024 * 1024
VMEM_LIMIT = VMEM_BYTES * 7 // 8

ADA_ROWS = 256
IN_ROW_TILE = 1024
ROW_TILE = 512
RET_CHUNK = MXU_DIM
ATTN_TQ = 2048
ATTN_TK = 1024
FF_CHUNK = MXU_DIM
N_FF_CHUNKS = D_FF // FF_CHUNK
STAGE_ROWS_SQ = 256
STAGE_ROWS_UP = 64
STAGE_ROWS_IN = 128
STAGE_SLOTS = 4
QK_PAD = 2 * LANES
BF16_ROWS = 2 * SUBLANES
ACC_ROWS = HEAD_DIM + BF16_ROWS
MASKED = -float("inf")


def _ln_plain(x):
    mu = jnp.mean(x, axis=-1, keepdims=True)
    xc = x - mu
    var = jnp.mean(xc * xc, axis=-1, keepdims=True)
    return xc * lax.rsqrt(var + LN_EPS)


def _rms(x):
    return x * lax.rsqrt(jnp.mean(x * x, axis=-1, keepdims=True) + RMS_EPS)


def _silu(x):
    return x * jax.nn.sigmoid(x)


def _dot(a, b):
    return jnp.dot(a, b, preferred_element_type=F32)


def _dot_nt(a, b):
    return lax.dot_general(a, b, (((1,), (1,)), ((), ())), preferred_element_type=F32)


def _dot_tn(a, b):
    return lax.dot_general(a, b, (((0,), (0,)), ((), ())), preferred_element_type=F32)


def _ada_kernel(c_ref, w_ref, b_ref, o_ref):
    @pl.when(pl.program_id(0) == 0)
    def _():
        o_ref[...] = jnp.broadcast_to(b_ref[...], o_ref.shape)

    cond = _silu(c_ref[...])
    o_ref[...] += _dot(cond.astype(BF16), w_ref[0].astype(BF16))


def _ada_call(c, w_ada, b_ada):
    n_batch = c.shape[0]
    _, d_in, n_out = w_ada.shape
    return pl.pallas_call(
        _ada_kernel,
        grid=(d_in // ADA_ROWS,),
        in_specs=[
            pl.BlockSpec((n_batch, ADA_ROWS), lambda j: (0, j)),
            pl.BlockSpec((1, ADA_ROWS, n_out), lambda j: (0, j, 0)),
            pl.BlockSpec((1, n_out), lambda j: (0, 0)),
        ],
        out_specs=pl.BlockSpec((n_batch, n_out), lambda j: (0, 0)),
        out_shape=jax.ShapeDtypeStruct((n_batch, n_out), F32),
        compiler_params=pltpu.CompilerParams(dimension_semantics=("arbitrary",), vmem_limit_bytes=VMEM_LIMIT),
        name="ada_mod",
    )(c, w_ada, b_ada)


def _rope128(x, cos, sin_signed):
    return x * cos + pltpu.roll(x, LANES // 2, axis=1) * sin_signed


def _decay_matrix(hd, c):
    row = lax.broadcasted_iota(jnp.int32, (c, c), 0)
    col = lax.broadcasted_iota(jnp.int32, (c, c), 1)
    rel = (row - col).astype(F32)
    return jnp.where(rel >= 0.0, jnp.exp(LOG_G[hd] * jnp.maximum(rel, 0.0)), 0.0)


def _retention_head(hd, q, k, v, gate, gnw, decay_ref, state_ref, ret_ref):
    c = decay_ref.shape[1]
    lg = LOG_G[hd]
    n = lax.broadcasted_iota(jnp.int32, (c, 1), 0).astype(F32)
    q_w = jnp.exp(lg * (n + 1.0))
    k_w = jnp.exp(lg * (c - 1.0 - n))
    for i in range(q.shape[0] // c):
        rows = slice(i * c, (i + 1) * c)
        qb = q[rows].astype(BF16)
        vb = v[rows].astype(BF16)
        scores = _dot_nt(qb, k[rows].astype(BF16)) * decay_ref[hd]
        inner = _dot(scores.astype(BF16), vb)
        state = state_ref[hd]
        cross = _dot(qb, state.astype(BF16)) * q_w
        state_ref[hd] = state * math.exp(lg * c) + _dot_tn((k[rows] * k_w).astype(BF16), vb)
        o = _ln_plain(inner + cross) * gnw
        ret_ref[0, rows, hd * HEAD_DIM:(hd + 1) * HEAD_DIM] = (o * gate[rows]).astype(BF16)


def _stream_cast(src_hbm, stage_ref, sem, store, n_rows=None):
    slots, rows = stage_ref.shape[0], stage_ref.shape[1]
    n_rows = src_hbm.shape[0] if n_rows is None else n_rows
    n_chunks = n_rows // rows
    assert n_chunks * rows == n_rows and n_chunks >= slots

    def copy(k):
        slot = k % slots
        return pltpu.make_async_copy(
            src_hbm.at[pl.ds(pl.multiple_of(k * rows, rows), rows), :], stage_ref.at[slot], sem.at[slot])

    for k in range(slots - 1):
        copy(k).start()

    def body(k, _):
        @pl.when(k + slots - 1 < n_chunks)
        def _():
            copy(k + slots - 1).start()

        copy(k).wait()
        store(pl.ds(pl.multiple_of(k * rows, rows), rows), stage_ref[k % slots])
        return 0

    lax.fori_loop(0, n_chunks, body, 0)


def _store_bf16(dst_ref):
    def store(rows, chunk):
        dst_ref[rows, :] = chunk.astype(BF16)
    return store


def _stage_w_in(wt_hbm, win_ref, stage_ref, sem):
    base = LAT_COL0 + 2 * MLA_RANK
    quarter = LANES // 4

    def store(cols, chunk):
        win_ref[:, cols] = chunk.T.astype(BF16)

    _stream_cast(wt_hbm, stage_ref, sem, store, n_rows=base)

    tail_copy = pltpu.make_async_copy(
        wt_hbm.at[pl.ds(base, MLA_ROPE_DIM), :], stage_ref.at[0, pl.ds(0, MLA_ROPE_DIM), :], sem.at[0])
    tail_copy.start()
    tail_copy.wait()
    tail = stage_ref[0, 0:MLA_ROPE_DIM, :].T
    t2 = jnp.concatenate([tail, tail], axis=1)
    lane = lax.broadcasted_iota(jnp.int32, t2.shape, 1)
    middle = jnp.abs(2 * lane - (LANES - 1)) < 2 * quarter
    win_ref[:, base:base + LANES] = jnp.where(middle, pltpu.roll(t2, quarter, axis=1), t2).astype(BF16)


def _inproj_kernel(x_ref, mod_ref, wint_hbm, gnw_ref, qnw_ref, kvnw_ref, wuq_ref, wukv_ref,
                   cosr_ref, sinr_ref, cosm_ref, sinm_ref,
                   ret_ref, q_ref, k_ref, vt_ref, state_ref, decay_ref, win_ref, stage_ref, sem,
                   *, q_scale, rk_scale):
    @pl.when(jnp.logical_and(pl.program_id(0) == 0, pl.program_id(1) == 0))
    def _():
        _stage_w_in(wint_hbm.at[0], win_ref, stage_ref, sem)
        for hd in range(N_HEADS):
            decay_ref[hd] = _decay_matrix(hd, decay_ref.shape[1])

    @pl.when(pl.program_id(1) == 0)
    def _():
        state_ref[...] = jnp.zeros_like(state_ref)

    shift = mod_ref[0, 0:1, :]
    scale = mod_ref[0, 1:2, :]
    tm = x_ref.shape[1]
    hs, pqs = [], []
    for rows in (slice(0, tm // 2), slice(tm // 2, tm)):
        hs.append((_ln_plain(x_ref[0, rows, :]) * (1.0 + scale) + shift).astype(BF16))
        pqs.append(_dot(hs[-1], win_ref[:, 0:RET_WIDTH]))
    h = jnp.concatenate(hs, axis=0)

    cosr = cosr_ref[...]
    sinr = sinr_ref[...]
    cosm = cosm_ref[...]
    sinm = sinm_ref[...]

    pq = jnp.concatenate(pqs, axis=0)
    pk = _dot(h, win_ref[:, RET_WIDTH:2 * RET_WIDTH])
    pv = _dot(h, win_ref[:, 2 * RET_WIDTH:3 * RET_WIDTH])
    pg = _silu(_dot(h, win_ref[:, 3 * RET_WIDTH:LAT_COL0]))
    for hd in range(N_HEADS):
        sl = slice(hd * HEAD_DIM, (hd + 1) * HEAD_DIM)
        _retention_head(hd, _rope128(pq[:, sl], cosr, sinr), _rope128(pk[:, sl], cosr, sinr) * rk_scale,
                        pv[:, sl], pg[:, sl], gnw_ref[:, sl], decay_ref, state_ref, ret_ref)

    lat = _dot(h, win_ref[:, LAT_COL0:])
    cq = (_rms(lat[:, 0:MLA_RANK]) * qnw_ref[...]).astype(BF16)
    ckv = (_rms(lat[:, MLA_RANK:2 * MLA_RANK]) * kvnw_ref[...]).astype(BF16)
    kr = _rope128(lat[:, 2 * MLA_RANK:2 * MLA_RANK + LANES], cosm, sinm)
    lane = lax.broadcasted_iota(jnp.int32, kr.shape, 1)
    first_of_pair = (lane % (LANES // 2)) < (LANES // 4)
    kr_even = jnp.where(first_of_pair, kr, 0.0).astype(BF16)
    kr_odd = jnp.where(first_of_pair, 0.0, kr).astype(BF16)

    qf = _dot(cq, wuq_ref[...])
    kvf = _dot(ckv, wukv_ref[...])
    for hd in range(N_HEADS):
        vt_ref[0, hd] = kvf[:, (2 * hd + 1) * HEAD_DIM:(2 * hd + 2) * HEAD_DIM].T.astype(BF16)
    for p in range(N_HEADS // 2):
        qr = qf[:, RET_WIDTH + p * LANES:RET_WIDTH + (p + 1) * LANES]
        qr = (_rope128(qr, cosm, sinm) * q_scale).astype(BF16)
        for hd in (2 * p, 2 * p + 1):
            sl = slice(hd * HEAD_DIM, (hd + 1) * HEAD_DIM)
            q_ref[0, :, hd * QK_PAD:hd * QK_PAD + LANES] = (qf[:, sl] * q_scale).astype(BF16)
            q_ref[0, :, hd * QK_PAD + LANES:(hd + 1) * QK_PAD] = qr
            k_ref[0, :, hd * QK_PAD:hd * QK_PAD + LANES] = kvf[:, 2 * hd * HEAD_DIM:(2 * hd + 1) * HEAD_DIM].astype(BF16)
            k_ref[0, :, hd * QK_PAD + LANES:(hd + 1) * QK_PAD] = kr_even if hd % 2 == 0 else kr_odd


def _const_spec(shape):
    nd = len(shape)
    return pl.BlockSpec(shape, lambda *_: (0,) * nd, pipeline_mode=pl.Buffered(1))


def _inproj_call(x, mod, w_in, gnw, qnw, kvnw, w_uq, w_ukv, cosr, sinr, cosm, sinm):
    B, S, D = x.shape
    tm = IN_ROW_TILE
    q_scale = (MLA_QK_DIM ** -0.5) * math.log2(math.e)
    rk_scale = HEAD_DIM ** -0.5
    row = lambda w: pl.BlockSpec((1, tm, w), lambda b, s: (b, s, 0))
    tab = pl.BlockSpec((tm, LANES), lambda b, s: (s, 0))
    out_shapes = (
        [jax.ShapeDtypeStruct((B, S, RET_WIDTH), BF16)]
        + [jax.ShapeDtypeStruct((B, S, N_HEADS * QK_PAD), BF16)] * 2
        + [jax.ShapeDtypeStruct((B, N_HEADS, HEAD_DIM, S), BF16)]
    )
    vt_spec = pl.BlockSpec((1, N_HEADS, HEAD_DIM, tm), lambda b, s: (b, 0, 0, s))
    return pl.pallas_call(
        functools.partial(_inproj_kernel, q_scale=q_scale, rk_scale=rk_scale),
        grid=(B, S // tm),
        in_specs=[
            row(D),
            pl.BlockSpec((1, N_MOD, D), lambda b, s: (b, 0, 0)),
            pl.BlockSpec(memory_space=pl.ANY), _const_spec(gnw.shape),
            _const_spec(qnw.shape), _const_spec(kvnw.shape),
            _const_spec(w_uq.shape), _const_spec(w_ukv.shape),
            tab, tab, tab, tab,
        ],
        out_specs=[row(RET_WIDTH)] + [row(N_HEADS * QK_PAD)] * 2 + [vt_spec],
        out_shape=out_shapes,
        scratch_shapes=[
            pltpu.VMEM((N_HEADS, HEAD_DIM, HEAD_DIM), F32),
            pltpu.VMEM((N_HEADS, RET_CHUNK, RET_CHUNK), F32),
            pltpu.VMEM((D, LAT_COL0 + 2 * MLA_RANK + LANES), BF16),
            pltpu.VMEM((STAGE_SLOTS, STAGE_ROWS_IN, w_in.shape[-1]), F32),
            pltpu.SemaphoreType.DMA((STAGE_SLOTS,)),
        ],
        compiler_params=pltpu.CompilerParams(
            dimension_semantics=("arbitrary", "arbitrary"), vmem_limit_bytes=VMEM_LIMIT),
        name="inproj_retention",
    )(x, mod, w_in, gnw, qnw, kvnw, w_uq, w_ukv, cosr, sinr, cosm, sinm)


def _attn_kernel(q_ref, k_ref, vt_ref, o_ref, qt_ref, s_buf, p_buf, acc_ref, *, tq, tk):
    qi = pl.program_id(2)
    qt_ref[...] = q_ref[0].T

    ones = jnp.ones((BF16_ROWS, tk), BF16)

    def k_blk(j):
        return k_ref[0, pl.ds(pl.multiple_of(j * tk, tk), tk), :]

    def v_blk(j):
        return jnp.concatenate([vt_ref[0, 0, :, pl.ds(pl.multiple_of(j * tk, tk), tk)], ones], axis=0)

    def scores(j, slot):
        s = _dot(k_blk(j), qt_ref[...])
        s_buf[slot] = s
        return jnp.max(s, axis=0, keepdims=True)

    def value_update(j, slot, alpha):
        acc_ref[...] = alpha * acc_ref[...] + _dot(v_blk(j), p_buf[slot])

    def softmax(s, m, block_max=None):
        if block_max is None:
            block_max = jnp.max(s, axis=0, keepdims=True)
        m_new = jnp.maximum(m, block_max)
        return m_new, jnp.exp2(m - m_new), jnp.exp2(s - m_new).astype(BF16)

    def two_blocks(u, carry):
        alpha_prev, m, max_even = carry
        t = 2 * u
        max_odd = scores(t + 1, 1)
        value_update(jnp.maximum(t - 1, 0), 1, alpha_prev)
        m, alpha_even, p_buf[0] = softmax(s_buf[0], m, max_even)
        max_next = scores(t + 2, 0)
        value_update(t, 0, alpha_even)
        m, alpha_odd, p_buf[1] = softmax(s_buf[1], m, max_odd)
        return alpha_odd, m, max_next

    def reset():
        p_buf[1] = jnp.zeros((tk, tq), BF16)
        acc_ref[...] = jnp.zeros_like(acc_ref)

    @pl.when((pl.program_id(0) == 0) & (pl.program_id(1) == 0) & (qi == 0))
    def _():
        reset()

    max_0 = scores(0, 0)
    init = (jnp.ones((1, tq), F32), jnp.full((1, tq), MASKED, F32), max_0)
    alpha_prev, m, _ = lax.fori_loop(0, qi, two_blocks, init)

    t = 2 * qi
    key = lax.broadcasted_iota(jnp.int32, (tk, tk), 0)
    qry = lax.broadcasted_iota(jnp.int32, (tk, tk), 1)
    causal = key <= qry
    s_right = _dot(k_blk(t + 1), qt_ref[:, tk:])
    value_update(jnp.maximum(t - 1, 0), 1, alpha_prev)
    s = s_buf[0]
    s = jnp.concatenate([jnp.where(causal, s[:, :tk], MASKED), s[:, tk:]], axis=1)
    m, alpha, p_buf[0] = softmax(s, m)
    value_update(t, 0, alpha)
    _, alpha_r, p_r = softmax(jnp.where(causal, s_right, MASKED), m[:, tk:])
    acc_ref[:, tk:] = alpha_r * acc_ref[:, tk:] + _dot(v_blk(t + 1), p_r)
    acc = acc_ref[...]
    o_ref[0] = (acc[:HEAD_DIM] / acc[HEAD_DIM:HEAD_DIM + 1]).T.astype(BF16)
    reset()


def _attn_call(q, k, vt):
    B, S, _ = q.shape
    tq, tk = ATTN_TQ, ATTN_TK
    assert tq == 2 * tk
    return pl.pallas_call(
        functools.partial(_attn_kernel, tq=tq, tk=tk),
        grid=(B, N_HEADS, S // tq),
        in_specs=[
            pl.BlockSpec((1, tq, QK_PAD), lambda b, h, i: (b, i, h)),
            pl.BlockSpec((1, S, QK_PAD), lambda b, h, i: (b, 0, h)),
            pl.BlockSpec((1, 1, HEAD_DIM, S), lambda b, h, i: (b, h, 0, 0)),
        ],
        out_specs=pl.BlockSpec((1, tq, HEAD_DIM), lambda b, h, i: (b, i, h)),
        out_shape=jax.ShapeDtypeStruct((B, S, N_HEADS * HEAD_DIM), BF16),
        scratch_shapes=[
            pltpu.VMEM((QK_PAD, tq), BF16),
            pltpu.VMEM((2, tk, tq), F32),
            pltpu.VMEM((2, tk, tq), BF16),
            pltpu.VMEM((ACC_ROWS, tq), F32),
        ],
        compiler_params=pltpu.CompilerParams(
            dimension_semantics=("arbitrary", "arbitrary", "arbitrary"), vmem_limit_bytes=VMEM_LIMIT),
        name="mla_attn",
    )(q, k, vt)


def _mlp_kernel(x_ref, ret_ref, mla_ref, mod_ref, wo32_hbm, ln1w_ref, ln1b_ref, wup32_hbm, cw_ref, cb_ref,
                wdn32_hbm, ln2w_ref, ln2b_ref, o_ref, carry_ref, ubuf_ref, act_ref,
                wo_ref, wup_ref, wdn_ref, stage_sq_ref, stage_up_ref, sem, *, tm):
    @pl.when(jnp.logical_and(pl.program_id(0) == 0, pl.program_id(1) == 0))
    def _():
        _stream_cast(wo32_hbm.at[0], stage_sq_ref, sem, _store_bf16(wo_ref))
        _stream_cast(wup32_hbm.at[0], stage_up_ref, sem, _store_bf16(wup_ref))
        _stream_cast(wdn32_hbm.at[0], stage_sq_ref, sem, _store_bf16(wdn_ref))

    @pl.when(pl.program_id(1) == 0)
    def _():
        carry_ref[...] = jnp.zeros_like(carry_ref)

    gate1 = mod_ref[0, 2:3, :]
    shift2 = mod_ref[0, 3:4, :]
    scale2 = mod_ref[0, 4:5, :]
    gate2 = mod_ref[0, 5:6, :]

    halves = (slice(0, tm // 2), slice(tm // 2, tm))

    def out_proj(rows):
        return (_dot(ret_ref[0, rows, :], wo_ref[0:RET_WIDTH, :])
                + _dot(mla_ref[0, rows, :], wo_ref[RET_WIDTH:2 * RET_WIDTH, :]))

    def norms(y, rows):
        x1 = _ln_plain(DN_ALPHA * x_ref[0, rows, :] + (1.0 + gate1) * y) * ln1w_ref[...] + ln1b_ref[...]
        return x1, (_ln_plain(x1) * (1.0 + scale2) + shift2).astype(BF16)

    ys = [out_proj(rows) for rows in halves]
    x1s, h2s = zip(*[norms(y, rows) for y, rows in zip(ys, halves)])
    h2 = jnp.concatenate(h2s, axis=0)

    def conv(u, col0, kind):
        outs = []
        for g in range(FF_CHUNK // LANES):
            cols = slice(col0 + g * LANES, col0 + (g + 1) * LANES)
            ug = u[:, g * LANES:(g + 1) * LANES]
            buf = ubuf_ref.at[kind, g]
            buf[0:SUBLANES, :] = carry_ref[:, cols]
            buf[SUBLANES:SUBLANES + tm, :] = ug
            carry_ref[:, cols] = ug[tm - SUBLANES:tm, :]
            u1 = buf[SUBLANES - 1:SUBLANES - 1 + tm, :]
            u2 = buf[SUBLANES - 2:SUBLANES - 2 + tm, :]
            outs.append(cb_ref[:, cols] + cw_ref[2:3, cols] * ug + cw_ref[1:2, cols] * u1
                        + cw_ref[0:1, cols] * u2)
        return jnp.concatenate(outs, axis=1)

    for c in range(N_FF_CHUNKS):
        gcol0 = c * FF_CHUNK
        vcol0 = D_FF + c * FF_CHUNK
        g = conv(_dot(h2, wup_ref[:, gcol0:gcol0 + FF_CHUNK]), gcol0, 0)
        val = conv(_dot(h2, wup_ref[:, vcol0:vcol0 + FF_CHUNK]), vcol0, 1)
        act_ref[:, gcol0:gcol0 + FF_CHUNK] = (_silu(g) * val).astype(BF16)

    y2s = [_dot(act_ref[rows, :], wdn_ref[...]) for rows in halves]
    for x1, y2, rows in zip(x1s, y2s, halves):
        o_ref[0, rows, :] = _ln_plain(DN_ALPHA * x1 + (1.0 + gate2) * y2) * ln2w_ref[...] + ln2b_ref[...]


def _mlp_call(x, ret, mla, mod, w_out, ln1w, ln1b, w_up, conv_w, conv_b, w_down, ln2w, ln2b):
    B, S, D = x.shape
    tm = ROW_TILE
    row = lambda w: pl.BlockSpec((1, tm, w), lambda b, s: (b, s, 0))
    hbm = pl.BlockSpec(memory_space=pl.ANY)
    return pl.pallas_call(
        functools.partial(_mlp_kernel, tm=tm),
        grid=(B, S // tm),
        in_specs=[
            row(D), row(RET_WIDTH), row(RET_WIDTH),
            pl.BlockSpec((1, N_MOD, D), lambda b, s: (b, 0, 0)),
            hbm, _const_spec(ln1w.shape), _const_spec(ln1b.shape),
            hbm, _const_spec(conv_w.shape), _const_spec(conv_b.shape),
            hbm, _const_spec(ln2w.shape), _const_spec(ln2b.shape),
        ],
        out_specs=row(D),
        out_shape=jax.ShapeDtypeStruct((B, S, D), F32),
        scratch_shapes=[
            pltpu.VMEM((SUBLANES, 2 * D_FF), F32),
            pltpu.VMEM((2, FF_CHUNK // LANES, SUBLANES + tm, LANES), F32),
            pltpu.VMEM((tm, D_FF), BF16),
            pltpu.VMEM(w_out.shape[1:], BF16),
            pltpu.VMEM(w_up.shape[1:], BF16),
            pltpu.VMEM(w_down.shape[1:], BF16),
            pltpu.VMEM((STAGE_SLOTS, STAGE_ROWS_SQ, D), F32),
            pltpu.VMEM((STAGE_SLOTS, STAGE_ROWS_UP, 2 * D_FF), F32),
            pltpu.SemaphoreType.DMA((STAGE_SLOTS,)),
        ],
        compiler_params=pltpu.CompilerParams(
            dimension_semantics=("arbitrary", "arbitrary"), vmem_limit_bytes=VMEM_LIMIT),
        name="outproj_mlp",
    )(x, ret, mla, mod, w_out, ln1w, ln1b, w_up, conv_w, conv_b, w_down, ln2w, ln2b)


def _rope_tables(seq, half, reps):
    pos = np.arange(seq, dtype=np.float32)
    inv = np.float32(ROPE_BASE) ** (-np.arange(half, dtype=np.float32) / np.float32(half))
    ang = (pos[:, None] * inv[None, :]).astype(np.float32)
    cos = np.cos(ang).astype(np.float32)
    sin = np.sin(ang).astype(np.float32)
    cos_t = np.tile(cos, (1, 2 * reps))
    sin_t = np.concatenate([np.tile(-sin, (1, reps)), np.tile(sin, (1, reps))], axis=-1)
    return jnp.asarray(cos_t), jnp.asarray(sin_t)


def _uq_columns():
    half = MLA_ROPE_DIM // 2
    nope = [h * MLA_QK_DIM + np.arange(HEAD_DIM) for h in range(N_HEADS)]
    rope = []
    for p in range(N_HEADS // 2):
        for part in range(2):
            for h in (2 * p, 2 * p + 1):
                rope.append(h * MLA_QK_DIM + HEAD_DIM + part * half + np.arange(half))
    return np.concatenate(nope + rope)


def _take_columns(w, cols):
    cols = np.asarray(cols)
    cuts = np.flatnonzero(np.diff(cols) != 1) + 1
    runs = np.split(cols, cuts)
    return jnp.concatenate([w[:, r[0]:r[-1] + 1] for r in runs], axis=1)


def kernel(x, c, w_ada, b_ada, w_in, ret_gn_w, mla_q_norm_w, w_uq, mla_kv_norm_w, w_ukv, w_out,
           ln1_w, ln1_b, w_up, conv_w, conv_b, w_down, ln2_w, ln2_b):
    B, S, D = x.shape
    assert D == D_MODEL and S % IN_ROW_TILE == 0 and S % ROW_TILE == 0 and S % ATTN_TQ == 0
    assert w_ada.shape[0] == DEPTH == 1
    l = 0

    mod = _ada_call(c, w_ada, b_ada[l][None, :]).reshape(B, N_MOD, D)

    w_uq_p = _take_columns(w_uq[l].astype(BF16), _uq_columns())
    w_ukv_p = w_ukv[l].astype(BF16)
    cosr, sinr = _rope_tables(S, HEAD_DIM // 2, 1)
    cosm, sinm = _rope_tables(S, MLA_ROPE_DIM // 2, 2)

    ret, q, k, vt = _inproj_call(
        x, mod, jnp.swapaxes(w_in, 1, 2), ret_gn_w[l][None, :], mla_q_norm_w[l][None, :], mla_kv_norm_w[l][None, :],
        w_uq_p, w_ukv_p, cosr, sinr, cosm, sinm)
    mla = _attn_call(q, k, vt)
    return _mlp_call(
        x, ret, mla, mod, w_out, ln1_w[l][None, :], ln1_b[l][None, :],
        w_up, conv_w[l], conv_b[l][None, :], w_down, ln2_w[l][None, :], ln2_b[l][None, :])
```

```python
import functools
import math

import numpy as np
import jax
import jax.numpy as jnp
from jax import lax
from jax.experimental import pallas as pl
from jax.experimental.pallas import tpu as pltpu

F32 = jnp.float32
BF16 = jnp.bfloat16

D_MODEL = 1024
DEPTH = 1
N_HEADS = 4
HEAD_DIM = 128
RET_WIDTH = N_HEADS * HEAD_DIM
LAT_COL0 = 4 * RET_WIDTH
MLA_RANK = 256
MLA_ROPE_DIM = 64
MLA_QK_DIM = HEAD_DIM + MLA_ROPE_DIM
D_FF = 2816
ROPE_BASE = 10000.0
LN_EPS = 1e-5
RMS_EPS = 1e-6
DN_ALPHA = (2.0 * DEPTH) ** 0.25
N_MOD = 6
LOG_G = [math.log1p(-(2.0 ** (-5.0 - h))) for h in range(N_HEADS)]

LANES = 128
SUBLANES = 8
MXU_DIM = 256
VMEM_BYTES = 64 * 1024 * 1024
VMEM_LIMIT = VMEM_BYTES * 7 // 8

ADA_ROWS = 256
IN_ROW_TILE = 1024
ROW_TILE = 512
RET_CHUNK = MXU_DIM
ATTN_TQ = 2048
ATTN_TK = 1024
FF_CHUNK = MXU_DIM
N_FF_CHUNKS = D_FF // FF_CHUNK
STAGE_ROWS_SQ = 256
STAGE_ROWS_UP = 64
STAGE_ROWS_IN = 128
STAGE_SLOTS = 4
QK_PAD = 2 * LANES
BF16_ROWS = 2 * SUBLANES
ACC_ROWS = HEAD_DIM + BF16_ROWS
MASKED = -float("inf")


def _ln_plain(x):
    mu = jnp.mean(x, axis=-1, keepdims=True)
    xc = x - mu
    var = jnp.mean(xc * xc, axis=-1, keepdims=True)
    return xc * lax.rsqrt(var + LN_EPS)


def _rms(x):
    return x * lax.rsqrt(jnp.mean(x * x, axis=-1, keepdims=True) + RMS_EPS)


def _silu(x):
    return x * jax.nn.sigmoid(x)


def _dot(a, b):
    return jnp.dot(a, b, preferred_element_type=F32)


def _dot_nt(a, b):
    return lax.dot_general(a, b, (((1,), (1,)), ((), ())), preferred_element_type=F32)


def _dot_tn(a, b):
    return lax.dot_general(a, b, (((0,), (0,)), ((), ())), preferred_element_type=F32)


def _ada_kernel(c_ref, w_ref, b_ref, o_ref):
    @pl.when(pl.program_id(0) == 0)
    def _():
        o_ref[...] = jnp.broadcast_to(b_ref[...], o_ref.shape)

    cond = _silu(c_ref[...])
    o_ref[...] += _dot(cond.astype(BF16), w_ref[0].astype(BF16))


def _ada_call(c, w_ada, b_ada):
    n_batch = c.shape[0]
    _, d_in, n_out = w_ada.shape
    return pl.pallas_call(
        _ada_kernel,
        grid=(d_in // ADA_ROWS,),
        in_specs=[
            pl.BlockSpec((n_batch, ADA_ROWS), lambda j: (0, j)),
            pl.BlockSpec((1, ADA_ROWS, n_out), lambda j: (0, j, 0)),
            pl.BlockSpec((1, n_out), lambda j: (0, 0)),
        ],
        out_specs=pl.BlockSpec((n_batch, n_out), lambda j: (0, 0)),
        out_shape=jax.ShapeDtypeStruct((n_batch, n_out), F32),
        compiler_params=pltpu.CompilerParams(dimension_semantics=("arbitrary",), vmem_limit_bytes=VMEM_LIMIT),
        name="ada_mod",
    )(c, w_ada, b_ada)


def _rope128(x, cos, sin_signed):
    return x * cos + pltpu.roll(x, LANES // 2, axis=1) * sin_signed


def _decay_matrix(hd, c):
    row = lax.broadcasted_iota(jnp.int32, (c, c), 0)
    col = lax.broadcasted_iota(jnp.int32, (c, c), 1)
    rel = (row - col).astype(F32)
    return jnp.where(rel >= 0.0, jnp.exp(LOG_G[hd] * jnp.maximum(rel, 0.0)), 0.0)


def _retention_head(hd, q, k, v, gate, gnw, decay_ref, state_ref, ret_ref):
    c = decay_ref.shape[1]
    lg = LOG_G[hd]
    n = lax.broadcasted_iota(jnp.int32, (c, 1), 0).astype(F32)
    q_w = jnp.exp(lg * (n + 1.0))
    k_w = jnp.exp(lg * (c - 1.0 - n))
    for i in range(q.shape[0] // c):
        rows = slice(i * c, (i + 1) * c)
        qb = q[rows].astype(BF16)
        vb = v[rows].astype(BF16)
        scores = _dot_nt(qb, k[rows].astype(BF16)) * decay_ref[hd]
        inner = _dot(scores.astype(BF16), vb)
        state = state_ref[hd]
        cross = _dot(qb, state.astype(BF16)) * q_w
        state_ref[hd] = state * math.exp(lg * c) + _dot_tn((k[rows] * k_w).astype(BF16), vb)
        o = _ln_plain(inner + cross) * gnw
        ret_ref[0, rows, hd * HEAD_DIM:(hd + 1) * HEAD_DIM] = (o * gate[rows]).astype(BF16)


def _stream_cast(src_hbm, stage_ref, sem, store, n_rows=None):
    slots, rows = stage_ref.shape[0], stage_ref.shape[1]
    n_rows = src_hbm.shape[0] if n_rows is None else n_rows
    n_chunks = n_rows // rows
    assert n_chunks * rows == n_rows and n_chunks >= slots

    def copy(k):
        slot = k % slots
        return pltpu.make_async_copy(
            src_hbm.at[pl.ds(pl.multiple_of(k * rows, rows), rows), :], stage_ref.at[slot], sem.at[slot])

    for k in range(slots - 1):
        copy(k).start()

    def body(k, _):
        @pl.when(k + slots - 1 < n_chunks)
        def _():
            copy(k + slots - 1).start()

        copy(k).wait()
        store(pl.ds(pl.multiple_of(k * rows, rows), rows), stage_ref[k % slots])
        return 0

    lax.fori_loop(0, n_chunks, body, 0)


def _store_bf16(dst_ref):
    def store(rows, chunk):
        dst_ref[rows, :] = chunk.astype(BF16)
    return store


def _stage_w_in(wt_hbm, win_ref, stage_ref, sem):
    base = LAT_COL0 + 2 * MLA_RANK
    quarter = LANES // 4

    def store(cols, chunk):
        win_ref[:, cols] = chunk.T.astype(BF16)

    _stream_cast(wt_hbm, stage_ref, sem, store, n_rows=base)

    tail_copy = pltpu.make_async_copy(
        wt_hbm.at[pl.ds(base, MLA_ROPE_DIM), :], stage_ref.at[0, pl.ds(0, MLA_ROPE_DIM), :], sem.at[0])
    tail_copy.start()
    tail_copy.wait()
    tail = stage_ref[0, 0:MLA_ROPE_DIM, :].T
    t2 = jnp.concatenate([tail, tail], axis=1)
    lane = lax.broadcasted_iota(jnp.int32, t2.shape, 1)
    middle = jnp.abs(2 * lane - (LANES - 1)) < 2 * quarter
    win_ref[:, base:base + LANES] = jnp.where(middle, pltpu.roll(t2, quarter, axis=1), t2).astype(BF16)


def _inproj_kernel(x_ref, mod_ref, wint_hbm, gnw_ref, qnw_ref, kvnw_ref, wuq_ref, wukv_ref,
                   cosr_ref, sinr_ref, cosm_ref, sinm_ref,
                   ret_ref, q_ref, k_ref, vt_ref, state_ref, decay_ref, win_ref, stage_ref, sem,
                   *, q_scale, rk_scale):
    @pl.when(jnp.logical_and(pl.program_id(0) == 0, pl.program_id(1) == 0))
    def _():
        _stage_w_in(wint_hbm.at[0], win_ref, stage_ref, sem)
        for hd in range(N_HEADS):
            decay_ref[hd] = _decay_matrix(hd, decay_ref.shape[1])

    @pl.when(pl.program_id(1) == 0)
    def _():
        state_ref[...] = jnp.zeros_like(state_ref)

    x = x_ref[0]
    shift = mod_ref[0, 0:1, :]
    scale = mod_ref[0, 1:2, :]
    h = (_ln_plain(x) * (1.0 + scale) + shift).astype(BF16)

    cosr = cosr_ref[...]
    sinr = sinr_ref[...]
    cosm = cosm_ref[...]
    sinm = sinm_ref[...]

    pq = _dot(h, win_ref[:, 0:RET_WIDTH])
    pk = _dot(h, win_ref[:, RET_WIDTH:2 * RET_WIDTH])
    pv = _dot(h, win_ref[:, 2 * RET_WIDTH:3 * RET_WIDTH])
    pg = _silu(_dot(h, win_ref[:, 3 * RET_WIDTH:LAT_COL0]))
    for hd in range(N_HEADS):
        sl = slice(hd * HEAD_DIM, (hd + 1) * HEAD_DIM)
        _retention_head(hd, _rope128(pq[:, sl], cosr, sinr), _rope128(pk[:, sl], cosr, sinr) * rk_scale,
                        pv[:, sl], pg[:, sl], gnw_ref[:, sl], decay_ref, state_ref, ret_ref)

    lat = _dot(h, win_ref[:, LAT_COL0:])
    cq = (_rms(lat[:, 0:MLA_RANK]) * qnw_ref[...]).astype(BF16)
    ckv = (_rms(lat[:, MLA_RANK:2 * MLA_RANK]) * kvnw_ref[...]).astype(BF16)
    kr = _rope128(lat[:, 2 * MLA_RANK:2 * MLA_RANK + LANES], cosm, sinm)
    lane = lax.broadcasted_iota(jnp.int32, kr.shape, 1)
    first_of_pair = (lane % (LANES // 2)) < (LANES // 4)
    kr_even = jnp.where(first_of_pair, kr, 0.0).astype(BF16)
    kr_odd = jnp.where(first_of_pair, 0.0, kr).astype(BF16)

    qf = _dot(cq, wuq_ref[...])
    kvf = _dot(ckv, wukv_ref[...])
    for hd in range(N_HEADS):
        vt_ref[0, hd] = kvf[:, (2 * hd + 1) * HEAD_DIM:(2 * hd + 2) * HEAD_DIM].T.astype(BF16)
    for p in range(N_HEADS // 2):
        qr = qf[:, RET_WIDTH + p * LANES:RET_WIDTH + (p + 1) * LANES]
        qr = (_rope128(qr, cosm, sinm) * q_scale).astype(BF16)
        for hd in (2 * p, 2 * p + 1):
            sl = slice(hd * HEAD_DIM, (hd + 1) * HEAD_DIM)
            q_ref[0, :, hd * QK_PAD:hd * QK_PAD + LANES] = (qf[:, sl] * q_scale).astype(BF16)
            q_ref[0, :, hd * QK_PAD + LANES:(hd + 1) * QK_PAD] = qr
            k_ref[0, :, hd * QK_PAD:hd * QK_PAD + LANES] = kvf[:, 2 * hd * HEAD_DIM:(2 * hd + 1) * HEAD_DIM].astype(BF16)
            k_ref[0, :, hd * QK_PAD + LANES:(hd + 1) * QK_PAD] = kr_even if hd % 2 == 0 else kr_odd


def _const_spec(shape):
    nd = len(shape)
    return pl.BlockSpec(shape, lambda *_: (0,) * nd, pipeline_mode=pl.Buffered(1))


def _inproj_call(x, mod, w_in, gnw, qnw, kvnw, w_uq, w_ukv, cosr, sinr, cosm, sinm):
    B, S, D = x.shape
    tm = IN_ROW_TILE
    q_scale = (MLA_QK_DIM ** -0.5) * math.log2(math.e)
    rk_scale = HEAD_DIM ** -0.5
    row = lambda w: pl.BlockSpec((1, tm, w), lambda b, s: (b, s, 0))
    tab = pl.BlockSpec((tm, LANES), lambda b, s: (s, 0))
    out_shapes = (
        [jax.ShapeDtypeStruct((B, S, RET_WIDTH), BF16)]
        + [jax.ShapeDtypeStruct((B, S, N_HEADS * QK_PAD), BF16)] * 2
        + [jax.ShapeDtypeStruct((B, N_HEADS, HEAD_DIM, S), BF16)]
    )
    vt_spec = pl.BlockSpec((1, N_HEADS, HEAD_DIM, tm), lambda b, s: (b, 0, 0, s))
    return pl.pallas_call(
        functools.partial(_inproj_kernel, q_scale=q_scale, rk_scale=rk_scale),
        grid=(B, S // tm),
        in_specs=[
            row(D),
            pl.BlockSpec((1, N_MOD, D), lambda b, s: (b, 0, 0)),
            pl.BlockSpec(memory_space=pl.ANY), _const_spec(gnw.shape),
            _const_spec(qnw.shape), _const_spec(kvnw.shape),
            _const_spec(w_uq.shape), _const_spec(w_ukv.shape),
            tab, tab, tab, tab,
        ],
        out_specs=[row(RET_WIDTH)] + [row(N_HEADS * QK_PAD)] * 2 + [vt_spec],
        out_shape=out_shapes,
        scratch_shapes=[
            pltpu.VMEM((N_HEADS, HEAD_DIM, HEAD_DIM), F32),
            pltpu.VMEM((N_HEADS, RET_CHUNK, RET_CHUNK), F32),
            pltpu.VMEM((D, LAT_COL0 + 2 * MLA_RANK + LANES), BF16),
            pltpu.VMEM((STAGE_SLOTS, STAGE_ROWS_IN, w_in.shape[-1]), F32),
            pltpu.SemaphoreType.DMA((STAGE_SLOTS,)),
        ],
        compiler_params=pltpu.CompilerParams(
            dimension_semantics=("arbitrary", "arbitrary"), vmem_limit_bytes=VMEM_LIMIT),
        name="inproj_retention",
    )(x, mod, w_in, gnw, qnw, kvnw, w_uq, w_ukv, cosr, sinr, cosm, sinm)


def _attn_kernel(q_ref, k_ref, vt_ref, o_ref, qt_ref, s_buf, p_buf, acc_ref, *, tq, tk):
    qi = pl.program_id(2)
    qt_ref[...] = q_ref[0].T

    ones = jnp.ones((BF16_ROWS, tk), BF16)

    def k_blk(j):
        return k_ref[0, pl.ds(pl.multiple_of(j * tk, tk), tk), :]

    def v_blk(j):
        return jnp.concatenate([vt_ref[0, 0, :, pl.ds(pl.multiple_of(j * tk, tk), tk)], ones], axis=0)

    def scores(j, slot):
        s = _dot(k_blk(j), qt_ref[...])
        s_buf[slot] = s
        return jnp.max(s, axis=0, keepdims=True)

    def value_update(j, slot, alpha):
        acc_ref[...] = alpha * acc_ref[...] + _dot(v_blk(j), p_buf[slot])

    def softmax(s, m, block_max=None):
        if block_max is None:
            block_max = jnp.max(s, axis=0, keepdims=True)
        m_new = jnp.maximum(m, block_max)
        return m_new, jnp.exp2(m - m_new), jnp.exp2(s - m_new).astype(BF16)

    def two_blocks(u, carry):
        alpha_prev, m, max_even = carry
        t = 2 * u
        max_odd = scores(t + 1, 1)
        value_update(t - 1, 1, alpha_prev)
        m, alpha_even, p_buf[0] = softmax(s_buf[0], m, max_even)
        max_next = scores(t + 2, 0)
        value_update(t, 0, alpha_even)
        m, alpha_odd, p_buf[1] = softmax(s_buf[1], m, max_odd)
        return alpha_odd, m, max_next

    def first_two_blocks(carry):
        _, m, max_even = carry
        max_odd = scores(1, 1)
        m, alpha_even, p_buf[0] = softmax(s_buf[0], m, max_even)
        max_next = scores(2, 0)
        value_update(0, 0, alpha_even)
        m, alpha_odd, p_buf[1] = softmax(s_buf[1], m, max_odd)
        return alpha_odd, m, max_next

    max_0 = scores(0, 0)
    acc_ref[...] = jnp.zeros_like(acc_ref)
    init = (jnp.ones((1, tq), F32), jnp.full((1, tq), MASKED, F32), max_0)
    carry = lax.cond(qi > 0, first_two_blocks, lambda c: c, init)
    alpha_prev, m, _ = lax.fori_loop(1, jnp.maximum(qi, 1), two_blocks, carry)

    def diagonal_tile(pending):
        t = 2 * qi
        key = lax.broadcasted_iota(jnp.int32, (tk, tk), 0)
        qry = lax.broadcasted_iota(jnp.int32, (tk, tk), 1)
        causal = key <= qry
        s_right = _dot(k_blk(t + 1), qt_ref[:, tk:])
        if pending:
            value_update(t - 1, 1, alpha_prev)
        s = s_buf[0]
        s = jnp.concatenate([jnp.where(causal, s[:, :tk], MASKED), s[:, tk:]], axis=1)
        m_t, alpha, p_buf[0] = softmax(s, m)
        value_update(t, 0, alpha)
        _, alpha_r, p_r = softmax(jnp.where(causal, s_right, MASKED), m_t[:, tk:])
        acc_ref[:, tk:] = alpha_r * acc_ref[:, tk:] + _dot(v_blk(t + 1), p_r)
        acc = acc_ref[...]
        o_ref[0] = (acc[:HEAD_DIM] / acc[HEAD_DIM:HEAD_DIM + 1]).T.astype(BF16)

    @pl.when(qi == 0)
    def _():
        diagonal_tile(pending=False)

    @pl.when(qi > 0)
    def _():
        diagonal_tile(pending=True)


def _attn_call(q, k, vt):
    B, S, _ = q.shape
    tq, tk = ATTN_TQ, ATTN_TK
    assert tq == 2 * tk
    return pl.pallas_call(
        functools.partial(_attn_kernel, tq=tq, tk=tk),
        grid=(B, N_HEADS, S // tq),
        in_specs=[
            pl.BlockSpec((1, tq, QK_PAD), lambda b, h, i: (b, i, h)),
            pl.BlockSpec((1, S, QK_PAD), lambda b, h, i: (b, 0, h)),
            pl.BlockSpec((1, 1, HEAD_DIM, S), lambda b, h, i: (b, h, 0, 0)),
        ],
        out_specs=pl.BlockSpec((1, tq, HEAD_DIM), lambda b, h, i: (b, i, h)),
        out_shape=jax.ShapeDtypeStruct((B, S, N_HEADS * HEAD_DIM), BF16),
        scratch_shapes=[
            pltpu.VMEM((QK_PAD, tq), BF16),
            pltpu.VMEM((2, tk, tq), F32),
            pltpu.VMEM((2, tk, tq), BF16),
            pltpu.VMEM((ACC_ROWS, tq), F32),
        ],
        compiler_params=pltpu.CompilerParams(
            dimension_semantics=("arbitrary", "arbitrary", "arbitrary"), vmem_limit_bytes=VMEM_LIMIT),
        name="mla_attn",
    )(q, k, vt)


def _mlp_kernel(x_ref, ret_ref, mla_ref, mod_ref, wo32_hbm, ln1w_ref, ln1b_ref, wup32_hbm, cw_ref, cb_ref,
                wdn32_hbm, ln2w_ref, ln2b_ref, o_ref, carry_ref, ubuf_ref, act_ref,
                wo_ref, wup_ref, wdn_ref, stage_sq_ref, stage_up_ref, sem, *, tm):
    @pl.when(jnp.logical_and(pl.program_id(0) == 0, pl.program_id(1) == 0))
    def _():
        _stream_cast(wo32_hbm.at[0], stage_sq_ref, sem, _store_bf16(wo_ref))
        _stream_cast(wup32_hbm.at[0], stage_up_ref, sem, _store_bf16(wup_ref))
        _stream_cast(wdn32_hbm.at[0], stage_sq_ref, sem, _store_bf16(wdn_ref))

    @pl.when(pl.program_id(1) == 0)
    def _():
        carry_ref[...] = jnp.zeros_like(carry_ref)

    gate1 = mod_ref[0, 2:3, :]
    shift2 = mod_ref[0, 3:4, :]
    scale2 = mod_ref[0, 4:5, :]
    gate2 = mod_ref[0, 5:6, :]

    halves = (slice(0, tm // 2), slice(tm // 2, tm))

    def out_proj(rows):
        return (_dot(ret_ref[0, rows, :], wo_ref[0:RET_WIDTH, :])
                + _dot(mla_ref[0, rows, :], wo_ref[RET_WIDTH:2 * RET_WIDTH, :]))

    def norms(y, rows):
        x1 = _ln_plain(DN_ALPHA * x_ref[0, rows, :] + (1.0 + gate1) * y) * ln1w_ref[...] + ln1b_ref[...]
        return x1, (_ln_plain(x1) * (1.0 + scale2) + shift2).astype(BF16)

    ys = [out_proj(rows) for rows in halves]
    x1s, h2s = zip(*[norms(y, rows) for y, rows in zip(ys, halves)])
    h2 = jnp.concatenate(h2s, axis=0)

    def conv(u, col0, kind):
        outs = []
        for g in range(FF_CHUNK // LANES):
            cols = slice(col0 + g * LANES, col0 + (g + 1) * LANES)
            ug = u[:, g * LANES:(g + 1) * LANES]
            buf = ubuf_ref.at[kind, g]
            buf[0:SUBLANES, :] = carry_ref[:, cols]
            buf[SUBLANES:SUBLANES + tm, :] = ug
            carry_ref[:, cols] = ug[tm - SUBLANES:tm, :]
            u1 = buf[SUBLANES - 1:SUBLANES - 1 + tm, :]
            u2 = buf[SUBLANES - 2:SUBLANES - 2 + tm, :]
            outs.append(cb_ref[:, cols] + cw_ref[2:3, cols] * ug + cw_ref[1:2, cols] * u1
                        + cw_ref[0:1, cols] * u2)
        return jnp.concatenate(outs, axis=1)

    for c in range(N_FF_CHUNKS):
        gcol0 = c * FF_CHUNK
        vcol0 = D_FF + c * FF_CHUNK
        g = conv(_dot(h2, wup_ref[:, gcol0:gcol0 + FF_CHUNK]), gcol0, 0)
        val = conv(_dot(h2, wup_ref[:, vcol0:vcol0 + FF_CHUNK]), vcol0, 1)
        act_ref[:, gcol0:gcol0 + FF_CHUNK] = (_silu(g) * val).astype(BF16)

    y2s = [_dot(act_ref[rows, :], wdn_ref[...]) for rows in halves]
    for x1, y2, rows in zip(x1s, y2s, halves):
        o_ref[0, rows, :] = _ln_plain(DN_ALPHA * x1 + (1.0 + gate2) * y2) * ln2w_ref[...] + ln2b_ref[...]


def _mlp_call(x, ret, mla, mod, w_out, ln1w, ln1b, w_up, conv_w, conv_b, w_down, ln2w, ln2b):
    B, S, D = x.shape
    tm = ROW_TILE
    row = lambda w: pl.BlockSpec((1, tm, w), lambda b, s: (b, s, 0))
    hbm = pl.BlockSpec(memory_space=pl.ANY)
    return pl.pallas_call(
        functools.partial(_mlp_kernel, tm=tm),
        grid=(B, S // tm),
        in_specs=[
            row(D), row(RET_WIDTH), row(RET_WIDTH),
            pl.BlockSpec((1, N_MOD, D), lambda b, s: (b, 0, 0)),
            hbm, _const_spec(ln1w.shape), _const_spec(ln1b.shape),
            hbm, _const_spec(conv_w.shape), _const_spec(conv_b.shape),
            hbm, _const_spec(ln2w.shape), _const_spec(ln2b.shape),
        ],
        out_specs=row(D),
        out_shape=jax.ShapeDtypeStruct((B, S, D), F32),
        scratch_shapes=[
            pltpu.VMEM((SUBLANES, 2 * D_FF), F32),
            pltpu.VMEM((2, FF_CHUNK // LANES, SUBLANES + tm, LANES), F32),
            pltpu.VMEM((tm, D_FF), BF16),
            pltpu.VMEM(w_out.shape[1:], BF16),
            pltpu.VMEM(w_up.shape[1:], BF16),
            pltpu.VMEM(w_down.shape[1:], BF16),
            pltpu.VMEM((STAGE_SLOTS, STAGE_ROWS_SQ, D), F32),
            pltpu.VMEM((STAGE_SLOTS, STAGE_ROWS_UP, 2 * D_FF), F32),
            pltpu.SemaphoreType.DMA((STAGE_SLOTS,)),
        ],
        compiler_params=pltpu.CompilerParams(
            dimension_semantics=("arbitrary", "arbitrary"), vmem_limit_bytes=VMEM_LIMIT),
        name="outproj_mlp",
    )(x, ret, mla, mod, w_out, ln1w, ln1b, w_up, conv_w, conv_b, w_down, ln2w, ln2b)


def _rope_tables(seq, half, reps):
    pos = np.arange(seq, dtype=np.float32)
    inv = np.float32(ROPE_BASE) ** (-np.arange(half, dtype=np.float32) / np.float32(half))
    ang = (pos[:, None] * inv[None, :]).astype(np.float32)
    cos = np.cos(ang).astype(np.float32)
    sin = np.sin(ang).astype(np.float32)
    cos_t = np.tile(cos, (1, 2 * reps))
    sin_t = np.concatenate([np.tile(-sin, (1, reps)), np.tile(sin, (1, reps))], axis=-1)
    return jnp.asarray(cos_t), jnp.asarray(sin_t)


def _uq_columns():
    half = MLA_ROPE_DIM // 2
    nope = [h * MLA_QK_DIM + np.arange(HEAD_DIM) for h in range(N_HEADS)]
    rope = []
    for p in range(N_HEADS // 2):
        for part in range(2):
            for h in (2 * p, 2 * p + 1):
                rope.append(h * MLA_QK_DIM + HEAD_DIM + part * half + np.arange(half))
    return np.concatenate(nope + rope)


def _take_columns(w, cols):
    cols = np.asarray(cols)
    cuts = np.flatnonzero(np.diff(cols) != 1) + 1
    runs = np.split(cols, cuts)
    return jnp.concatenate([w[:, r[0]:r[-1] + 1] for r in runs], axis=1)


def kernel(x, c, w_ada, b_ada, w_in, ret_gn_w, mla_q_norm_w, w_uq, mla_kv_norm_w, w_ukv, w_out,
           ln1_w, ln1_b, w_up, conv_w, conv_b, w_down, ln2_w, ln2_b):
    B, S, D = x.shape
    assert D == D_MODEL and S % IN_ROW_TILE == 0 and S % ROW_TILE == 0 and S % ATTN_TQ == 0
    assert w_ada.shape[0] == DEPTH == 1
    l = 0

    mod = _ada_call(c, w_ada, b_ada[l][None, :]).reshape(B, N_MOD, D)

    w_uq_p = _take_columns(w_uq[l].astype(BF16), _uq_columns())
    w_ukv_p = w_ukv[l].astype(BF16)
    cosr, sinr = _rope_tables(S, HEAD_DIM // 2, 1)
    cosm, sinm = _rope_tables(S, MLA_ROPE_DIM // 2, 2)

    ret, q, k, vt = _inproj_call(
        x, mod, jnp.swapaxes(w_in, 1, 2), ret_gn_w[l][None, :], mla_q_norm_w[l][None, :], mla_kv_norm_w[l][None, :],
        w_uq_p, w_ukv_p, cosr, sinr, cosm, sinm)
    mla = _attn_call(q, k, vt)
    return _mlp_call(
        x, ret, mla, mod, w_out, ln1_w[l][None, :], ln1_b[l][None, :],
        w_up, conv_w[l], conv_b[l][None, :], w_down, ln2_w[l][None, :], ln2_b[l][None, :])
```

```python
import functools
import math

import numpy as np
import jax
import jax.numpy as jnp
from jax import lax
from jax.experimental import pallas as pl
from jax.experimental.pallas import tpu as pltpu

F32 = jnp.float32
BF16 = jnp.bfloat16

D_MODEL = 1024
DEPTH = 1
N_HEADS = 4
HEAD_DIM = 128
RET_WIDTH = N_HEADS * HEAD_DIM
LAT_COL0 = 4 * RET_WIDTH
MLA_RANK = 256
MLA_ROPE_DIM = 64
MLA_QK_DIM = HEAD_DIM + MLA_ROPE_DIM
D_FF = 2816
ROPE_BASE = 10000.0
LN_EPS = 1e-5
RMS_EPS = 1e-6
DN_ALPHA = (2.0 * DEPTH) ** 0.25
N_MOD = 6
LOG_G = [math.log1p(-(2.0 ** (-5.0 - h))) for h in range(N_HEADS)]

LANES = 128
SUBLANES = 8
MXU_DIM = 256
VMEM_BYTES = 64 * 1024 * 1024
VMEM_LIMIT = VMEM_BYTES * 7 // 8

ADA_ROWS = 256
IN_ROW_TILE = 1024
ROW_TILE = 512
RET_CHUNK = MXU_DIM
ATTN_TQ = 2048
ATTN_TK = 1024
FF_CHUNK = MXU_DIM
N_FF_CHUNKS = D_FF // FF_CHUNK
STAGE_ROWS_SQ = 256
STAGE_ROWS_UP = 64
STAGE_ROWS_IN = 128
STAGE_SLOTS = 4
QK_PAD = 2 * LANES
BF16_ROWS = 2 * SUBLANES
ACC_ROWS = HEAD_DIM + BF16_ROWS
MASKED = -float("inf")


def _ln_plain(x):
    mu = jnp.mean(x, axis=-1, keepdims=True)
    xc = x - mu
    var = jnp.mean(xc * xc, axis=-1, keepdims=True)
    return xc * lax.rsqrt(var + LN_EPS)


def _rms(x):
    return x * lax.rsqrt(jnp.mean(x * x, axis=-1, keepdims=True) + RMS_EPS)


def _silu(x):
    return x * jax.nn.sigmoid(x)


def _dot(a, b):
    return jnp.dot(a, b, preferred_element_type=F32)


def _dot_nt(a, b):
    return lax.dot_general(a, b, (((1,), (1,)), ((), ())), preferred_element_type=F32)


def _dot_tn(a, b):
    return lax.dot_general(a, b, (((0,), (0,)), ((), ())), preferred_element_type=F32)


def _ada_kernel(c_ref, w_ref, b_ref, o_ref):
    @pl.when(pl.program_id(0) == 0)
    def _():
        o_ref[...] = jnp.broadcast_to(b_ref[...], o_ref.shape)

    cond = _silu(c_ref[...])
    o_ref[...] += _dot(cond.astype(BF16), w_ref[0].astype(BF16))


def _ada_call(c, w_ada, b_ada):
    n_batch = c.shape[0]
    _, d_in, n_out = w_ada.shape
    return pl.pallas_call(
        _ada_kernel,
        grid=(d_in // ADA_ROWS,),
        in_specs=[
            pl.BlockSpec((n_batch, ADA_ROWS), lambda j: (0, j)),
            pl.BlockSpec((1, ADA_ROWS, n_out), lambda j: (0, j, 0)),
            pl.BlockSpec((1, n_out), lambda j: (0, 0)),
        ],
        out_specs=pl.BlockSpec((n_batch, n_out), lambda j: (0, 0)),
        out_shape=jax.ShapeDtypeStruct((n_batch, n_out), F32),
        compiler_params=pltpu.CompilerParams(dimension_semantics=("arbitrary",), vmem_limit_bytes=VMEM_LIMIT),
        name="ada_mod",
    )(c, w_ada, b_ada)


def _rope128(x, cos, sin_signed):
    return x * cos + pltpu.roll(x, LANES // 2, axis=1) * sin_signed


def _decay_matrix(hd, c):
    row = lax.broadcasted_iota(jnp.int32, (c, c), 0)
    col = lax.broadcasted_iota(jnp.int32, (c, c), 1)
    rel = (row - col).astype(F32)
    return jnp.where(rel >= 0.0, jnp.exp(LOG_G[hd] * jnp.maximum(rel, 0.0)), 0.0)


def _retention_head(hd, q, k, v, gate, gnw, decay_ref, state_ref, ret_ref):
    c = decay_ref.shape[1]
    lg = LOG_G[hd]
    n = lax.broadcasted_iota(jnp.int32, (c, 1), 0).astype(F32)
    q_w = jnp.exp(lg * (n + 1.0))
    k_w = jnp.exp(lg * (c - 1.0 - n))
    for i in range(q.shape[0] // c):
        rows = slice(i * c, (i + 1) * c)
        qb = q[rows].astype(BF16)
        vb = v[rows].astype(BF16)
        scores = _dot_nt(qb, k[rows].astype(BF16)) * decay_ref[hd]
        inner = _dot(scores.astype(BF16), vb)
        state = state_ref[hd]
        cross = _dot(qb, state.astype(BF16)) * q_w
        state_ref[hd] = state * math.exp(lg * c) + _dot_tn((k[rows] * k_w).astype(BF16), vb)
        o = _ln_plain(inner + cross) * gnw
        ret_ref[0, rows, hd * HEAD_DIM:(hd + 1) * HEAD_DIM] = (o * gate[rows]).astype(BF16)


def _stream_cast(src_hbm, stage_ref, sem, store, n_rows=None):
    slots, rows = stage_ref.shape[0], stage_ref.shape[1]
    n_rows = src_hbm.shape[0] if n_rows is None else n_rows
    n_chunks = n_rows // rows
    assert n_chunks * rows == n_rows and n_chunks >= slots

    def copy(k):
        slot = k % slots
        return pltpu.make_async_copy(
            src_hbm.at[pl.ds(pl.multiple_of(k * rows, rows), rows), :], stage_ref.at[slot], sem.at[slot])

    for k in range(slots - 1):
        copy(k).start()

    def body(k, _):
        @pl.when(k + slots - 1 < n_chunks)
        def _():
            copy(k + slots - 1).start()

        copy(k).wait()
        store(pl.ds(pl.multiple_of(k * rows, rows), rows), stage_ref[k % slots])
        return 0

    lax.fori_loop(0, n_chunks, body, 0)


def _store_bf16(dst_ref):
    def store(rows, chunk):
        dst_ref[rows, :] = chunk.astype(BF16)
    return store


def _stage_w_in(wt_hbm, win_ref, stage_ref, sem):
    base = LAT_COL0 + 2 * MLA_RANK
    quarter = LANES // 4

    def store(cols, chunk):
        win_ref[:, cols] = chunk.T.astype(BF16)

    _stream_cast(wt_hbm, stage_ref, sem, store, n_rows=base)

    tail_copy = pltpu.make_async_copy(
        wt_hbm.at[pl.ds(base, MLA_ROPE_DIM), :], stage_ref.at[0, pl.ds(0, MLA_ROPE_DIM), :], sem.at[0])
    tail_copy.start()
    tail_copy.wait()
    tail = stage_ref[0, 0:MLA_ROPE_DIM, :].T
    t2 = jnp.concatenate([tail, tail], axis=1)
    lane = lax.broadcasted_iota(jnp.int32, t2.shape, 1)
    middle = jnp.abs(2 * lane - (LANES - 1)) < 2 * quarter
    win_ref[:, base:base + LANES] = jnp.where(middle, pltpu.roll(t2, quarter, axis=1), t2).astype(BF16)


def _inproj_kernel(x_ref, mod_ref, wint_hbm, gnw_ref, qnw_ref, kvnw_ref, wuq_ref, wukv_ref,
                   cosr_ref, sinr_ref, cosm_ref, sinm_ref,
                   ret_ref, q_ref, k_ref, vt_ref, state_ref, decay_ref, win_ref, stage_ref, sem,
                   *, q_scale, rk_scale):
    @pl.when(jnp.logical_and(pl.program_id(0) == 0, pl.program_id(1) == 0))
    def _():
        _stage_w_in(wint_hbm.at[0], win_ref, stage_ref, sem)
        for hd in range(N_HEADS):
            decay_ref[hd] = _decay_matrix(hd, decay_ref.shape[1])

    @pl.when(pl.program_id(1) == 0)
    def _():
        state_ref[...] = jnp.zeros_like(state_ref)

    x = x_ref[0]
    shift = mod_ref[0, 0:1, :]
    scale = mod_ref[0, 1:2, :]
    h = (_ln_plain(x) * (1.0 + scale) + shift).astype(BF16)

    cosr = cosr_ref[...]
    sinr = sinr_ref[...]
    cosm = cosm_ref[...]
    sinm = sinm_ref[...]

    pq = _dot(h, win_ref[:, 0:RET_WIDTH])
    pk = _dot(h, win_ref[:, RET_WIDTH:2 * RET_WIDTH])
    pv = _dot(h, win_ref[:, 2 * RET_WIDTH:3 * RET_WIDTH])
    pg = _silu(_dot(h, win_ref[:, 3 * RET_WIDTH:LAT_COL0]))
    for hd in range(N_HEADS):
        sl = slice(hd * HEAD_DIM, (hd + 1) * HEAD_DIM)
        _retention_head(hd, _rope128(pq[:, sl], cosr, sinr), _rope128(pk[:, sl], cosr, sinr) * rk_scale,
                        pv[:, sl], pg[:, sl], gnw_ref[:, sl], decay_ref, state_ref, ret_ref)

    lat = _dot(h, win_ref[:, LAT_COL0:])
    cq = (_rms(lat[:, 0:MLA_RANK]) * qnw_ref[...]).astype(BF16)
    ckv = (_rms(lat[:, MLA_RANK:2 * MLA_RANK]) * kvnw_ref[...]).astype(BF16)
    kr = _rope128(lat[:, 2 * MLA_RANK:2 * MLA_RANK + LANES], cosm, sinm)
    lane = lax.broadcasted_iota(jnp.int32, kr.shape, 1)
    first_of_pair = (lane % (LANES // 2)) < (LANES // 4)
    kr_even = jnp.where(first_of_pair, kr, 0.0).astype(BF16)
    kr_odd = jnp.where(first_of_pair, 0.0, kr).astype(BF16)

    qf = _dot(cq, wuq_ref[...])
    kvf = _dot(ckv, wukv_ref[...])
    for hd in range(N_HEADS):
        vt_ref[0, hd] = kvf[:, (2 * hd + 1) * HEAD_DIM:(2 * hd + 2) * HEAD_DIM].T.astype(BF16)
    for p in range(N_HEADS // 2):
        qr = qf[:, RET_WIDTH + p * LANES:RET_WIDTH + (p + 1) * LANES]
        qr = (_rope128(qr, cosm, sinm) * q_scale).astype(BF16)
        for hd in (2 * p, 2 * p + 1):
            sl = slice(hd * HEAD_DIM, (hd + 1) * HEAD_DIM)
            q_ref[0, :, hd * QK_PAD:hd * QK_PAD + LANES] = (qf[:, sl] * q_scale).astype(BF16)
            q_ref[0, :, hd * QK_PAD + LANES:(hd + 1) * QK_PAD] = qr
            k_ref[0, :, hd * QK_PAD:hd * QK_PAD + LANES] = kvf[:, 2 * hd * HEAD_DIM:(2 * hd + 1) * HEAD_DIM].astype(BF16)
            k_ref[0, :, hd * QK_PAD + LANES:(hd + 1) * QK_PAD] = kr_even if hd % 2 == 0 else kr_odd


def _const_spec(shape):
    nd = len(shape)
    return pl.BlockSpec(shape, lambda *_: (0,) * nd, pipeline_mode=pl.Buffered(1))


def _inproj_call(x, mod, w_in, gnw, qnw, kvnw, w_uq, w_ukv, cosr, sinr, cosm, sinm):
    B, S, D = x.shape
    tm = IN_ROW_TILE
    q_scale = (MLA_QK_DIM ** -0.5) * math.log2(math.e)
    rk_scale = HEAD_DIM ** -0.5
    row = lambda w: pl.BlockSpec((1, tm, w), lambda b, s: (b, s, 0))
    tab = pl.BlockSpec((tm, LANES), lambda b, s: (s, 0))
    out_shapes = (
        [jax.ShapeDtypeStruct((B, S, RET_WIDTH), BF16)]
        + [jax.ShapeDtypeStruct((B, S, N_HEADS * QK_PAD), BF16)] * 2
        + [jax.ShapeDtypeStruct((B, N_HEADS, HEAD_DIM, S), BF16)]
    )
    vt_spec = pl.BlockSpec((1, N_HEADS, HEAD_DIM, tm), lambda b, s: (b, 0, 0, s))
    return pl.pallas_call(
        functools.partial(_inproj_kernel, q_scale=q_scale, rk_scale=rk_scale),
        grid=(B, S // tm),
        in_specs=[
            row(D),
            pl.BlockSpec((1, N_MOD, D), lambda b, s: (b, 0, 0)),
            pl.BlockSpec(memory_space=pl.ANY), _const_spec(gnw.shape),
            _const_spec(qnw.shape), _const_spec(kvnw.shape),
            _const_spec(w_uq.shape), _const_spec(w_ukv.shape),
            tab, tab, tab, tab,
        ],
        out_specs=[row(RET_WIDTH)] + [row(N_HEADS * QK_PAD)] * 2 + [vt_spec],
        out_shape=out_shapes,
        scratch_shapes=[
            pltpu.VMEM((N_HEADS, HEAD_DIM, HEAD_DIM), F32),
            pltpu.VMEM((N_HEADS, RET_CHUNK, RET_CHUNK), F32),
            pltpu.VMEM((D, LAT_COL0 + 2 * MLA_RANK + LANES), BF16),
            pltpu.VMEM((STAGE_SLOTS, STAGE_ROWS_IN, w_in.shape[-1]), F32),
            pltpu.SemaphoreType.DMA((STAGE_SLOTS,)),
        ],
        compiler_params=pltpu.CompilerParams(
            dimension_semantics=("arbitrary", "arbitrary"), vmem_limit_bytes=VMEM_LIMIT),
        name="inproj_retention",
    )(x, mod, w_in, gnw, qnw, kvnw, w_uq, w_ukv, cosr, sinr, cosm, sinm)


def _attn_kernel(q_ref, k_ref, vt_ref, o_ref, qt_ref, s_buf, p_buf, acc_ref, *, tq, tk, n_q):
    qi = pl.program_id(2)
    qt_ref[...] = q_ref[0].T

    ones = jnp.ones((BF16_ROWS, tk), BF16)

    def k_blk(j):
        return k_ref[0, j * tk:(j + 1) * tk, :]

    def v_blk(j):
        return jnp.concatenate([vt_ref[0, 0, :, j * tk:(j + 1) * tk], ones], axis=0)

    def scores(j, slot):
        s = _dot(k_blk(j), qt_ref[...])
        s_buf[slot] = s
        return jnp.max(s, axis=0, keepdims=True)

    def value_update(j, slot, alpha):
        acc_ref[...] = alpha * acc_ref[...] + _dot(v_blk(j), p_buf[slot])

    def softmax(s, m, block_max=None):
        if block_max is None:
            block_max = jnp.max(s, axis=0, keepdims=True)
        m_new = jnp.maximum(m, block_max)
        return m_new, jnp.exp2(m - m_new), jnp.exp2(s - m_new).astype(BF16)

    def two_blocks(u, carry):
        alpha_prev, m, max_even = carry
        t = 2 * u
        max_odd = scores(t + 1, 1)
        value_update(max(t - 1, 0), 1, alpha_prev)
        m, alpha_even, p_buf[0] = softmax(s_buf[0], m, max_even)
        max_next = scores(t + 2, 0)
        value_update(t, 0, alpha_even)
        m, alpha_odd, p_buf[1] = softmax(s_buf[1], m, max_odd)
        return alpha_odd, m, max_next

    def q_tile(n_pairs):
        max_0 = scores(0, 0)
        p_buf[1] = jnp.zeros((tk, tq), BF16)
        acc_ref[...] = jnp.zeros_like(acc_ref)
        carry = (jnp.ones((1, tq), F32), jnp.full((1, tq), MASKED, F32), max_0)
        for u in range(n_pairs):
            carry = two_blocks(u, carry)
        alpha_prev, m, _ = carry

        t = 2 * n_pairs
        key = lax.broadcasted_iota(jnp.int32, (tk, tk), 0)
        qry = lax.broadcasted_iota(jnp.int32, (tk, tk), 1)
        causal = key <= qry
        s_right = _dot(k_blk(t + 1), qt_ref[:, tk:])
        value_update(max(t - 1, 0), 1, alpha_prev)
        s = s_buf[0]
        s = jnp.concatenate([jnp.where(causal, s[:, :tk], MASKED), s[:, tk:]], axis=1)
        m, alpha, p_buf[0] = softmax(s, m)
        value_update(t, 0, alpha)
        _, alpha_r, p_r = softmax(jnp.where(causal, s_right, MASKED), m[:, tk:])
        acc_ref[:, tk:] = alpha_r * acc_ref[:, tk:] + _dot(v_blk(t + 1), p_r)
        acc = acc_ref[...]
        o_ref[0] = (acc[:HEAD_DIM] / acc[HEAD_DIM:HEAD_DIM + 1]).T.astype(BF16)

    for n_pairs in range(n_q):
        pl.when(qi == n_pairs)(functools.partial(q_tile, n_pairs))


def _attn_call(q, k, vt):
    B, S, _ = q.shape
    tq, tk = ATTN_TQ, ATTN_TK
    assert tq == 2 * tk
    return pl.pallas_call(
        functools.partial(_attn_kernel, tq=tq, tk=tk, n_q=S // tq),
        grid=(B, N_HEADS, S // tq),
        in_specs=[
            pl.BlockSpec((1, tq, QK_PAD), lambda b, h, i: (b, i, h)),
            pl.BlockSpec((1, S, QK_PAD), lambda b, h, i: (b, 0, h)),
            pl.BlockSpec((1, 1, HEAD_DIM, S), lambda b, h, i: (b, h, 0, 0)),
        ],
        out_specs=pl.BlockSpec((1, tq, HEAD_DIM), lambda b, h, i: (b, i, h)),
        out_shape=jax.ShapeDtypeStruct((B, S, N_HEADS * HEAD_DIM), BF16),
        scratch_shapes=[
            pltpu.VMEM((QK_PAD, tq), BF16),
            pltpu.VMEM((2, tk, tq), F32),
            pltpu.VMEM((2, tk, tq), BF16),
            pltpu.VMEM((ACC_ROWS, tq), F32),
        ],
        compiler_params=pltpu.CompilerParams(
            dimension_semantics=("arbitrary", "arbitrary", "arbitrary"), vmem_limit_bytes=VMEM_LIMIT),
        name="mla_attn",
    )(q, k, vt)


def _mlp_kernel(x_ref, ret_ref, mla_ref, mod_ref, wo32_hbm, ln1w_ref, ln1b_ref, wup32_hbm, cw_ref, cb_ref,
                wdn32_hbm, ln2w_ref, ln2b_ref, o_ref, carry_ref, ubuf_ref, act_ref,
                wo_ref, wup_ref, wdn_ref, stage_sq_ref, stage_up_ref, sem, *, tm):
    @pl.when(jnp.logical_and(pl.program_id(0) == 0, pl.program_id(1) == 0))
    def _():
        _stream_cast(wo32_hbm.at[0], stage_sq_ref, sem, _store_bf16(wo_ref))
        _stream_cast(wup32_hbm.at[0], stage_up_ref, sem, _store_bf16(wup_ref))
        _stream_cast(wdn32_hbm.at[0], stage_sq_ref, sem, _store_bf16(wdn_ref))

    @pl.when(pl.program_id(1) == 0)
    def _():
        carry_ref[...] = jnp.zeros_like(carry_ref)

    gate1 = mod_ref[0, 2:3, :]
    shift2 = mod_ref[0, 3:4, :]
    scale2 = mod_ref[0, 4:5, :]
    gate2 = mod_ref[0, 5:6, :]

    halves = (slice(0, tm // 2), slice(tm // 2, tm))

    def out_proj(rows):
        return (_dot(ret_ref[0, rows, :], wo_ref[0:RET_WIDTH, :])
                + _dot(mla_ref[0, rows, :], wo_ref[RET_WIDTH:2 * RET_WIDTH, :]))

    def norms(y, rows):
        x1 = _ln_plain(DN_ALPHA * x_ref[0, rows, :] + (1.0 + gate1) * y) * ln1w_ref[...] + ln1b_ref[...]
        return x1, (_ln_plain(x1) * (1.0 + scale2) + shift2).astype(BF16)

    ys = [out_proj(rows) for rows in halves]
    x1s, h2s = zip(*[norms(y, rows) for y, rows in zip(ys, halves)])
    h2 = jnp.concatenate(h2s, axis=0)

    def conv(u, col0, kind):
        outs = []
        for g in range(FF_CHUNK // LANES):
            cols = slice(col0 + g * LANES, col0 + (g + 1) * LANES)
            ug = u[:, g * LANES:(g + 1) * LANES]
            buf = ubuf_ref.at[kind, g]
            buf[0:SUBLANES, :] = carry_ref[:, cols]
            buf[SUBLANES:SUBLANES + tm, :] = ug
            carry_ref[:, cols] = ug[tm - SUBLANES:tm, :]
            u1 = buf[SUBLANES - 1:SUBLANES - 1 + tm, :]
            u2 = buf[SUBLANES - 2:SUBLANES - 2 + tm, :]
            outs.append(cb_ref[:, cols] + cw_ref[2:3, cols] * ug + cw_ref[1:2, cols] * u1
                        + cw_ref[0:1, cols] * u2)
        return jnp.concatenate(outs, axis=1)

    for c in range(N_FF_CHUNKS):
        gcol0 = c * FF_CHUNK
        vcol0 = D_FF + c * FF_CHUNK
        g = conv(_dot(h2, wup_ref[:, gcol0:gcol0 + FF_CHUNK]), gcol0, 0)
        val = conv(_dot(h2, wup_ref[:, vcol0:vcol0 + FF_CHUNK]), vcol0, 1)
        act_ref[:, gcol0:gcol0 + FF_CHUNK] = (_silu(g) * val).astype(BF16)

    y2s = [_dot(act_ref[rows, :], wdn_ref[...]) for rows in halves]
    for x1, y2, rows in zip(x1s, y2s, halves):
        o_ref[0, rows, :] = _ln_plain(DN_ALPHA * x1 + (1.0 + gate2) * y2) * ln2w_ref[...] + ln2b_ref[...]


def _mlp_call(x, ret, mla, mod, w_out, ln1w, ln1b, w_up, conv_w, conv_b, w_down, ln2w, ln2b):
    B, S, D = x.shape
    tm = ROW_TILE
    row = lambda w: pl.BlockSpec((1, tm, w), lambda b, s: (b, s, 0))
    hbm = pl.BlockSpec(memory_space=pl.ANY)
    return pl.pallas_call(
        functools.partial(_mlp_kernel, tm=tm),
        grid=(B, S // tm),
        in_specs=[
            row(D), row(RET_WIDTH), row(RET_WIDTH),
            pl.BlockSpec((1, N_MOD, D), lambda b, s: (b, 0, 0)),
            hbm, _const_spec(ln1w.shape), _const_spec(ln1b.shape),
            hbm, _const_spec(conv_w.shape), _const_spec(conv_b.shape),
            hbm, _const_spec(ln2w.shape), _const_spec(ln2b.shape),
        ],
        out_specs=row(D),
        out_shape=jax.ShapeDtypeStruct((B, S, D), F32),
        scratch_shapes=[
            pltpu.VMEM((SUBLANES, 2 * D_FF), F32),
            pltpu.VMEM((2, FF_CHUNK // LANES, SUBLANES + tm, LANES), F32),
            pltpu.VMEM((tm, D_FF), BF16),
            pltpu.VMEM(w_out.shape[1:], BF16),
            pltpu.VMEM(w_up.shape[1:], BF16),
            pltpu.VMEM(w_down.shape[1:], BF16),
            pltpu.VMEM((STAGE_SLOTS, STAGE_ROWS_SQ, D), F32),
            pltpu.VMEM((STAGE_SLOTS, STAGE_ROWS_UP, 2 * D_FF), F32),
            pltpu.SemaphoreType.DMA((STAGE_SLOTS,)),
        ],
        compiler_params=pltpu.CompilerParams(
            dimension_semantics=("arbitrary", "arbitrary"), vmem_limit_bytes=VMEM_LIMIT),
        name="outproj_mlp",
    )(x, ret, mla, mod, w_out, ln1w, ln1b, w_up, conv_w, conv_b, w_down, ln2w, ln2b)


def _rope_tables(seq, half, reps):
    pos = np.arange(seq, dtype=np.float32)
    inv = np.float32(ROPE_BASE) ** (-np.arange(half, dtype=np.float32) / np.float32(half))
    ang = (pos[:, None] * inv[None, :]).astype(np.float32)
    cos = np.cos(ang).astype(np.float32)
    sin = np.sin(ang).astype(np.float32)
    cos_t = np.tile(cos, (1, 2 * reps))
    sin_t = np.concatenate([np.tile(-sin, (1, reps)), np.tile(sin, (1, reps))], axis=-1)
    return jnp.asarray(cos_t), jnp.asarray(sin_t)


def _uq_columns():
    half = MLA_ROPE_DIM // 2
    nope = [h * MLA_QK_DIM + np.arange(HEAD_DIM) for h in range(N_HEADS)]
    rope = []
    for p in range(N_HEADS // 2):
        for part in range(2):
            for h in (2 * p, 2 * p + 1):
                rope.append(h * MLA_QK_DIM + HEAD_DIM + part * half + np.arange(half))
    return np.concatenate(nope + rope)


def _take_columns(w, cols):
    cols = np.asarray(cols)
    cuts = np.flatnonzero(np.diff(cols) != 1) + 1
    runs = np.split(cols, cuts)
    return jnp.concatenate([w[:, r[0]:r[-1] + 1] for r in runs], axis=1)


def kernel(x, c, w_ada, b_ada, w_in, ret_gn_w, mla_q_norm_w, w_uq, mla_kv_norm_w, w_ukv, w_out,
           ln1_w, ln1_b, w_up, conv_w, conv_b, w_down, ln2_w, ln2_b):
    B, S, D = x.shape
    assert D == D_MODEL and S % IN_ROW_TILE == 0 and S % ROW_TILE == 0 and S % ATTN_TQ == 0
    assert w_ada.shape[0] == DEPTH == 1
    l = 0

    mod = _ada_call(c, w_ada, b_ada[l][None, :]).reshape(B, N_MOD, D)

    w_uq_p = _take_columns(w_uq[l].astype(BF16), _uq_columns())
    w_ukv_p = w_ukv[l].astype(BF16)
    cosr, sinr = _rope_tables(S, HEAD_DIM // 2, 1)
    cosm, sinm = _rope_tables(S, MLA_ROPE_DIM // 2, 2)

    ret, q, k, vt = _inproj_call(
        x, mod, jnp.swapaxes(w_in, 1, 2), ret_gn_w[l][None, :], mla_q_norm_w[l][None, :], mla_kv_norm_w[l][None, :],
        w_uq_p, w_ukv_p, cosr, sinr, cosm, sinm)
    mla = _attn_call(q, k, vt)
    return _mlp_call(
        x, ret, mla, mod, w_out, ln1_w[l][None, :], ln1_b[l][None, :],
        w_up, conv_w[l], conv_b[l][None, :], w_down, ln2_w[l][None, :], ln2_b[l][None, :])
```

```python
import functools
import math

import numpy as np
import jax
import jax.numpy as jnp
from jax import lax
from jax.experimental import pallas as pl
from jax.experimental.pallas import tpu as pltpu

F32 = jnp.float32
BF16 = jnp.bfloat16

D_MODEL = 1024
DEPTH = 1
N_HEADS = 4
HEAD_DIM = 128
RET_WIDTH = N_HEADS * HEAD_DIM
LAT_COL0 = 4 * RET_WIDTH
MLA_RANK = 256
MLA_ROPE_DIM = 64
MLA_QK_DIM = HEAD_DIM + MLA_ROPE_DIM
D_FF = 2816
ROPE_BASE = 10000.0
LN_EPS = 1e-5
RMS_EPS = 1e-6
DN_ALPHA = (2.0 * DEPTH) ** 0.25
N_MOD = 6
LOG_G = [math.log1p(-(2.0 ** (-5.0 - h))) for h in range(N_HEADS)]

LANES = 128
SUBLANES = 8
MXU_DIM = 256
VMEM_BYTES = 64 * 1024 * 1024
VMEM_LIMIT = VMEM_BYTES * 7 // 8

ADA_ROWS = 256
IN_ROW_TILE = 1024
ROW_TILE = 512
RET_CHUNK = MXU_DIM
ATTN_TQ = 2048
ATTN_TK = 1024
FF_CHUNK = MXU_DIM
N_FF_CHUNKS = D_FF // FF_CHUNK
STAGE_ROWS_SQ = 256
STAGE_ROWS_UP = 64
STAGE_ROWS_IN = 128
STAGE_SLOTS = 4
QK_PAD = 2 * LANES
BF16_ROWS = 2 * SUBLANES
ACC_ROWS = HEAD_DIM + BF16_ROWS
MASKED = -float("inf")


def _ln_plain(x):
    mu = jnp.mean(x, axis=-1, keepdims=True)
    xc = x - mu
    var = jnp.mean(xc * xc, axis=-1, keepdims=True)
    return xc * lax.rsqrt(var + LN_EPS)


def _rms(x):
    return x * lax.rsqrt(jnp.mean(x * x, axis=-1, keepdims=True) + RMS_EPS)


def _silu(x):
    return x * jax.nn.sigmoid(x)


def _dot(a, b):
    return jnp.dot(a, b, preferred_element_type=F32)


def _dot_nt(a, b):
    return lax.dot_general(a, b, (((1,), (1,)), ((), ())), preferred_element_type=F32)


def _dot_tn(a, b):
    return lax.dot_general(a, b, (((0,), (0,)), ((), ())), preferred_element_type=F32)


def _ada_kernel(c_ref, w_ref, b_ref, o_ref):
    @pl.when(pl.program_id(0) == 0)
    def _():
        o_ref[...] = jnp.broadcast_to(b_ref[...], o_ref.shape)

    cond = _silu(c_ref[...])
    o_ref[...] += _dot(cond.astype(BF16), w_ref[0].astype(BF16))


def _ada_call(c, w_ada, b_ada):
    n_batch = c.shape[0]
    _, d_in, n_out = w_ada.shape
    return pl.pallas_call(
        _ada_kernel,
        grid=(d_in // ADA_ROWS,),
        in_specs=[
            pl.BlockSpec((n_batch, ADA_ROWS), lambda j: (0, j)),
            pl.BlockSpec((1, ADA_ROWS, n_out), lambda j: (0, j, 0)),
            pl.BlockSpec((1, n_out), lambda j: (0, 0)),
        ],
        out_specs=pl.BlockSpec((n_batch, n_out), lambda j: (0, 0)),
        out_shape=jax.ShapeDtypeStruct((n_batch, n_out), F32),
        compiler_params=pltpu.CompilerParams(dimension_semantics=("arbitrary",), vmem_limit_bytes=VMEM_LIMIT),
        name="ada_mod",
    )(c, w_ada, b_ada)


def _rope128(x, cos, sin_signed):
    return x * cos + pltpu.roll(x, LANES // 2, axis=1) * sin_signed


def _decay_matrix(hd, c):
    row = lax.broadcasted_iota(jnp.int32, (c, c), 0)
    col = lax.broadcasted_iota(jnp.int32, (c, c), 1)
    rel = (row - col).astype(F32)
    return jnp.where(rel >= 0.0, jnp.exp(LOG_G[hd] * jnp.maximum(rel, 0.0)), 0.0)


def _retention_head(hd, q, k, v, gate, gnw, decay_ref, state_ref, ret_ref):
    c = decay_ref.shape[1]
    lg = LOG_G[hd]
    n = lax.broadcasted_iota(jnp.int32, (c, 1), 0).astype(F32)
    q_w = jnp.exp(lg * (n + 1.0))
    k_w = jnp.exp(lg * (c - 1.0 - n))
    for i in range(q.shape[0] // c):
        rows = slice(i * c, (i + 1) * c)
        qb = q[rows].astype(BF16)
        vb = v[rows].astype(BF16)
        scores = _dot_nt(qb, k[rows].astype(BF16)) * decay_ref[hd]
        inner = _dot(scores.astype(BF16), vb)
        state = state_ref[hd]
        cross = _dot(qb, state.astype(BF16)) * q_w
        state_ref[hd] = state * math.exp(lg * c) + _dot_tn((k[rows] * k_w).astype(BF16), vb)
        o = _ln_plain(inner + cross) * gnw
        ret_ref[0, rows, hd * HEAD_DIM:(hd + 1) * HEAD_DIM] = (o * gate[rows]).astype(BF16)


def _stream_cast(src_hbm, stage_ref, sem, store, n_rows=None):
    slots, rows = stage_ref.shape[0], stage_ref.shape[1]
    n_rows = src_hbm.shape[0] if n_rows is None else n_rows
    n_chunks = n_rows // rows
    assert n_chunks * rows == n_rows and n_chunks >= slots

    def copy(k):
        slot = k % slots
        return pltpu.make_async_copy(
            src_hbm.at[pl.ds(pl.multiple_of(k * rows, rows), rows), :], stage_ref.at[slot], sem.at[slot])

    for k in range(slots - 1):
        copy(k).start()

    def body(k, _):
        @pl.when(k + slots - 1 < n_chunks)
        def _():
            copy(k + slots - 1).start()

        copy(k).wait()
        store(pl.ds(pl.multiple_of(k * rows, rows), rows), stage_ref[k % slots])
        return 0

    lax.fori_loop(0, n_chunks, body, 0)


def _store_bf16(dst_ref):
    def store(rows, chunk):
        dst_ref[rows, :] = chunk.astype(BF16)
    return store


def _stage_w_in(wt_hbm, win_ref, stage_ref, sem):
    base = LAT_COL0 + 2 * MLA_RANK
    quarter = LANES // 4

    def store(cols, chunk):
        win_ref[:, cols] = chunk.T.astype(BF16)

    _stream_cast(wt_hbm, stage_ref, sem, store, n_rows=base)

    tail_copy = pltpu.make_async_copy(
        wt_hbm.at[pl.ds(base, MLA_ROPE_DIM), :], stage_ref.at[0, pl.ds(0, MLA_ROPE_DIM), :], sem.at[0])
    tail_copy.start()
    tail_copy.wait()
    tail = stage_ref[0, 0:MLA_ROPE_DIM, :].T
    t2 = jnp.concatenate([tail, tail], axis=1)
    lane = lax.broadcasted_iota(jnp.int32, t2.shape, 1)
    middle = jnp.abs(2 * lane - (LANES - 1)) < 2 * quarter
    win_ref[:, base:base + LANES] = jnp.where(middle, pltpu.roll(t2, quarter, axis=1), t2).astype(BF16)


def _inproj_kernel(x_ref, mod_ref, wint_hbm, gnw_ref, qnw_ref, kvnw_ref, wuq_ref, wukv_ref,
                   cosr_ref, sinr_ref, cosm_ref, sinm_ref,
                   ret_ref, q_ref, k_ref, vt_ref, state_ref, decay_ref, win_ref, stage_ref, sem,
                   *, q_scale, rk_scale):
    @pl.when(jnp.logical_and(pl.program_id(0) == 0, pl.program_id(1) == 0))
    def _():
        _stage_w_in(wint_hbm.at[0], win_ref, stage_ref, sem)
        for hd in range(N_HEADS):
            decay_ref[hd] = _decay_matrix(hd, decay_ref.shape[1])

    @pl.when(pl.program_id(1) == 0)
    def _():
        state_ref[...] = jnp.zeros_like(state_ref)

    x = x_ref[0]
    shift = mod_ref[0, 0:1, :]
    scale = mod_ref[0, 1:2, :]
    h = (_ln_plain(x) * (1.0 + scale) + shift).astype(BF16)

    cosr = cosr_ref[...]
    sinr = sinr_ref[...]
    cosm = cosm_ref[...]
    sinm = sinm_ref[...]

    pq = _dot(h, win_ref[:, 0:RET_WIDTH])
    pk = _dot(h, win_ref[:, RET_WIDTH:2 * RET_WIDTH])
    pv = _dot(h, win_ref[:, 2 * RET_WIDTH:3 * RET_WIDTH])
    pg = _silu(_dot(h, win_ref[:, 3 * RET_WIDTH:LAT_COL0]))
    for hd in range(N_HEADS):
        sl = slice(hd * HEAD_DIM, (hd + 1) * HEAD_DIM)
        _retention_head(hd, _rope128(pq[:, sl], cosr, sinr), _rope128(pk[:, sl], cosr, sinr) * rk_scale,
                        pv[:, sl], pg[:, sl], gnw_ref[:, sl], decay_ref, state_ref, ret_ref)

    lat = _dot(h, win_ref[:, LAT_COL0:])
    cq = (_rms(lat[:, 0:MLA_RANK]) * qnw_ref[...]).astype(BF16)
    ckv = (_rms(lat[:, MLA_RANK:2 * MLA_RANK]) * kvnw_ref[...]).astype(BF16)
    kr = _rope128(lat[:, 2 * MLA_RANK:2 * MLA_RANK + LANES], cosm, sinm)
    lane = lax.broadcasted_iota(jnp.int32, kr.shape, 1)
    first_of_pair = (lane % (LANES // 2)) < (LANES // 4)
    kr_even = jnp.where(first_of_pair, kr, 0.0).astype(BF16)
    kr_odd = jnp.where(first_of_pair, 0.0, kr).astype(BF16)

    qf = _dot(cq, wuq_ref[...])
    kvf = _dot(ckv, wukv_ref[...])
    for hd in range(N_HEADS):
        vt_ref[0, hd] = kvf[:, (2 * hd + 1) * HEAD_DIM:(2 * hd + 2) * HEAD_DIM].T.astype(BF16)
    for p in range(N_HEADS // 2):
        qr = qf[:, RET_WIDTH + p * LANES:RET_WIDTH + (p + 1) * LANES]
        qr = (_rope128(qr, cosm, sinm) * q_scale).astype(BF16)
        for hd in (2 * p, 2 * p + 1):
            sl = slice(hd * HEAD_DIM, (hd + 1) * HEAD_DIM)
            q_ref[0, :, hd * QK_PAD:hd * QK_PAD + LANES] = (qf[:, sl] * q_scale).astype(BF16)
            q_ref[0, :, hd * QK_PAD + LANES:(hd + 1) * QK_PAD] = qr
            k_ref[0, :, hd * QK_PAD:hd * QK_PAD + LANES] = kvf[:, 2 * hd * HEAD_DIM:(2 * hd + 1) * HEAD_DIM].astype(BF16)
            k_ref[0, :, hd * QK_PAD + LANES:(hd + 1) * QK_PAD] = kr_even if hd % 2 == 0 else kr_odd


def _const_spec(shape):
    nd = len(shape)
    return pl.BlockSpec(shape, lambda *_: (0,) * nd, pipeline_mode=pl.Buffered(1))


def _inproj_call(x, mod, w_in, gnw, qnw, kvnw, w_uq, w_ukv, cosr, sinr, cosm, sinm):
    B, S, D = x.shape
    tm = IN_ROW_TILE
    q_scale = (MLA_QK_DIM ** -0.5) * math.log2(math.e)
    rk_scale = HEAD_DIM ** -0.5
    row = lambda w: pl.BlockSpec((1, tm, w), lambda b, s: (b, s, 0))
    tab = pl.BlockSpec((tm, LANES), lambda b, s: (s, 0))
    out_shapes = (
        [jax.ShapeDtypeStruct((B, S, RET_WIDTH), BF16)]
        + [jax.ShapeDtypeStruct((B, S, N_HEADS * QK_PAD), BF16)] * 2
        + [jax.ShapeDtypeStruct((B, N_HEADS, HEAD_DIM, S), BF16)]
    )
    vt_spec = pl.BlockSpec((1, N_HEADS, HEAD_DIM, tm), lambda b, s: (b, 0, 0, s))
    return pl.pallas_call(
        functools.partial(_inproj_kernel, q_scale=q_scale, rk_scale=rk_scale),
        grid=(B, S // tm),
        in_specs=[
            row(D),
            pl.BlockSpec((1, N_MOD, D), lambda b, s: (b, 0, 0)),
            pl.BlockSpec(memory_space=pl.ANY), _const_spec(gnw.shape),
            _const_spec(qnw.shape), _const_spec(kvnw.shape),
            _const_spec(w_uq.shape), _const_spec(w_ukv.shape),
            tab, tab, tab, tab,
        ],
        out_specs=[row(RET_WIDTH)] + [row(N_HEADS * QK_PAD)] * 2 + [vt_spec],
        out_shape=out_shapes,
        scratch_shapes=[
            pltpu.VMEM((N_HEADS, HEAD_DIM, HEAD_DIM), F32),
            pltpu.VMEM((N_HEADS, RET_CHUNK, RET_CHUNK), F32),
            pltpu.VMEM((D, LAT_COL0 + 2 * MLA_RANK + LANES), BF16),
            pltpu.VMEM((STAGE_SLOTS, STAGE_ROWS_IN, w_in.shape[-1]), F32),
            pltpu.SemaphoreType.DMA((STAGE_SLOTS,)),
        ],
        compiler_params=pltpu.CompilerParams(
            dimension_semantics=("arbitrary", "arbitrary"), vmem_limit_bytes=VMEM_LIMIT),
        name="inproj_retention",
    )(x, mod, w_in, gnw, qnw, kvnw, w_uq, w_ukv, cosr, sinr, cosm, sinm)


def _attn_kernel(q_ref, k_ref, vt_ref, o_ref, qt_ref, s_buf, p_buf, acc_ref, *, tq, tk):
    qi = pl.program_id(2)
    qt_ref[...] = q_ref[0].T

    ones = jnp.ones((BF16_ROWS, tk), BF16)

    def k_blk(j):
        return k_ref[0, pl.ds(pl.multiple_of(j * tk, tk), tk), :]

    def v_blk(j):
        return jnp.concatenate([vt_ref[0, 0, :, pl.ds(pl.multiple_of(j * tk, tk), tk)], ones], axis=0)

    def scores(j, slot):
        s = _dot(k_blk(j), qt_ref[...])
        s_buf[slot] = s
        return jnp.max(s, axis=0, keepdims=True)

    def value_update(j, slot, alpha):
        acc_ref[...] = alpha * acc_ref[...] + _dot(v_blk(j), p_buf[slot])

    def softmax(s, m, block_max=None):
        if block_max is None:
            block_max = jnp.max(s, axis=0, keepdims=True)
        m_new = jnp.maximum(m, block_max)
        return m_new, jnp.exp2(m - m_new), jnp.exp2(s - m_new).astype(BF16)

    def two_blocks(u, carry):
        alpha_prev, m, max_even = carry
        t = 2 * u
        max_odd = scores(t + 1, 1)
        value_update(jnp.maximum(t - 1, 0), 1, alpha_prev)
        m, alpha_even, p_buf[0] = softmax(s_buf[0], m, max_even)
        max_next = scores(t + 2, 0)
        value_update(t, 0, alpha_even)
        m, alpha_odd, p_buf[1] = softmax(s_buf[1], m, max_odd)
        return alpha_odd, m, max_next

    max_0 = scores(0, 0)
    p_buf[1] = jnp.zeros((tk, tq), BF16)
    acc_ref[...] = jnp.zeros_like(acc_ref)
    init = (jnp.ones((1, tq), F32), jnp.full((1, tq), MASKED, F32), max_0)
    alpha_prev, m, _ = lax.fori_loop(0, qi, two_blocks, init)

    t = 2 * qi
    key = lax.broadcasted_iota(jnp.int32, (tk, tk), 0)
    qry = lax.broadcasted_iota(jnp.int32, (tk, tk), 1)
    causal = key <= qry
    s_right = _dot(k_blk(t + 1), qt_ref[:, tk:])
    value_update(jnp.maximum(t - 1, 0), 1, alpha_prev)
    s = s_buf[0]
    s = jnp.concatenate([jnp.where(causal, s[:, :tk], MASKED), s[:, tk:]], axis=1)
    m, alpha, p_buf[0] = softmax(s, m)
    value_update(t, 0, alpha)
    _, alpha_r, p_r = softmax(jnp.where(causal, s_right, MASKED), m[:, tk:])
    acc_ref[:, tk:] = alpha_r * acc_ref[:, tk:] + _dot(v_blk(t + 1), p_r)
    acc = acc_ref[...]
    o_ref[0] = (acc[:HEAD_DIM] / acc[HEAD_DIM:HEAD_DIM + 1]).T.astype(BF16)


def _attn_call(q, k, vt):
    B, S, _ = q.shape
    tq, tk = ATTN_TQ, ATTN_TK
    assert tq == 2 * tk
    return pl.pallas_call(
        functools.partial(_attn_kernel, tq=tq, tk=tk),
        grid=(B, N_HEADS, S // tq),
        in_specs=[
            pl.BlockSpec((1, tq, QK_PAD), lambda b, h, i: (b, i, h)),
            pl.BlockSpec((1, S, QK_PAD), lambda b, h, i: (b, 0, h)),
            pl.BlockSpec((1, 1, HEAD_DIM, S), lambda b, h, i: (b, h, 0, 0)),
        ],
        out_specs=pl.BlockSpec((1, tq, HEAD_DIM), lambda b, h, i: (b, i, h)),
        out_shape=jax.ShapeDtypeStruct((B, S, N_HEADS * HEAD_DIM), BF16),
        scratch_shapes=[
            pltpu.VMEM((QK_PAD, tq), BF16),
            pltpu.VMEM((2, tk, tq), F32),
            pltpu.VMEM((2, tk, tq), BF16),
            pltpu.VMEM((ACC_ROWS, tq), F32),
        ],
        compiler_params=pltpu.CompilerParams(
            dimension_semantics=("arbitrary", "arbitrary", "arbitrary"), vmem_limit_bytes=VMEM_LIMIT),
        name="mla_attn",
    )(q, k, vt)


def _mlp_kernel(x_ref, ret_ref, mla_ref, mod_ref, wo32_hbm, ln1w_ref, ln1b_ref, wup32_hbm, cw_ref, cb_ref,
                wdn32_hbm, ln2w_ref, ln2b_ref, o_ref, carry_ref, ubuf_ref, act_ref,
                wo_ref, wup_ref, wdn_ref, stage_sq_ref, stage_up_ref, sem, u_ref, *, tm):
    @pl.when(jnp.logical_and(pl.program_id(0) == 0, pl.program_id(1) == 0))
    def _():
        _stream_cast(wo32_hbm.at[0], stage_sq_ref, sem, _store_bf16(wo_ref))
        _stream_cast(wup32_hbm.at[0], stage_up_ref, sem, _store_bf16(wup_ref))
        _stream_cast(wdn32_hbm.at[0], stage_sq_ref, sem, _store_bf16(wdn_ref))

    @pl.when(pl.program_id(1) == 0)
    def _():
        carry_ref[...] = jnp.zeros_like(carry_ref)

    gate1 = mod_ref[0, 2:3, :]
    shift2 = mod_ref[0, 3:4, :]
    scale2 = mod_ref[0, 4:5, :]
    gate2 = mod_ref[0, 5:6, :]

    halves = (slice(0, tm // 2), slice(tm // 2, tm))

    def out_proj(rows):
        return (_dot(ret_ref[0, rows, :], wo_ref[0:RET_WIDTH, :])
                + _dot(mla_ref[0, rows, :], wo_ref[RET_WIDTH:2 * RET_WIDTH, :]))

    def norms(y, rows):
        x1 = _ln_plain(DN_ALPHA * x_ref[0, rows, :] + (1.0 + gate1) * y) * ln1w_ref[...] + ln1b_ref[...]
        return x1, (_ln_plain(x1) * (1.0 + scale2) + shift2).astype(BF16)

    ys = [out_proj(rows) for rows in halves]
    x1s, h2s = zip(*[norms(y, rows) for y, rows in zip(ys, halves)])
    h2 = jnp.concatenate(h2s, axis=0)

    def conv(u, col0, kind):
        outs = []
        for g in range(FF_CHUNK // LANES):
            cols = pl.ds(pl.multiple_of(col0 + g * LANES, LANES), LANES)
            ug = u[:, g * LANES:(g + 1) * LANES]
            buf = ubuf_ref.at[kind, g]
            buf[0:SUBLANES, :] = carry_ref[:, cols]
            buf[SUBLANES:SUBLANES + tm, :] = ug
            carry_ref[:, cols] = ug[tm - SUBLANES:tm, :]
            u1 = buf[SUBLANES - 1:SUBLANES - 1 + tm, :]
            u2 = buf[SUBLANES - 2:SUBLANES - 2 + tm, :]
            outs.append(cb_ref[:, cols] + cw_ref[2:3, cols] * ug + cw_ref[1:2, cols] * u1
                        + cw_ref[0:1, cols] * u2)
        return jnp.concatenate(outs, axis=1)

    def up(c):
        for kind, col0 in ((0, c * FF_CHUNK), (1, D_FF + c * FF_CHUNK)):
            u_ref[kind] = _dot(h2, wup_ref[:, pl.ds(pl.multiple_of(col0, LANES), FF_CHUNK)])

    def conv_act(c, u_gate, u_val):
        g = conv(u_gate, c * FF_CHUNK, 0)
        val = conv(u_val, D_FF + c * FF_CHUNK, 1)
        act_ref[:, pl.ds(pl.multiple_of(c * FF_CHUNK, LANES), FF_CHUNK)] = (_silu(g) * val).astype(BF16)

    def ff_chunk(c, _):
        u_gate, u_val = u_ref[0], u_ref[1]
        up(c + 1)
        conv_act(c, u_gate, u_val)
        return 0

    up(0)
    lax.fori_loop(0, N_FF_CHUNKS - 1, ff_chunk, 0)
    conv_act(N_FF_CHUNKS - 1, u_ref[0], u_ref[1])

    y2s = [_dot(act_ref[rows, :], wdn_ref[...]) for rows in halves]
    for x1, y2, rows in zip(x1s, y2s, halves):
        o_ref[0, rows, :] = _ln_plain(DN_ALPHA * x1 + (1.0 + gate2) * y2) * ln2w_ref[...] + ln2b_ref[...]


def _mlp_call(x, ret, mla, mod, w_out, ln1w, ln1b, w_up, conv_w, conv_b, w_down, ln2w, ln2b):
    B, S, D = x.shape
    tm = ROW_TILE
    row = lambda w: pl.BlockSpec((1, tm, w), lambda b, s: (b, s, 0))
    hbm = pl.BlockSpec(memory_space=pl.ANY)
    return pl.pallas_call(
        functools.partial(_mlp_kernel, tm=tm),
        grid=(B, S // tm),
        in_specs=[
            row(D), row(RET_WIDTH), row(RET_WIDTH),
            pl.BlockSpec((1, N_MOD, D), lambda b, s: (b, 0, 0)),
            hbm, _const_spec(ln1w.shape), _const_spec(ln1b.shape),
            hbm, _const_spec(conv_w.shape), _const_spec(conv_b.shape),
            hbm, _const_spec(ln2w.shape), _const_spec(ln2b.shape),
        ],
        out_specs=row(D),
        out_shape=jax.ShapeDtypeStruct((B, S, D), F32),
        scratch_shapes=[
            pltpu.VMEM((SUBLANES, 2 * D_FF), F32),
            pltpu.VMEM((2, FF_CHUNK // LANES, SUBLANES + tm, LANES), F32),
            pltpu.VMEM((tm, D_FF), BF16),
            pltpu.VMEM(w_out.shape[1:], BF16),
            pltpu.VMEM(w_up.shape[1:], BF16),
            pltpu.VMEM(w_down.shape[1:], BF16),
            pltpu.VMEM((STAGE_SLOTS, STAGE_ROWS_SQ, D), F32),
            pltpu.VMEM((STAGE_SLOTS, STAGE_ROWS_UP, 2 * D_FF), F32),
            pltpu.SemaphoreType.DMA((STAGE_SLOTS,)),
            pltpu.VMEM((2, tm, FF_CHUNK), F32),
        ],
        compiler_params=pltpu.CompilerParams(
            dimension_semantics=("arbitrary", "arbitrary"), vmem_limit_bytes=VMEM_LIMIT),
        name="outproj_mlp",
    )(x, ret, mla, mod, w_out, ln1w, ln1b, w_up, conv_w, conv_b, w_down, ln2w, ln2b)


def _rope_tables(seq, half, reps):
    pos = np.arange(seq, dtype=np.float32)
    inv = np.float32(ROPE_BASE) ** (-np.arange(half, dtype=np.float32) / np.float32(half))
    ang = (pos[:, None] * inv[None, :]).astype(np.float32)
    cos = np.cos(ang).astype(np.float32)
    sin = np.sin(ang).astype(np.float32)
    cos_t = np.tile(cos, (1, 2 * reps))
    sin_t = np.concatenate([np.tile(-sin, (1, reps)), np.tile(sin, (1, reps))], axis=-1)
    return jnp.asarray(cos_t), jnp.asarray(sin_t)


def _uq_columns():
    half = MLA_ROPE_DIM // 2
    nope = [h * MLA_QK_DIM + np.arange(HEAD_DIM) for h in range(N_HEADS)]
    rope = []
    for p in range(N_HEADS // 2):
        for part in range(2):
            for h in (2 * p, 2 * p + 1):
                rope.append(h * MLA_QK_DIM + HEAD_DIM + part * half + np.arange(half))
    return np.concatenate(nope + rope)


def _take_columns(w, cols):
    cols = np.asarray(cols)
    cuts = np.flatnonzero(np.diff(cols) != 1) + 1
    runs = np.split(cols, cuts)
    return jnp.concatenate([w[:, r[0]:r[-1] + 1] for r in runs], axis=1)


def kernel(x, c, w_ada, b_ada, w_in, ret_gn_w, mla_q_norm_w, w_uq, mla_kv_norm_w, w_ukv, w_out,
           ln1_w, ln1_b, w_up, conv_w, conv_b, w_down, ln2_w, ln2_b):
    B, S, D = x.shape
    assert D == D_MODEL and S % IN_ROW_TILE == 0 and S % ROW_TILE == 0 and S % ATTN_TQ == 0
    assert w_ada.shape[0] == DEPTH == 1
    l = 0

    mod = _ada_call(c, w_ada, b_ada[l][None, :]).reshape(B, N_MOD, D)

    w_uq_p = _take_columns(w_uq[l].astype(BF16), _uq_columns())
    w_ukv_p = w_ukv[l].astype(BF16)
    cosr, sinr = _rope_tables(S, HEAD_DIM // 2, 1)
    cosm, sinm = _rope_tables(S, MLA_ROPE_DIM // 2, 2)

    ret, q, k, vt = _inproj_call(
        x, mod, jnp.swapaxes(w_in, 1, 2), ret_gn_w[l][None, :], mla_q_norm_w[l][None, :], mla_kv_norm_w[l][None, :],
        w_uq_p, w_ukv_p, cosr, sinr, cosm, sinm)
    mla = _attn_call(q, k, vt)
    return _mlp_call(
        x, ret, mla, mod, w_out, ln1_w[l][None, :], ln1_b[l][None, :],
        w_up, conv_w[l], conv_b[l][None, :], w_down, ln2_w[l][None, :], ln2_b[l][None, :])
```

```python
import functools
import math

import numpy as np
import jax
import jax.numpy as jnp
from jax import lax
from jax.experimental import pallas as pl
from jax.experimental.pallas import tpu as pltpu

F32 = jnp.float32
BF16 = jnp.bfloat16

D_MODEL = 1024
DEPTH = 1
N_HEADS = 4
HEAD_DIM = 128
RET_WIDTH = N_HEADS * HEAD_DIM
LAT_COL0 = 4 * RET_WIDTH
MLA_RANK = 256
MLA_ROPE_DIM = 64
MLA_QK_DIM = HEAD_DIM + MLA_ROPE_DIM
D_FF = 2816
ROPE_BASE = 10000.0
LN_EPS = 1e-5
RMS_EPS = 1e-6
DN_ALPHA = (2.0 * DEPTH) ** 0.25
N_MOD = 6
LOG_G = [math.log1p(-(2.0 ** (-5.0 - h))) for h in range(N_HEADS)]

LANES = 128
SUBLANES = 8
MXU_DIM = 256
VMEM_BYTES = 64 * 1024 * 1024
VMEM_LIMIT = VMEM_BYTES * 7 // 8

ADA_ROWS = 256
IN_ROW_TILE = 1024
ROW_TILE = 512
RET_CHUNK = MXU_DIM
ATTN_TQ = 2048
ATTN_TK = 1024
FF_CHUNK = MXU_DIM
N_FF_CHUNKS = D_FF // FF_CHUNK
STAGE_ROWS_SQ = 256
STAGE_ROWS_UP = 64
STAGE_ROWS_IN = 128
STAGE_SLOTS = 4
QK_PAD = 2 * LANES
BF16_ROWS = 2 * SUBLANES
ACC_ROWS = HEAD_DIM + BF16_ROWS
MASKED = -float("inf")


def _ln_plain(x):
    mu = jnp.mean(x, axis=-1, keepdims=True)
    xc = x - mu
    var = jnp.mean(xc * xc, axis=-1, keepdims=True)
    return xc * lax.rsqrt(var + LN_EPS)


def _rms(x):
    return x * lax.rsqrt(jnp.mean(x * x, axis=-1, keepdims=True) + RMS_EPS)


def _silu(x):
    return x * jax.nn.sigmoid(x)


def _dot(a, b):
    return jnp.dot(a, b, preferred_element_type=F32)


def _dot_nt(a, b):
    return lax.dot_general(a, b, (((1,), (1,)), ((), ())), preferred_element_type=F32)


def _dot_tn(a, b):
    return lax.dot_general(a, b, (((0,), (0,)), ((), ())), preferred_element_type=F32)


def _ada_kernel(c_ref, w_ref, b_ref, o_ref):
    @pl.when(pl.program_id(0) == 0)
    def _():
        o_ref[...] = jnp.broadcast_to(b_ref[...], o_ref.shape)

    cond = _silu(c_ref[...])
    o_ref[...] += _dot(cond.astype(BF16), w_ref[0].astype(BF16))


def _ada_call(c, w_ada, b_ada):
    n_batch = c.shape[0]
    _, d_in, n_out = w_ada.shape
    return pl.pallas_call(
        _ada_kernel,
        grid=(d_in // ADA_ROWS,),
        in_specs=[
            pl.BlockSpec((n_batch, ADA_ROWS), lambda j: (0, j)),
            pl.BlockSpec((1, ADA_ROWS, n_out), lambda j: (0, j, 0)),
            pl.BlockSpec((1, n_out), lambda j: (0, 0)),
        ],
        out_specs=pl.BlockSpec((n_batch, n_out), lambda j: (0, 0)),
        out_shape=jax.ShapeDtypeStruct((n_batch, n_out), F32),
        compiler_params=pltpu.CompilerParams(dimension_semantics=("arbitrary",), vmem_limit_bytes=VMEM_LIMIT),
        name="ada_mod",
    )(c, w_ada, b_ada)


def _rope128(x, cos, sin_signed):
    return x * cos + pltpu.roll(x, LANES // 2, axis=1) * sin_signed


def _decay_matrix(hd, c):
    row = lax.broadcasted_iota(jnp.int32, (c, c), 0)
    col = lax.broadcasted_iota(jnp.int32, (c, c), 1)
    rel = (row - col).astype(F32)
    return jnp.where(rel >= 0.0, jnp.exp(LOG_G[hd] * jnp.maximum(rel, 0.0)), 0.0)


def _retention_head(hd, q, k, v, gate, gnw, decay_ref, state_ref, ret_ref):
    c = decay_ref.shape[1]
    lg = LOG_G[hd]
    n = lax.broadcasted_iota(jnp.int32, (c, 1), 0).astype(F32)
    q_w = jnp.exp(lg * (n + 1.0))
    k_w = jnp.exp(lg * (c - 1.0 - n))
    for i in range(q.shape[0] // c):
        rows = slice(i * c, (i + 1) * c)
        qb = q[rows].astype(BF16)
        vb = v[rows].astype(BF16)
        scores = _dot_nt(qb, k[rows].astype(BF16)) * decay_ref[hd]
        inner = _dot(scores.astype(BF16), vb)
        state = state_ref[hd]
        cross = _dot(qb, state.astype(BF16)) * q_w
        state_ref[hd] = state * math.exp(lg * c) + _dot_tn((k[rows] * k_w).astype(BF16), vb)
        o = _ln_plain(inner + cross) * gnw
        ret_ref[0, rows, hd * HEAD_DIM:(hd + 1) * HEAD_DIM] = (o * gate[rows]).astype(BF16)


def _stream_cast(src_hbm, stage_ref, sem, store, n_rows=None):
    slots, rows = stage_ref.shape[0], stage_ref.shape[1]
    n_rows = src_hbm.shape[0] if n_rows is None else n_rows
    n_chunks = n_rows // rows
    assert n_chunks * rows == n_rows and n_chunks >= slots

    def copy(k):
        slot = k % slots
        return pltpu.make_async_copy(
            src_hbm.at[pl.ds(pl.multiple_of(k * rows, rows), rows), :], stage_ref.at[slot], sem.at[slot])

    for k in range(slots - 1):
        copy(k).start()

    def body(k, _):
        @pl.when(k + slots - 1 < n_chunks)
        def _():
            copy(k + slots - 1).start()

        copy(k).wait()
        store(pl.ds(pl.multiple_of(k * rows, rows), rows), stage_ref[k % slots])
        return 0

    lax.fori_loop(0, n_chunks, body, 0)


def _store_bf16(dst_ref):
    def store(rows, chunk):
        dst_ref[rows, :] = chunk.astype(BF16)
    return store


def _stage_w_in(wt_hbm, win_ref, stage_ref, sem):
    base = LAT_COL0 + 2 * MLA_RANK
    quarter = LANES // 4

    def store(cols, chunk):
        win_ref[:, cols] = chunk.T.astype(BF16)

    _stream_cast(wt_hbm, stage_ref, sem, store, n_rows=base)

    tail_copy = pltpu.make_async_copy(
        wt_hbm.at[pl.ds(base, MLA_ROPE_DIM), :], stage_ref.at[0, pl.ds(0, MLA_ROPE_DIM), :], sem.at[0])
    tail_copy.start()
    tail_copy.wait()
    tail = stage_ref[0, 0:MLA_ROPE_DIM, :].T
    t2 = jnp.concatenate([tail, tail], axis=1)
    lane = lax.broadcasted_iota(jnp.int32, t2.shape, 1)
    middle = jnp.abs(2 * lane - (LANES - 1)) < 2 * quarter
    win_ref[:, base:base + LANES] = jnp.where(middle, pltpu.roll(t2, quarter, axis=1), t2).astype(BF16)


def _inproj_kernel(x_ref, mod_ref, wint_hbm, gnw_ref, qnw_ref, kvnw_ref, wuq_ref, wukv_ref,
                   cosr_ref, sinr_ref, cosm_ref, sinm_ref,
                   ret_ref, q_ref, k_ref, vt_ref, state_ref, decay_ref, win_ref, stage_ref, sem,
                   *, q_scale, rk_scale):
    @pl.when(jnp.logical_and(pl.program_id(0) == 0, pl.program_id(1) == 0))
    def _():
        _stage_w_in(wint_hbm.at[0], win_ref, stage_ref, sem)
        for hd in range(N_HEADS):
            decay_ref[hd] = _decay_matrix(hd, decay_ref.shape[1])

    @pl.when(pl.program_id(1) == 0)
    def _():
        state_ref[...] = jnp.zeros_like(state_ref)

    x = x_ref[0]
    shift = mod_ref[0, 0:1, :]
    scale = mod_ref[0, 1:2, :]
    h = (_ln_plain(x) * (1.0 + scale) + shift).astype(BF16)

    cosr = cosr_ref[...]
    sinr = sinr_ref[...]
    cosm = cosm_ref[...]
    sinm = sinm_ref[...]

    pq = _dot(h, win_ref[:, 0:RET_WIDTH])
    pk = _dot(h, win_ref[:, RET_WIDTH:2 * RET_WIDTH])
    pv = _dot(h, win_ref[:, 2 * RET_WIDTH:3 * RET_WIDTH])
    pg = _silu(_dot(h, win_ref[:, 3 * RET_WIDTH:LAT_COL0]))
    for hd in range(N_HEADS):
        sl = slice(hd * HEAD_DIM, (hd + 1) * HEAD_DIM)
        _retention_head(hd, _rope128(pq[:, sl], cosr, sinr), _rope128(pk[:, sl], cosr, sinr) * rk_scale,
                        pv[:, sl], pg[:, sl], gnw_ref[:, sl], decay_ref, state_ref, ret_ref)

    lat = _dot(h, win_ref[:, LAT_COL0:])
    cq = (_rms(lat[:, 0:MLA_RANK]) * qnw_ref[...]).astype(BF16)
    ckv = (_rms(lat[:, MLA_RANK:2 * MLA_RANK]) * kvnw_ref[...]).astype(BF16)
    kr = _rope128(lat[:, 2 * MLA_RANK:2 * MLA_RANK + LANES], cosm, sinm)
    lane = lax.broadcasted_iota(jnp.int32, kr.shape, 1)
    first_of_pair = (lane % (LANES // 2)) < (LANES // 4)
    kr_even = jnp.where(first_of_pair, kr, 0.0).astype(BF16)
    kr_odd = jnp.where(first_of_pair, 0.0, kr).astype(BF16)

    qf = _dot(cq, wuq_ref[...])
    kvf = _dot(ckv, wukv_ref[...])
    for hd in range(N_HEADS):
        vt_ref[0, hd] = kvf[:, (2 * hd + 1) * HEAD_DIM:(2 * hd + 2) * HEAD_DIM].T.astype(BF16)
    for p in range(N_HEADS // 2):
        qr = qf[:, RET_WIDTH + p * LANES:RET_WIDTH + (p + 1) * LANES]
        qr = (_rope128(qr, cosm, sinm) * q_scale).astype(BF16)
        for hd in (2 * p, 2 * p + 1):
            sl = slice(hd * HEAD_DIM, (hd + 1) * HEAD_DIM)
            q_ref[0, :, hd * QK_PAD:hd * QK_PAD + LANES] = (qf[:, sl] * q_scale).astype(BF16)
            q_ref[0, :, hd * QK_PAD + LANES:(hd + 1) * QK_PAD] = qr
            k_ref[0, :, hd * QK_PAD:hd * QK_PAD + LANES] = kvf[:, 2 * hd * HEAD_DIM:(2 * hd + 1) * HEAD_DIM].astype(BF16)
            k_ref[0, :, hd * QK_PAD + LANES:(hd + 1) * QK_PAD] = kr_even if hd % 2 == 0 else kr_odd


def _const_spec(shape):
    nd = len(shape)
    return pl.BlockSpec(shape, lambda *_: (0,) * nd, pipeline_mode=pl.Buffered(1))


def _inproj_call(x, mod, w_in, gnw, qnw, kvnw, w_uq, w_ukv, cosr, sinr, cosm, sinm):
    B, S, D = x.shape
    tm = IN_ROW_TILE
    q_scale = (MLA_QK_DIM ** -0.5) * math.log2(math.e)
    rk_scale = HEAD_DIM ** -0.5
    row = lambda w: pl.BlockSpec((1, tm, w), lambda b, s: (b, s, 0))
    tab = pl.BlockSpec((tm, LANES), lambda b, s: (s, 0))
    out_shapes = (
        [jax.ShapeDtypeStruct((B, S, RET_WIDTH), BF16)]
        + [jax.ShapeDtypeStruct((B, S, N_HEADS * QK_PAD), BF16)] * 2
        + [jax.ShapeDtypeStruct((B, N_HEADS, HEAD_DIM, S), BF16)]
    )
    vt_spec = pl.BlockSpec((1, N_HEADS, HEAD_DIM, tm), lambda b, s: (b, 0, 0, s))
    return pl.pallas_call(
        functools.partial(_inproj_kernel, q_scale=q_scale, rk_scale=rk_scale),
        grid=(B, S // tm),
        in_specs=[
            row(D),
            pl.BlockSpec((1, N_MOD, D), lambda b, s: (b, 0, 0)),
            pl.BlockSpec(memory_space=pl.ANY), _const_spec(gnw.shape),
            _const_spec(qnw.shape), _const_spec(kvnw.shape),
            _const_spec(w_uq.shape), _const_spec(w_ukv.shape),
            tab, tab, tab, tab,
        ],
        out_specs=[row(RET_WIDTH)] + [row(N_HEADS * QK_PAD)] * 2 + [vt_spec],
        out_shape=out_shapes,
        scratch_shapes=[
            pltpu.VMEM((N_HEADS, HEAD_DIM, HEAD_DIM), F32),
            pltpu.VMEM((N_HEADS, RET_CHUNK, RET_CHUNK), F32),
            pltpu.VMEM((D, LAT_COL0 + 2 * MLA_RANK + LANES), BF16),
            pltpu.VMEM((STAGE_SLOTS, STAGE_ROWS_IN, w_in.shape[-1]), F32),
            pltpu.SemaphoreType.DMA((STAGE_SLOTS,)),
        ],
        compiler_params=pltpu.CompilerParams(
            dimension_semantics=("arbitrary", "arbitrary"), vmem_limit_bytes=VMEM_LIMIT),
        name="inproj_retention",
    )(x, mod, w_in, gnw, qnw, kvnw, w_uq, w_ukv, cosr, sinr, cosm, sinm)


def _attn_kernel(q_ref, k_ref, vt_ref, o_ref, qt_ref, s_buf, p_buf, acc_ref, *, tq, tk):
    qi = pl.program_id(2)
    qt_ref[...] = q_ref[0].T

    ones = jnp.ones((BF16_ROWS, tk), BF16)

    def k_blk(j):
        return k_ref[0, pl.ds(pl.multiple_of(j * tk, tk), tk), :]

    def v_blk(j):
        return jnp.concatenate([vt_ref[0, 0, :, pl.ds(pl.multiple_of(j * tk, tk), tk)], ones], axis=0)

    def scores(j, slot):
        s = _dot(k_blk(j), qt_ref[...])
        s_buf[slot] = s
        return jnp.max(s, axis=0, keepdims=True)

    def value_update(j, slot, alpha):
        acc_ref[...] = alpha * acc_ref[...] + _dot(v_blk(j), p_buf[slot])

    def softmax(s, m, block_max=None):
        if block_max is None:
            block_max = jnp.max(s, axis=0, keepdims=True)
        m_new = jnp.maximum(m, block_max)
        return m_new, jnp.exp2(m - m_new), jnp.exp2(s - m_new).astype(BF16)

    def two_blocks(u, carry):
        alpha_prev, m, max_even = carry
        t = 2 * u
        max_odd = scores(t + 1, 1)
        value_update(jnp.maximum(t - 1, 0), 1, alpha_prev)
        m, alpha_even, p_buf[0] = softmax(s_buf[0], m, max_even)
        max_next = scores(t + 2, 0)
        value_update(t, 0, alpha_even)
        m, alpha_odd, p_buf[1] = softmax(s_buf[1], m, max_odd)
        return alpha_odd, m, max_next

    max_0 = scores(0, 0)
    p_buf[1] = jnp.zeros((tk, tq), BF16)
    acc_ref[...] = jnp.zeros_like(acc_ref)
    init = (jnp.ones((1, tq), F32), jnp.full((1, tq), MASKED, F32), max_0)
    alpha_prev, m, _ = lax.fori_loop(0, qi, two_blocks, init)

    t = 2 * qi
    key = lax.broadcasted_iota(jnp.int32, (tk, tk), 0)
    qry = lax.broadcasted_iota(jnp.int32, (tk, tk), 1)
    causal = key <= qry
    s_right = _dot(k_blk(t + 1), qt_ref[:, tk:])
    value_update(jnp.maximum(t - 1, 0), 1, alpha_prev)
    s = s_buf[0]
    s = jnp.concatenate([jnp.where(causal, s[:, :tk], MASKED), s[:, tk:]], axis=1)
    m, alpha, p_buf[0] = softmax(s, m)
    value_update(t, 0, alpha)
    _, alpha_r, p_r = softmax(jnp.where(causal, s_right, MASKED), m[:, tk:])
    acc_ref[:, tk:] = alpha_r * acc_ref[:, tk:] + _dot(v_blk(t + 1), p_r)
    acc = acc_ref[...]
    o_ref[0] = (acc[:HEAD_DIM] / acc[HEAD_DIM:HEAD_DIM + 1]).T.astype(BF16)


def _attn_call(q, k, vt):
    B, S, _ = q.shape
    tq, tk = ATTN_TQ, ATTN_TK
    assert tq == 2 * tk
    return pl.pallas_call(
        functools.partial(_attn_kernel, tq=tq, tk=tk),
        grid=(B, N_HEADS, S // tq),
        in_specs=[
            pl.BlockSpec((1, tq, QK_PAD), lambda b, h, i: (b, i, h)),
            pl.BlockSpec((1, S, QK_PAD), lambda b, h, i: (b, 0, h)),
            pl.BlockSpec((1, 1, HEAD_DIM, S), lambda b, h, i: (b, h, 0, 0)),
        ],
        out_specs=pl.BlockSpec((1, tq, HEAD_DIM), lambda b, h, i: (b, i, h)),
        out_shape=jax.ShapeDtypeStruct((B, S, N_HEADS * HEAD_DIM), BF16),
        scratch_shapes=[
            pltpu.VMEM((QK_PAD, tq), BF16),
            pltpu.VMEM((2, tk, tq), F32),
            pltpu.VMEM((2, tk, tq), BF16),
            pltpu.VMEM((ACC_ROWS, tq), F32),
        ],
        compiler_params=pltpu.CompilerParams(
            dimension_semantics=("arbitrary", "arbitrary", "arbitrary"), vmem_limit_bytes=VMEM_LIMIT),
        name="mla_attn",
    )(q, k, vt)


def _mlp_kernel(x_ref, ret_ref, mla_ref, mod_ref, wo32_hbm, ln1w_ref, ln1b_ref, wup32_hbm, cw_ref, cb_ref,
                wdn32_hbm, ln2w_ref, ln2b_ref, o_ref, carry_ref, ubuf_ref, act_ref,
                wo_ref, wup_ref, wdn_ref, stage_sq_ref, stage_up_ref, sem, *, tm):
    @pl.when(jnp.logical_and(pl.program_id(0) == 0, pl.program_id(1) == 0))
    def _():
        _stream_cast(wo32_hbm.at[0], stage_sq_ref, sem, _store_bf16(wo_ref))
        _stream_cast(wup32_hbm.at[0], stage_up_ref, sem, _store_bf16(wup_ref))
        _stream_cast(wdn32_hbm.at[0], stage_sq_ref, sem, _store_bf16(wdn_ref))

    @pl.when(pl.program_id(1) == 0)
    def _():
        carry_ref[...] = jnp.zeros_like(carry_ref)

    gate1 = mod_ref[0, 2:3, :]
    shift2 = mod_ref[0, 3:4, :]
    scale2 = mod_ref[0, 4:5, :]
    gate2 = mod_ref[0, 5:6, :]

    halves = (slice(0, tm // 2), slice(tm // 2, tm))

    def out_proj(rows):
        return (_dot(ret_ref[0, rows, :], wo_ref[0:RET_WIDTH, :])
                + _dot(mla_ref[0, rows, :], wo_ref[RET_WIDTH:2 * RET_WIDTH, :]))

    def norms(y, rows):
        x1 = _ln_plain(DN_ALPHA * x_ref[0, rows, :] + (1.0 + gate1) * y) * ln1w_ref[...] + ln1b_ref[...]
        return x1, (_ln_plain(x1) * (1.0 + scale2) + shift2).astype(BF16)

    ys = [out_proj(rows) for rows in halves]
    x1s, h2s = zip(*[norms(y, rows) for y, rows in zip(ys, halves)])
    h2 = jnp.concatenate(h2s, axis=0)

    def conv(u, col0, kind):
        outs = []
        for g in range(FF_CHUNK // LANES):
            cols = slice(col0 + g * LANES, col0 + (g + 1) * LANES)
            ug = u[:, g * LANES:(g + 1) * LANES]
            buf = ubuf_ref.at[kind, g]
            buf[0:SUBLANES, :] = carry_ref[:, cols]
            buf[SUBLANES:SUBLANES + tm, :] = ug
            carry_ref[:, cols] = ug[tm - SUBLANES:tm, :]
            u1 = buf[SUBLANES - 1:SUBLANES - 1 + tm, :]
            u2 = buf[SUBLANES - 2:SUBLANES - 2 + tm, :]
            outs.append(cb_ref[:, cols] + cw_ref[2:3, cols] * ug + cw_ref[1:2, cols] * u1
                        + cw_ref[0:1, cols] * u2)
        return jnp.concatenate(outs, axis=1)

    def up(c, col0):
        w = wup_ref[:, col0:col0 + FF_CHUNK]
        if c == 0:
            return jnp.concatenate([_dot(part, w) for part in h2s], axis=0)
        return _dot(h2, w)

    for c in range(N_FF_CHUNKS):
        gcol0 = c * FF_CHUNK
        vcol0 = D_FF + c * FF_CHUNK
        g = conv(up(c, gcol0), gcol0, 0)
        val = conv(up(c, vcol0), vcol0, 1)
        act_ref[:, gcol0:gcol0 + FF_CHUNK] = (_silu(g) * val).astype(BF16)

    y2s = [_dot(act_ref[rows, :], wdn_ref[...]) for rows in halves]
    for x1, y2, rows in zip(x1s, y2s, halves):
        o_ref[0, rows, :] = _ln_plain(DN_ALPHA * x1 + (1.0 + gate2) * y2) * ln2w_ref[...] + ln2b_ref[...]


def _mlp_call(x, ret, mla, mod, w_out, ln1w, ln1b, w_up, conv_w, conv_b, w_down, ln2w, ln2b):
    B, S, D = x.shape
    tm = ROW_TILE
    row = lambda w: pl.BlockSpec((1, tm, w), lambda b, s: (b, s, 0))
    hbm = pl.BlockSpec(memory_space=pl.ANY)
    return pl.pallas_call(
        functools.partial(_mlp_kernel, tm=tm),
        grid=(B, S // tm),
        in_specs=[
            row(D), row(RET_WIDTH), row(RET_WIDTH),
            pl.BlockSpec((1, N_MOD, D), lambda b, s: (b, 0, 0)),
            hbm, _const_spec(ln1w.shape), _const_spec(ln1b.shape),
            hbm, _const_spec(conv_w.shape), _const_spec(conv_b.shape),
            hbm, _const_spec(ln2w.shape), _const_spec(ln2b.shape),
        ],
        out_specs=row(D),
        out_shape=jax.ShapeDtypeStruct((B, S, D), F32),
        scratch_shapes=[
            pltpu.VMEM((SUBLANES, 2 * D_FF), F32),
            pltpu.VMEM((2, FF_CHUNK // LANES, SUBLANES + tm, LANES), F32),
            pltpu.VMEM((tm, D_FF), BF16),
            pltpu.VMEM(w_out.shape[1:], BF16),
            pltpu.VMEM(w_up.shape[1:], BF16),
            pltpu.VMEM(w_down.shape[1:], BF16),
            pltpu.VMEM((STAGE_SLOTS, STAGE_ROWS_SQ, D), F32),
            pltpu.VMEM((STAGE_SLOTS, STAGE_ROWS_UP, 2 * D_FF), F32),
            pltpu.SemaphoreType.DMA((STAGE_SLOTS,)),
        ],
        compiler_params=pltpu.CompilerParams(
            dimension_semantics=("arbitrary", "arbitrary"), vmem_limit_bytes=VMEM_LIMIT),
        name="outproj_mlp",
    )(x, ret, mla, mod, w_out, ln1w, ln1b, w_up, conv_w, conv_b, w_down, ln2w, ln2b)


def _rope_tables(seq, half, reps):
    pos = np.arange(seq, dtype=np.float32)
    inv = np.float32(ROPE_BASE) ** (-np.arange(half, dtype=np.float32) / np.float32(half))
    ang = (pos[:, None] * inv[None, :]).astype(np.float32)
    cos = np.cos(ang).astype(np.float32)
    sin = np.sin(ang).astype(np.float32)
    cos_t = np.tile(cos, (1, 2 * reps))
    sin_t = np.concatenate([np.tile(-sin, (1, reps)), np.tile(sin, (1, reps))], axis=-1)
    return jnp.asarray(cos_t), jnp.asarray(sin_t)


def _uq_columns():
    half = MLA_ROPE_DIM // 2
    nope = [h * MLA_QK_DIM + np.arange(HEAD_DIM) for h in range(N_HEADS)]
    rope = []
    for p in range(N_HEADS // 2):
        for part in range(2):
            for h in (2 * p, 2 * p + 1):
                rope.append(h * MLA_QK_DIM + HEAD_DIM + part * half + np.arange(half))
    return np.concatenate(nope + rope)


def _take_columns(w, cols):
    cols = np.asarray(cols)
    cuts = np.flatnonzero(np.diff(cols) != 1) + 1
    runs = np.split(cols, cuts)
    return jnp.concatenate([w[:, r[0]:r[-1] + 1] for r in runs], axis=1)


def kernel(x, c, w_ada, b_ada, w_in, ret_gn_w, mla_q_norm_w, w_uq, mla_kv_norm_w, w_ukv, w_out,
           ln1_w, ln1_b, w_up, conv_w, conv_b, w_down, ln2_w, ln2_b):
    B, S, D = x.shape
    assert D == D_MODEL and S % IN_ROW_TILE == 0 and S % ROW_TILE == 0 and S % ATTN_TQ == 0
    assert w_ada.shape[0] == DEPTH == 1
    l = 0

    mod = _ada_call(c, w_ada, b_ada[l][None, :]).reshape(B, N_MOD, D)

    w_uq_p = _take_columns(w_uq[l].astype(BF16), _uq_columns())
    w_ukv_p = w_ukv[l].astype(BF16)
    cosr, sinr = _rope_tables(S, HEAD_DIM // 2, 1)
    cosm, sinm = _rope_tables(S, MLA_ROPE_DIM // 2, 2)

    ret, q, k, vt = _inproj_call(
        x, mod, jnp.swapaxes(w_in, 1, 2), ret_gn_w[l][None, :], mla_q_norm_w[l][None, :], mla_kv_norm_w[l][None, :],
        w_uq_p, w_ukv_p, cosr, sinr, cosm, sinm)
    mla = _attn_call(q, k, vt)
    return _mlp_call(
        x, ret, mla, mod, w_out, ln1_w[l][None, :], ln1_b[l][None, :],
        w_up, conv_w[l], conv_b[l][None, :], w_down, ln2_w[l][None, :], ln2_b[l][None, :])
```

```python
import functools
import math

import numpy as np
import jax
import jax.numpy as jnp
from jax import lax
from jax.experimental import pallas as pl
from jax.experimental.pallas import tpu as pltpu

F32 = jnp.float32
BF16 = jnp.bfloat16

D_MODEL = 1024
DEPTH = 1
N_HEADS = 4
HEAD_DIM = 128
RET_WIDTH = N_HEADS * HEAD_DIM
LAT_COL0 = 4 * RET_WIDTH
MLA_RANK = 256
MLA_ROPE_DIM = 64
MLA_QK_DIM = HEAD_DIM + MLA_ROPE_DIM
D_FF = 2816
ROPE_BASE = 10000.0
LN_EPS = 1e-5
RMS_EPS = 1e-6
DN_ALPHA = (2.0 * DEPTH) ** 0.25
N_MOD = 6
LOG_G = [math.log1p(-(2.0 ** (-5.0 - h))) for h in range(N_HEADS)]

LANES = 128
SUBLANES = 8
MXU_DIM = 256
VMEM_BYTES = 64 * 1024 * 1024
VMEM_LIMIT = VMEM_BYTES * 7 // 8

ADA_ROWS = 256
IN_ROW_TILE = 1024
ROW_TILE = 512
RET_CHUNK = MXU_DIM
ATTN_TQ = 2048
ATTN_TK = 1024
FF_CHUNK = MXU_DIM
N_FF_CHUNKS = D_FF // FF_CHUNK
STAGE_ROWS_SQ = 256
STAGE_ROWS_UP = 64
STAGE_ROWS_IN = 128
STAGE_SLOTS = 4
QK_PAD = 2 * LANES
BF16_ROWS = 2 * SUBLANES
ACC_ROWS = HEAD_DIM + BF16_ROWS
MASKED = -float("inf")


def _ln_plain(x):
    mu = jnp.mean(x, axis=-1, keepdims=True)
    xc = x - mu
    var = jnp.mean(xc * xc, axis=-1, keepdims=True)
    return xc * lax.rsqrt(var + LN_EPS)


def _rms(x):
    return x * lax.rsqrt(jnp.mean(x * x, axis=-1, keepdims=True) + RMS_EPS)


def _silu(x):
    return x * jax.nn.sigmoid(x)


def _dot(a, b):
    return jnp.dot(a, b, preferred_element_type=F32)


def _dot_nt(a, b):
    return lax.dot_general(a, b, (((1,), (1,)), ((), ())), preferred_element_type=F32)


def _dot_tn(a, b):
    return lax.dot_general(a, b, (((0,), (0,)), ((), ())), preferred_element_type=F32)


def _ada_kernel(c_ref, w_ref, b_ref, o_ref):
    @pl.when(pl.program_id(0) == 0)
    def _():
        o_ref[...] = jnp.broadcast_to(b_ref[...], o_ref.shape)

    cond = _silu(c_ref[...])
    o_ref[...] += _dot(cond.astype(BF16), w_ref[0].astype(BF16))


def _ada_call(c, w_ada, b_ada):
    n_batch = c.shape[0]
    _, d_in, n_out = w_ada.shape
    return pl.pallas_call(
        _ada_kernel,
        grid=(d_in // ADA_ROWS,),
        in_specs=[
            pl.BlockSpec((n_batch, ADA_ROWS), lambda j: (0, j)),
            pl.BlockSpec((1, ADA_ROWS, n_out), lambda j: (0, j, 0)),
            pl.BlockSpec((1, n_out), lambda j: (0, 0)),
        ],
        out_specs=pl.BlockSpec((n_batch, n_out), lambda j: (0, 0)),
        out_shape=jax.ShapeDtypeStruct((n_batch, n_out), F32),
        compiler_params=pltpu.CompilerParams(dimension_semantics=("arbitrary",), vmem_limit_bytes=VMEM_LIMIT),
        name="ada_mod",
    )(c, w_ada, b_ada)


def _rope128(x, cos, sin_signed):
    return x * cos + pltpu.roll(x, LANES // 2, axis=1) * sin_signed


def _decay_matrix(hd, c):
    row = lax.broadcasted_iota(jnp.int32, (c, c), 0)
    col = lax.broadcasted_iota(jnp.int32, (c, c), 1)
    rel = (row - col).astype(F32)
    return jnp.where(rel >= 0.0, jnp.exp(LOG_G[hd] * jnp.maximum(rel, 0.0)), 0.0)


def _retention_head(hd, q, k, v, gate, gnw, decay_ref, state_ref, ret_ref):
    c = decay_ref.shape[1]
    lg = LOG_G[hd]
    n = lax.broadcasted_iota(jnp.int32, (c, 1), 0).astype(F32)
    q_w = jnp.exp(lg * (n + 1.0))
    k_w = jnp.exp(lg * (c - 1.0 - n))
    for i in range(q.shape[0] // c):
        rows = slice(i * c, (i + 1) * c)
        qb = q[rows].astype(BF16)
        vb = v[rows].astype(BF16)
        scores = _dot_nt(qb, k[rows].astype(BF16)) * decay_ref[hd]
        inner = _dot(scores.astype(BF16), vb)
        state = state_ref[hd]
        cross = _dot(qb, state.astype(BF16)) * q_w
        state_ref[hd] = state * math.exp(lg * c) + _dot_tn((k[rows] * k_w).astype(BF16), vb)
        o = _ln_plain(inner + cross) * gnw
        ret_ref[0, rows, hd * HEAD_DIM:(hd + 1) * HEAD_DIM] = (o * gate[rows]).astype(BF16)


def _stream_cast(src_hbm, stage_ref, sem, store, n_rows=None):
    slots, rows = stage_ref.shape[0], stage_ref.shape[1]
    n_rows = src_hbm.shape[0] if n_rows is None else n_rows
    n_chunks = n_rows // rows
    assert n_chunks * rows == n_rows and n_chunks >= slots

    def copy(k):
        slot = k % slots
        return pltpu.make_async_copy(
            src_hbm.at[pl.ds(pl.multiple_of(k * rows, rows), rows), :], stage_ref.at[slot], sem.at[slot])

    for k in range(slots - 1):
        copy(k).start()

    def body(k, _):
        @pl.when(k + slots - 1 < n_chunks)
        def _():
            copy(k + slots - 1).start()

        copy(k).wait()
        store(pl.ds(pl.multiple_of(k * rows, rows), rows), stage_ref[k % slots])
        return 0

    lax.fori_loop(0, n_chunks, body, 0)


def _store_bf16(dst_ref):
    def store(rows, chunk):
        dst_ref[rows, :] = chunk.astype(BF16)
    return store


def _stage_w_in(wt_hbm, win_ref, stage_ref, sem):
    base = LAT_COL0 + 2 * MLA_RANK
    quarter = LANES // 4

    def store(cols, chunk):
        win_ref[:, cols] = chunk.T.astype(BF16)

    _stream_cast(wt_hbm, stage_ref, sem, store, n_rows=base)

    tail_copy = pltpu.make_async_copy(
        wt_hbm.at[pl.ds(base, MLA_ROPE_DIM), :], stage_ref.at[0, pl.ds(0, MLA_ROPE_DIM), :], sem.at[0])
    tail_copy.start()
    tail_copy.wait()
    tail = stage_ref[0, 0:MLA_ROPE_DIM, :].T
    t2 = jnp.concatenate([tail, tail], axis=1)
    lane = lax.broadcasted_iota(jnp.int32, t2.shape, 1)
    middle = jnp.abs(2 * lane - (LANES - 1)) < 2 * quarter
    win_ref[:, base:base + LANES] = jnp.where(middle, pltpu.roll(t2, quarter, axis=1), t2).astype(BF16)


def _inproj_kernel(x_ref, mod_ref, wint_hbm, gnw_ref, qnw_ref, kvnw_ref, wuq_ref, wukv_ref,
                   cosr_ref, sinr_ref, cosm_ref, sinm_ref,
                   ret_ref, q_ref, k_ref, vt_ref, state_ref, decay_ref, win_ref, stage_ref, sem,
                   *, q_scale, rk_scale):
    @pl.when(jnp.logical_and(pl.program_id(0) == 0, pl.program_id(1) == 0))
    def _():
        _stage_w_in(wint_hbm.at[0], win_ref, stage_ref, sem)
        for hd in range(N_HEADS):
            decay_ref[hd] = _decay_matrix(hd, decay_ref.shape[1])

    @pl.when(pl.program_id(1) == 0)
    def _():
        state_ref[...] = jnp.zeros_like(state_ref)

    shift = mod_ref[0, 0:1, :]
    scale = mod_ref[0, 1:2, :]
    tm = x_ref.shape[1]
    hs, pqs = [], []
    for rows in (slice(0, tm // 2), slice(tm // 2, tm)):
        hs.append((_ln_plain(x_ref[0, rows, :]) * (1.0 + scale) + shift).astype(BF16))
        pqs.append(_dot(hs[-1], win_ref[:, 0:RET_WIDTH]))
    h = jnp.concatenate(hs, axis=0)

    cosr = cosr_ref[...]
    sinr = sinr_ref[...]
    cosm = cosm_ref[...]
    sinm = sinm_ref[...]

    pq = jnp.concatenate(pqs, axis=0)
    pk = _dot(h, win_ref[:, RET_WIDTH:2 * RET_WIDTH])
    pv = _dot(h, win_ref[:, 2 * RET_WIDTH:3 * RET_WIDTH])
    pg = _silu(_dot(h, win_ref[:, 3 * RET_WIDTH:LAT_COL0]))
    for hd in range(N_HEADS):
        sl = slice(hd * HEAD_DIM, (hd + 1) * HEAD_DIM)
        _retention_head(hd, _rope128(pq[:, sl], cosr, sinr), _rope128(pk[:, sl], cosr, sinr) * rk_scale,
                        pv[:, sl], pg[:, sl], gnw_ref[:, sl], decay_ref, state_ref, ret_ref)

    lat = _dot(h, win_ref[:, LAT_COL0:])
    cq = (_rms(lat[:, 0:MLA_RANK]) * qnw_ref[...]).astype(BF16)
    ckv = (_rms(lat[:, MLA_RANK:2 * MLA_RANK]) * kvnw_ref[...]).astype(BF16)
    kr = _rope128(lat[:, 2 * MLA_RANK:2 * MLA_RANK + LANES], cosm, sinm)
    lane = lax.broadcasted_iota(jnp.int32, kr.shape, 1)
    first_of_pair = (lane % (LANES // 2)) < (LANES // 4)
    kr_even = jnp.where(first_of_pair, kr, 0.0).astype(BF16)
    kr_odd = jnp.where(first_of_pair, 0.0, kr).astype(BF16)

    qf = _dot(cq, wuq_ref[...])
    kvf = _dot(ckv, wukv_ref[...])
    for hd in range(N_HEADS):
        vt_ref[0, hd] = kvf[:, (2 * hd + 1) * HEAD_DIM:(2 * hd + 2) * HEAD_DIM].T.astype(BF16)
    for p in range(N_HEADS // 2):
        qr = qf[:, RET_WIDTH + p * LANES:RET_WIDTH + (p + 1) * LANES]
        qr = (_rope128(qr, cosm, sinm) * q_scale).astype(BF16)
        for hd in (2 * p, 2 * p + 1):
            sl = slice(hd * HEAD_DIM, (hd + 1) * HEAD_DIM)
            q_ref[0, :, hd * QK_PAD:hd * QK_PAD + LANES] = (qf[:, sl] * q_scale).astype(BF16)
            q_ref[0, :, hd * QK_PAD + LANES:(hd + 1) * QK_PAD] = qr
            k_ref[0, :, hd * QK_PAD:hd * QK_PAD + LANES] = kvf[:, 2 * hd * HEAD_DIM:(2 * hd + 1) * HEAD_DIM].astype(BF16)
            k_ref[0, :, hd * QK_PAD + LANES:(hd + 1) * QK_PAD] = kr_even if hd % 2 == 0 else kr_odd


def _const_spec(shape):
    nd = len(shape)
    return pl.BlockSpec(shape, lambda *_: (0,) * nd, pipeline_mode=pl.Buffered(1))


def _inproj_call(x, mod, w_in, gnw, qnw, kvnw, w_uq, w_ukv, cosr, sinr, cosm, sinm):
    B, S, D = x.shape
    tm = IN_ROW_TILE
    q_scale = (MLA_QK_DIM ** -0.5) * math.log2(math.e)
    rk_scale = HEAD_DIM ** -0.5
    row = lambda w: pl.BlockSpec((1, tm, w), lambda b, s: (b, s, 0))
    tab = pl.BlockSpec((tm, LANES), lambda b, s: (s, 0))
    out_shapes = (
        [jax.ShapeDtypeStruct((B, S, RET_WIDTH), BF16)]
        + [jax.ShapeDtypeStruct((B, S, N_HEADS * QK_PAD), BF16)] * 2
        + [jax.ShapeDtypeStruct((B, N_HEADS, HEAD_DIM, S), BF16)]
    )
    vt_spec = pl.BlockSpec((1, N_HEADS, HEAD_DIM, tm), lambda b, s: (b, 0, 0, s))
    return pl.pallas_call(
        functools.partial(_inproj_kernel, q_scale=q_scale, rk_scale=rk_scale),
        grid=(B, S // tm),
        in_specs=[
            row(D),
            pl.BlockSpec((1, N_MOD, D), lambda b, s: (b, 0, 0)),
            pl.BlockSpec(memory_space=pl.ANY), _const_spec(gnw.shape),
            _const_spec(qnw.shape), _const_spec(kvnw.shape),
            _const_spec(w_uq.shape), _const_spec(w_ukv.shape),
            tab, tab, tab, tab,
        ],
        out_specs=[row(RET_WIDTH)] + [row(N_HEADS * QK_PAD)] * 2 + [vt_spec],
        out_shape=out_shapes,
        scratch_shapes=[
            pltpu.VMEM((N_HEADS, HEAD_DIM, HEAD_DIM), F32),
            pltpu.VMEM((N_HEADS, RET_CHUNK, RET_CHUNK), F32),
            pltpu.VMEM((D, LAT_COL0 + 2 * MLA_RANK + LANES), BF16),
            pltpu.VMEM((STAGE_SLOTS, STAGE_ROWS_IN, w_in.shape[-1]), F32),
            pltpu.SemaphoreType.DMA((STAGE_SLOTS,)),
        ],
        compiler_params=pltpu.CompilerParams(
            dimension_semantics=("arbitrary", "arbitrary"), vmem_limit_bytes=VMEM_LIMIT),
        name="inproj_retention",
    )(x, mod, w_in, gnw, qnw, kvnw, w_uq, w_ukv, cosr, sinr, cosm, sinm)


def _attn_kernel(q_ref, k_ref, vt_ref, o_ref, qt_ref, s_buf, p_buf, acc_ref, *, tq, tk):
    qi = pl.program_id(2)
    qt_ref[...] = q_ref[0].T

    ones = jnp.ones((BF16_ROWS, tk), BF16)

    def k_blk(j):
        return k_ref[0, pl.ds(pl.multiple_of(j * tk, tk), tk), :]

    def v_blk(j):
        return jnp.concatenate([vt_ref[0, 0, :, pl.ds(pl.multiple_of(j * tk, tk), tk)], ones], axis=0)

    def scores(j, slot):
        s = _dot(k_blk(j), qt_ref[...])
        s_buf[slot] = s
        return jnp.max(s, axis=0, keepdims=True)

    def value_update(j, slot, alpha):
        acc_ref[...] = alpha * acc_ref[...] + _dot(v_blk(j), p_buf[slot])

    def softmax(s, m, block_max=None):
        if block_max is None:
            block_max = jnp.max(s, axis=0, keepdims=True)
        m_new = jnp.maximum(m, block_max)
        return m_new, jnp.exp2(m - m_new), jnp.exp2(s - m_new).astype(BF16)

    def two_blocks(u, carry):
        alpha_prev, m, max_even = carry
        t = 2 * u
        max_odd = scores(t + 1, 1)
        value_update(jnp.maximum(t - 1, 0), 1, alpha_prev)
        m, alpha_even, p_buf[0] = softmax(s_buf[0], m, max_even)
        max_next = scores(t + 2, 0)
        value_update(t, 0, alpha_even)
        m, alpha_odd, p_buf[1] = softmax(s_buf[1], m, max_odd)
        return alpha_odd, m, max_next

    max_0 = scores(0, 0)
    p_buf[1] = jnp.zeros((tk, tq), BF16)
    acc_ref[...] = jnp.zeros_like(acc_ref)
    init = (jnp.ones((1, tq), F32), jnp.full((1, tq), MASKED, F32), max_0)
    alpha_prev, m, _ = lax.fori_loop(0, qi, two_blocks, init)

    t = 2 * qi
    key = lax.broadcasted_iota(jnp.int32, (tk, tk), 0)
    qry = lax.broadcasted_iota(jnp.int32, (tk, tk), 1)
    causal = key <= qry
    s_right = _dot(k_blk(t + 1), qt_ref[:, tk:])
    value_update(jnp.maximum(t - 1, 0), 1, alpha_prev)
    s = s_buf[0]
    s = jnp.concatenate([jnp.where(causal, s[:, :tk], MASKED), s[:, tk:]], axis=1)
    m, alpha, p_buf[0] = softmax(s, m)
    value_update(t, 0, alpha)
    _, alpha_r, p_r = softmax(jnp.where(causal, s_right, MASKED), m[:, tk:])
    acc_ref[:, tk:] = alpha_r * acc_ref[:, tk:] + _dot(v_blk(t + 1), p_r)
    acc = acc_ref[...]
    o_ref[0] = (acc[:HEAD_DIM] / acc[HEAD_DIM:HEAD_DIM + 1]).T.astype(BF16)


def _attn_call(q, k, vt):
    B, S, _ = q.shape
    tq, tk = ATTN_TQ, ATTN_TK
    assert tq == 2 * tk
    return pl.pallas_call(
        functools.partial(_attn_kernel, tq=tq, tk=tk),
        grid=(B, N_HEADS, S // tq),
        in_specs=[
            pl.BlockSpec((1, tq, QK_PAD), lambda b, h, i: (b, i, h)),
            pl.BlockSpec((1, S, QK_PAD), lambda b, h, i: (b, 0, h)),
            pl.BlockSpec((1, 1, HEAD_DIM, S), lambda b, h, i: (b, h, 0, 0)),
        ],
        out_specs=pl.BlockSpec((1, tq, HEAD_DIM), lambda b, h, i: (b, i, h)),
        out_shape=jax.ShapeDtypeStruct((B, S, N_HEADS * HEAD_DIM), BF16),
        scratch_shapes=[
            pltpu.VMEM((QK_PAD, tq), BF16),
            pltpu.VMEM((2, tk, tq), F32),
            pltpu.VMEM((2, tk, tq), BF16),
            pltpu.VMEM((ACC_ROWS, tq), F32),
        ],
        compiler_params=pltpu.CompilerParams(
            dimension_semantics=("arbitrary", "arbitrary", "arbitrary"), vmem_limit_bytes=VMEM_LIMIT),
        name="mla_attn",
    )(q, k, vt)


def _mlp_kernel(x_ref, ret_ref, mla_ref, mod_ref, wo32_hbm, ln1w_ref, ln1b_ref, wup32_hbm, cw_ref, cb_ref,
                wdn32_hbm, ln2w_ref, ln2b_ref, o_ref, carry_ref, ubuf_ref, act_ref,
                wo_ref, wup_ref, wdn_ref, stage_sq_ref, stage_up_ref, sem, *, tm):
    @pl.when(jnp.logical_and(pl.program_id(0) == 0, pl.program_id(1) == 0))
    def _():
        _stream_cast(wo32_hbm.at[0], stage_sq_ref, sem, _store_bf16(wo_ref))
        _stream_cast(wup32_hbm.at[0], stage_up_ref, sem, _store_bf16(wup_ref))
        _stream_cast(wdn32_hbm.at[0], stage_sq_ref, sem, _store_bf16(wdn_ref))

    @pl.when(pl.program_id(1) == 0)
    def _():
        carry_ref[...] = jnp.zeros_like(carry_ref)

    gate1 = mod_ref[0, 2:3, :]
    shift2 = mod_ref[0, 3:4, :]
    scale2 = mod_ref[0, 4:5, :]
    gate2 = mod_ref[0, 5:6, :]

    halves = (slice(0, tm // 2), slice(tm // 2, tm))

    def out_proj(rows):
        return (_dot(ret_ref[0, rows, :], wo_ref[0:RET_WIDTH, :])
                + _dot(mla_ref[0, rows, :], wo_ref[RET_WIDTH:2 * RET_WIDTH, :]))

    def norms(y, rows):
        x1 = _ln_plain(DN_ALPHA * x_ref[0, rows, :] + (1.0 + gate1) * y) * ln1w_ref[...] + ln1b_ref[...]
        return x1, (_ln_plain(x1) * (1.0 + scale2) + shift2).astype(BF16)

    ys = [out_proj(rows) for rows in halves]
    x1s, h2s = zip(*[norms(y, rows) for y, rows in zip(ys, halves)])
    h2 = jnp.concatenate(h2s, axis=0)

    def conv(u, col0, kind):
        outs = []
        for g in range(FF_CHUNK // LANES):
            cols = slice(col0 + g * LANES, col0 + (g + 1) * LANES)
            ug = u[:, g * LANES:(g + 1) * LANES]
            buf = ubuf_ref.at[kind, g]
            buf[0:SUBLANES, :] = carry_ref[:, cols]
            buf[SUBLANES:SUBLANES + tm, :] = ug
            carry_ref[:, cols] = ug[tm - SUBLANES:tm, :]
            u1 = buf[SUBLANES - 1:SUBLANES - 1 + tm, :]
            u2 = buf[SUBLANES - 2:SUBLANES - 2 + tm, :]
            outs.append(cb_ref[:, cols] + cw_ref[2:3, cols] * ug + cw_ref[1:2, cols] * u1
                        + cw_ref[0:1, cols] * u2)
        return jnp.concatenate(outs, axis=1)

    for c in range(N_FF_CHUNKS):
        gcol0 = c * FF_CHUNK
        vcol0 = D_FF + c * FF_CHUNK
        g = conv(_dot(h2, wup_ref[:, gcol0:gcol0 + FF_CHUNK]), gcol0, 0)
        val = conv(_dot(h2, wup_ref[:, vcol0:vcol0 + FF_CHUNK]), vcol0, 1)
        act_ref[:, gcol0:gcol0 + FF_CHUNK] = (_silu(g) * val).astype(BF16)

    y2s = [_dot(act_ref[rows, :], wdn_ref[...]) for rows in halves]
    for x1, y2, rows in zip(x1s, y2s, halves):
        o_ref[0, rows, :] = _ln_plain(DN_ALPHA * x1 + (1.0 + gate2) * y2) * ln2w_ref[...] + ln2b_ref[...]


def _mlp_call(x, ret, mla, mod, w_out, ln1w, ln1b, w_up, conv_w, conv_b, w_down, ln2w, ln2b):
    B, S, D = x.shape
    tm = ROW_TILE
    row = lambda w: pl.BlockSpec((1, tm, w), lambda b, s: (b, s, 0))
    hbm = pl.BlockSpec(memory_space=pl.ANY)
    return pl.pallas_call(
        functools.partial(_mlp_kernel, tm=tm),
        grid=(B, S // tm),
        in_specs=[
            row(D), row(RET_WIDTH), row(RET_WIDTH),
            pl.BlockSpec((1, N_MOD, D), lambda b, s: (b, 0, 0)),
            hbm, _const_spec(ln1w.shape), _const_spec(ln1b.shape),
            hbm, _const_spec(conv_w.shape), _const_spec(conv_b.shape),
            hbm, _const_spec(ln2w.shape), _const_spec(ln2b.shape),
        ],
        out_specs=row(D),
        out_shape=jax.ShapeDtypeStruct((B, S, D), F32),
        scratch_shapes=[
            pltpu.VMEM((SUBLANES, 2 * D_FF), F32),
            pltpu.VMEM((2, FF_CHUNK // LANES, SUBLANES + tm, LANES), F32),
            pltpu.VMEM((tm, D_FF), BF16),
            pltpu.VMEM(w_out.shape[1:], BF16),
            pltpu.VMEM(w_up.shape[1:], BF16),
            pltpu.VMEM(w_down.shape[1:], BF16),
            pltpu.VMEM((STAGE_SLOTS, STAGE_ROWS_SQ, D), F32),
            pltpu.VMEM((STAGE_SLOTS, STAGE_ROWS_UP, 2 * D_FF), F32),
            pltpu.SemaphoreType.DMA((STAGE_SLOTS,)),
        ],
        compiler_params=pltpu.CompilerParams(
            dimension_semantics=("arbitrary", "arbitrary"), vmem_limit_bytes=VMEM_LIMIT),
        name="outproj_mlp",
    )(x, ret, mla, mod, w_out, ln1w, ln1b, w_up, conv_w, conv_b, w_down, ln2w, ln2b)


def _rope_tables(seq, half, reps):
    pos = np.arange(seq, dtype=np.float32)
    inv = np.float32(ROPE_BASE) ** (-np.arange(half, dtype=np.float32) / np.float32(half))
    ang = (pos[:, None] * inv[None, :]).astype(np.float32)
    cos = np.cos(ang).astype(np.float32)
    sin = np.sin(ang).astype(np.float32)
    cos_t = np.tile(cos, (1, 2 * reps))
    sin_t = np.concatenate([np.tile(-sin, (1, reps)), np.tile(sin, (1, reps))], axis=-1)
    return jnp.asarray(cos_t), jnp.asarray(sin_t)


def _uq_columns():
    half = MLA_ROPE_DIM // 2
    nope = [h * MLA_QK_DIM + np.arange(HEAD_DIM) for h in range(N_HEADS)]
    rope = []
    for p in range(N_HEADS // 2):
        for part in range(2):
            for h in (2 * p, 2 * p + 1):
                rope.append(h * MLA_QK_DIM + HEAD_DIM + part * half + np.arange(half))
    return np.concatenate(nope + rope)


def _take_columns(w, cols):
    cols = np.asarray(cols)
    cuts = np.flatnonzero(np.diff(cols) != 1) + 1
    runs = np.split(cols, cuts)
    return jnp.concatenate([w[:, r[0]:r[-1] + 1] for r in runs], axis=1)


def kernel(x, c, w_ada, b_ada, w_in, ret_gn_w, mla_q_norm_w, w_uq, mla_kv_norm_w, w_ukv, w_out,
           ln1_w, ln1_b, w_up, conv_w, conv_b, w_down, ln2_w, ln2_b):
    B, S, D = x.shape
    assert D == D_MODEL and S % IN_ROW_TILE == 0 and S % ROW_TILE == 0 and S % ATTN_TQ == 0
    assert w_ada.shape[0] == DEPTH == 1
    l = 0

    mod = _ada_call(c, w_ada, b_ada[l][None, :]).reshape(B, N_MOD, D)

    w_uq_p = _take_columns(w_uq[l].astype(BF16), _uq_columns())
    w_ukv_p = w_ukv[l].astype(BF16)
    cosr, sinr = _rope_tables(S, HEAD_DIM // 2, 1)
    cosm, sinm = _rope_tables(S, MLA_ROPE_DIM // 2, 2)

    ret, q, k, vt = _inproj_call(
        x, mod, jnp.swapaxes(w_in, 1, 2), ret_gn_w[l][None, :], mla_q_norm_w[l][None, :], mla_kv_norm_w[l][None, :],
        w_uq_p, w_ukv_p, cosr, sinr, cosm, sinm)
    mla = _attn_call(q, k, vt)
    return _mlp_call(
        x, ret, mla, mod, w_out, ln1_w[l][None, :], ln1_b[l][None, :],
        w_up, conv_w[l], conv_b[l][None, :], w_down, ln2_w[l][None, :], ln2_b[l][None, :])
```

```python
import functools
import math

import numpy as np
import jax
import jax.numpy as jnp
from jax import lax
from jax.experimental import pallas as pl
from jax.experimental.pallas import tpu as pltpu

F32 = jnp.float32
BF16 = jnp.bfloat16

D_MODEL = 1024
DEPTH = 1
N_HEADS = 4
HEAD_DIM = 128
RET_WIDTH = N_HEADS * HEAD_DIM
LAT_COL0 = 4 * RET_WIDTH
MLA_RANK = 256
MLA_ROPE_DIM = 64
MLA_QK_DIM = HEAD_DIM + MLA_ROPE_DIM
D_FF = 2816
ROPE_BASE = 10000.0
LN_EPS = 1e-5
RMS_EPS = 1e-6
DN_ALPHA = (2.0 * DEPTH) ** 0.25
N_MOD = 6
LOG_G = [math.log1p(-(2.0 ** (-5.0 - h))) for h in range(N_HEADS)]

LANES = 128
SUBLANES = 8
MXU_DIM = 256
VMEM_BYTES = 64 * 1024 * 1024
VMEM_LIMIT = VMEM_BYTES * 7 // 8

ADA_ROWS = 256
IN_ROW_TILE = 1024
ROW_TILE = 512
RET_CHUNK = MXU_DIM
ATTN_TQ = 2048
ATTN_TK = 1024
FF_CHUNK = MXU_DIM
N_FF_CHUNKS = D_FF // FF_CHUNK
STAGE_ROWS_SQ = 256
STAGE_ROWS_UP = 64
STAGE_ROWS_IN = 128
STAGE_SLOTS = 4
QK_PAD = 2 * LANES
BF16_ROWS = 2 * SUBLANES
ACC_ROWS = HEAD_DIM + BF16_ROWS
MASKED = -float("inf")


def _ln_plain(x):
    mu = jnp.mean(x, axis=-1, keepdims=True)
    xc = x - mu
    var = jnp.mean(xc * xc, axis=-1, keepdims=True)
    return xc * lax.rsqrt(var + LN_EPS)


def _rms(x):
    return x * lax.rsqrt(jnp.mean(x * x, axis=-1, keepdims=True) + RMS_EPS)


def _silu(x):
    return x * jax.nn.sigmoid(x)


def _dot(a, b):
    return jnp.dot(a, b, preferred_element_type=F32)


def _dot_nt(a, b):
    return lax.dot_general(a, b, (((1,), (1,)), ((), ())), preferred_element_type=F32)


def _dot_tn(a, b):
    return lax.dot_general(a, b, (((0,), (0,)), ((), ())), preferred_element_type=F32)


def _ada_kernel(c_ref, w_ref, b_ref, o_ref):
    @pl.when(pl.program_id(0) == 0)
    def _():
        o_ref[...] = jnp.broadcast_to(b_ref[...], o_ref.shape)

    cond = _silu(c_ref[...])
    o_ref[...] += _dot(cond.astype(BF16), w_ref[0].astype(BF16))


def _ada_call(c, w_ada, b_ada):
    n_batch = c.shape[0]
    _, d_in, n_out = w_ada.shape
    return pl.pallas_call(
        _ada_kernel,
        grid=(d_in // ADA_ROWS,),
        in_specs=[
            pl.BlockSpec((n_batch, ADA_ROWS), lambda j: (0, j)),
            pl.BlockSpec((1, ADA_ROWS, n_out), lambda j: (0, j, 0)),
            pl.BlockSpec((1, n_out), lambda j: (0, 0)),
        ],
        out_specs=pl.BlockSpec((n_batch, n_out), lambda j: (0, 0)),
        out_shape=jax.ShapeDtypeStruct((n_batch, n_out), F32),
        compiler_params=pltpu.CompilerParams(dimension_semantics=("arbitrary",), vmem_limit_bytes=VMEM_LIMIT),
        name="ada_mod",
    )(c, w_ada, b_ada)


def _rope128(x, cos, sin_signed):
    return x * cos + pltpu.roll(x, LANES // 2, axis=1) * sin_signed


def _decay_matrix(hd, c):
    row = lax.broadcasted_iota(jnp.int32, (c, c), 0)
    col = lax.broadcasted_iota(jnp.int32, (c, c), 1)
    rel = (row - col).astype(F32)
    return jnp.where(rel >= 0.0, jnp.exp(LOG_G[hd] * jnp.maximum(rel, 0.0)), 0.0)


def _retention_head(hd, q, k, v, gate, gnw, decay_ref, state_ref, ret_ref):
    c = decay_ref.shape[1]
    lg = LOG_G[hd]
    n = lax.broadcasted_iota(jnp.int32, (c, 1), 0).astype(F32)
    q_w = jnp.exp(lg * (n + 1.0))
    k_w = jnp.exp(lg * (c - 1.0 - n))
    for i in range(q.shape[0] // c):
        rows = slice(i * c, (i + 1) * c)
        qb = q[rows].astype(BF16)
        vb = v[rows].astype(BF16)
        scores = _dot_nt(qb, k[rows].astype(BF16)) * decay_ref[hd]
        inner = _dot(scores.astype(BF16), vb)
        state = state_ref[hd]
        cross = _dot(qb, state.astype(BF16)) * q_w
        state_ref[hd] = state * math.exp(lg * c) + _dot_tn((k[rows] * k_w).astype(BF16), vb)
        o = _ln_plain(inner + cross) * gnw
        ret_ref[0, rows, hd * HEAD_DIM:(hd + 1) * HEAD_DIM] = (o * gate[rows]).astype(BF16)


def _stream_cast(src_hbm, stage_ref, sem, store, n_rows=None):
    slots, rows = stage_ref.shape[0], stage_ref.shape[1]
    n_rows = src_hbm.shape[0] if n_rows is None else n_rows
    n_chunks = n_rows // rows
    assert n_chunks * rows == n_rows and n_chunks >= slots

    def copy(k):
        slot = k % slots
        return pltpu.make_async_copy(
            src_hbm.at[pl.ds(pl.multiple_of(k * rows, rows), rows), :], stage_ref.at[slot], sem.at[slot])

    for k in range(slots - 1):
        copy(k).start()

    def body(k, _):
        @pl.when(k + slots - 1 < n_chunks)
        def _():
            copy(k + slots - 1).start()

        copy(k).wait()
        store(pl.ds(pl.multiple_of(k * rows, rows), rows), stage_ref[k % slots])
        return 0

    lax.fori_loop(0, n_chunks, body, 0)


def _store_bf16(dst_ref):
    def store(rows, chunk):
        dst_ref[rows, :] = chunk.astype(BF16)
    return store


def _stage_w_in(wt_hbm, win_ref, stage_ref, sem):
    base = LAT_COL0 + 2 * MLA_RANK
    quarter = LANES // 4

    def store(cols, chunk):
        win_ref[:, cols] = chunk.T.astype(BF16)

    _stream_cast(wt_hbm, stage_ref, sem, store, n_rows=base)

    tail_copy = pltpu.make_async_copy(
        wt_hbm.at[pl.ds(base, MLA_ROPE_DIM), :], stage_ref.at[0, pl.ds(0, MLA_ROPE_DIM), :], sem.at[0])
    tail_copy.start()
    tail_copy.wait()
    tail = stage_ref[0, 0:MLA_ROPE_DIM, :].T
    t2 = jnp.concatenate([tail, tail], axis=1)
    lane = lax.broadcasted_iota(jnp.int32, t2.shape, 1)
    middle = jnp.abs(2 * lane - (LANES - 1)) < 2 * quarter
    win_ref[:, base:base + LANES] = jnp.where(middle, pltpu.roll(t2, quarter, axis=1), t2).astype(BF16)


def _inproj_kernel(x_ref, mod_ref, wint_hbm, gnw_ref, qnw_ref, kvnw_ref, wuq_ref, wukv_ref,
                   cosr_ref, sinr_ref, cosm_ref, sinm_ref,
                   ret_ref, q_ref, k_ref, vt_ref, state_ref, decay_ref, win_ref, stage_ref, sem,
                   *, q_scale, rk_scale):
    @pl.when(jnp.logical_and(pl.program_id(0) == 0, pl.program_id(1) == 0))
    def _():
        _stage_w_in(wint_hbm.at[0], win_ref, stage_ref, sem)
        for hd in range(N_HEADS):
            decay_ref[hd] = _decay_matrix(hd, decay_ref.shape[1])

    @pl.when(pl.program_id(1) == 0)
    def _():
        state_ref[...] = jnp.zeros_like(state_ref)

    shift = mod_ref[0, 0:1, :]
    scale = mod_ref[0, 1:2, :]
    tm = x_ref.shape[1]
    hs, pqs = [], []
    for rows in (slice(0, tm // 2), slice(tm // 2, tm)):
        hs.append((_ln_plain(x_ref[0, rows, :]) * (1.0 + scale) + shift).astype(BF16))
        pqs.append(_dot(hs[-1], win_ref[:, 0:RET_WIDTH]))
    h = jnp.concatenate(hs, axis=0)

    cosr = cosr_ref[...]
    sinr = sinr_ref[...]
    cosm = cosm_ref[...]
    sinm = sinm_ref[...]

    pq = jnp.concatenate(pqs, axis=0)
    pk = _dot(h, win_ref[:, RET_WIDTH:2 * RET_WIDTH])
    pv = _dot(h, win_ref[:, 2 * RET_WIDTH:3 * RET_WIDTH])
    pg = _silu(_dot(h, win_ref[:, 3 * RET_WIDTH:LAT_COL0]))
    for hd in range(N_HEADS):
        sl = slice(hd * HEAD_DIM, (hd + 1) * HEAD_DIM)
        _retention_head(hd, _rope128(pq[:, sl], cosr, sinr), _rope128(pk[:, sl], cosr, sinr) * rk_scale,
                        pv[:, sl], pg[:, sl], gnw_ref[:, sl], decay_ref, state_ref, ret_ref)

    qfs, kvfs, krs = [], [], []
    for hh in hs:
        lat = _dot(hh, win_ref[:, LAT_COL0:])
        cq = (_rms(lat[:, 0:MLA_RANK]) * qnw_ref[...]).astype(BF16)
        ckv = (_rms(lat[:, MLA_RANK:2 * MLA_RANK]) * kvnw_ref[...]).astype(BF16)
        krs.append(lat[:, 2 * MLA_RANK:2 * MLA_RANK + LANES])
        qfs.append(_dot(cq, wuq_ref[...]))
        kvfs.append(_dot(ckv, wukv_ref[...]))
    kr = _rope128(jnp.concatenate(krs, axis=0), cosm, sinm)
    lane = lax.broadcasted_iota(jnp.int32, kr.shape, 1)
    first_of_pair = (lane % (LANES // 2)) < (LANES // 4)
    kr_even = jnp.where(first_of_pair, kr, 0.0).astype(BF16)
    kr_odd = jnp.where(first_of_pair, 0.0, kr).astype(BF16)

    qf = jnp.concatenate(qfs, axis=0)
    kvf = jnp.concatenate(kvfs, axis=0)
    for hd in range(N_HEADS):
        vt_ref[0, hd] = kvf[:, (2 * hd + 1) * HEAD_DIM:(2 * hd + 2) * HEAD_DIM].T.astype(BF16)
    for p in range(N_HEADS // 2):
        qr = qf[:, RET_WIDTH + p * LANES:RET_WIDTH + (p + 1) * LANES]
        qr = (_rope128(qr, cosm, sinm) * q_scale).astype(BF16)
        for hd in (2 * p, 2 * p + 1):
            sl = slice(hd * HEAD_DIM, (hd + 1) * HEAD_DIM)
            q_ref[0, :, hd * QK_PAD:hd * QK_PAD + LANES] = (qf[:, sl] * q_scale).astype(BF16)
            q_ref[0, :, hd * QK_PAD + LANES:(hd + 1) * QK_PAD] = qr
            k_ref[0, :, hd * QK_PAD:hd * QK_PAD + LANES] = kvf[:, 2 * hd * HEAD_DIM:(2 * hd + 1) * HEAD_DIM].astype(BF16)
            k_ref[0, :, hd * QK_PAD + LANES:(hd + 1) * QK_PAD] = kr_even if hd % 2 == 0 else kr_odd


def _const_spec(shape):
    nd = len(shape)
    return pl.BlockSpec(shape, lambda *_: (0,) * nd, pipeline_mode=pl.Buffered(1))


def _inproj_call(x, mod, w_in, gnw, qnw, kvnw, w_uq, w_ukv, cosr, sinr, cosm, sinm):
    B, S, D = x.shape
    tm = IN_ROW_TILE
    q_scale = (MLA_QK_DIM ** -0.5) * math.log2(math.e)
    rk_scale = HEAD_DIM ** -0.5
    row = lambda w: pl.BlockSpec((1, tm, w), lambda b, s: (b, s, 0))
    tab = pl.BlockSpec((tm, LANES), lambda b, s: (s, 0))
    out_shapes = (
        [jax.ShapeDtypeStruct((B, S, RET_WIDTH), BF16)]
        + [jax.ShapeDtypeStruct((B, S, N_HEADS * QK_PAD), BF16)] * 2
        + [jax.ShapeDtypeStruct((B, N_HEADS, HEAD_DIM, S), BF16)]
    )
    vt_spec = pl.BlockSpec((1, N_HEADS, HEAD_DIM, tm), lambda b, s: (b, 0, 0, s))
    return pl.pallas_call(
        functools.partial(_inproj_kernel, q_scale=q_scale, rk_scale=rk_scale),
        grid=(B, S // tm),
        in_specs=[
            row(D),
            pl.BlockSpec((1, N_MOD, D), lambda b, s: (b, 0, 0)),
            pl.BlockSpec(memory_space=pl.ANY), _const_spec(gnw.shape),
            _const_spec(qnw.shape), _const_spec(kvnw.shape),
            _const_spec(w_uq.shape), _const_spec(w_ukv.shape),
            tab, tab, tab, tab,
        ],
        out_specs=[row(RET_WIDTH)] + [row(N_HEADS * QK_PAD)] * 2 + [vt_spec],
        out_shape=out_shapes,
        scratch_shapes=[
            pltpu.VMEM((N_HEADS, HEAD_DIM, HEAD_DIM), F32),
            pltpu.VMEM((N_HEADS, RET_CHUNK, RET_CHUNK), F32),
            pltpu.VMEM((D, LAT_COL0 + 2 * MLA_RANK + LANES), BF16),
            pltpu.VMEM((STAGE_SLOTS, STAGE_ROWS_IN, w_in.shape[-1]), F32),
            pltpu.SemaphoreType.DMA((STAGE_SLOTS,)),
        ],
        compiler_params=pltpu.CompilerParams(
            dimension_semantics=("arbitrary", "arbitrary"), vmem_limit_bytes=VMEM_LIMIT),
        name="inproj_retention",
    )(x, mod, w_in, gnw, qnw, kvnw, w_uq, w_ukv, cosr, sinr, cosm, sinm)


def _attn_kernel(q_ref, k_ref, vt_ref, o_ref, qt_ref, s_buf, p_buf, acc_ref, *, tq, tk):
    qi = pl.program_id(2)
    qt_ref[...] = q_ref[0].T

    ones = jnp.ones((BF16_ROWS, tk), BF16)

    def k_blk(j):
        return k_ref[0, pl.ds(pl.multiple_of(j * tk, tk), tk), :]

    def v_blk(j):
        return jnp.concatenate([vt_ref[0, 0, :, pl.ds(pl.multiple_of(j * tk, tk), tk)], ones], axis=0)

    def scores(j, slot):
        s = _dot(k_blk(j), qt_ref[...])
        s_buf[slot] = s
        return jnp.max(s, axis=0, keepdims=True)

    def value_update(j, slot, alpha):
        acc_ref[...] = alpha * acc_ref[...] + _dot(v_blk(j), p_buf[slot])

    def softmax(s, m, block_max=None):
        if block_max is None:
            block_max = jnp.max(s, axis=0, keepdims=True)
        m_new = jnp.maximum(m, block_max)
        return m_new, jnp.exp2(m - m_new), jnp.exp2(s - m_new).astype(BF16)

    def two_blocks(u, carry):
        alpha_prev, m, max_even = carry
        t = 2 * u
        max_odd = scores(t + 1, 1)
        value_update(jnp.maximum(t - 1, 0), 1, alpha_prev)
        m, alpha_even, p_buf[0] = softmax(s_buf[0], m, max_even)
        max_next = scores(t + 2, 0)
        value_update(t, 0, alpha_even)
        m, alpha_odd, p_buf[1] = softmax(s_buf[1], m, max_odd)
        return alpha_odd, m, max_next

    max_0 = scores(0, 0)
    p_buf[1] = jnp.zeros((tk, tq), BF16)
    acc_ref[...] = jnp.zeros_like(acc_ref)
    init = (jnp.ones((1, tq), F32), jnp.full((1, tq), MASKED, F32), max_0)
    alpha_prev, m, _ = lax.fori_loop(0, qi, two_blocks, init)

    t = 2 * qi
    key = lax.broadcasted_iota(jnp.int32, (tk, tk), 0)
    qry = lax.broadcasted_iota(jnp.int32, (tk, tk), 1)
    causal = key <= qry
    s_right = _dot(k_blk(t + 1), qt_ref[:, tk:])
    value_update(jnp.maximum(t - 1, 0), 1, alpha_prev)
    s = s_buf[0]
    s = jnp.concatenate([jnp.where(causal, s[:, :tk], MASKED), s[:, tk:]], axis=1)
    m, alpha, p_buf[0] = softmax(s, m)
    value_update(t, 0, alpha)
    _, alpha_r, p_r = softmax(jnp.where(causal, s_right, MASKED), m[:, tk:])
    acc_ref[:, tk:] = alpha_r * acc_ref[:, tk:] + _dot(v_blk(t + 1), p_r)
    acc = acc_ref[...]
    o_ref[0] = (acc[:HEAD_DIM] / acc[HEAD_DIM:HEAD_DIM + 1]).T.astype(BF16)


def _attn_call(q, k, vt):
    B, S, _ = q.shape
    tq, tk = ATTN_TQ, ATTN_TK
    assert tq == 2 * tk
    return pl.pallas_call(
        functools.partial(_attn_kernel, tq=tq, tk=tk),
        grid=(B, N_HEADS, S // tq),
        in_specs=[
            pl.BlockSpec((1, tq, QK_PAD), lambda b, h, i: (b, i, h)),
            pl.BlockSpec((1, S, QK_PAD), lambda b, h, i: (b, 0, h)),
            pl.BlockSpec((1, 1, HEAD_DIM, S), lambda b, h, i: (b, h, 0, 0)),
        ],
        out_specs=pl.BlockSpec((1, tq, HEAD_DIM), lambda b, h, i: (b, i, h)),
        out_shape=jax.ShapeDtypeStruct((B, S, N_HEADS * HEAD_DIM), BF16),
        scratch_shapes=[
            pltpu.VMEM((QK_PAD, tq), BF16),
            pltpu.VMEM((2, tk, tq), F32),
            pltpu.VMEM((2, tk, tq), BF16),
            pltpu.VMEM((ACC_ROWS, tq), F32),
        ],
        compiler_params=pltpu.CompilerParams(
            dimension_semantics=("arbitrary", "arbitrary", "arbitrary"), vmem_limit_bytes=VMEM_LIMIT),
        name="mla_attn",
    )(q, k, vt)


def _mlp_kernel(x_ref, ret_ref, mla_ref, mod_ref, wo32_hbm, ln1w_ref, ln1b_ref, wup32_hbm, cw_ref, cb_ref,
                wdn32_hbm, ln2w_ref, ln2b_ref, o_ref, carry_ref, ubuf_ref, act_ref,
                wo_ref, wup_ref, wdn_ref, stage_sq_ref, stage_up_ref, sem, *, tm):
    @pl.when(jnp.logical_and(pl.program_id(0) == 0, pl.program_id(1) == 0))
    def _():
        _stream_cast(wo32_hbm.at[0], stage_sq_ref, sem, _store_bf16(wo_ref))
        _stream_cast(wup32_hbm.at[0], stage_up_ref, sem, _store_bf16(wup_ref))
        _stream_cast(wdn32_hbm.at[0], stage_sq_ref, sem, _store_bf16(wdn_ref))

    @pl.when(pl.program_id(1) == 0)
    def _():
        carry_ref[...] = jnp.zeros_like(carry_ref)

    gate1 = mod_ref[0, 2:3, :]
    shift2 = mod_ref[0, 3:4, :]
    scale2 = mod_ref[0, 4:5, :]
    gate2 = mod_ref[0, 5:6, :]

    halves = (slice(0, tm // 2), slice(tm // 2, tm))

    def out_proj(rows):
        return (_dot(ret_ref[0, rows, :], wo_ref[0:RET_WIDTH, :])
                + _dot(mla_ref[0, rows, :], wo_ref[RET_WIDTH:2 * RET_WIDTH, :]))

    def norms(y, rows):
        x1 = _ln_plain(DN_ALPHA * x_ref[0, rows, :] + (1.0 + gate1) * y) * ln1w_ref[...] + ln1b_ref[...]
        return x1, (_ln_plain(x1) * (1.0 + scale2) + shift2).astype(BF16)

    ys = [out_proj(rows) for rows in halves]
    x1s, h2s = zip(*[norms(y, rows) for y, rows in zip(ys, halves)])
    h2 = jnp.concatenate(h2s, axis=0)

    def conv(u, col0, kind):
        outs = []
        for g in range(FF_CHUNK // LANES):
            cols = slice(col0 + g * LANES, col0 + (g + 1) * LANES)
            ug = u[:, g * LANES:(g + 1) * LANES]
            buf = ubuf_ref.at[kind, g]
            buf[0:SUBLANES, :] = carry_ref[:, cols]
            buf[SUBLANES:SUBLANES + tm, :] = ug
            carry_ref[:, cols] = ug[tm - SUBLANES:tm, :]
            u1 = buf[SUBLANES - 1:SUBLANES - 1 + tm, :]
            u2 = buf[SUBLANES - 2:SUBLANES - 2 + tm, :]
            outs.append(cb_ref[:, cols] + cw_ref[2:3, cols] * ug + cw_ref[1:2, cols] * u1
                        + cw_ref[0:1, cols] * u2)
        return jnp.concatenate(outs, axis=1)

    for c in range(N_FF_CHUNKS):
        gcol0 = c * FF_CHUNK
        vcol0 = D_FF + c * FF_CHUNK
        g = conv(_dot(h2, wup_ref[:, gcol0:gcol0 + FF_CHUNK]), gcol0, 0)
        val = conv(_dot(h2, wup_ref[:, vcol0:vcol0 + FF_CHUNK]), vcol0, 1)
        act_ref[:, gcol0:gcol0 + FF_CHUNK] = (_silu(g) * val).astype(BF16)

    y2s = [_dot(act_ref[rows, :], wdn_ref[...]) for rows in halves]
    for x1, y2, rows in zip(x1s, y2s, halves):
        o_ref[0, rows, :] = _ln_plain(DN_ALPHA * x1 + (1.0 + gate2) * y2) * ln2w_ref[...] + ln2b_ref[...]


def _mlp_call(x, ret, mla, mod, w_out, ln1w, ln1b, w_up, conv_w, conv_b, w_down, ln2w, ln2b):
    B, S, D = x.shape
    tm = ROW_TILE
    row = lambda w: pl.BlockSpec((1, tm, w), lambda b, s: (b, s, 0))
    hbm = pl.BlockSpec(memory_space=pl.ANY)
    return pl.pallas_call(
        functools.partial(_mlp_kernel, tm=tm),
        grid=(B, S // tm),
        in_specs=[
            row(D), row(RET_WIDTH), row(RET_WIDTH),
            pl.BlockSpec((1, N_MOD, D), lambda b, s: (b, 0, 0)),
            hbm, _const_spec(ln1w.shape), _const_spec(ln1b.shape),
            hbm, _const_spec(conv_w.shape), _const_spec(conv_b.shape),
            hbm, _const_spec(ln2w.shape), _const_spec(ln2b.shape),
        ],
        out_specs=row(D),
        out_shape=jax.ShapeDtypeStruct((B, S, D), F32),
        scratch_shapes=[
            pltpu.VMEM((SUBLANES, 2 * D_FF), F32),
            pltpu.VMEM((2, FF_CHUNK // LANES, SUBLANES + tm, LANES), F32),
            pltpu.VMEM((tm, D_FF), BF16),
            pltpu.VMEM(w_out.shape[1:], BF16),
            pltpu.VMEM(w_up.shape[1:], BF16),
            pltpu.VMEM(w_down.shape[1:], BF16),
            pltpu.VMEM((STAGE_SLOTS, STAGE_ROWS_SQ, D), F32),
            pltpu.VMEM((STAGE_SLOTS, STAGE_ROWS_UP, 2 * D_FF), F32),
            pltpu.SemaphoreType.DMA((STAGE_SLOTS,)),
        ],
        compiler_params=pltpu.CompilerParams(
            dimension_semantics=("arbitrary", "arbitrary"), vmem_limit_bytes=VMEM_LIMIT),
        name="outproj_mlp",
    )(x, ret, mla, mod, w_out, ln1w, ln1b, w_up, conv_w, conv_b, w_down, ln2w, ln2b)


def _rope_tables(seq, half, reps):
    pos = np.arange(seq, dtype=np.float32)
    inv = np.float32(ROPE_BASE) ** (-np.arange(half, dtype=np.float32) / np.float32(half))
    ang = (pos[:, None] * inv[None, :]).astype(np.float32)
    cos = np.cos(ang).astype(np.float32)
    sin = np.sin(ang).astype(np.float32)
    cos_t = np.tile(cos, (1, 2 * reps))
    sin_t = np.concatenate([np.tile(-sin, (1, reps)), np.tile(sin, (1, reps))], axis=-1)
    return jnp.asarray(cos_t), jnp.asarray(sin_t)


def _uq_columns():
    half = MLA_ROPE_DIM // 2
    nope = [h * MLA_QK_DIM + np.arange(HEAD_DIM) for h in range(N_HEADS)]
    rope = []
    for p in range(N_HEADS // 2):
        for part in range(2):
            for h in (2 * p, 2 * p + 1):
                rope.append(h * MLA_QK_DIM + HEAD_DIM + part * half + np.arange(half))
    return np.concatenate(nope + rope)


def _take_columns(w, cols):
    cols = np.asarray(cols)
    cuts = np.flatnonzero(np.diff(cols) != 1) + 1
    runs = np.split(cols, cuts)
    return jnp.concatenate([w[:, r[0]:r[-1] + 1] for r in runs], axis=1)


def kernel(x, c, w_ada, b_ada, w_in, ret_gn_w, mla_q_norm_w, w_uq, mla_kv_norm_w, w_ukv, w_out,
           ln1_w, ln1_b, w_up, conv_w, conv_b, w_down, ln2_w, ln2_b):
    B, S, D = x.shape
    assert D == D_MODEL and S % IN_ROW_TILE == 0 and S % ROW_TILE == 0 and S % ATTN_TQ == 0
    assert w_ada.shape[0] == DEPTH == 1
    l = 0

    mod = _ada_call(c, w_ada, b_ada[l][None, :]).reshape(B, N_MOD, D)

    w_uq_p = _take_columns(w_uq[l].astype(BF16), _uq_columns())
    w_ukv_p = w_ukv[l].astype(BF16)
    cosr, sinr = _rope_tables(S, HEAD_DIM // 2, 1)
    cosm, sinm = _rope_tables(S, MLA_ROPE_DIM // 2, 2)

    ret, q, k, vt = _inproj_call(
        x, mod, jnp.swapaxes(w_in, 1, 2), ret_gn_w[l][None, :], mla_q_norm_w[l][None, :], mla_kv_norm_w[l][None, :],
        w_uq_p, w_ukv_p, cosr, sinr, cosm, sinm)
    mla = _attn_call(q, k, vt)
    return _mlp_call(
        x, ret, mla, mod, w_out, ln1_w[l][None, :], ln1_b[l][None, :],
        w_up, conv_w[l], conv_b[l][None, :], w_down, ln2_w[l][None, :], ln2_b[l][None, :])
```
